```python
import math
import jax, jax.numpy as jnp
from jax import lax
import numpy as np

D_MODEL = 1024
BATCH = 8
SEQ = 2048
DEPTH = 1
DEC_BATCH = 128
DEC_SEQ = 4
PAST_LEN = 8192
PAGE_SIZE = 128

N_HEADS = 8
KV_HEADS = 2
GROUP = N_HEADS // KV_HEADS
HEAD_DIM = 64
Q_W = N_HEADS * HEAD_DIM
KV_W = KV_HEADS * HEAD_DIM
WINDOW = 128
REL_BUCKETS = 32
REL_MAX_DIST = 128
D_RNN = D_MODEL
RNN_BLOCKS = 16
RNN_BS = D_RNN // RNN_BLOCKS
CONV_W = 4
RG_C = 8.0
D_FF = 4 * D_MODEL
PLE_DIM = 256
EPS = 1e-6
NEG_INF = -1e30
IN_SIZES = (Q_W, KV_W, KV_W, D_RNN, D_RNN, D_MODEL, D_MODEL)
IN_COLS = sum(IN_SIZES)
IN_SPLITS = [int(s) for s in np.cumsum(IN_SIZES)[:-1]]

kernel_name = "hybrid_swa_sink_rglru_decode_step"


def rmsnorm(x, g):
    xf = x.astype(jnp.float32)
    y = xf * lax.rsqrt(jnp.mean(xf * xf, axis=-1, keepdims=True) + EPS) * g.astype(jnp.float32)
    return y.astype(x.dtype)


def rel_bucket(dist):
    n = jnp.maximum(dist, 0)
    max_exact = REL_BUCKETS // 2
    nf = jnp.maximum(n, 1).astype(jnp.float32)
    large = max_exact + (jnp.log(nf / max_exact) / math.log(REL_MAX_DIST / max_exact)
                         * (REL_BUCKETS - max_exact)).astype(jnp.int32)
    large = jnp.minimum(large, REL_BUCKETS - 1)
    return jnp.where(n < max_exact, n, large)


def window_attend(q, k, v, key_valid, sinks, rel_bias):
    B, N, Tq, H, Dh = q.shape
    Tk = k.shape[2]
    qg = q.reshape(B, N, Tq, KV_HEADS, GROUP, Dh)
    logits = jnp.einsum('bnqkgd,bnskd->bnkgqs', qg, k).astype(jnp.float32) * (Dh ** -0.5)
    dist = jnp.arange(Tq)[:, None] + (Tk - Tq) - jnp.arange(Tk)[None, :]
    bias = rel_bias[rel_bucket(dist)].astype(jnp.float32)
    bias = jnp.transpose(bias, (2, 0, 1)).reshape(KV_HEADS, GROUP, Tq, Tk)
    mask = (dist >= 0) & (dist <= WINDOW)
    mask = mask & key_valid[:, None, None, None, :]
    logits = jnp.where(mask, logits + bias, NEG_INF)
    sink = sinks.astype(jnp.float32).reshape(KV_HEADS, GROUP, 1, 1)
    m = jnp.maximum(jnp.max(logits, axis=-1, keepdims=True), sink)
    e = jnp.exp(logits - m)
    probs = e / (jnp.sum(e, axis=-1, keepdims=True) + jnp.exp(sink - m))
    out = jnp.einsum('bnkgqs,bnskd->bnqkgd', probs.astype(v.dtype), v)
    return out.reshape(B, N, Tq, H * Dh)


def causal_conv(x, prev, w, b):
    T = x.shape[1]
    xp = jnp.concatenate([prev.astype(x.dtype), x], axis=1)
    y = b + sum(w[j] * xp[:, j:j + T] for j in range(CONV_W))
    return y, xp[:, -(CONV_W - 1):]


def rglru(x, h0, wa, ba, wx, bx, lam):
    B, T, _ = x.shape
    xb = x.reshape(B, T, RNN_BLOCKS, RNN_BS)
    r = jax.nn.sigmoid((jnp.einsum('btnd,nde->btne', xb, wa).reshape(B, T, D_RNN) + ba).astype(jnp.float32))
    i = jax.nn.sigmoid((jnp.einsum('btnd,nde->btne', xb, wx).reshape(B, T, D_RNN) + bx).astype(jnp.float32))
    log_a = -RG_C * jax.nn.softplus(-lam.astype(jnp.float32)) * r
    a = jnp.exp(log_a)
    bterm = jnp.sqrt(-jnp.expm1(2.0 * log_a)) * (i * x.astype(jnp.float32))
    bterm = bterm.at[:, 0].add(a[:, 0] * h0.astype(jnp.float32))

    def combine(c1, c2):
        a1, b1 = c1
        a2, b2 = c2
        return a1 * a2, a2 * b1 + b2

    _, h = lax.associative_scan(combine, (a, bterm), axis=1)
    return h, h[:, -1]


def layer(x, p, k_past, v_past, conv_prev, h0, rel_bias, lw):
    B, T, _ = x.shape
    xn = rmsnorm(x, lw['norm1_g'])
    proj = xn @ lw['w_in']
    q, k, v, xr, gr, ga_logit, gr_logit = jnp.split(proj, IN_SPLITS, axis=-1)
    q = rmsnorm(q.reshape(B, T, N_HEADS, HEAD_DIM), lw['q_norm_g'])
    k = rmsnorm(k.reshape(B, T, KV_HEADS, HEAD_DIM), lw['k_norm_g'])
    v = v.reshape(B, T, KV_HEADS, HEAD_DIM)
    if k_past is None:
        nb = T // WINDOW
        qb = q.reshape(B, nb, WINDOW, N_HEADS, HEAD_DIM)
        kb = k.reshape(B, nb, WINDOW, KV_HEADS, HEAD_DIM)
        vb = v.reshape(B, nb, WINDOW, KV_HEADS, HEAD_DIM)
        kk = jnp.concatenate([jnp.concatenate([jnp.zeros_like(kb[:, :1]), kb[:, :-1]], axis=1), kb], axis=2)
        vv = jnp.concatenate([jnp.concatenate([jnp.zeros_like(vb[:, :1]), vb[:, :-1]], axis=1), vb], axis=2)
        key_valid = (jnp.arange(nb)[:, None] > 0) | (jnp.arange(2 * WINDOW)[None, :] >= WINDOW)
        att = window_attend(qb, kk, vv, key_valid, lw['sinks'], rel_bias).reshape(B, T, Q_W)
        k_all, v_all = k, v
        conv_prev = jnp.zeros((B, CONV_W - 1, D_RNN), x.dtype)
        h0 = jnp.zeros((B, D_RNN), jnp.float32)
    else:
        k_all = jnp.concatenate([k_past.astype(k.dtype), k], axis=1)
        v_all = jnp.concatenate([v_past.astype(v.dtype), v], axis=1)
        key_valid = jnp.ones((1, k_all.shape[1]), bool)
        att = window_attend(q[:, None], k_all[:, None], v_all[:, None], key_valid,
                            lw['sinks'], rel_bias)[:, 0]
    new_k = k_all[:, -WINDOW:]
    new_v = v_all[:, -WINDOW:]
    xc, new_conv = causal_conv(xr, conv_prev, lw['conv_w'], lw['conv_b'])
    h, h_last = rglru(xc, h0, lw['rg_wa'], lw['rg_ba'], lw['rg_wx'], lw['rg_bx'], lw['rg_lambda'])
    rnn = (h.astype(x.dtype) * jax.nn.gelu(gr)) @ lw['w_o_rnn']
    mix = (jax.nn.sigmoid(ga_logit) * (att @ lw['w_o_attn']) + jax.nn.sigmoid(gr_logit) * rnn) @ lw['w_out']
    x = x + mix
    hmid = jax.nn.relu(rmsnorm(x, lw['norm2_g']) @ lw['w_up'])
    x = x + (hmid * hmid) @ lw['w_down']
    gate = jax.nn.sigmoid(rmsnorm(x, lw['ple_norm_g']) @ lw['w_ple_gate'])
    x = x + gate * (p.astype(x.dtype) @ lw['w_ple'])
    return x, new_k, new_v, new_conv, h_last


def setup_inputs(seed: int = 0) -> dict:
    key = jax.random.key(seed)
    ks = jax.random.split(key, 32)
    f32 = jnp.float32

    def nrm(k, shape, scale=1.0):
        return jax.random.normal(k, shape, f32) * scale

    def gain(k, n):
        return 1.0 + 0.02 * jax.random.normal(k, (DEPTH, n), f32)

    u = jax.random.uniform(ks[20], (DEPTH, D_RNN), f32, minval=0.9, maxval=0.999)
    a0 = u ** (1.0 / RG_C)
    return {
        'x_prompt': nrm(ks[0], (BATCH, SEQ, D_MODEL)),
        'x_sample': nrm(ks[1], (DEC_BATCH, DEC_SEQ, D_MODEL)),
        'cache_k_win': nrm(ks[2], (DEPTH, DEC_BATCH, WINDOW, KV_HEADS, HEAD_DIM)),
        'cache_v_win': nrm(ks[3], (DEPTH, DEC_BATCH, WINDOW, KV_HEADS, HEAD_DIM)),
        'state_conv': nrm(ks[4], (DEPTH, DEC_BATCH, CONV_W - 1, D_RNN)),
        'state_h': nrm(ks[5], (DEPTH, DEC_BATCH, D_RNN), 0.5),
        'p_prompt': nrm(ks[6], (DEPTH, BATCH, SEQ, PLE_DIM)),
        'p_sample': nrm(ks[7], (DEPTH, DEC_BATCH, DEC_SEQ, PLE_DIM)),
        'rel_bias': nrm(ks[8], (REL_BUCKETS, N_HEADS), 0.2),
        'norm1_g': gain(ks[9], D_MODEL),
        'w_in': nrm(ks[10], (DEPTH, D_MODEL, IN_COLS), D_MODEL ** -0.5),
        'q_norm_g': gain(ks[11], HEAD_DIM),
        'k_norm_g': gain(ks[12], HEAD_DIM),
        'sinks': nrm(ks[13], (DEPTH, N_HEADS), 0.5),
        'w_o_attn': nrm(ks[14], (DEPTH, Q_W, D_MODEL), Q_W ** -0.5),
        'conv_w': nrm(ks[15], (DEPTH, CONV_W, D_RNN), CONV_W ** -0.5),
        'conv_b': nrm(ks[16], (DEPTH, D_RNN), 0.02),
        'rg_wa': nrm(ks[17], (DEPTH, RNN_BLOCKS, RNN_BS, RNN_BS), RNN_BS ** -0.5),
        'rg_ba': nrm(ks[18], (DEPTH, D_RNN), 0.02),
        'rg_wx': nrm(ks[19], (DEPTH, RNN_BLOCKS, RNN_BS, RNN_BS), RNN_BS ** -0.5),
        'rg_bx': nrm(ks[21], (DEPTH, D_RNN), 0.02),
        'rg_lambda': jnp.log(a0) - jnp.log1p(-a0),
        'w_o_rnn': nrm(ks[22], (DEPTH, D_RNN, D_MODEL), D_RNN ** -0.5),
        'w_out': nrm(ks[23], (DEPTH, D_MODEL, D_MODEL), D_MODEL ** -0.5),
        'norm2_g': gain(ks[24], D_MODEL),
        'w_up': nrm(ks[25], (DEPTH, D_MODEL, D_FF), D_MODEL ** -0.5),
        'w_down': nrm(ks[26], (DEPTH, D_FF, D_MODEL), D_FF ** -0.5),
        'ple_norm_g': gain(ks[27], D_MODEL),
        'w_ple_gate': nrm(ks[28], (DEPTH, D_MODEL, D_MODEL), D_MODEL ** -0.5),
        'w_ple': nrm(ks[29], (DEPTH, PLE_DIM, D_MODEL), PLE_DIM ** -0.5),
    }


def reference(x_prompt, x_sample, cache_k_win, cache_v_win, state_conv, state_h, p_prompt, p_sample,
              rel_bias, norm1_g, w_in, q_norm_g, k_norm_g, sinks, w_o_attn, conv_w, conv_b,
              rg_wa, rg_ba, rg_wx, rg_bx, rg_lambda, w_o_rnn, w_out, norm2_g, w_up, w_down,
              ple_norm_g, w_ple_gate, w_ple):
    xp, xs = x_prompt, x_sample
    kp_l, vp_l, cp_l, hp_l, ksl, vsl, csl, hsl = [], [], [], [], [], [], [], []
    for i in range(DEPTH):
        lw = dict(norm1_g=norm1_g[i], w_in=w_in[i], q_norm_g=q_norm_g[i], k_norm_g=k_norm_g[i],
                  sinks=sinks[i], w_o_attn=w_o_attn[i], conv_w=conv_w[i], conv_b=conv_b[i],
                  rg_wa=rg_wa[i], rg_ba=rg_ba[i], rg_wx=rg_wx[i], rg_bx=rg_bx[i],
                  rg_lambda=rg_lambda[i], w_o_rnn=w_o_rnn[i], w_out=w_out[i], norm2_g=norm2_g[i],
                  w_up=w_up[i], w_down=w_down[i], ple_norm_g=ple_norm_g[i],
                  w_ple_gate=w_ple_gate[i], w_ple=w_ple[i])
        xp, kp, vp, cp, hp = layer(xp, p_prompt[i], None, None, None, None, rel_bias, lw)
        xs, kss, vss, css, hss = layer(xs, p_sample[i], cache_k_win[i], cache_v_win[i],
                                       state_conv[i], state_h[i], rel_bias, lw)
        kp_l.append(kp); vp_l.append(vp); cp_l.append(cp); hp_l.append(hp)
        ksl.append(kss); vsl.append(vss); csl.append(css); hsl.append(hss)
    return (xp, xs,
            jnp.stack(kp_l), jnp.stack(vp_l), jnp.stack(cp_l), jnp.stack(hp_l),
            jnp.stack(ksl), jnp.stack(vsl), jnp.stack(csl), jnp.stack(hsl))
```

```python
import functools
import math

import numpy as np
import jax
import jax.numpy as jnp
from jax import lax
from jax.experimental import pallas as pl
from jax.experimental.pallas import tpu as pltpu

F32 = jnp.float32
BF16 = jnp.bfloat16

N_HEADS = 8
KV_HEADS = 2
GROUP = N_HEADS // KV_HEADS
HEAD_DIM = 64
Q_W = N_HEADS * HEAD_DIM
KV_W = KV_HEADS * HEAD_DIM
WINDOW = 128
REL_BUCKETS = 32
REL_MAX_DIST = 128
RNN_BS = 64
CONV_W = 4
RG_C = 8.0
EPS = 1e-6
NEG_INF = -1e30
EXPM1_SERIES_BELOW = 1.0 / 64.0

LANES = 128
MXU_DIM = 256
VMEM_LIMIT = 56 * 1024 * 1024

TM_MIX = 256
TM_MLP = 512
FF_CHUNK = 1024
SEQ_BLOCK = 16


def _dot(a, b):
    return jnp.dot(a, b, preferred_element_type=F32)


def _dot_nt(a, b):
    return lax.dot_general(a, b, (((1,), (1,)), ((), ())), preferred_element_type=F32)


def _rms(x, g):
    ms = jnp.mean(x * x, axis=-1, keepdims=True)
    return x * lax.rsqrt(ms + EPS) * g


def _head_rms(x, ones_ref, g):
    x2 = x * x
    hi = x2.astype(BF16)
    lo = (x2 - hi.astype(F32)).astype(BF16)
    ms = _dot(hi, ones_ref[...]) + _dot(lo, ones_ref[...])
    return x * lax.rsqrt(ms + EPS) * g


def _lower_half(shape):
    return lax.broadcasted_iota(jnp.int32, shape, len(shape) - 1) < HEAD_DIM


def _shift_rows(x, d, fill):
    rolled = pltpu.roll(x, d, axis=0)
    row = lax.broadcasted_iota(jnp.int32, x.shape, 0)
    return jnp.where(row >= d, rolled, fill)


def _scan_rows(a, b):
    d = 1
    n = a.shape[0]
    while d < n:
        b = a * _shift_rows(b, d, 0.0) + b
        if 2 * d < n:
            a = a * _shift_rows(a, d, 1.0)
        d *= 2
    return b


def _neg_expm1(x, exp_x):
    series = -x * (1.0 + x * (0.5 + x * (1.0 / 6.0 + x * (1.0 / 24.0))))
    return jnp.where(x > -EXPM1_SERIES_BELOW, series, 1.0 - exp_x)


def _rglru_gates(xc, bd_ref, ba, bx, lam):
    xcb = xc.astype(BF16)
    n_grp = xc.shape[1] // MXU_DIM
    ya, yx = [], []
    for gi in range(n_grp):
        y = _dot(xcb[:, gi * MXU_DIM:(gi + 1) * MXU_DIM], bd_ref[gi])
        ya.append(y[:, :MXU_DIM])
        yx.append(y[:, MXU_DIM:])
    r = jax.nn.sigmoid(jnp.concatenate(ya, axis=1) + ba)
    i = jax.nn.sigmoid(jnp.concatenate(yx, axis=1) + bx)
    log_a = (-RG_C * jax.nn.softplus(-lam)) * r
    a = jnp.exp(log_a)
    bterm = jnp.sqrt(_neg_expm1(2.0 * log_a, a * a)) * (i * xc)
    return a, bterm


def _merge(x, xn, h, att, wrest_ref, woa_ref, wor_ref, wout_ref):
    d = x.shape[1]
    gr = _dot(xn, wrest_ref[:, d:2 * d])
    rnn = _dot((h * jax.nn.gelu(gr)).astype(BF16), wor_ref[...])
    atto = _dot(att.astype(BF16), woa_ref[...])
    ga = _dot(xn, wrest_ref[:, 2 * d:3 * d])
    grl = _dot(xn, wrest_ref[:, 3 * d:4 * d])
    mix = (jax.nn.sigmoid(ga) * atto + jax.nn.sigmoid(grl) * rnn).astype(BF16)
    return x + _dot(mix, wout_ref[...])


def _prompt_mixer_kernel(x_ref, n1g_ref, wqkv_ref, wrest_ref, gq_ref, gk_ref, oq_ref, ok_ref,
                         bias_ref, sink_ref, cw_ref, cb_ref, bd_ref, ba_ref, bx_ref, lam_ref,
                         woa_ref, wor_ref, wout_ref,
                         x1_ref, kwin_ref, vwin_ref, cst_ref, hst_ref,
                         kbuf, vbuf, xp, hc, att):
    tm = x_ref.shape[0]
    t = pl.program_id(1)
    last = pl.num_programs(1) - 1

    @pl.when(t == 0)
    def _():
        kbuf[0:WINDOW, :] = jnp.zeros((WINDOW, KV_W), BF16)
        vbuf[0:WINDOW, :] = jnp.zeros((WINDOW, KV_W), BF16)
        xp[0:8, :] = jnp.zeros((8, xp.shape[1]), F32)
        hc[...] = jnp.zeros(hc.shape, F32)

    x = x_ref[...]
    xn = _rms(x, n1g_ref[...]).astype(BF16)

    qkv = _dot(xn, wqkv_ref[...])
    qn = _head_rms(qkv[:, :Q_W], oq_ref, gq_ref[...]).astype(BF16)
    kn = _head_rms(qkv[:, Q_W:Q_W + KV_W], ok_ref, gk_ref[...])
    v = qkv[:, Q_W + KV_W:]
    kbuf[WINDOW:WINDOW + tm, :] = kn.astype(BF16)
    vbuf[WINDOW:WINDOW + tm, :] = v.astype(BF16)

    @pl.when(t == last)
    def _():
        kwin_ref[...] = kn[tm - WINDOW:, :]
        vwin_ref[...] = v[tm - WINDOW:, :]

    first = jnp.where(t == 0, 1, 0)
    lower = _lower_half((WINDOW, LANES))
    zero = jnp.zeros((WINDOW, LANES), BF16)
    for s in range(tm // WINDOW):
        rows = slice(s * WINDOW, (s + 1) * WINDOW)
        keys = kbuf[s * WINDOW:(s + 2) * WINDOW, :]
        vals = vbuf[s * WINDOW:(s + 2) * WINDOW, :]
        outs = []
        for g in range(KV_HEADS):
            qs = []
            for j in range(GROUP):
                slab = qn[rows, j * LANES:(j + 1) * LANES]
                qs.append(jnp.where(lower, slab, zero) if g == 0 else jnp.where(lower, zero, slab))
            logit = _dot_nt(jnp.concatenate(qs, axis=0), keys)
            logit = logit + (bias_ref[first, g] if s == 0 else bias_ref[0, g])
            sink = sink_ref[g]
            m = jnp.maximum(jnp.max(logit, axis=-1, keepdims=True), sink)
            e = jnp.exp(logit - m)
            den = jnp.sum(e, axis=-1, keepdims=True) + jnp.exp(sink - m)
            outs.append(_dot(e.astype(BF16), vals) / den)
        for j in range(GROUP):
            blk = slice(j * WINDOW, (j + 1) * WINDOW)
            att[rows, j * LANES:(j + 1) * LANES] = jnp.where(lower, outs[0][blk], outs[1][blk])

    kbuf[0:WINDOW, :] = kbuf[tm:tm + WINDOW, :]
    vbuf[0:WINDOW, :] = vbuf[tm:tm + WINDOW, :]

    d = x.shape[1]
    xr = _dot(xn, wrest_ref[:, 0:d])
    xp[8:8 + tm, :] = xr
    xc = cb_ref[...] + cw_ref[3:4, :] * xr
    for j in range(CONV_W - 1):
        xc = xc + cw_ref[j:j + 1, :] * xp[pl.ds(8 - (CONV_W - 1) + j, tm), :]

    @pl.when(t == last)
    def _():
        cst_ref[...] = xp[pl.ds(8 + tm - (CONV_W - 1), CONV_W - 1), :]

    xp[0:8, :] = xp[tm:tm + 8, :]

    a, bterm = _rglru_gates(xc, bd_ref, ba_ref[...], bx_ref[...], lam_ref[...])
    row = lax.broadcasted_iota(jnp.int32, a.shape, 0)
    bterm = bterm + jnp.where(row == 0, a * hc[...], 0.0)
    h = _scan_rows(a, bterm)
    hc[...] = h[tm - 1:tm, :]

    @pl.when(t == last)
    def _():
        hst_ref[...] = h[tm - 1:tm, :]

    x1_ref[...] = _merge(x, xn, h, att[...], wrest_ref, woa_ref, wor_ref, wout_ref)


def _const_spec(shape):
    nd = len(shape)
    return pl.BlockSpec(shape, lambda *_: (0,) * nd, pipeline_mode=pl.Buffered(1))


def _prompt_mixer(x, w):
    b, t, d = x.shape
    tm = TM_MIX
    consts = [w['n1g'], w['wqkv'], w['wrest'], w['gq'], w['gk'], w['oq'], w['ok'],
              w['bias_p'], w['sink_p'], w['cw'], w['cb'], w['bd'], w['ba'], w['bx'], w['lam'],
              w['woa'], w['wor'], w['wout']]
    out_shape = (
        jax.ShapeDtypeStruct((b, t, d), F32),
        jax.ShapeDtypeStruct((b, WINDOW, KV_W), F32),
        jax.ShapeDtypeStruct((b, WINDOW, KV_W), F32),
        jax.ShapeDtypeStruct((b, CONV_W - 1, d), F32),
        jax.ShapeDtypeStruct((b, 1, d), F32),
    )
    per_seq = lambda bi, ti: (bi, 0, 0)
    return pl.pallas_call(
        _prompt_mixer_kernel,
        out_shape=out_shape,
        grid=(b, t // tm),
        in_specs=[pl.BlockSpec((None, tm, d), lambda bi, ti: (bi, ti, 0))]
                 + [_const_spec(c.shape) for c in consts],
        out_specs=(
            pl.BlockSpec((None, tm, d), lambda bi, ti: (bi, ti, 0)),
            pl.BlockSpec((None, WINDOW, KV_W), per_seq),
            pl.BlockSpec((None, WINDOW, KV_W), per_seq),
            pl.BlockSpec((None, CONV_W - 1, d), per_seq),
            pl.BlockSpec((None, 1, d), per_seq),
        ),
        scratch_shapes=[
            pltpu.VMEM((WINDOW + tm, KV_W), BF16),
            pltpu.VMEM((WINDOW + tm, KV_W), BF16),
            pltpu.VMEM((8 + tm, d), F32),
            pltpu.VMEM((1, d), F32),
            pltpu.VMEM((tm, Q_W), F32),
        ],
        compiler_params=pltpu.CompilerParams(
            dimension_semantics=("arbitrary", "arbitrary"), vmem_limit_bytes=VMEM_LIMIT),
        name="prompt_mixer",
    )(x, *consts)


def _mlp_kernel(x_ref, p_ref, n2g_ref, wup_ref, wdn_ref, pg_ref, wpg_ref, wple_ref, o_ref):
    x = x_ref[...]
    xn = _rms(x, n2g_ref[...]).astype(BF16)
    acc = x
    for c in range(wup_ref.shape[1] // FF_CHUNK):
        cols = slice(c * FF_CHUNK, (c + 1) * FF_CHUNK)
        hmid = jnp.maximum(_dot(xn, wup_ref[:, cols]), 0.0)
        acc = acc + _dot((hmid * hmid).astype(BF16), wdn_ref[cols, :])
    gate = jax.nn.sigmoid(_dot(_rms(acc, pg_ref[...]).astype(BF16), wpg_ref[...]))
    o_ref[...] = acc + gate * _dot(p_ref[...].astype(BF16), wple_ref[...])


def _mlp(x, p, w):
    m, d = x.shape
    tm = min(TM_MLP, m)
    consts = [w['n2g'], w['wup'], w['wdn'], w['pg'], w['wpg'], w['wple']]
    return pl.pallas_call(
        _mlp_kernel,
        out_shape=jax.ShapeDtypeStruct((m, d), F32),
        grid=(m // tm,),
        in_specs=[pl.BlockSpec((tm, d), lambda i: (i, 0)),
                  pl.BlockSpec((tm, p.shape[1]), lambda i: (i, 0))]
                 + [_const_spec(c.shape) for c in consts],
        out_specs=pl.BlockSpec((tm, d), lambda i: (i, 0)),
        compiler_params=pltpu.CompilerParams(
            dimension_semantics=("arbitrary",), vmem_limit_bytes=VMEM_LIMIT),
        name="mlp_ple",
    )(x, p, *consts)


def _decode_attn_kernel(x_ref, ck_ref, cv_ref, n1g_ref, wqkv_ref, gq_ref, gk_ref, oq_ref, ok_ref,
                        bias_c_ref, bias_n_ref, sink_ref,
                        att_ref, kwin_ref, vwin_ref):
    sb = ck_ref.shape[0]
    nt = x_ref.shape[0] // sb
    xn = _rms(x_ref[...], n1g_ref[...]).astype(BF16)
    qkv = _dot(xn, wqkv_ref[...])
    qn = _head_rms(qkv[:, :Q_W], oq_ref, gq_ref[...])
    kn = _head_rms(qkv[:, Q_W:Q_W + KV_W], ok_ref, gk_ref[...])
    v = qkv[:, Q_W + KV_W:]

    lower = _lower_half((nt, LANES))
    sink = sink_ref[...]
    for b in range(sb):
        rows = slice(b * nt, (b + 1) * nt)
        ck = ck_ref[b]
        cv = cv_ref[b]
        kwin_ref[b, 0:WINDOW - nt, :] = ck[nt:, :]
        vwin_ref[b, 0:WINDOW - nt, :] = cv[nt:, :]
        kwin_ref[b, WINDOW - nt:WINDOW, :] = kn[rows]
        vwin_ref[b, WINDOW - nt:WINDOW, :] = v[rows]

        qs = []
        for g in range(KV_HEADS):
            for j in range(GROUP):
                slab = qn[rows, j * LANES:(j + 1) * LANES]
                qs.append(jnp.where(lower, slab, 0.0) if g == 0 else jnp.where(lower, 0.0, slab))
        qb = jnp.concatenate(qs, axis=0)
        lc = _dot_nt(qb.astype(BF16), ck.astype(BF16)) + bias_c_ref[...]
        ln = [jnp.sum(qb * kn[b * nt + u:b * nt + u + 1, :], axis=-1, keepdims=True)
              + bias_n_ref[:, u:u + 1] for u in range(nt)]
        m = jnp.maximum(jnp.max(lc, axis=-1, keepdims=True), sink)
        for u in range(nt):
            m = jnp.maximum(m, ln[u])
        ec = jnp.exp(lc - m)
        den = jnp.sum(ec, axis=-1, keepdims=True) + jnp.exp(sink - m)
        out = _dot(ec.astype(BF16), cv.astype(BF16))
        for u in range(nt):
            en = jnp.exp(ln[u] - m)
            den = den + en
            out = out + en * v[b * nt + u:b * nt + u + 1, :]
        out = out / den
        half = GROUP * nt
        for j in range(GROUP):
            att_ref[rows, j * LANES:(j + 1) * LANES] = jnp.where(
                lower, out[j * nt:(j + 1) * nt], out[half + j * nt:half + (j + 1) * nt])


def _decode_attn(x, ck, cv, w):
    m, d = x.shape
    nb = ck.shape[0]
    nt = m // nb
    sb = SEQ_BLOCK
    consts = [w['n1g'], w['wqkv'], w['gq'], w['gk'], w['oq'], w['ok'],
              w['bias_sc'], w['bias_sn'], w['sink_s']]
    cache_spec = pl.BlockSpec((sb, WINDOW, KV_W), lambda i: (i, 0, 0))
    return pl.pallas_call(
        _decode_attn_kernel,
        out_shape=(jax.ShapeDtypeStruct((m, Q_W), F32),
                   jax.ShapeDtypeStruct(ck.shape, F32),
                   jax.ShapeDtypeStruct(cv.shape, F32)),
        grid=(nb // sb,),
        in_specs=[pl.BlockSpec((sb * nt, d), lambda i: (i, 0)), cache_spec, cache_spec]
                 + [_const_spec(c.shape) for c in consts],
        out_specs=(pl.BlockSpec((sb * nt, Q_W), lambda i: (i, 0)), cache_spec, cache_spec),
        compiler_params=pltpu.CompilerParams(
            dimension_semantics=("arbitrary",), vmem_limit_bytes=VMEM_LIMIT),
        name="decode_attn",
    )(x, ck, cv, *consts)


def _decode_mixer_kernel(x_ref, att_ref, cst_ref, h0_ref, n1g_ref, wrest_ref, cw_ref, cb_ref, bd_ref,
                         ba_ref, bx_ref, lam_ref, woa_ref, wor_ref, wout_ref,
                         x1_ref, cnew_ref, hnew_ref):
    nb = h0_ref.shape[0]
    nt = x_ref.shape[0] // nb
    d = x_ref.shape[1]
    x = x_ref[...]
    xn = _rms(x, n1g_ref[...]).astype(BF16)
    xr = _dot(xn, wrest_ref[:, 0:d])
    prev = cst_ref[...]
    slabs = [prev[k * nb:(k + 1) * nb] for k in range(CONV_W - 1)]
    slabs += [xr[k * nb:(k + 1) * nb] for k in range(nt)]
    xc = jnp.concatenate(
        [cb_ref[...] + sum(cw_ref[j:j + 1, :] * slabs[ti + j] for j in range(CONV_W))
         for ti in range(nt)], axis=0)
    cnew_ref[...] = jnp.concatenate(slabs[-(CONV_W - 1):], axis=0)

    a, bterm = _rglru_gates(xc, bd_ref, ba_ref[...], bx_ref[...], lam_ref[...])
    h = h0_ref[...]
    hs = []
    for ti in range(nt):
        h = a[ti * nb:(ti + 1) * nb] * h + bterm[ti * nb:(ti + 1) * nb]
        hs.append(h)
    hnew_ref[...] = h
    x1_ref[...] = _merge(x, xn, jnp.concatenate(hs, axis=0), att_ref[...],
                         wrest_ref, woa_ref, wor_ref, wout_ref)


def _decode_mixer(x, att, cst, h0, w):
    m, d = x.shape
    consts = [w['n1g'], w['wrest'], w['cw'], w['cb'], w['bd'], w['ba'], w['bx'], w['lam'],
              w['woa'], w['wor'], w['wout']]
    ins = [x, att, cst, h0] + consts
    return pl.pallas_call(
        _decode_mixer_kernel,
        out_shape=(jax.ShapeDtypeStruct((m, d), F32),
                   jax.ShapeDtypeStruct(cst.shape, F32),
                   jax.ShapeDtypeStruct(h0.shape, F32)),
        grid=(1,),
        in_specs=[_const_spec(c.shape) for c in ins],
        out_specs=(pl.BlockSpec((m, d), lambda i: (0, 0)),
                   pl.BlockSpec(cst.shape, lambda i: (0, 0)),
                   pl.BlockSpec(h0.shape, lambda i: (0, 0))),
        compiler_params=pltpu.CompilerParams(
            dimension_semantics=("arbitrary",), vmem_limit_bytes=VMEM_LIMIT),
        name="decode_mixer",
    )(*ins)


def _rel_bucket(dist):
    n = jnp.maximum(dist, 0)
    max_exact = REL_BUCKETS // 2
    nf = jnp.maximum(n, 1).astype(F32)
    large = max_exact + (jnp.log(nf / max_exact) / math.log(REL_MAX_DIST / max_exact)
                         * (REL_BUCKETS - max_exact)).astype(jnp.int32)
    large = jnp.minimum(large, REL_BUCKETS - 1)
    return jnp.where(n < max_exact, n, large)


def _bias_rows(rel_bias, dist, mask):
    tb = rel_bias[_rel_bucket(dist)].astype(F32)
    tb = jnp.where(mask[:, :, None], tb, NEG_INF)
    tq, tk = dist.shape
    return jnp.transpose(tb, (2, 0, 1)).reshape(KV_HEADS, GROUP * tq, tk)


def _block_diag(wb):
    n, bs, _ = wb.shape
    eye = jnp.eye(n, dtype=wb.dtype)
    return jnp.einsum('nde,nm->ndme', wb, eye).reshape(n * bs, n * bs)


def _head_avg(width):
    idx = np.arange(width) // HEAD_DIM
    return jnp.asarray((idx[:, None] == idx[None, :]).astype(np.float32) / HEAD_DIM, BF16)


def _prepare(rel_bias, norm1_g, w_in, q_norm_g, k_norm_g, sinks, w_o_attn, conv_w, conv_b,
             rg_wa, rg_ba, rg_wx, rg_bx, rg_lambda, w_o_rnn, w_out, norm2_g, w_up, w_down,
             ple_norm_g, w_ple_gate, w_ple, n_dec):
    d = w_in.shape[0]
    order = [g * GROUP + j for j in range(GROUP) for g in range(KV_HEADS)]
    perm = np.concatenate([np.arange(h * HEAD_DIM, (h + 1) * HEAD_DIM) for h in order])
    row = lambda a: a.reshape(1, -1).astype(F32)
    w = {}
    w['n1g'] = row(norm1_g)
    w['wqkv'] = jnp.concatenate([w_in[:, :Q_W][:, perm], w_in[:, Q_W:Q_W + 2 * KV_W]], axis=1).astype(BF16)
    w['wrest'] = w_in[:, Q_W + 2 * KV_W:].astype(BF16)
    w['gq'] = row(jnp.tile(q_norm_g, N_HEADS)) * (HEAD_DIM ** -0.5)
    w['gk'] = row(jnp.tile(k_norm_g, KV_HEADS))
    w['oq'] = _head_avg(Q_W)
    w['ok'] = _head_avg(KV_W)
    w['woa'] = w_o_attn[perm, :].astype(BF16)
    w['cw'] = conv_w.astype(F32)
    w['cb'] = row(conv_b)
    n_grp = d // MXU_DIM
    per = MXU_DIM // RNN_BS
    w['bd'] = jnp.stack([
        jnp.concatenate([_block_diag(rg_wa[gi * per:(gi + 1) * per]),
                         _block_diag(rg_wx[gi * per:(gi + 1) * per])], axis=1)
        for gi in range(n_grp)]).astype(BF16)
    w['ba'] = row(rg_ba)
    w['bx'] = row(rg_bx)
    w['lam'] = row(rg_lambda)
    w['wor'] = w_o_rnn.astype(BF16)
    w['wout'] = w_out.astype(BF16)
    w['n2g'] = row(norm2_g)
    w['wup'] = w_up.astype(BF16)
    w['wdn'] = w_down.astype(BF16)
    w['pg'] = row(ple_norm_g)
    w['wpg'] = w_ple_gate.astype(BF16)
    w['wple'] = w_ple.astype(BF16)

    qi = jnp.arange(WINDOW)[:, None]
    kj = jnp.arange(2 * WINDOW)[None, :]
    dist = qi + WINDOW - kj
    mask = (dist >= 0) & (dist <= WINDOW)
    w['bias_p'] = jnp.stack([_bias_rows(rel_bias, dist, mask),
                             _bias_rows(rel_bias, dist, mask & (kj >= WINDOW))])
    sink_rows = sinks.astype(F32).reshape(KV_HEADS, GROUP, 1)
    w['sink_p'] = jnp.repeat(sink_rows, WINDOW, axis=1).reshape(KV_HEADS, GROUP * WINDOW, 1)

    ti = jnp.arange(n_dec)[:, None]
    dist_c = ti + WINDOW - jnp.arange(WINDOW)[None, :]
    dist_n = ti - jnp.arange(n_dec)[None, :]
    w['bias_sc'] = _bias_rows(rel_bias, dist_c, dist_c <= WINDOW).reshape(N_HEADS * n_dec, WINDOW)
    w['bias_sn'] = _bias_rows(rel_bias, dist_n, dist_n >= 0).reshape(N_HEADS * n_dec, n_dec)
    w['sink_s'] = jnp.repeat(sink_rows, n_dec, axis=1).reshape(N_HEADS * n_dec, 1)
    return w


def kernel(x_prompt, x_sample, cache_k_win, cache_v_win, state_conv, state_h, p_prompt, p_sample, rel_bias, norm1_g, w_in, q_norm_g, k_norm_g, sinks, w_o_attn, conv_w, conv_b, rg_wa, rg_ba, rg_wx, rg_bx, rg_lambda, w_o_rnn, w_out, norm2_g, w_up, w_down, ple_norm_g, w_ple_gate, w_ple):
    depth = w_in.shape[0]
    assert depth == 1, "single-layer step"
    b, t, d = x_prompt.shape
    nb, nt, _ = x_sample.shape
    w = _prepare(rel_bias, norm1_g[0], w_in[0], q_norm_g[0], k_norm_g[0], sinks[0], w_o_attn[0],
                 conv_w[0], conv_b[0], rg_wa[0], rg_ba[0], rg_wx[0], rg_bx[0], rg_lambda[0],
                 w_o_rnn[0], w_out[0], norm2_g[0], w_up[0], w_down[0], ple_norm_g[0],
                 w_ple_gate[0], w_ple[0], nt)

    x1p, kp, vp, cp, hp = _prompt_mixer(x_prompt, w)
    yp = _mlp(x1p.reshape(b * t, d), p_prompt[0].reshape(b * t, -1), w).reshape(b, t, d)

    att, ks, vs = _decode_attn(x_sample.reshape(nb * nt, d),
                               cache_k_win[0].reshape(nb, WINDOW, KV_W),
                               cache_v_win[0].reshape(nb, WINDOW, KV_W), w)
    to_tm = lambda a: jnp.swapaxes(a, 0, 1).reshape(-1, a.shape[-1])
    x1s, cs, hs = _decode_mixer(to_tm(x_sample), to_tm(att.reshape(nb, nt, Q_W)),
                                to_tm(state_conv[0]), state_h[0], w)
    ys = _mlp(x1s, to_tm(p_sample[0]), w)
    from_tm = lambda a, n: jnp.swapaxes(a.reshape(n, nb, a.shape[-1]), 0, 1)

    kv_shape = (1, -1, WINDOW, KV_HEADS, HEAD_DIM)
    return (yp, from_tm(ys, nt),
            kp.reshape(kv_shape), vp.reshape(kv_shape), cp[None], hp.reshape(1, b, d),
            ks.reshape(kv_shape), vs.reshape(kv_shape), from_tm(cs, CONV_W - 1)[None], hs[None])
```

```python
import functools
import math

import numpy as np
import jax
import jax.numpy as jnp
from jax import lax
from jax.experimental import pallas as pl
from jax.experimental.pallas import tpu as pltpu

F32 = jnp.float32
BF16 = jnp.bfloat16

N_HEADS = 8
KV_HEADS = 2
GROUP = N_HEADS // KV_HEADS
HEAD_DIM = 64
Q_W = N_HEADS * HEAD_DIM
KV_W = KV_HEADS * HEAD_DIM
WINDOW = 128
REL_BUCKETS = 32
REL_MAX_DIST = 128
RNN_BS = 64
CONV_W = 4
RG_C = 8.0
EPS = 1e-6
NEG_INF = -1e30
EXPM1_SERIES_BELOW = 2.0 ** -11
SQRT_FLOOR = 1e-30

LANES = 128
SUBLANES = 8
MXU_DIM = 256
VMEM_LIMIT = 56 * 1024 * 1024

TM_MIX = 256
TM_MLP = 512
FF_CHUNK = 1024
SEQ_BLOCK = 16


def _dot(a, b):
    return jnp.dot(a, b, preferred_element_type=F32)


def _dot_nt(a, b):
    return lax.dot_general(a, b, (((1,), (1,)), ((), ())), preferred_element_type=F32)


def _rms(x, g):
    ms = jnp.mean(x * x, axis=-1, keepdims=True)
    return x * lax.rsqrt(ms + EPS) * g


def _head_rms(x, ones_ref, g):
    ms = _dot((x * x).astype(BF16), ones_ref[...])
    return x * lax.rsqrt(ms + EPS) * g


def _lower_half(shape):
    return lax.broadcasted_iota(jnp.int32, shape, len(shape) - 1) < HEAD_DIM


def _scan_rows(a, b, h0):
    n, d = a.shape
    a3 = a.reshape(n // SUBLANES, SUBLANES, d)
    b3 = b.reshape(n // SUBLANES, SUBLANES, d)
    sub = lax.broadcasted_iota(jnp.int32, (1, SUBLANES, d), 1)
    step = 1
    while step < SUBLANES:
        keep = sub >= step
        b3 = jnp.where(keep, a3 * pltpu.roll(b3, step, axis=1) + b3, b3)
        a3 = jnp.where(keep, a3 * pltpu.roll(a3, step, axis=1), a3)
        step *= 2
    hs = []
    for k in range(n // SUBLANES):
        hk = a3[k] * h0 + b3[k]
        hs.append(hk)
        h0 = hk[SUBLANES - 1:SUBLANES, :]
    return jnp.concatenate(hs, axis=0)


def _shift_rows(x, prev, k):
    rolled = pltpu.roll(x, k, axis=0)
    sub = lax.broadcasted_iota(jnp.int32, prev.shape, 0)
    head = jnp.where(sub < k, pltpu.roll(prev, k, axis=0), rolled[0:SUBLANES])
    return jnp.concatenate([head, rolled[SUBLANES:]], axis=0)


def _neg_expm1_2x(x, exp_x):
    return jnp.where(x > -EXPM1_SERIES_BELOW, (-2.0 * x) * (1.0 + x), 1.0 - exp_x * exp_x)


def _rglru_gates(xc, bd_ref, ba, bx, lam):
    xcb = xc.astype(BF16)
    n_grp = xc.shape[1] // MXU_DIM
    ya, yx = [], []
    for gi in range(n_grp):
        y = _dot(xcb[:, gi * MXU_DIM:(gi + 1) * MXU_DIM], bd_ref[gi])
        ya.append(y[:, :MXU_DIM])
        yx.append(y[:, MXU_DIM:])
    r = jax.nn.sigmoid(jnp.concatenate(ya, axis=1) + ba)
    i = jax.nn.sigmoid(jnp.concatenate(yx, axis=1) + bx)
    log_a = (-RG_C * jax.nn.softplus(-lam)) * r
    a = jnp.exp(log_a)
    y = _neg_expm1_2x(log_a, a)
    bterm = (y * lax.rsqrt(jnp.maximum(y, SQRT_FLOOR))) * (i * xc)
    return a, bterm


def _merge(x, xn, h, att, wrest_ref, woa_ref, wor_ref, wout_ref):
    d = x.shape[1]
    gr = _dot(xn, wrest_ref[:, d:2 * d])
    rnn = _dot((h * jax.nn.gelu(gr)).astype(BF16), wor_ref[...])
    atto = _dot(att.astype(BF16), woa_ref[...])
    ga = _dot(xn, wrest_ref[:, 2 * d:3 * d])
    grl = _dot(xn, wrest_ref[:, 3 * d:4 * d])
    mix = (jax.nn.sigmoid(ga) * atto + jax.nn.sigmoid(grl) * rnn).astype(BF16)
    return x + _dot(mix, wout_ref[...])


def _prompt_mixer_kernel(x_ref, n1g_ref, wqkv_ref, wrest_ref, gq_ref, gk_ref, oq_ref, ok_ref,
                         bias_ref, sink_ref, cw_ref, cb_ref, bd_ref, ba_ref, bx_ref, lam_ref,
                         woa_ref, wor_ref, wout_ref,
                         x1_ref, kwin_ref, vwin_ref, cst_ref, hst_ref,
                         kbuf, vbuf, xp, hc, att):
    tm = x_ref.shape[0]
    t = pl.program_id(1)
    last = pl.num_programs(1) - 1

    @pl.when(t == 0)
    def _():
        kbuf[0:WINDOW, :] = jnp.zeros((WINDOW, KV_W), BF16)
        vbuf[0:WINDOW, :] = jnp.zeros((WINDOW, KV_W), BF16)
        xp[...] = jnp.zeros(xp.shape, F32)
        hc[...] = jnp.zeros(hc.shape, F32)

    x = x_ref[...]
    xn = _rms(x, n1g_ref[...]).astype(BF16)

    qkv = _dot(xn, wqkv_ref[...])
    qn = _head_rms(qkv[:, :Q_W], oq_ref, gq_ref[...]).astype(BF16)
    kn = _head_rms(qkv[:, Q_W:Q_W + KV_W], ok_ref, gk_ref[...])
    v = qkv[:, Q_W + KV_W:]
    kbuf[WINDOW:WINDOW + tm, :] = kn.astype(BF16)
    vbuf[WINDOW:WINDOW + tm, :] = v.astype(BF16)

    @pl.when(t == last)
    def _():
        kwin_ref[...] = kn[tm - WINDOW:, :]
        vwin_ref[...] = v[tm - WINDOW:, :]

    first = jnp.where(t == 0, 1, 0)
    lower = _lower_half((WINDOW, LANES))
    zero = jnp.zeros((WINDOW, LANES), BF16)
    for s in range(tm // WINDOW):
        rows = slice(s * WINDOW, (s + 1) * WINDOW)
        keys = kbuf[s * WINDOW:(s + 2) * WINDOW, :]
        vals = vbuf[s * WINDOW:(s + 2) * WINDOW, :]
        outs = []
        for g in range(KV_HEADS):
            qs = []
            for j in range(GROUP):
                slab = qn[rows, j * LANES:(j + 1) * LANES]
                qs.append(jnp.where(lower, slab, zero) if g == 0 else jnp.where(lower, zero, slab))
            logit = _dot_nt(jnp.concatenate(qs, axis=0), keys)
            logit = logit + (bias_ref[first, g] if s == 0 else bias_ref[0, g])
            sink = sink_ref[g]
            m = jnp.maximum(jnp.max(logit, axis=-1, keepdims=True), sink)
            e = jnp.exp(logit - m)
            den = jnp.sum(e, axis=-1, keepdims=True) + jnp.exp(sink - m)
            outs.append(_dot(e.astype(BF16), vals) / den)
        for j in range(GROUP):
            blk = slice(j * WINDOW, (j + 1) * WINDOW)
            att[rows, j * LANES:(j + 1) * LANES] = jnp.where(lower, outs[0][blk], outs[1][blk])

    kbuf[0:WINDOW, :] = kbuf[tm:tm + WINDOW, :]
    vbuf[0:WINDOW, :] = vbuf[tm:tm + WINDOW, :]

    d = x.shape[1]
    xr = _dot(xn, wrest_ref[:, 0:d])
    prev = xp[...]
    xc = cb_ref[...] + cw_ref[CONV_W - 1:CONV_W, :] * xr
    for j in range(CONV_W - 1):
        xc = xc + cw_ref[j:j + 1, :] * _shift_rows(xr, prev, CONV_W - 1 - j)
    xp[...] = xr[tm - SUBLANES:, :]

    @pl.when(t == last)
    def _():
        cst_ref[...] = xr[tm - (CONV_W - 1):, :]

    a, bterm = _rglru_gates(xc, bd_ref, ba_ref[...], bx_ref[...], lam_ref[...])
    h = _scan_rows(a, bterm, hc[...])
    hc[...] = h[tm - 1:tm, :]

    @pl.when(t == last)
    def _():
        hst_ref[...] = h[tm - 1:tm, :]

    x1_ref[...] = _merge(x, xn, h, att[...], wrest_ref, woa_ref, wor_ref, wout_ref)


def _const_spec(shape):
    nd = len(shape)
    return pl.BlockSpec(shape, lambda *_: (0,) * nd, pipeline_mode=pl.Buffered(1))


def _prompt_mixer(x, w):
    b, t, d = x.shape
    tm = TM_MIX
    consts = [w['n1g'], w['wqkv'], w['wrest'], w['gq'], w['gk'], w['oq'], w['ok'],
              w['bias_p'], w['sink_p'], w['cw'], w['cb'], w['bd'], w['ba'], w['bx'], w['lam'],
              w['woa'], w['wor'], w['wout']]
    out_shape = (
        jax.ShapeDtypeStruct((b, t, d), F32),
        jax.ShapeDtypeStruct((b, WINDOW, KV_W), F32),
        jax.ShapeDtypeStruct((b, WINDOW, KV_W), F32),
        jax.ShapeDtypeStruct((b, CONV_W - 1, d), F32),
        jax.ShapeDtypeStruct((b, 1, d), F32),
    )
    per_seq = lambda bi, ti: (bi, 0, 0)
    return pl.pallas_call(
        _prompt_mixer_kernel,
        out_shape=out_shape,
        grid=(b, t // tm),
        in_specs=[pl.BlockSpec((None, tm, d), lambda bi, ti: (bi, ti, 0))]
                 + [_const_spec(c.shape) for c in consts],
        out_specs=(
            pl.BlockSpec((None, tm, d), lambda bi, ti: (bi, ti, 0)),
            pl.BlockSpec((None, WINDOW, KV_W), per_seq),
            pl.BlockSpec((None, WINDOW, KV_W), per_seq),
            pl.BlockSpec((None, CONV_W - 1, d), per_seq),
            pl.BlockSpec((None, 1, d), per_seq),
        ),
        scratch_shapes=[
            pltpu.VMEM((WINDOW + tm, KV_W), BF16),
            pltpu.VMEM((WINDOW + tm, KV_W), BF16),
            pltpu.VMEM((SUBLANES, d), F32),
            pltpu.VMEM((1, d), F32),
            pltpu.VMEM((tm, Q_W), F32),
        ],
        compiler_params=pltpu.CompilerParams(
            dimension_semantics=("arbitrary", "arbitrary"), vmem_limit_bytes=VMEM_LIMIT),
        name="prompt_mixer",
    )(x, *consts)


def _mlp_kernel(x_ref, p_ref, n2g_ref, wup_ref, wdn_ref, pg_ref, wpg_ref, wple_ref, o_ref):
    x = x_ref[...]
    xn = _rms(x, n2g_ref[...]).astype(BF16)
    acc = x
    for c in range(wup_ref.shape[1] // FF_CHUNK):
        cols = slice(c * FF_CHUNK, (c + 1) * FF_CHUNK)
        hmid = jnp.maximum(_dot(xn, wup_ref[:, cols]), 0.0)
        acc = acc + _dot((hmid * hmid).astype(BF16), wdn_ref[cols, :])
    gate = jax.nn.sigmoid(_dot(_rms(acc, pg_ref[...]).astype(BF16), wpg_ref[...]))
    o_ref[...] = acc + gate * _dot(p_ref[...].astype(BF16), wple_ref[...])


def _mlp(x, p, w):
    m, d = x.shape
    tm = min(TM_MLP, m)
    consts = [w['n2g'], w['wup'], w['wdn'], w['pg'], w['wpg'], w['wple']]
    return pl.pallas_call(
        _mlp_kernel,
        out_shape=jax.ShapeDtypeStruct((m, d), F32),
        grid=(m // tm,),
        in_specs=[pl.BlockSpec((tm, d), lambda i: (i, 0)),
                  pl.BlockSpec((tm, p.shape[1]), lambda i: (i, 0))]
                 + [_const_spec(c.shape) for c in consts],
        out_specs=pl.BlockSpec((tm, d), lambda i: (i, 0)),
        compiler_params=pltpu.CompilerParams(
            dimension_semantics=("arbitrary",), vmem_limit_bytes=VMEM_LIMIT),
        name="mlp_ple",
    )(x, p, *consts)


def _decode_attn_kernel(x_ref, ck_ref, cv_ref, n1g_ref, wqkv_ref, gq_ref, gk_ref, oq_ref, ok_ref,
                        bias_c_ref, bias_n_ref, sink_ref,
                        att_ref, kwin_ref, vwin_ref):
    sb = ck_ref.shape[0]
    nt = x_ref.shape[0] // sb
    xn = _rms(x_ref[...], n1g_ref[...]).astype(BF16)
    qkv = _dot(xn, wqkv_ref[...])
    qn = _head_rms(qkv[:, :Q_W], oq_ref, gq_ref[...])
    kn = _head_rms(qkv[:, Q_W:Q_W + KV_W], ok_ref, gk_ref[...])
    v = qkv[:, Q_W + KV_W:]

    lower = _lower_half((nt, LANES))
    sink = sink_ref[...]
    for b in range(sb):
        rows = slice(b * nt, (b + 1) * nt)
        ck = ck_ref[b]
        cv = cv_ref[b]
        kwin_ref[b, 0:WINDOW - nt, :] = ck[nt:, :]
        vwin_ref[b, 0:WINDOW - nt, :] = cv[nt:, :]
        kwin_ref[b, WINDOW - nt:WINDOW, :] = kn[rows]
        vwin_ref[b, WINDOW - nt:WINDOW, :] = v[rows]

        qs = []
        for g in range(KV_HEADS):
            for j in range(GROUP):
                slab = qn[rows, j * LANES:(j + 1) * LANES]
                qs.append(jnp.where(lower, slab, 0.0) if g == 0 else jnp.where(lower, 0.0, slab))
        qb = jnp.concatenate(qs, axis=0)
        lc = _dot_nt(qb.astype(BF16), ck.astype(BF16)) + bias_c_ref[...]
        ln = [jnp.sum(qb * kn[b * nt + u:b * nt + u + 1, :], axis=-1, keepdims=True)
              + bias_n_ref[:, u:u + 1] for u in range(nt)]
        m = jnp.maximum(jnp.max(lc, axis=-1, keepdims=True), sink)
        for u in range(nt):
            m = jnp.maximum(m, ln[u])
        ec = jnp.exp(lc - m)
        den = jnp.sum(ec, axis=-1, keepdims=True) + jnp.exp(sink - m)
        out = _dot(ec.astype(BF16), cv.astype(BF16))
        for u in range(nt):
            en = jnp.exp(ln[u] - m)
            den = den + en
            out = out + en * v[b * nt + u:b * nt + u + 1, :]
        out = out / den
        half = GROUP * nt
        for j in range(GROUP):
            att_ref[rows, j * LANES:(j + 1) * LANES] = jnp.where(
                lower, out[j * nt:(j + 1) * nt], out[half + j * nt:half + (j + 1) * nt])


def _decode_attn(x, ck, cv, w):
    m, d = x.shape
    nb = ck.shape[0]
    nt = m // nb
    sb = SEQ_BLOCK
    consts = [w['n1g'], w['wqkv'], w['gq'], w['gk'], w['oq'], w['ok'],
              w['bias_sc'], w['bias_sn'], w['sink_s']]
    cache_spec = pl.BlockSpec((sb, WINDOW, KV_W), lambda i: (i, 0, 0))
    return pl.pallas_call(
        _decode_attn_kernel,
        out_shape=(jax.ShapeDtypeStruct((m, Q_W), F32),
                   jax.ShapeDtypeStruct(ck.shape, F32),
                   jax.ShapeDtypeStruct(cv.shape, F32)),
        grid=(nb // sb,),
        in_specs=[pl.BlockSpec((sb * nt, d), lambda i: (i, 0)), cache_spec, cache_spec]
                 + [_const_spec(c.shape) for c in consts],
        out_specs=(pl.BlockSpec((sb * nt, Q_W), lambda i: (i, 0)), cache_spec, cache_spec),
        compiler_params=pltpu.CompilerParams(
            dimension_semantics=("arbitrary",), vmem_limit_bytes=VMEM_LIMIT),
        name="decode_attn",
    )(x, ck, cv, *consts)


def _decode_mixer_kernel(x_ref, att_ref, cst_ref, h0_ref, n1g_ref, wrest_ref, cw_ref, cb_ref, bd_ref,
                         ba_ref, bx_ref, lam_ref, woa_ref, wor_ref, wout_ref,
                         x1_ref, cnew_ref, hnew_ref):
    nb = h0_ref.shape[0]
    nt = x_ref.shape[0] // nb
    d = x_ref.shape[1]
    x = x_ref[...]
    xn = _rms(x, n1g_ref[...]).astype(BF16)
    xr = _dot(xn, wrest_ref[:, 0:d])
    prev = cst_ref[...]
    slabs = [prev[k * nb:(k + 1) * nb] for k in range(CONV_W - 1)]
    slabs += [xr[k * nb:(k + 1) * nb] for k in range(nt)]
    xc = jnp.concatenate(
        [cb_ref[...] + sum(cw_ref[j:j + 1, :] * slabs[ti + j] for j in range(CONV_W))
         for ti in range(nt)], axis=0)
    cnew_ref[...] = jnp.concatenate(slabs[-(CONV_W - 1):], axis=0)

    a, bterm = _rglru_gates(xc, bd_ref, ba_ref[...], bx_ref[...], lam_ref[...])
    h = h0_ref[...]
    hs = []
    for ti in range(nt):
        h = a[ti * nb:(ti + 1) * nb] * h + bterm[ti * nb:(ti + 1) * nb]
        hs.append(h)
    hnew_ref[...] = h
    x1_ref[...] = _merge(x, xn, jnp.concatenate(hs, axis=0), att_ref[...],
                         wrest_ref, woa_ref, wor_ref, wout_ref)


def _decode_mixer(x, att, cst, h0, w):
    m, d = x.shape
    consts = [w['n1g'], w['wrest'], w['cw'], w['cb'], w['bd'], w['ba'], w['bx'], w['lam'],
              w['woa'], w['wor'], w['wout']]
    ins = [x, att, cst, h0] + consts
    return pl.pallas_call(
        _decode_mixer_kernel,
        out_shape=(jax.ShapeDtypeStruct((m, d), F32),
                   jax.ShapeDtypeStruct(cst.shape, F32),
                   jax.ShapeDtypeStruct(h0.shape, F32)),
        grid=(1,),
        in_specs=[_const_spec(c.shape) for c in ins],
        out_specs=(pl.BlockSpec((m, d), lambda i: (0, 0)),
                   pl.BlockSpec(cst.shape, lambda i: (0, 0)),
                   pl.BlockSpec(h0.shape, lambda i: (0, 0))),
        compiler_params=pltpu.CompilerParams(
            dimension_semantics=("arbitrary",), vmem_limit_bytes=VMEM_LIMIT),
        name="decode_mixer",
    )(*ins)


def _rel_bucket(dist):
    n = np.maximum(dist, 0)
    max_exact = REL_BUCKETS // 2
    nf = np.maximum(n, 1).astype(np.float32)
    large = max_exact + (np.log(nf / max_exact) / math.log(REL_MAX_DIST / max_exact)
                         * (REL_BUCKETS - max_exact)).astype(np.int32)
    large = np.minimum(large, REL_BUCKETS - 1)
    return np.where(n < max_exact, n, large)


def _bias_rows(rel_bias, dist, mask):
    tb = jnp.where(mask[:, :, None], rel_bias[_rel_bucket(dist)].astype(F32), NEG_INF)
    tq, tk = dist.shape
    return jnp.transpose(tb, (2, 0, 1)).reshape(N_HEADS * tq, tk)


def _prompt_bias(rel_bias):
    span = 3 * WINDOW
    k = np.arange(span)
    m = np.where(k < 2 * WINDOW, k, k - span)
    dist = WINDOW - m
    valid = (dist >= 0) & (dist <= WINDOW)
    v = jnp.where(valid[:, None], rel_bias[_rel_bucket(dist)].astype(F32), NEG_INF).T
    skew = jnp.tile(v, (1, WINDOW))[:, :WINDOW * (span - 1)].reshape(N_HEADS, WINDOW, span - 1)
    return skew[:, :, :2 * WINDOW]


def _block_diag(wb):
    n, bs, _ = wb.shape
    eye = jnp.eye(n, dtype=wb.dtype)
    return jnp.einsum('nde,nm->ndme', wb, eye).reshape(n * bs, n * bs)


def _head_avg(width):
    idx = np.arange(width) // HEAD_DIM
    return jnp.asarray((idx[:, None] == idx[None, :]).astype(np.float32) / HEAD_DIM, BF16)


def _prepare(rel_bias, norm1_g, w_in, q_norm_g, k_norm_g, sinks, w_o_attn, conv_w, conv_b,
             rg_wa, rg_ba, rg_wx, rg_bx, rg_lambda, w_o_rnn, w_out, norm2_g, w_up, w_down,
             ple_norm_g, w_ple_gate, w_ple, n_dec):
    d = w_in.shape[0]
    order = [g * GROUP + j for j in range(GROUP) for g in range(KV_HEADS)]
    perm = np.concatenate([np.arange(h * HEAD_DIM, (h + 1) * HEAD_DIM) for h in order])
    row = lambda a: a.reshape(1, -1).astype(F32)
    w = {}
    w['n1g'] = row(norm1_g)
    w['wqkv'] = jnp.concatenate([w_in[:, :Q_W][:, perm], w_in[:, Q_W:Q_W + 2 * KV_W]], axis=1).astype(BF16)
    w['wrest'] = w_in[:, Q_W + 2 * KV_W:].astype(BF16)
    w['gq'] = row(jnp.tile(q_norm_g, N_HEADS)) * (HEAD_DIM ** -0.5)
    w['gk'] = row(jnp.tile(k_norm_g, KV_HEADS))
    w['oq'] = _head_avg(Q_W)
    w['ok'] = _head_avg(KV_W)
    w['woa'] = w_o_attn[perm, :].astype(BF16)
    w['cw'] = conv_w.astype(F32)
    w['cb'] = row(conv_b)
    n_grp = d // MXU_DIM
    per = MXU_DIM // RNN_BS
    w['bd'] = jnp.stack([
        jnp.concatenate([_block_diag(rg_wa[gi * per:(gi + 1) * per]),
                         _block_diag(rg_wx[gi * per:(gi + 1) * per])], axis=1)
        for gi in range(n_grp)]).astype(BF16)
    w['ba'] = row(rg_ba)
    w['bx'] = row(rg_bx)
    w['lam'] = row(rg_lambda)
    w['wor'] = w_o_rnn.astype(BF16)
    w['wout'] = w_out.astype(BF16)
    w['n2g'] = row(norm2_g)
    w['wup'] = w_up.astype(BF16)
    w['wdn'] = w_down.astype(BF16)
    w['pg'] = row(ple_norm_g)
    w['wpg'] = w_ple_gate.astype(BF16)
    w['wple'] = w_ple.astype(BF16)

    bias = _prompt_bias(rel_bias)
    first = jnp.where(np.arange(2 * WINDOW) >= WINDOW, bias, NEG_INF)
    w['bias_p'] = jnp.stack([bias, first]).reshape(2, KV_HEADS, GROUP * WINDOW, 2 * WINDOW)
    sink_rows = sinks.astype(F32).reshape(KV_HEADS, GROUP, 1)
    w['sink_p'] = jnp.repeat(sink_rows, WINDOW, axis=1).reshape(KV_HEADS, GROUP * WINDOW, 1)

    ti = np.arange(n_dec)[:, None]
    dist_c = ti + WINDOW - np.arange(WINDOW)[None, :]
    dist_n = ti - np.arange(n_dec)[None, :]
    w['bias_sc'] = _bias_rows(rel_bias, dist_c, dist_c <= WINDOW)
    w['bias_sn'] = _bias_rows(rel_bias, dist_n, dist_n >= 0)
    w['sink_s'] = jnp.repeat(sink_rows, n_dec, axis=1).reshape(N_HEADS * n_dec, 1)
    return w


def kernel(x_prompt, x_sample, cache_k_win, cache_v_win, state_conv, state_h, p_prompt, p_sample, rel_bias, norm1_g, w_in, q_norm_g, k_norm_g, sinks, w_o_attn, conv_w, conv_b, rg_wa, rg_ba, rg_wx, rg_bx, rg_lambda, w_o_rnn, w_out, norm2_g, w_up, w_down, ple_norm_g, w_ple_gate, w_ple):
    depth = w_in.shape[0]
    assert depth == 1, "single-layer step"
    b, t, d = x_prompt.shape
    nb, nt, _ = x_sample.shape
    w = _prepare(rel_bias, norm1_g[0], w_in[0], q_norm_g[0], k_norm_g[0], sinks[0], w_o_attn[0],
                 conv_w[0], conv_b[0], rg_wa[0], rg_ba[0], rg_wx[0], rg_bx[0], rg_lambda[0],
                 w_o_rnn[0], w_out[0], norm2_g[0], w_up[0], w_down[0], ple_norm_g[0],
                 w_ple_gate[0], w_ple[0], nt)

    x1p, kp, vp, cp, hp = _prompt_mixer(x_prompt, w)
    yp = _mlp(x1p.reshape(b * t, d), p_prompt[0].reshape(b * t, -1), w).reshape(b, t, d)

    att, ks, vs = _decode_attn(x_sample.reshape(nb * nt, d),
                               cache_k_win[0].reshape(nb, WINDOW, KV_W),
                               cache_v_win[0].reshape(nb, WINDOW, KV_W), w)
    to_tm = lambda a: jnp.swapaxes(a, 0, 1).reshape(-1, a.shape[-1])
    x1s, cs, hs = _decode_mixer(to_tm(x_sample), to_tm(att.reshape(nb, nt, Q_W)),
                                to_tm(state_conv[0]), state_h[0], w)
    ys = _mlp(x1s, to_tm(p_sample[0]), w)
    from_tm = lambda a, n: jnp.swapaxes(a.reshape(n, nb, a.shape[-1]), 0, 1)

    kv_shape = (1, -1, WINDOW, KV_HEADS, HEAD_DIM)
    return (yp, from_tm(ys, nt),
            kp.reshape(kv_shape), vp.reshape(kv_shape), cp[None], hp.reshape(1, b, d),
            ks.reshape(kv_shape), vs.reshape(kv_shape), from_tm(cs, CONV_W - 1)[None], hs[None])
```

```python
import functools
import math

import numpy as np
import jax
import jax.numpy as jnp
from jax import lax
from jax.experimental import pallas as pl
from jax.experimental.pallas import tpu as pltpu

F32 = jnp.float32
BF16 = jnp.bfloat16

N_HEADS = 8
KV_HEADS = 2
GROUP = N_HEADS // KV_HEADS
HEAD_DIM = 64
Q_W = N_HEADS * HEAD_DIM
KV_W = KV_HEADS * HEAD_DIM
WINDOW = 128
REL_BUCKETS = 32
REL_MAX_DIST = 128
RNN_BS = 64
CONV_W = 4
RG_C = 8.0
EPS = 1e-6
NEG_INF = -1e30
EXPM1_SERIES_BELOW = 2.0 ** -11
SQRT_FLOOR = 1e-30

LANES = 128
SUBLANES = 8
MXU_DIM = 256
VMEM_LIMIT = 56 * 1024 * 1024

TM_MIX = 256
TM_MLP = 512
FF_CHUNK = 1024
SEQ_BLOCK = 16


def _dot(a, b):
    return jnp.dot(a, b, preferred_element_type=F32)


def _dot_nt(a, b):
    return lax.dot_general(a, b, (((1,), (1,)), ((), ())), preferred_element_type=F32)


def _rms(x, g):
    ms = jnp.mean(x * x, axis=-1, keepdims=True)
    return x * lax.rsqrt(ms + EPS) * g


def _head_rms(x, ones_ref, g):
    ms = _dot((x * x).astype(BF16), ones_ref[...])
    return x * lax.rsqrt(ms + EPS) * g


def _lower_half(shape):
    return lax.broadcasted_iota(jnp.int32, shape, len(shape) - 1) < HEAD_DIM


def _scan_rows(a, b, h0):
    n, d = a.shape
    a3 = a.reshape(n // SUBLANES, SUBLANES, d)
    b3 = b.reshape(n // SUBLANES, SUBLANES, d)
    sub = lax.broadcasted_iota(jnp.int32, (1, SUBLANES, d), 1)
    step = 1
    while step < SUBLANES:
        keep = sub >= step
        b3 = jnp.where(keep, a3 * pltpu.roll(b3, step, axis=1) + b3, b3)
        a3 = jnp.where(keep, a3 * pltpu.roll(a3, step, axis=1), a3)
        step *= 2
    hs = []
    for k in range(n // SUBLANES):
        hk = a3[k] * h0 + b3[k]
        hs.append(hk)
        h0 = hk[SUBLANES - 1:SUBLANES, :]
    return jnp.concatenate(hs, axis=0)


def _shift_rows(x, prev, k):
    rolled = pltpu.roll(x, k, axis=0)
    sub = lax.broadcasted_iota(jnp.int32, prev.shape, 0)
    head = jnp.where(sub < k, pltpu.roll(prev, k, axis=0), rolled[0:SUBLANES])
    return jnp.concatenate([head, rolled[SUBLANES:]], axis=0)


def _neg_expm1_2x(x, exp_x):
    return jnp.where(x > -EXPM1_SERIES_BELOW, (-2.0 * x) * (1.0 + x), 1.0 - exp_x * exp_x)


def _gate_math(ya, yx, xc, ba, bx, lam):
    r = jax.nn.sigmoid(ya + ba)
    i = jax.nn.sigmoid(yx + bx)
    log_a = (-RG_C * jax.nn.softplus(-lam)) * r
    a = jnp.exp(log_a)
    y = _neg_expm1_2x(log_a, a)
    bterm = (y * lax.rsqrt(jnp.maximum(y, SQRT_FLOOR))) * (i * xc)
    return a, bterm


def _rglru_gates(xc, bd_ref, ba, bx, lam):
    xcb = xc.astype(BF16)
    ya, yx = [], []
    for gi in range(xc.shape[1] // MXU_DIM):
        y = _dot(xcb[:, gi * MXU_DIM:(gi + 1) * MXU_DIM], bd_ref[gi])
        ya.append(y[:, :MXU_DIM])
        yx.append(y[:, MXU_DIM:])
    return _gate_math(jnp.concatenate(ya, axis=1), jnp.concatenate(yx, axis=1), xc, ba, bx, lam)


def _merge(x, xn, h, att, wrest_ref, woa_ref, wor_ref, wout_ref):
    d = x.shape[1]
    gr = _dot(xn, wrest_ref[:, d:2 * d])
    rnn = _dot((h * jax.nn.gelu(gr)).astype(BF16), wor_ref[...])
    atto = _dot(att.astype(BF16), woa_ref[...])
    ga = _dot(xn, wrest_ref[:, 2 * d:3 * d])
    grl = _dot(xn, wrest_ref[:, 3 * d:4 * d])
    mix = (jax.nn.sigmoid(ga) * atto + jax.nn.sigmoid(grl) * rnn).astype(BF16)
    return x + _dot(mix, wout_ref[...])


def _prompt_mixer_kernel(x_ref, xres_ref, n1g_ref, wqkv_ref, wrest_ref, gq_ref, gk_ref, oq_ref, ok_ref,
                         bias_ref, sink_ref, cw_ref, cb_ref, bd_ref, ba_ref, bx_ref, lam_ref,
                         woa_ref, wor_ref, wout_ref,
                         x1_ref, kwin_ref, vwin_ref, cst_ref, hst_ref,
                         proj, kbuf, vbuf, xp, hc, att, hg, *, blocks_per_seq):
    tm, d = x_ref.shape
    n = pl.program_id(0)

    @pl.when(n == 0)
    def _():
        proj[1] = jnp.zeros(proj.shape[1:], F32)
        kbuf[...] = jnp.zeros(kbuf.shape, BF16)
        vbuf[...] = jnp.zeros(vbuf.shape, BF16)
        xp[...] = jnp.zeros(xp.shape, F32)
        hc[...] = jnp.zeros(hc.shape, F32)

    wslot = lax.rem(n, 2)
    rslot = 1 - wslot
    fresh = lax.rem(n + blocks_per_seq - 1, blocks_per_seq) == 0

    xn = _rms(x_ref[...], n1g_ref[...]).astype(BF16)
    n_qkv = wqkv_ref.shape[1]

    def project(c):
        lo = c * MXU_DIM
        if lo < n_qkv:
            res = _dot(xn, wqkv_ref[:, lo:lo + MXU_DIM])
        else:
            res = _dot(xn, wrest_ref[:, lo - n_qkv:lo - n_qkv + MXU_DIM])
        proj[wslot, :, lo:lo + MXU_DIM] = res

    pending = list(range((n_qkv + wrest_ref.shape[1]) // MXU_DIM))

    def emit_stage1(count):
        for _ in range(min(count, len(pending))):
            project(pending.pop(0))

    pr = proj.at[rslot]
    emit_stage1(2)
    qn = _head_rms(pr[:, 0:Q_W], oq_ref, gq_ref[...]).astype(BF16)
    kn = _head_rms(pr[:, Q_W:Q_W + KV_W], ok_ref, gk_ref[...])
    v = pr[:, Q_W + KV_W:n_qkv]
    kbuf[0:WINDOW, :] = jnp.where(fresh, jnp.zeros((WINDOW, KV_W), BF16), kbuf[tm:tm + WINDOW, :])
    vbuf[:, 0:WINDOW] = jnp.where(fresh, jnp.zeros((KV_W, WINDOW), BF16), vbuf[:, tm:tm + WINDOW])
    kbuf[WINDOW:WINDOW + tm, :] = kn.astype(BF16)
    vbuf[:, WINDOW:WINDOW + tm] = v.T.astype(BF16)
    kwin_ref[...] = kn[tm - WINDOW:, :]
    vwin_ref[...] = v[tm - WINDOW:, :]

    first = jnp.where(fresh, 1, 0)
    lower = _lower_half((WINDOW, LANES))
    zero = jnp.zeros((WINDOW, LANES), BF16)

    def attend(s):
        rows = slice(s * WINDOW, (s + 1) * WINDOW)
        keys = kbuf[s * WINDOW:(s + 2) * WINDOW, :]
        vals_t = vbuf[:, s * WINDOW:(s + 2) * WINDOW]
        outs = []
        for g in range(KV_HEADS):
            qs = []
            for j in range(GROUP):
                slab = qn[rows, j * LANES:(j + 1) * LANES]
                qs.append(jnp.where(lower, slab, zero) if g == 0 else jnp.where(lower, zero, slab))
            logit = _dot_nt(keys, jnp.concatenate(qs, axis=0))
            logit = logit + (bias_ref[first, g] if s == 0 else bias_ref[0, g])
            sink = sink_ref[g]
            m = jnp.maximum(jnp.max(logit, axis=0, keepdims=True), sink)
            e = jnp.exp(logit - m)
            den = jnp.sum(e, axis=0, keepdims=True) + jnp.exp(sink - m)
            outs.append(_dot(vals_t, e.astype(BF16)) / den)
        for j in range(GROUP):
            blk = slice(j * WINDOW, (j + 1) * WINDOW)
            both = jnp.concatenate([outs[0][:HEAD_DIM, blk], outs[1][HEAD_DIM:, blk]], axis=0)
            att[rows, j * LANES:(j + 1) * LANES] = both.T.astype(BF16)

    def recur(gi):
        cols = slice(gi * MXU_DIM, (gi + 1) * MXU_DIM)
        xr = pr[:, n_qkv + gi * MXU_DIM:n_qkv + (gi + 1) * MXU_DIM]
        prev = jnp.where(fresh, 0.0, xp[:, cols])
        xc = cb_ref[:, cols] + cw_ref[CONV_W - 1:CONV_W, cols] * xr
        for j in range(CONV_W - 1):
            xc = xc + cw_ref[j:j + 1, cols] * _shift_rows(xr, prev, CONV_W - 1 - j)
        xp[:, cols] = xr[tm - SUBLANES:, :]
        cst_ref[:, cols] = xr[tm - (CONV_W - 1):, :]
        y = _dot(xc.astype(BF16), bd_ref[gi])
        a, bterm = _gate_math(y[:, :MXU_DIM], y[:, MXU_DIM:], xc,
                              ba_ref[:, cols], bx_ref[:, cols], lam_ref[:, cols])
        h = _scan_rows(a, bterm, jnp.where(fresh, 0.0, hc[:, cols]))
        hc[:, cols] = h[tm - 1:tm, :]
        hst_ref[:, cols] = h[tm - 1:tm, :]
        gr = pr[:, n_qkv + d + gi * MXU_DIM:n_qkv + d + (gi + 1) * MXU_DIM]
        hg[:, cols] = (h * jax.nn.gelu(gr)).astype(BF16)

    n_sub = tm // WINDOW
    n_grp = d // MXU_DIM
    units = []
    for u in range(max(n_sub, n_grp)):
        if u < n_sub:
            units.append(functools.partial(attend, u))
        if u < n_grp:
            units.append(functools.partial(recur, u))
    per_unit = -(-len(pending) // len(units))
    for unit in units:
        unit()
        emit_stage1(per_unit)
    emit_stage1(len(pending))

    rnn = _dot(hg[...], wor_ref[...])
    atto = _dot(att[...], woa_ref[...])
    ga = pr[:, n_qkv + 2 * d:n_qkv + 3 * d]
    grl = pr[:, n_qkv + 3 * d:n_qkv + 4 * d]
    mix = (jax.nn.sigmoid(ga) * atto + jax.nn.sigmoid(grl) * rnn).astype(BF16)
    x1_ref[...] = xres_ref[...] + _dot(mix, wout_ref[...])


def _const_spec(shape):
    nd = len(shape)
    return pl.BlockSpec(shape, lambda *_: (0,) * nd, pipeline_mode=pl.Buffered(1))


def _prompt_mixer(x, w):
    b, t, d = x.shape
    tm = TM_MIX
    nt = t // tm
    n_blocks = b * nt
    consts = [w['n1g'], w['wqkv'], w['wrest'], w['gq'], w['gk'], w['oq'], w['ok'],
              w['bias_p'], w['sink_p'], w['cw'], w['cb'], w['bd'], w['ba'], w['bx'], w['lam'],
              w['woa'], w['wor'], w['wout']]
    out_shape = (
        jax.ShapeDtypeStruct((b, t, d), F32),
        jax.ShapeDtypeStruct((b, WINDOW, KV_W), F32),
        jax.ShapeDtypeStruct((b, WINDOW, KV_W), F32),
        jax.ShapeDtypeStruct((b, CONV_W - 1, d), F32),
        jax.ShapeDtypeStruct((b, 1, d), F32),
    )
    def cur(n):
        i = jnp.minimum(n, n_blocks - 1)
        return (i // nt, i % nt, 0)

    def prv(n):
        i = jnp.maximum(n - 1, 0)
        return (i // nt, i % nt, 0)

    per_seq = lambda n: (jnp.maximum(n - 1, 0) // nt, 0, 0)
    n_proj = w['wqkv'].shape[1] + w['wrest'].shape[1]
    return pl.pallas_call(
        functools.partial(_prompt_mixer_kernel, blocks_per_seq=nt),
        out_shape=out_shape,
        grid=(n_blocks + 1,),
        in_specs=[pl.BlockSpec((None, tm, d), cur), pl.BlockSpec((None, tm, d), prv)]
                 + [_const_spec(c.shape) for c in consts],
        out_specs=(
            pl.BlockSpec((None, tm, d), prv),
            pl.BlockSpec((None, WINDOW, KV_W), per_seq),
            pl.BlockSpec((None, WINDOW, KV_W), per_seq),
            pl.BlockSpec((None, CONV_W - 1, d), per_seq),
            pl.BlockSpec((None, 1, d), per_seq),
        ),
        scratch_shapes=[
            pltpu.VMEM((2, tm, n_proj), F32),
            pltpu.VMEM((WINDOW + tm, KV_W), BF16),
            pltpu.VMEM((KV_W, WINDOW + tm), BF16),
            pltpu.VMEM((SUBLANES, d), F32),
            pltpu.VMEM((1, d), F32),
            pltpu.VMEM((tm, Q_W), BF16),
            pltpu.VMEM((tm, d), BF16),
        ],
        compiler_params=pltpu.CompilerParams(
            dimension_semantics=("arbitrary",), vmem_limit_bytes=VMEM_LIMIT),
        name="prompt_mixer",
    )(x, x, *consts)


def _mlp_kernel(x_ref, p_ref, n2g_ref, wup_ref, wdn_ref, pg_ref, wpg_ref, wple_ref, o_ref):
    x = x_ref[...]
    xn = _rms(x, n2g_ref[...]).astype(BF16)
    acc = x
    for c in range(wup_ref.shape[1] // FF_CHUNK):
        cols = slice(c * FF_CHUNK, (c + 1) * FF_CHUNK)
        hmid = jnp.maximum(_dot(xn, wup_ref[:, cols]), 0.0)
        acc = acc + _dot((hmid * hmid).astype(BF16), wdn_ref[cols, :])
    gate = jax.nn.sigmoid(_dot(_rms(acc, pg_ref[...]).astype(BF16), wpg_ref[...]))
    o_ref[...] = acc + gate * _dot(p_ref[...].astype(BF16), wple_ref[...])


def _mlp(x, p, w):
    m, d = x.shape
    tm = min(TM_MLP, m)
    consts = [w['n2g'], w['wup'], w['wdn'], w['pg'], w['wpg'], w['wple']]
    return pl.pallas_call(
        _mlp_kernel,
        out_shape=jax.ShapeDtypeStruct((m, d), F32),
        grid=(m // tm,),
        in_specs=[pl.BlockSpec((tm, d), lambda i: (i, 0)),
                  pl.BlockSpec((tm, p.shape[1]), lambda i: (i, 0))]
                 + [_const_spec(c.shape) for c in consts],
        out_specs=pl.BlockSpec((tm, d), lambda i: (i, 0)),
        compiler_params=pltpu.CompilerParams(
            dimension_semantics=("arbitrary",), vmem_limit_bytes=VMEM_LIMIT),
        name="mlp_ple",
    )(x, p, *consts)


def _decode_attn_kernel(x_ref, ck_ref, cv_ref, n1g_ref, wqkv_ref, gq_ref, gk_ref, oq_ref, ok_ref,
                        bias_c_ref, bias_n_ref, sink_ref,
                        att_ref, kwin_ref, vwin_ref):
    sb = ck_ref.shape[0]
    nt = x_ref.shape[0] // sb
    xn = _rms(x_ref[...], n1g_ref[...]).astype(BF16)
    qkv = _dot(xn, wqkv_ref[...])
    qn = _head_rms(qkv[:, :Q_W], oq_ref, gq_ref[...])
    kn = _head_rms(qkv[:, Q_W:Q_W + KV_W], ok_ref, gk_ref[...])
    v = qkv[:, Q_W + KV_W:]

    lower = _lower_half((nt, LANES))
    sink = sink_ref[...]
    for b in range(sb):
        rows = slice(b * nt, (b + 1) * nt)
        ck = ck_ref[b]
        cv = cv_ref[b]
        kwin_ref[b, 0:WINDOW - nt, :] = ck[nt:, :]
        vwin_ref[b, 0:WINDOW - nt, :] = cv[nt:, :]
        kwin_ref[b, WINDOW - nt:WINDOW, :] = kn[rows]
        vwin_ref[b, WINDOW - nt:WINDOW, :] = v[rows]

        qs = []
        for g in range(KV_HEADS):
            for j in range(GROUP):
                slab = qn[rows, j * LANES:(j + 1) * LANES]
                qs.append(jnp.where(lower, slab, 0.0) if g == 0 else jnp.where(lower, 0.0, slab))
        qb = jnp.concatenate(qs, axis=0)
        lc = _dot_nt(qb.astype(BF16), ck.astype(BF16)) + bias_c_ref[...]
        ln = [jnp.sum(qb * kn[b * nt + u:b * nt + u + 1, :], axis=-1, keepdims=True)
              + bias_n_ref[:, u:u + 1] for u in range(nt)]
        m = jnp.maximum(jnp.max(lc, axis=-1, keepdims=True), sink)
        for u in range(nt):
            m = jnp.maximum(m, ln[u])
        ec = jnp.exp(lc - m)
        den = jnp.sum(ec, axis=-1, keepdims=True) + jnp.exp(sink - m)
        out = _dot(ec.astype(BF16), cv.astype(BF16))
        for u in range(nt):
            en = jnp.exp(ln[u] - m)
            den = den + en
            out = out + en * v[b * nt + u:b * nt + u + 1, :]
        out = out / den
        half = GROUP * nt
        for j in range(GROUP):
            att_ref[rows, j * LANES:(j + 1) * LANES] = jnp.where(
                lower, out[j * nt:(j + 1) * nt], out[half + j * nt:half + (j + 1) * nt])


def _decode_attn(x, ck, cv, w):
    m, d = x.shape
    nb = ck.shape[0]
    nt = m // nb
    sb = SEQ_BLOCK
    consts = [w['n1g'], w['wqkv'], w['gq'], w['gk'], w['oq'], w['ok'],
              w['bias_sc'], w['bias_sn'], w['sink_s']]
    cache_spec = pl.BlockSpec((sb, WINDOW, KV_W), lambda i: (i, 0, 0))
    return pl.pallas_call(
        _decode_attn_kernel,
        out_shape=(jax.ShapeDtypeStruct((m, Q_W), F32),
                   jax.ShapeDtypeStruct(ck.shape, F32),
                   jax.ShapeDtypeStruct(cv.shape, F32)),
        grid=(nb // sb,),
        in_specs=[pl.BlockSpec((sb * nt, d), lambda i: (i, 0)), cache_spec, cache_spec]
                 + [_const_spec(c.shape) for c in consts],
        out_specs=(pl.BlockSpec((sb * nt, Q_W), lambda i: (i, 0)), cache_spec, cache_spec),
        compiler_params=pltpu.CompilerParams(
            dimension_semantics=("arbitrary",), vmem_limit_bytes=VMEM_LIMIT),
        name="decode_attn",
    )(x, ck, cv, *consts)


def _decode_mixer_kernel(x_ref, att_ref, cst_ref, h0_ref, n1g_ref, wrest_ref, cw_ref, cb_ref, bd_ref,
                         ba_ref, bx_ref, lam_ref, woa_ref, wor_ref, wout_ref,
                         x1_ref, cnew_ref, hnew_ref):
    nb = h0_ref.shape[0]
    nt = x_ref.shape[0] // nb
    d = x_ref.shape[1]
    x = x_ref[...]
    xn = _rms(x, n1g_ref[...]).astype(BF16)
    xr = _dot(xn, wrest_ref[:, 0:d])
    prev = cst_ref[...]
    slabs = [prev[k * nb:(k + 1) * nb] for k in range(CONV_W - 1)]
    slabs += [xr[k * nb:(k + 1) * nb] for k in range(nt)]
    xc = jnp.concatenate(
        [cb_ref[...] + sum(cw_ref[j:j + 1, :] * slabs[ti + j] for j in range(CONV_W))
         for ti in range(nt)], axis=0)
    cnew_ref[...] = jnp.concatenate(slabs[-(CONV_W - 1):], axis=0)

    a, bterm = _rglru_gates(xc, bd_ref, ba_ref[...], bx_ref[...], lam_ref[...])
    h = h0_ref[...]
    hs = []
    for ti in range(nt):
        h = a[ti * nb:(ti + 1) * nb] * h + bterm[ti * nb:(ti + 1) * nb]
        hs.append(h)
    hnew_ref[...] = h
    x1_ref[...] = _merge(x, xn, jnp.concatenate(hs, axis=0), att_ref[...],
                         wrest_ref, woa_ref, wor_ref, wout_ref)


def _decode_mixer(x, att, cst, h0, w):
    m, d = x.shape
    consts = [w['n1g'], w['wrest'], w['cw'], w['cb'], w['bd'], w['ba'], w['bx'], w['lam'],
              w['woa'], w['wor'], w['wout']]
    ins = [x, att, cst, h0] + consts
    return pl.pallas_call(
        _decode_mixer_kernel,
        out_shape=(jax.ShapeDtypeStruct((m, d), F32),
                   jax.ShapeDtypeStruct(cst.shape, F32),
                   jax.ShapeDtypeStruct(h0.shape, F32)),
        grid=(1,),
        in_specs=[_const_spec(c.shape) for c in ins],
        out_specs=(pl.BlockSpec((m, d), lambda i: (0, 0)),
                   pl.BlockSpec(cst.shape, lambda i: (0, 0)),
                   pl.BlockSpec(h0.shape, lambda i: (0, 0))),
        compiler_params=pltpu.CompilerParams(
            dimension_semantics=("arbitrary",), vmem_limit_bytes=VMEM_LIMIT),
        name="decode_mixer",
    )(*ins)


def _rel_bucket(dist):
    n = np.maximum(dist, 0)
    max_exact = REL_BUCKETS // 2
    nf = np.maximum(n, 1).astype(np.float32)
    large = max_exact + (np.log(nf / max_exact) / math.log(REL_MAX_DIST / max_exact)
                         * (REL_BUCKETS - max_exact)).astype(np.int32)
    large = np.minimum(large, REL_BUCKETS - 1)
    return np.where(n < max_exact, n, large)


def _bias_rows(rel_bias, dist, mask):
    tb = jnp.where(mask[:, :, None], rel_bias[_rel_bucket(dist)].astype(F32), NEG_INF)
    tq, tk = dist.shape
    return jnp.transpose(tb, (2, 0, 1)).reshape(N_HEADS * tq, tk)


def _prompt_bias(rel_bias):
    span = 3 * WINDOW
    k = np.arange(span)
    dist = WINDOW + np.where(k < WINDOW, k, k - span)
    valid = (dist >= 0) & (dist <= WINDOW)
    u = jnp.where(valid[:, None], rel_bias[_rel_bucket(dist)].astype(F32), NEG_INF).T
    n_keys = 2 * WINDOW
    skew = jnp.tile(u, (1, n_keys))[:, :n_keys * (span - 1)].reshape(N_HEADS, n_keys, span - 1)
    return skew[:, :, :WINDOW]


def _block_diag(wb):
    n, bs, _ = wb.shape
    eye = jnp.eye(n, dtype=wb.dtype)
    return jnp.einsum('nde,nm->ndme', wb, eye).reshape(n * bs, n * bs)


def _head_avg(width):
    idx = np.arange(width) // HEAD_DIM
    return jnp.asarray((idx[:, None] == idx[None, :]).astype(np.float32) / HEAD_DIM, BF16)


def _prepare(rel_bias, norm1_g, w_in, q_norm_g, k_norm_g, sinks, w_o_attn, conv_w, conv_b,
             rg_wa, rg_ba, rg_wx, rg_bx, rg_lambda, w_o_rnn, w_out, norm2_g, w_up, w_down,
             ple_norm_g, w_ple_gate, w_ple, n_dec):
    d = w_in.shape[0]
    order = [g * GROUP + j for j in range(GROUP) for g in range(KV_HEADS)]
    perm = np.concatenate([np.arange(h * HEAD_DIM, (h + 1) * HEAD_DIM) for h in order])
    row = lambda a: a.reshape(1, -1).astype(F32)
    w = {}
    w['n1g'] = row(norm1_g)
    w['wqkv'] = jnp.concatenate([w_in[:, :Q_W][:, perm], w_in[:, Q_W:Q_W + 2 * KV_W]], axis=1).astype(BF16)
    w['wrest'] = w_in[:, Q_W + 2 * KV_W:].astype(BF16)
    w['gq'] = row(jnp.tile(q_norm_g, N_HEADS)) * (HEAD_DIM ** -0.5)
    w['gk'] = row(jnp.tile(k_norm_g, KV_HEADS))
    w['oq'] = _head_avg(Q_W)
    w['ok'] = _head_avg(KV_W)
    w['woa'] = w_o_attn[perm, :].astype(BF16)
    w['cw'] = conv_w.astype(F32)
    w['cb'] = row(conv_b)
    n_grp = d // MXU_DIM
    per = MXU_DIM // RNN_BS
    w['bd'] = jnp.stack([
        jnp.concatenate([_block_diag(rg_wa[gi * per:(gi + 1) * per]),
                         _block_diag(rg_wx[gi * per:(gi + 1) * per])], axis=1)
        for gi in range(n_grp)]).astype(BF16)
    w['ba'] = row(rg_ba)
    w['bx'] = row(rg_bx)
    w['lam'] = row(rg_lambda)
    w['wor'] = w_o_rnn.astype(BF16)
    w['wout'] = w_out.astype(BF16)
    w['n2g'] = row(norm2_g)
    w['wup'] = w_up.astype(BF16)
    w['wdn'] = w_down.astype(BF16)
    w['pg'] = row(ple_norm_g)
    w['wpg'] = w_ple_gate.astype(BF16)
    w['wple'] = w_ple.astype(BF16)

    bias = _prompt_bias(rel_bias)
    first = jnp.where((np.arange(2 * WINDOW) >= WINDOW)[:, None], bias, NEG_INF)
    by_group = lambda a: jnp.concatenate(
        [a.reshape(KV_HEADS, GROUP, 2 * WINDOW, WINDOW)[:, j] for j in range(GROUP)], axis=-1)
    w['bias_p'] = jnp.stack([by_group(bias), by_group(first)])
    sink_rows = sinks.astype(F32).reshape(KV_HEADS, GROUP, 1)
    w['sink_p'] = jnp.repeat(sink_rows, WINDOW, axis=2).reshape(KV_HEADS, 1, GROUP * WINDOW)

    ti = np.arange(n_dec)[:, None]
    dist_c = ti + WINDOW - np.arange(WINDOW)[None, :]
    dist_n = ti - np.arange(n_dec)[None, :]
    w['bias_sc'] = _bias_rows(rel_bias, dist_c, dist_c <= WINDOW)
    w['bias_sn'] = _bias_rows(rel_bias, dist_n, dist_n >= 0)
    w['sink_s'] = jnp.repeat(sink_rows, n_dec, axis=1).reshape(N_HEADS * n_dec, 1)
    return w


def kernel(x_prompt, x_sample, cache_k_win, cache_v_win, state_conv, state_h, p_prompt, p_sample, rel_bias, norm1_g, w_in, q_norm_g, k_norm_g, sinks, w_o_attn, conv_w, conv_b, rg_wa, rg_ba, rg_wx, rg_bx, rg_lambda, w_o_rnn, w_out, norm2_g, w_up, w_down, ple_norm_g, w_ple_gate, w_ple):
    depth = w_in.shape[0]
    assert depth == 1, "single-layer step"
    b, t, d = x_prompt.shape
    nb, nt, _ = x_sample.shape
    w = _prepare(rel_bias, norm1_g[0], w_in[0], q_norm_g[0], k_norm_g[0], sinks[0], w_o_attn[0],
                 conv_w[0], conv_b[0], rg_wa[0], rg_ba[0], rg_wx[0], rg_bx[0], rg_lambda[0],
                 w_o_rnn[0], w_out[0], norm2_g[0], w_up[0], w_down[0], ple_norm_g[0],
                 w_ple_gate[0], w_ple[0], nt)

    x1p, kp, vp, cp, hp = _prompt_mixer(x_prompt, w)
    yp = _mlp(x1p.reshape(b * t, d), p_prompt[0].reshape(b * t, -1), w).reshape(b, t, d)

    att, ks, vs = _decode_attn(x_sample.reshape(nb * nt, d),
                               cache_k_win[0].reshape(nb, WINDOW, KV_W),
                               cache_v_win[0].reshape(nb, WINDOW, KV_W), w)
    to_tm = lambda a: jnp.swapaxes(a, 0, 1).reshape(-1, a.shape[-1])
    x1s, cs, hs = _decode_mixer(to_tm(x_sample), to_tm(att.reshape(nb, nt, Q_W)),
                                to_tm(state_conv[0]), state_h[0], w)
    ys = _mlp(x1s, to_tm(p_sample[0]), w)
    from_tm = lambda a, n: jnp.swapaxes(a.reshape(n, nb, a.shape[-1]), 0, 1)

    kv_shape = (1, -1, WINDOW, KV_HEADS, HEAD_DIM)
    return (yp, from_tm(ys, nt),
            kp.reshape(kv_shape), vp.reshape(kv_shape), cp[None], hp.reshape(1, b, d),
            ks.reshape(kv_shape), vs.reshape(kv_shape), from_tm(cs, CONV_W - 1)[None], hs[None])
```

```python
import functools
import math

import numpy as np
import jax
import jax.numpy as jnp
from jax import lax
from jax.experimental import pallas as pl
from jax.experimental.pallas import tpu as pltpu

F32 = jnp.float32
BF16 = jnp.bfloat16

N_HEADS = 8
KV_HEADS = 2
GROUP = N_HEADS // KV_HEADS
HEAD_DIM = 64
Q_W = N_HEADS * HEAD_DIM
KV_W = KV_HEADS * HEAD_DIM
WINDOW = 128
REL_BUCKETS = 32
REL_MAX_DIST = 128
RNN_BS = 64
CONV_W = 4
RG_C = 8.0
EPS = 1e-6
NEG_INF = -1e30
EXPM1_SERIES_BELOW = 2.0 ** -11
SQRT_FLOOR = 1e-30

LANES = 128
SUBLANES = 8
MXU_DIM = 256
VMEM_LIMIT = 56 * 1024 * 1024

TM_MIX = 256
TM_MLP = 512
FF_CHUNK = 1024
SEQ_BLOCK = 16


def _dot(a, b):
    return jnp.dot(a, b, preferred_element_type=F32)


def _dot_nt(a, b):
    return lax.dot_general(a, b, (((1,), (1,)), ((), ())), preferred_element_type=F32)


def _rms(x, g):
    ms = jnp.mean(x * x, axis=-1, keepdims=True)
    return x * lax.rsqrt(ms + EPS) * g


def _head_rms(x, ones_ref, g):
    ms = _dot((x * x).astype(BF16), ones_ref[...])
    return x * lax.rsqrt(ms + EPS) * g


def _lower_half(shape):
    return lax.broadcasted_iota(jnp.int32, shape, len(shape) - 1) < HEAD_DIM


def _scan_rows(a, b, h0):
    n, d = a.shape
    a3 = a.reshape(n // SUBLANES, SUBLANES, d)
    b3 = b.reshape(n // SUBLANES, SUBLANES, d)
    sub = lax.broadcasted_iota(jnp.int32, (1, SUBLANES, d), 1)
    step = 1
    while step < SUBLANES:
        keep = sub >= step
        b3 = jnp.where(keep, a3 * pltpu.roll(b3, step, axis=1) + b3, b3)
        a3 = jnp.where(keep, a3 * pltpu.roll(a3, step, axis=1), a3)
        step *= 2
    hs = []
    for k in range(n // SUBLANES):
        hk = a3[k] * h0 + b3[k]
        hs.append(hk)
        h0 = hk[SUBLANES - 1:SUBLANES, :]
    return jnp.concatenate(hs, axis=0)


def _shift_rows(x, prev, k):
    rolled = pltpu.roll(x, k, axis=0)
    sub = lax.broadcasted_iota(jnp.int32, prev.shape, 0)
    head = jnp.where(sub < k, pltpu.roll(prev, k, axis=0), rolled[0:SUBLANES])
    return jnp.concatenate([head, rolled[SUBLANES:]], axis=0)


def _neg_expm1_2x(x, exp_x):
    return jnp.where(x > -EXPM1_SERIES_BELOW, (-2.0 * x) * (1.0 + x), 1.0 - exp_x * exp_x)


def _gate_math(ya, yx, xc, ba, bx, lam):
    r = jax.nn.sigmoid(ya + ba)
    i = jax.nn.sigmoid(yx + bx)
    log_a = (-RG_C * jax.nn.softplus(-lam)) * r
    a = jnp.exp(log_a)
    y = _neg_expm1_2x(log_a, a)
    bterm = (y * lax.rsqrt(jnp.maximum(y, SQRT_FLOOR))) * (i * xc)
    return a, bterm


def _rglru_gates(xc, bd_ref, ba, bx, lam):
    xcb = xc.astype(BF16)
    ya, yx = [], []
    for gi in range(xc.shape[1] // MXU_DIM):
        y = _dot(xcb[:, gi * MXU_DIM:(gi + 1) * MXU_DIM], bd_ref[gi])
        ya.append(y[:, :MXU_DIM])
        yx.append(y[:, MXU_DIM:])
    return _gate_math(jnp.concatenate(ya, axis=1), jnp.concatenate(yx, axis=1), xc, ba, bx, lam)


def _merge(x, xn, h, att, wrest_ref, woa_ref, wor_ref, wout_ref):
    d = x.shape[1]
    gr = _dot(xn, wrest_ref[:, d:2 * d])
    rnn = _dot((h * jax.nn.gelu(gr)).astype(BF16), wor_ref[...])
    atto = _dot(att.astype(BF16), woa_ref[...])
    ga = _dot(xn, wrest_ref[:, 2 * d:3 * d])
    grl = _dot(xn, wrest_ref[:, 3 * d:4 * d])
    mix = (jax.nn.sigmoid(ga) * atto + jax.nn.sigmoid(grl) * rnn).astype(BF16)
    return x + _dot(mix, wout_ref[...])


def _prompt_mixer_kernel(x_ref, xres_ref, n1g_ref, wqkv_ref, wrest_ref, gq_ref, gk_ref, oq_ref, ok_ref,
                         bias_ref, sink_ref, cw_ref, cb_ref, bd_ref, ba_ref, bx_ref, lam_ref,
                         woa_ref, wor_ref, wout_ref,
                         x1_ref, kwin_ref, vwin_ref, cst_ref, hst_ref,
                         proj, kbuf, vbuf, xp, hc, att, hg, *, blocks_per_seq):
    tm, d = x_ref.shape
    n = pl.program_id(0)

    @pl.when(n == 0)
    def _():
        proj[1] = jnp.zeros(proj.shape[1:], F32)
        kbuf[...] = jnp.zeros(kbuf.shape, BF16)
        vbuf[...] = jnp.zeros(vbuf.shape, BF16)
        xp[...] = jnp.zeros(xp.shape, F32)
        hc[...] = jnp.zeros(hc.shape, F32)

    wslot = lax.rem(n, 2)
    rslot = 1 - wslot
    fresh = lax.rem(n + blocks_per_seq - 1, blocks_per_seq) == 0

    xn = _rms(x_ref[...], n1g_ref[...]).astype(BF16)
    n_qkv = wqkv_ref.shape[1]

    def project(c):
        lo = c * MXU_DIM
        if lo < n_qkv:
            res = _dot(xn, wqkv_ref[:, lo:lo + MXU_DIM])
        else:
            res = _dot(xn, wrest_ref[:, lo - n_qkv:lo - n_qkv + MXU_DIM])
        proj[wslot, :, lo:lo + MXU_DIM] = res

    pending = list(range((n_qkv + wrest_ref.shape[1]) // MXU_DIM))

    def emit_stage1(count):
        for _ in range(min(count, len(pending))):
            project(pending.pop(0))

    pr = proj.at[rslot]
    emit_stage1(2)
    qn = _head_rms(pr[:, 0:Q_W], oq_ref, gq_ref[...]).astype(BF16)
    kn = _head_rms(pr[:, Q_W:Q_W + KV_W], ok_ref, gk_ref[...])
    v = pr[:, Q_W + KV_W:n_qkv]
    kbuf[0:WINDOW, :] = jnp.where(fresh, jnp.zeros((WINDOW, KV_W), BF16), kbuf[tm:tm + WINDOW, :])
    vbuf[:, 0:WINDOW] = jnp.where(fresh, jnp.zeros((KV_W, WINDOW), BF16), vbuf[:, tm:tm + WINDOW])
    kbuf[WINDOW:WINDOW + tm, :] = kn.astype(BF16)
    vbuf[:, WINDOW:WINDOW + tm] = v.T.astype(BF16)
    kwin_ref[...] = kn[tm - WINDOW:, :]
    vwin_ref[...] = v[tm - WINDOW:, :]

    first = jnp.where(fresh, 1, 0)
    lower = _lower_half((WINDOW, LANES))
    zero = jnp.zeros((WINDOW, LANES), BF16)

    def attend(s):
        rows = slice(s * WINDOW, (s + 1) * WINDOW)
        keys = kbuf[s * WINDOW:(s + 2) * WINDOW, :]
        vals_t = vbuf[:, s * WINDOW:(s + 2) * WINDOW]
        outs = []
        for g in range(KV_HEADS):
            qs = []
            for j in range(GROUP):
                slab = qn[rows, j * LANES:(j + 1) * LANES]
                qs.append(jnp.where(lower, slab, zero) if g == 0 else jnp.where(lower, zero, slab))
            logit = _dot_nt(keys, jnp.concatenate(qs, axis=0))
            logit = logit + (bias_ref[first, g] if s == 0 else bias_ref[0, g])
            sink = sink_ref[g]
            m = jnp.maximum(jnp.max(logit, axis=0, keepdims=True), sink)
            e = jnp.exp(logit - m)
            den = jnp.sum(e, axis=0, keepdims=True) + jnp.exp(sink - m)
            outs.append(_dot(vals_t, e.astype(BF16)) / den)
        for j in range(GROUP):
            blk = slice(j * WINDOW, (j + 1) * WINDOW)
            both = jnp.concatenate([outs[0][:HEAD_DIM, blk], outs[1][HEAD_DIM:, blk]], axis=0)
            att[rows, j * LANES:(j + 1) * LANES] = both.T.astype(BF16)

    def recur(gi):
        cols = slice(gi * MXU_DIM, (gi + 1) * MXU_DIM)
        xr = pr[:, n_qkv + gi * MXU_DIM:n_qkv + (gi + 1) * MXU_DIM]
        prev = jnp.where(fresh, 0.0, xp[:, cols])
        xc = cb_ref[:, cols] + cw_ref[CONV_W - 1:CONV_W, cols] * xr
        for j in range(CONV_W - 1):
            xc = xc + cw_ref[j:j + 1, cols] * _shift_rows(xr, prev, CONV_W - 1 - j)
        xp[:, cols] = xr[tm - SUBLANES:, :]
        cst_ref[:, cols] = xr[tm - (CONV_W - 1):, :]
        y = _dot(xc.astype(BF16), bd_ref[gi])
        a, bterm = _gate_math(y[:, :MXU_DIM], y[:, MXU_DIM:], xc,
                              ba_ref[:, cols], bx_ref[:, cols], lam_ref[:, cols])
        h = _scan_rows(a, bterm, jnp.where(fresh, 0.0, hc[:, cols]))
        hc[:, cols] = h[tm - 1:tm, :]
        hst_ref[:, cols] = h[tm - 1:tm, :]
        gr = pr[:, n_qkv + d + gi * MXU_DIM:n_qkv + d + (gi + 1) * MXU_DIM]
        hg[:, cols] = (h * jax.nn.gelu(gr)).astype(BF16)

    n_sub = tm // WINDOW
    n_grp = d // MXU_DIM
    units = []
    for u in range(max(n_sub, n_grp)):
        if u < n_sub:
            units.append(functools.partial(attend, u))
        if u < n_grp:
            units.append(functools.partial(recur, u))
    per_unit = -(-len(pending) // len(units))
    for unit in units:
        unit()
        emit_stage1(per_unit)
    emit_stage1(len(pending))

    rnn = _dot(hg[...], wor_ref[...])
    atto = _dot(att[...], woa_ref[...])
    ga = pr[:, n_qkv + 2 * d:n_qkv + 3 * d]
    grl = pr[:, n_qkv + 3 * d:n_qkv + 4 * d]
    mix = (jax.nn.sigmoid(ga) * atto + jax.nn.sigmoid(grl) * rnn).astype(BF16)
    x1_ref[...] = xres_ref[...] + _dot(mix, wout_ref[...])


def _const_spec(shape):
    nd = len(shape)
    return pl.BlockSpec(shape, lambda *_: (0,) * nd, pipeline_mode=pl.Buffered(1))


def _prompt_mixer(x, w):
    b, t, d = x.shape
    tm = TM_MIX
    nt = t // tm
    n_blocks = b * nt
    consts = [w['n1g'], w['wqkv'], w['wrest'], w['gq'], w['gk'], w['oq'], w['ok'],
              w['bias_p'], w['sink_p'], w['cw'], w['cb'], w['bd'], w['ba'], w['bx'], w['lam'],
              w['woa'], w['wor'], w['wout']]
    out_shape = (
        jax.ShapeDtypeStruct((b, t, d), F32),
        jax.ShapeDtypeStruct((b, WINDOW, KV_W), F32),
        jax.ShapeDtypeStruct((b, WINDOW, KV_W), F32),
        jax.ShapeDtypeStruct((b, CONV_W - 1, d), F32),
        jax.ShapeDtypeStruct((b, 1, d), F32),
    )
    def cur(n):
        i = jnp.minimum(n, n_blocks - 1)
        return (i // nt, i % nt, 0)

    def prv(n):
        i = jnp.maximum(n - 1, 0)
        return (i // nt, i % nt, 0)

    per_seq = lambda n: (jnp.maximum(n - 1, 0) // nt, 0, 0)
    n_proj = w['wqkv'].shape[1] + w['wrest'].shape[1]
    return pl.pallas_call(
        functools.partial(_prompt_mixer_kernel, blocks_per_seq=nt),
        out_shape=out_shape,
        grid=(n_blocks + 1,),
        in_specs=[pl.BlockSpec((None, tm, d), cur), pl.BlockSpec((None, tm, d), prv)]
                 + [_const_spec(c.shape) for c in consts],
        out_specs=(
            pl.BlockSpec((None, tm, d), prv),
            pl.BlockSpec((None, WINDOW, KV_W), per_seq),
            pl.BlockSpec((None, WINDOW, KV_W), per_seq),
            pl.BlockSpec((None, CONV_W - 1, d), per_seq),
            pl.BlockSpec((None, 1, d), per_seq),
        ),
        scratch_shapes=[
            pltpu.VMEM((2, tm, n_proj), F32),
            pltpu.VMEM((WINDOW + tm, KV_W), BF16),
            pltpu.VMEM((KV_W, WINDOW + tm), BF16),
            pltpu.VMEM((SUBLANES, d), F32),
            pltpu.VMEM((1, d), F32),
            pltpu.VMEM((tm, Q_W), BF16),
            pltpu.VMEM((tm, d), BF16),
        ],
        compiler_params=pltpu.CompilerParams(
            dimension_semantics=("arbitrary",), vmem_limit_bytes=VMEM_LIMIT),
        name="prompt_mixer",
    )(x, x, *consts)


def _mlp_kernel(x_ref, p_ref, n2g_ref, wup_ref, wdn_ref, pg_ref, wpg_ref, wple_ref, o_ref):
    x = x_ref[...]
    xn = _rms(x, n2g_ref[...]).astype(BF16)
    acc = x
    for c in range(wup_ref.shape[1] // FF_CHUNK):
        cols = slice(c * FF_CHUNK, (c + 1) * FF_CHUNK)
        hmid = jnp.maximum(_dot(xn, wup_ref[:, cols]), 0.0)
        acc = acc + _dot((hmid * hmid).astype(BF16), wdn_ref[cols, :])
    gate = jax.nn.sigmoid(_dot(_rms(acc, pg_ref[...]).astype(BF16), wpg_ref[...]))
    o_ref[...] = acc + gate * _dot(p_ref[...].astype(BF16), wple_ref[...])


def _mlp(x, p, w):
    m, d = x.shape
    tm = min(TM_MLP, m)
    consts = [w['n2g'], w['wup'], w['wdn'], w['pg'], w['wpg'], w['wple']]
    return pl.pallas_call(
        _mlp_kernel,
        out_shape=jax.ShapeDtypeStruct((m, d), F32),
        grid=(m // tm,),
        in_specs=[pl.BlockSpec((tm, d), lambda i: (i, 0)),
                  pl.BlockSpec((tm, p.shape[1]), lambda i: (i, 0))]
                 + [_const_spec(c.shape) for c in consts],
        out_specs=pl.BlockSpec((tm, d), lambda i: (i, 0)),
        compiler_params=pltpu.CompilerParams(
            dimension_semantics=("arbitrary",), vmem_limit_bytes=VMEM_LIMIT),
        name="mlp_ple",
    )(x, p, *consts)


def _decode_attn_kernel(x_ref, ck_ref, cv_ref, n1g_ref, wqkv_ref, gq_ref, gk_ref, oq_ref, ok_ref,
                        bias_c_ref, bias_n_ref, sink_ref,
                        att_ref, kwin_ref, vwin_ref):
    sb = ck_ref.shape[0]
    nt = x_ref.shape[0] // sb
    xn = _rms(x_ref[...], n1g_ref[...]).astype(BF16)
    qkv = _dot(xn, wqkv_ref[...])
    qn = _head_rms(qkv[:, :Q_W], oq_ref, gq_ref[...])
    kn = _head_rms(qkv[:, Q_W:Q_W + KV_W], ok_ref, gk_ref[...])
    v = qkv[:, Q_W + KV_W:]

    n_rows = sb * nt
    ck = ck_ref[...]
    cv = cv_ref[...]
    for b in range(sb):
        rows = slice(b * nt, (b + 1) * nt)
        kwin_ref[b, 0:WINDOW - nt, :] = ck[b, nt:, :]
        vwin_ref[b, 0:WINDOW - nt, :] = cv[b, nt:, :]
        kwin_ref[b, WINDOW - nt:WINDOW, :] = kn[rows]
        vwin_ref[b, WINDOW - nt:WINDOW, :] = v[rows]

    qb = qn.astype(BF16)
    lower = _lower_half((n_rows, LANES))
    zero = jnp.zeros((n_rows, LANES), BF16)
    q_all = jnp.concatenate(
        [jnp.where(lower, qb[:, j * LANES:(j + 1) * LANES], zero) if g == 0
         else jnp.where(lower, zero, qb[:, j * LANES:(j + 1) * LANES])
         for g in range(KV_HEADS) for j in range(GROUP)], axis=0)
    n_col = q_all.shape[0]
    col_seq = (lax.broadcasted_iota(jnp.int32, (1, n_col), 1) // nt) % sb

    def own(big):
        out = big[0:WINDOW]
        for b in range(1, sb):
            out = jnp.where(col_seq == b, big[b * WINDOW:(b + 1) * WINDOW], out)
        return out

    st = own(_dot_nt(ck.reshape(sb * WINDOW, KV_W).astype(BF16), q_all)) + bias_c_ref[...]
    stx = _dot_nt(kn.astype(BF16), q_all) + bias_n_ref[...]
    sink = sink_ref[...]
    m = jnp.maximum(jnp.maximum(jnp.max(st, axis=0, keepdims=True),
                                jnp.max(stx, axis=0, keepdims=True)), sink)
    e = jnp.exp(st - m)
    ex = jnp.exp(stx - m)
    den = (jnp.sum(e, axis=0, keepdims=True) + jnp.sum(ex, axis=0, keepdims=True)
           + jnp.exp(sink - m))
    cv_t = jnp.concatenate([cv[b].T for b in range(sb)], axis=0).astype(BF16)
    out = own(_dot(cv_t, e.astype(BF16)))
    pad = WINDOW - n_rows
    v_t = jnp.concatenate([v, jnp.zeros((pad, KV_W), F32)], axis=0).T.astype(BF16)
    ex_pad = jnp.concatenate([ex, jnp.zeros((pad, n_col), F32)], axis=0).astype(BF16)
    out = (out + _dot(v_t, ex_pad)) / den
    half = n_col // KV_HEADS
    both_t = jnp.concatenate([out[:HEAD_DIM, :half], out[HEAD_DIM:, half:]], axis=0).T
    for j in range(GROUP):
        att_ref[:, j * LANES:(j + 1) * LANES] = both_t[j * n_rows:(j + 1) * n_rows].astype(att_ref.dtype)


def _decode_attn(x, ck, cv, w):
    m, d = x.shape
    nb = ck.shape[0]
    nt = m // nb
    sb = SEQ_BLOCK
    consts = [w['n1g'], w['wqkv'], w['gq'], w['gk'], w['oq'], w['ok'],
              w['bias_sc'], w['bias_sn'], w['sink_s']]
    cache_spec = pl.BlockSpec((sb, WINDOW, KV_W), lambda i: (i, 0, 0))
    return pl.pallas_call(
        _decode_attn_kernel,
        out_shape=(jax.ShapeDtypeStruct((m, Q_W), BF16),
                   jax.ShapeDtypeStruct(ck.shape, F32),
                   jax.ShapeDtypeStruct(cv.shape, F32)),
        grid=(nb // sb,),
        in_specs=[pl.BlockSpec((sb * nt, d), lambda i: (i, 0)), cache_spec, cache_spec]
                 + [_const_spec(c.shape) for c in consts],
        out_specs=(pl.BlockSpec((sb * nt, Q_W), lambda i: (i, 0)), cache_spec, cache_spec),
        compiler_params=pltpu.CompilerParams(
            dimension_semantics=("arbitrary",), vmem_limit_bytes=VMEM_LIMIT),
        name="decode_attn",
    )(x, ck, cv, *consts)


def _decode_mixer_kernel(x_ref, att_ref, cst_ref, h0_ref, n1g_ref, wrest_ref, cw_ref, cb_ref, bd_ref,
                         ba_ref, bx_ref, lam_ref, woa_ref, wor_ref, wout_ref,
                         x1_ref, cnew_ref, hnew_ref):
    nb = h0_ref.shape[0]
    nt = x_ref.shape[0] // nb
    d = x_ref.shape[1]
    x = x_ref[...]
    xn = _rms(x, n1g_ref[...]).astype(BF16)
    xr = _dot(xn, wrest_ref[:, 0:d])
    prev = cst_ref[...]
    slabs = [prev[k * nb:(k + 1) * nb] for k in range(CONV_W - 1)]
    slabs += [xr[k * nb:(k + 1) * nb] for k in range(nt)]
    xc = jnp.concatenate(
        [cb_ref[...] + sum(cw_ref[j:j + 1, :] * slabs[ti + j] for j in range(CONV_W))
         for ti in range(nt)], axis=0)
    cnew_ref[...] = jnp.concatenate(slabs[-(CONV_W - 1):], axis=0)

    a, bterm = _rglru_gates(xc, bd_ref, ba_ref[...], bx_ref[...], lam_ref[...])
    h = h0_ref[...]
    hs = []
    for ti in range(nt):
        h = a[ti * nb:(ti + 1) * nb] * h + bterm[ti * nb:(ti + 1) * nb]
        hs.append(h)
    hnew_ref[...] = h
    x1_ref[...] = _merge(x, xn, jnp.concatenate(hs, axis=0), att_ref[...],
                         wrest_ref, woa_ref, wor_ref, wout_ref)


def _decode_mixer(x, att, cst, h0, w):
    m, d = x.shape
    consts = [w['n1g'], w['wrest'], w['cw'], w['cb'], w['bd'], w['ba'], w['bx'], w['lam'],
              w['woa'], w['wor'], w['wout']]
    ins = [x, att, cst, h0] + consts
    return pl.pallas_call(
        _decode_mixer_kernel,
        out_shape=(jax.ShapeDtypeStruct((m, d), F32),
                   jax.ShapeDtypeStruct(cst.shape, F32),
                   jax.ShapeDtypeStruct(h0.shape, F32)),
        grid=(1,),
        in_specs=[_const_spec(c.shape) for c in ins],
        out_specs=(pl.BlockSpec((m, d), lambda i: (0, 0)),
                   pl.BlockSpec(cst.shape, lambda i: (0, 0)),
                   pl.BlockSpec(h0.shape, lambda i: (0, 0))),
        compiler_params=pltpu.CompilerParams(
            dimension_semantics=("arbitrary",), vmem_limit_bytes=VMEM_LIMIT),
        name="decode_mixer",
    )(*ins)


def _rel_bucket(dist):
    n = np.maximum(dist, 0)
    max_exact = REL_BUCKETS // 2
    nf = np.maximum(n, 1).astype(np.float32)
    large = max_exact + (np.log(nf / max_exact) / math.log(REL_MAX_DIST / max_exact)
                         * (REL_BUCKETS - max_exact)).astype(np.int32)
    large = np.minimum(large, REL_BUCKETS - 1)
    return np.where(n < max_exact, n, large)


def _bias_rows(rel_bias, dist, mask):
    tb = jnp.where(mask[:, :, None], rel_bias[_rel_bucket(dist)].astype(F32), NEG_INF)
    tq, tk = dist.shape
    return jnp.transpose(tb, (2, 0, 1)).reshape(N_HEADS * tq, tk)


def _prompt_bias(rel_bias):
    span = 3 * WINDOW
    k = np.arange(span)
    dist = WINDOW + np.where(k < WINDOW, k, k - span)
    valid = (dist >= 0) & (dist <= WINDOW)
    u = jnp.where(valid[:, None], rel_bias[_rel_bucket(dist)].astype(F32), NEG_INF).T
    n_keys = 2 * WINDOW
    skew = jnp.tile(u, (1, n_keys))[:, :n_keys * (span - 1)].reshape(N_HEADS, n_keys, span - 1)
    return skew[:, :, :WINDOW]


def _gate_blocks(rg_wa, rg_wx, n_grp):
    per = rg_wa.shape[0] // n_grp
    both = jnp.stack([rg_wa, rg_wx]).reshape(2, n_grp, per, RNN_BS, RNN_BS)
    on_diag = np.eye(per, dtype=bool)[None, None, :, None, :, None]
    blocks = jnp.where(on_diag, both[:, :, :, :, None, :], 0.0)
    blocks = jnp.transpose(blocks, (1, 2, 3, 0, 4, 5))
    return blocks.reshape(n_grp, per * RNN_BS, 2 * per * RNN_BS).astype(BF16)


def _head_avg(width):
    idx = np.arange(width) // HEAD_DIM
    return jnp.asarray((idx[:, None] == idx[None, :]).astype(np.float32) / HEAD_DIM, BF16)


def _prepare(rel_bias, norm1_g, w_in, q_norm_g, k_norm_g, sinks, w_o_attn, conv_w, conv_b,
             rg_wa, rg_ba, rg_wx, rg_bx, rg_lambda, w_o_rnn, w_out, norm2_g, w_up, w_down,
             ple_norm_g, w_ple_gate, w_ple, n_dec):
    d = w_in.shape[0]
    order = [g * GROUP + j for j in range(GROUP) for g in range(KV_HEADS)]
    perm = np.concatenate([np.arange(h * HEAD_DIM, (h + 1) * HEAD_DIM) for h in order])
    row = lambda a: a.reshape(1, -1).astype(F32)
    w = {}
    w['n1g'] = row(norm1_g)
    w['wqkv'] = jnp.concatenate([w_in[:, :Q_W][:, perm], w_in[:, Q_W:Q_W + 2 * KV_W]], axis=1).astype(BF16)
    w['wrest'] = w_in[:, Q_W + 2 * KV_W:].astype(BF16)
    w['gq'] = row(jnp.tile(q_norm_g, N_HEADS)) * (HEAD_DIM ** -0.5)
    w['gk'] = row(jnp.tile(k_norm_g, KV_HEADS))
    w['oq'] = _head_avg(Q_W)
    w['ok'] = _head_avg(KV_W)
    w['woa'] = w_o_attn[perm, :].astype(BF16)
    w['cw'] = conv_w.astype(F32)
    w['cb'] = row(conv_b)
    w['bd'] = _gate_blocks(rg_wa, rg_wx, d // MXU_DIM)
    w['ba'] = row(rg_ba)
    w['bx'] = row(rg_bx)
    w['lam'] = row(rg_lambda)
    w['wor'] = w_o_rnn.astype(BF16)
    w['wout'] = w_out.astype(BF16)
    w['n2g'] = row(norm2_g)
    w['wup'] = w_up.astype(BF16)
    w['wdn'] = w_down.astype(BF16)
    w['pg'] = row(ple_norm_g)
    w['wpg'] = w_ple_gate.astype(BF16)
    w['wple'] = w_ple.astype(BF16)

    bias = _prompt_bias(rel_bias)
    first = jnp.where((np.arange(2 * WINDOW) >= WINDOW)[:, None], bias, NEG_INF)
    by_group = lambda a: jnp.concatenate(
        [a.reshape(KV_HEADS, GROUP, 2 * WINDOW, WINDOW)[:, j] for j in range(GROUP)], axis=-1)
    w['bias_p'] = jnp.stack([by_group(bias), by_group(first)])
    sink_rows = sinks.astype(F32).reshape(KV_HEADS, GROUP, 1)
    w['sink_p'] = jnp.repeat(sink_rows, WINDOW, axis=2).reshape(KV_HEADS, 1, GROUP * WINDOW)

    sb = SEQ_BLOCK
    ti = np.arange(n_dec)[:, None]
    dist_c = ti + WINDOW - np.arange(WINDOW)[None, :]
    dist_n = ti - np.arange(n_dec)[None, :]
    per_head = lambda a: a.reshape(N_HEADS, n_dec, -1)
    bias_c = jnp.transpose(per_head(_bias_rows(rel_bias, dist_c, dist_c <= WINDOW)), (2, 0, 1))
    w['bias_sc'] = jnp.broadcast_to(bias_c[:, :, None, :], (WINDOW, N_HEADS, sb, n_dec)
                                    ).reshape(WINDOW, N_HEADS * sb * n_dec)
    bias_n = jnp.transpose(per_head(_bias_rows(rel_bias, dist_n, dist_n >= 0)), (2, 0, 1))
    same_seq = np.eye(sb, dtype=bool)[:, None, None, :, None]
    w['bias_sn'] = jnp.where(same_seq, bias_n[None, :, :, None, :], NEG_INF
                             ).reshape(sb * n_dec, N_HEADS * sb * n_dec)
    w['sink_s'] = jnp.broadcast_to(sinks.astype(F32)[:, None, None], (N_HEADS, sb, n_dec)
                                   ).reshape(1, N_HEADS * sb * n_dec)
    return w


def kernel(x_prompt, x_sample, cache_k_win, cache_v_win, state_conv, state_h, p_prompt, p_sample, rel_bias, norm1_g, w_in, q_norm_g, k_norm_g, sinks, w_o_attn, conv_w, conv_b, rg_wa, rg_ba, rg_wx, rg_bx, rg_lambda, w_o_rnn, w_out, norm2_g, w_up, w_down, ple_norm_g, w_ple_gate, w_ple):
    depth = w_in.shape[0]
    assert depth == 1, "single-layer step"
    b, t, d = x_prompt.shape
    nb, nt, _ = x_sample.shape
    w = _prepare(rel_bias, norm1_g[0], w_in[0], q_norm_g[0], k_norm_g[0], sinks[0], w_o_attn[0],
                 conv_w[0], conv_b[0], rg_wa[0], rg_ba[0], rg_wx[0], rg_bx[0], rg_lambda[0],
                 w_o_rnn[0], w_out[0], norm2_g[0], w_up[0], w_down[0], ple_norm_g[0],
                 w_ple_gate[0], w_ple[0], nt)

    x1p, kp, vp, cp, hp = _prompt_mixer(x_prompt, w)
    yp = _mlp(x1p.reshape(b * t, d), p_prompt[0].reshape(b * t, -1), w).reshape(b, t, d)

    att, ks, vs = _decode_attn(x_sample.reshape(nb * nt, d),
                               cache_k_win[0].reshape(nb, WINDOW, KV_W),
                               cache_v_win[0].reshape(nb, WINDOW, KV_W), w)
    to_tm = lambda a: jnp.swapaxes(a, 0, 1).reshape(-1, a.shape[-1])
    x1s, cs, hs = _decode_mixer(to_tm(x_sample), to_tm(att.reshape(nb, nt, Q_W)),
                                to_tm(state_conv[0]), state_h[0], w)
    ys = _mlp(x1s, to_tm(p_sample[0]), w)
    from_tm = lambda a, n: jnp.swapaxes(a.reshape(n, nb, a.shape[-1]), 0, 1)

    kv_shape = (1, -1, WINDOW, KV_HEADS, HEAD_DIM)
    return (yp, from_tm(ys, nt),
            kp.reshape(kv_shape), vp.reshape(kv_shape), cp[None], hp.reshape(1, b, d),
            ks.reshape(kv_shape), vs.reshape(kv_shape), from_tm(cs, CONV_W - 1)[None], hs[None])
```

```python
import functools
import math

import numpy as np
import jax
import jax.numpy as jnp
from jax import lax
from jax.experimental import pallas as pl
from jax.experimental.pallas import tpu as pltpu

F32 = jnp.float32
BF16 = jnp.bfloat16

N_HEADS = 8
KV_HEADS = 2
GROUP = N_HEADS // KV_HEADS
HEAD_DIM = 64
Q_W = N_HEADS * HEAD_DIM
KV_W = KV_HEADS * HEAD_DIM
REST0 = Q_W + 2 * KV_W
WINDOW = 128
REL_BUCKETS = 32
REL_MAX_DIST = 128
RNN_BS = 64
CONV_W = 4
RG_C = 8.0
EPS = 1e-6
NEG_INF = -1e30
EXPM1_SERIES_BELOW = 2.0 ** -11
SQRT_FLOOR = 1e-30

LANES = 128
SUBLANES = 8
MXU_DIM = 256
VMEM_LIMIT = 56 * 1024 * 1024

TM_MIX = 256
TM_MLP = 512
FF_CHUNK = 1024
SEQ_BLOCK = 16
MERGE_PARTS = 2


def _dot(a, b):
    return jnp.dot(a, b, preferred_element_type=F32)


def _dot_nt(a, b):
    return lax.dot_general(a, b, (((1,), (1,)), ((), ())), preferred_element_type=F32)


def _rms(x, g):
    ms = jnp.mean(x * x, axis=-1, keepdims=True)
    return x * lax.rsqrt(ms + EPS) * g


def _head_rms(x, ones_ref, g):
    ms = _dot((x * x).astype(BF16), ones_ref[...])
    return x * lax.rsqrt(ms + EPS) * g


def _lower_half(shape):
    return lax.broadcasted_iota(jnp.int32, shape, len(shape) - 1) < HEAD_DIM


def _scan_rows(a, b, h0):
    n, d = a.shape
    a3 = a.reshape(n // SUBLANES, SUBLANES, d)
    b3 = b.reshape(n // SUBLANES, SUBLANES, d)
    sub = lax.broadcasted_iota(jnp.int32, (1, SUBLANES, d), 1)
    step = 1
    while step < SUBLANES:
        keep = sub >= step
        b3 = jnp.where(keep, a3 * pltpu.roll(b3, step, axis=1) + b3, b3)
        a3 = jnp.where(keep, a3 * pltpu.roll(a3, step, axis=1), a3)
        step *= 2
    hs = []
    for k in range(n // SUBLANES):
        hk = a3[k] * h0 + b3[k]
        hs.append(hk)
        h0 = hk[SUBLANES - 1:SUBLANES, :]
    return jnp.concatenate(hs, axis=0)


def _shift_rows(x, prev, k):
    rolled = pltpu.roll(x, k, axis=0)
    sub = lax.broadcasted_iota(jnp.int32, prev.shape, 0)
    head = jnp.where(sub < k, pltpu.roll(prev, k, axis=0), rolled[0:SUBLANES])
    return jnp.concatenate([head, rolled[SUBLANES:]], axis=0)


def _neg_expm1_2x(x, exp_x):
    return jnp.where(x > -EXPM1_SERIES_BELOW, (-2.0 * x) * (1.0 + x), 1.0 - exp_x * exp_x)


def _gate_math(ya, yx, xc, ba, bx, lam):
    r = jax.nn.sigmoid(ya + ba)
    i = jax.nn.sigmoid(yx + bx)
    log_a = (-RG_C * jax.nn.softplus(-lam)) * r
    a = jnp.exp(log_a)
    y = _neg_expm1_2x(log_a, a)
    bterm = (y * lax.rsqrt(jnp.maximum(y, SQRT_FLOOR))) * (i * xc)
    return a, bterm


def _rglru_gates(xc, bd_ref, ba, bx, lam):
    xcb = xc.astype(BF16)
    ya, yx = [], []
    for gi in range(xc.shape[1] // MXU_DIM):
        y = _dot(xcb[:, gi * MXU_DIM:(gi + 1) * MXU_DIM], bd_ref[gi])
        ya.append(y[:, :MXU_DIM])
        yx.append(y[:, MXU_DIM:])
    return _gate_math(jnp.concatenate(ya, axis=1), jnp.concatenate(yx, axis=1), xc, ba, bx, lam)


def _merge(x, xn, h, att, win_ref, woa_ref, wor_ref, wout_ref):
    d = x.shape[1]
    gr = _dot(xn, win_ref[:, REST0 + d:REST0 + 2 * d])
    rnn = _dot((h * jax.nn.gelu(gr)).astype(BF16), wor_ref[...])
    atto = _dot(att.astype(BF16), woa_ref[...])
    ga = _dot(xn, win_ref[:, REST0 + 2 * d:REST0 + 3 * d])
    grl = _dot(xn, win_ref[:, REST0 + 3 * d:REST0 + 4 * d])
    mix = (jax.nn.sigmoid(ga) * atto + jax.nn.sigmoid(grl) * rnn).astype(BF16)
    return x + _dot(mix, wout_ref[...])


def _prompt_mixer_kernel(x_ref, xres_ref, n1g_ref, wqkv_ref, win_ref, gq_ref, gk_ref, oq_ref, ok_ref,
                         bias_ref, sink_ref, cw_ref, cb_ref, bd_ref, ba_ref, bx_ref, lam_ref,
                         woa_ref, wor_ref, wout_ref,
                         x1_ref, kwin_ref, vwin_ref, cst_ref, hst_ref,
                         proj, kbuf, vbuf, xp, hc, att, hg, *, blocks_per_seq):
    tm, d = x_ref.shape
    n = pl.program_id(0)

    @pl.when(n == 0)
    def _():
        proj[1] = jnp.zeros(proj.shape[1:], F32)
        kbuf[...] = jnp.zeros(kbuf.shape, BF16)
        vbuf[...] = jnp.zeros(vbuf.shape, BF16)
        xp[...] = jnp.zeros(xp.shape, F32)
        hc[...] = jnp.zeros(hc.shape, F32)

    wslot = lax.rem(n, 2)
    rslot = 1 - wslot
    fresh = lax.rem(n + blocks_per_seq - 1, blocks_per_seq) == 0

    xn = _rms(x_ref[...], n1g_ref[...]).astype(BF16)
    n_qkv = wqkv_ref.shape[1]

    def project(c):
        lo = c * MXU_DIM
        if lo < n_qkv:
            res = _dot(xn, wqkv_ref[:, lo:lo + MXU_DIM])
        else:
            res = _dot(xn, win_ref[:, lo:lo + MXU_DIM])
        proj[wslot, :, lo:lo + MXU_DIM] = res

    pending = list(range(win_ref.shape[1] // MXU_DIM))

    def emit_stage1(count):
        for _ in range(min(count, len(pending))):
            project(pending.pop(0))

    pr = proj.at[rslot]
    emit_stage1(2)
    qn = _head_rms(pr[:, 0:Q_W], oq_ref, gq_ref[...]).astype(BF16)
    kn = _head_rms(pr[:, Q_W:Q_W + KV_W], ok_ref, gk_ref[...])
    v = pr[:, Q_W + KV_W:n_qkv]
    kbuf[0:WINDOW, :] = jnp.where(fresh, jnp.zeros((WINDOW, KV_W), BF16), kbuf[tm:tm + WINDOW, :])
    vbuf[:, 0:WINDOW] = jnp.where(fresh, jnp.zeros((KV_W, WINDOW), BF16), vbuf[:, tm:tm + WINDOW])
    kbuf[WINDOW:WINDOW + tm, :] = kn.astype(BF16)
    vbuf[:, WINDOW:WINDOW + tm] = v.T.astype(BF16)
    kwin_ref[...] = kn[tm - WINDOW:, :]
    vwin_ref[...] = v[tm - WINDOW:, :]

    first = jnp.where(fresh, 1, 0)
    lower = _lower_half((WINDOW, LANES))
    zero = jnp.zeros((WINDOW, LANES), BF16)

    def attend(s):
        rows = slice(s * WINDOW, (s + 1) * WINDOW)
        keys = kbuf[s * WINDOW:(s + 2) * WINDOW, :]
        vals_t = vbuf[:, s * WINDOW:(s + 2) * WINDOW]
        outs = []
        for g in range(KV_HEADS):
            qs = []
            for j in range(GROUP):
                slab = qn[rows, j * LANES:(j + 1) * LANES]
                qs.append(jnp.where(lower, slab, zero) if g == 0 else jnp.where(lower, zero, slab))
            logit = _dot_nt(keys, jnp.concatenate(qs, axis=0))
            yield
            logit = logit + (bias_ref[first, g] if s == 0 else bias_ref[0, g])
            sink = sink_ref[g]
            m = jnp.maximum(jnp.max(logit, axis=0, keepdims=True), sink)
            e = jnp.exp(logit - m)
            den = jnp.sum(e, axis=0, keepdims=True) + jnp.exp(sink - m)
            outs.append(_dot(vals_t, e.astype(BF16)) / den)
            yield
        for j in range(GROUP):
            blk = slice(j * WINDOW, (j + 1) * WINDOW)
            both = jnp.concatenate([outs[0][:HEAD_DIM, blk], outs[1][HEAD_DIM:, blk]], axis=0)
            att[rows, j * LANES:(j + 1) * LANES] = both.T.astype(BF16)

    def recur(gi):
        cols = slice(gi * MXU_DIM, (gi + 1) * MXU_DIM)
        xr = pr[:, REST0 + gi * MXU_DIM:REST0 + (gi + 1) * MXU_DIM]
        prev = jnp.where(fresh, 0.0, xp[:, cols])
        xc = cb_ref[:, cols] + cw_ref[CONV_W - 1:CONV_W, cols] * xr
        for j in range(CONV_W - 1):
            xc = xc + cw_ref[j:j + 1, cols] * _shift_rows(xr, prev, CONV_W - 1 - j)
        xp[:, cols] = xr[tm - SUBLANES:, :]
        cst_ref[:, cols] = xr[tm - (CONV_W - 1):, :]
        y = _dot(xc.astype(BF16), bd_ref[gi])
        yield
        a, bterm = _gate_math(y[:, :MXU_DIM], y[:, MXU_DIM:], xc,
                              ba_ref[:, cols], bx_ref[:, cols], lam_ref[:, cols])
        h = _scan_rows(a, bterm, jnp.where(fresh, 0.0, hc[:, cols]))
        hc[:, cols] = h[tm - 1:tm, :]
        hst_ref[:, cols] = h[tm - 1:tm, :]
        gr = pr[:, REST0 + d + gi * MXU_DIM:REST0 + d + (gi + 1) * MXU_DIM]
        hg[:, cols] = (h * jax.nn.gelu(gr)).astype(BF16)
        yield

    def merge(part, n_parts):
        rows = slice(part * (tm // n_parts), (part + 1) * (tm // n_parts))
        rnn = _dot(hg[rows, :], wor_ref[...])
        atto = _dot(att[rows, :], woa_ref[...])
        yield
        ga = pr[rows, REST0 + 2 * d:REST0 + 3 * d]
        grl = pr[rows, REST0 + 3 * d:REST0 + 4 * d]
        mix = (jax.nn.sigmoid(ga) * atto + jax.nn.sigmoid(grl) * rnn).astype(BF16)
        x1_ref[rows, :] = xres_ref[rows, :] + _dot(mix, wout_ref[...])
        yield

    n_sub = tm // WINDOW
    n_grp = d // MXU_DIM
    pieces = []
    for u in range(max(n_sub, n_grp)):
        if u < n_sub:
            pieces.append(attend(u))
        if u < n_grp:
            pieces.append(recur(u))
    pieces += [merge(part, MERGE_PARTS) for part in range(MERGE_PARTS)]
    for piece in pieces:
        for _ in piece:
            emit_stage1(1)
    emit_stage1(len(pending))


def _const_spec(shape):
    nd = len(shape)
    return pl.BlockSpec(shape, lambda *_: (0,) * nd, pipeline_mode=pl.Buffered(1))


def _prompt_mixer(x, w):
    b, t, d = x.shape
    tm = TM_MIX
    nt = t // tm
    n_blocks = b * nt
    consts = [w['n1g'], w['wqkv'], w['win'], w['gq'], w['gk'], w['oq'], w['ok'],
              w['bias_p'], w['sink_p'], w['cw'], w['cb'], w['bd'], w['ba'], w['bx'], w['lam'],
              w['woa'], w['wor'], w['wout']]
    out_shape = (
        jax.ShapeDtypeStruct((b, t, d), F32),
        jax.ShapeDtypeStruct((b, WINDOW, KV_W), F32),
        jax.ShapeDtypeStruct((b, WINDOW, KV_W), F32),
        jax.ShapeDtypeStruct((b, CONV_W - 1, d), F32),
        jax.ShapeDtypeStruct((b, 1, d), F32),
    )
    def cur(n):
        i = jnp.minimum(n, n_blocks - 1)
        return (i // nt, i % nt, 0)

    def prv(n):
        i = jnp.maximum(n - 1, 0)
        return (i // nt, i % nt, 0)

    per_seq = lambda n: (jnp.maximum(n - 1, 0) // nt, 0, 0)
    n_proj = w['win'].shape[1]
    return pl.pallas_call(
        functools.partial(_prompt_mixer_kernel, blocks_per_seq=nt),
        out_shape=out_shape,
        grid=(n_blocks + 1,),
        in_specs=[pl.BlockSpec((None, tm, d), cur), pl.BlockSpec((None, tm, d), prv)]
                 + [_const_spec(c.shape) for c in consts],
        out_specs=(
            pl.BlockSpec((None, tm, d), prv),
            pl.BlockSpec((None, WINDOW, KV_W), per_seq),
            pl.BlockSpec((None, WINDOW, KV_W), per_seq),
            pl.BlockSpec((None, CONV_W - 1, d), per_seq),
            pl.BlockSpec((None, 1, d), per_seq),
        ),
        scratch_shapes=[
            pltpu.VMEM((2, tm, n_proj), F32),
            pltpu.VMEM((WINDOW + tm, KV_W), BF16),
            pltpu.VMEM((KV_W, WINDOW + tm), BF16),
            pltpu.VMEM((SUBLANES, d), F32),
            pltpu.VMEM((1, d), F32),
            pltpu.VMEM((tm, Q_W), BF16),
            pltpu.VMEM((tm, d), BF16),
        ],
        compiler_params=pltpu.CompilerParams(
            dimension_semantics=("arbitrary",), vmem_limit_bytes=VMEM_LIMIT),
        name="prompt_mixer",
    )(x, x, *consts)


def _mlp_kernel(x_ref, p_ref, n2g_ref, wup_ref, wdn_ref, pg_ref, wpg_ref, wple_ref, o_ref):
    x = x_ref[...]
    xn = _rms(x, n2g_ref[...]).astype(BF16)
    acc = x
    for c in range(wup_ref.shape[1] // FF_CHUNK):
        cols = slice(c * FF_CHUNK, (c + 1) * FF_CHUNK)
        hmid = jnp.maximum(_dot(xn, wup_ref[:, cols]), 0.0)
        acc = acc + _dot((hmid * hmid).astype(BF16), wdn_ref[cols, :])
    gate = jax.nn.sigmoid(_dot(_rms(acc, pg_ref[...]).astype(BF16), wpg_ref[...]))
    o_ref[...] = acc + gate * _dot(p_ref[...].astype(BF16), wple_ref[...])


def _mlp(x, p, w):
    m, d = x.shape
    tm = min(TM_MLP, m)
    consts = [w['n2g'], w['wup'], w['wdn'], w['pg'], w['wpg'], w['wple']]
    return pl.pallas_call(
        _mlp_kernel,
        out_shape=jax.ShapeDtypeStruct((m, d), F32),
        grid=(m // tm,),
        in_specs=[pl.BlockSpec((tm, d), lambda i: (i, 0)),
                  pl.BlockSpec((tm, p.shape[1]), lambda i: (i, 0))]
                 + [_const_spec(c.shape) for c in consts],
        out_specs=pl.BlockSpec((tm, d), lambda i: (i, 0)),
        compiler_params=pltpu.CompilerParams(
            dimension_semantics=("arbitrary",), vmem_limit_bytes=VMEM_LIMIT),
        name="mlp_ple",
    )(x, p, *consts)


def _decode_attn_kernel(x_ref, ck_ref, cv_ref, n1g_ref, wqkv_ref, gq_ref, gk_ref, oq_ref, ok_ref,
                        bias_c_ref, bias_n_ref, sink_ref,
                        att_ref, kwin_ref, vwin_ref):
    sb = ck_ref.shape[0]
    nt = x_ref.shape[0] // sb
    xn = _rms(x_ref[...], n1g_ref[...]).astype(BF16)
    qkv = _dot(xn, wqkv_ref[...])
    qn = _head_rms(qkv[:, :Q_W], oq_ref, gq_ref[...])
    kn = _head_rms(qkv[:, Q_W:Q_W + KV_W], ok_ref, gk_ref[...])
    v = qkv[:, Q_W + KV_W:]

    n_rows = sb * nt
    ck = ck_ref[...]
    cv = cv_ref[...]
    for b in range(sb):
        rows = slice(b * nt, (b + 1) * nt)
        kwin_ref[b, 0:WINDOW - nt, :] = ck[b, nt:, :]
        vwin_ref[b, 0:WINDOW - nt, :] = cv[b, nt:, :]
        kwin_ref[b, WINDOW - nt:WINDOW, :] = kn[rows]
        vwin_ref[b, WINDOW - nt:WINDOW, :] = v[rows]

    qb = qn.astype(BF16)
    lower = _lower_half((n_rows, LANES))
    zero = jnp.zeros((n_rows, LANES), BF16)
    q_all = jnp.concatenate(
        [jnp.where(lower, qb[:, j * LANES:(j + 1) * LANES], zero) if g == 0
         else jnp.where(lower, zero, qb[:, j * LANES:(j + 1) * LANES])
         for g in range(KV_HEADS) for j in range(GROUP)], axis=0)
    n_col = q_all.shape[0]
    col_seq = (lax.broadcasted_iota(jnp.int32, (1, n_col), 1) // nt) % sb

    def own(big):
        out = big[0:WINDOW]
        for b in range(1, sb):
            out = jnp.where(col_seq == b, big[b * WINDOW:(b + 1) * WINDOW], out)
        return out

    st = own(_dot_nt(ck.reshape(sb * WINDOW, KV_W).astype(BF16), q_all)) + bias_c_ref[...]
    stx = _dot_nt(kn.astype(BF16), q_all) + bias_n_ref[...]
    sink = sink_ref[...]
    m = jnp.maximum(jnp.maximum(jnp.max(st, axis=0, keepdims=True),
                                jnp.max(stx, axis=0, keepdims=True)), sink)
    e = jnp.exp(st - m)
    ex = jnp.exp(stx - m)
    den = (jnp.sum(e, axis=0, keepdims=True) + jnp.sum(ex, axis=0, keepdims=True)
           + jnp.exp(sink - m))
    cv_t = jnp.concatenate([cv[b].T for b in range(sb)], axis=0).astype(BF16)
    out = own(_dot(cv_t, e.astype(BF16)))
    pad = WINDOW - n_rows
    v_t = jnp.concatenate([v, jnp.zeros((pad, KV_W), F32)], axis=0).T.astype(BF16)
    ex_pad = jnp.concatenate([ex, jnp.zeros((pad, n_col), F32)], axis=0).astype(BF16)
    out = (out + _dot(v_t, ex_pad)) / den
    half = n_col // KV_HEADS
    both_t = jnp.concatenate([out[:HEAD_DIM, :half], out[HEAD_DIM:, half:]], axis=0).T
    for j in range(GROUP):
        att_ref[:, j * LANES:(j + 1) * LANES] = both_t[j * n_rows:(j + 1) * n_rows].astype(att_ref.dtype)


def _decode_attn(x, ck, cv, w):
    m, d = x.shape
    nb = ck.shape[0]
    nt = m // nb
    sb = SEQ_BLOCK
    consts = [w['n1g'], w['wqkv'], w['gq'], w['gk'], w['oq'], w['ok'],
              w['bias_sc'], w['bias_sn'], w['sink_s']]
    cache_spec = pl.BlockSpec((sb, WINDOW, KV_W), lambda i: (i, 0, 0))
    return pl.pallas_call(
        _decode_attn_kernel,
        out_shape=(jax.ShapeDtypeStruct((m, Q_W), BF16),
                   jax.ShapeDtypeStruct(ck.shape, F32),
                   jax.ShapeDtypeStruct(cv.shape, F32)),
        grid=(nb // sb,),
        in_specs=[pl.BlockSpec((sb * nt, d), lambda i: (i, 0)), cache_spec, cache_spec]
                 + [_const_spec(c.shape) for c in consts],
        out_specs=(pl.BlockSpec((sb * nt, Q_W), lambda i: (i, 0)), cache_spec, cache_spec),
        compiler_params=pltpu.CompilerParams(
            dimension_semantics=("arbitrary",), vmem_limit_bytes=VMEM_LIMIT),
        name="decode_attn",
    )(x, ck, cv, *consts)


def _decode_mixer_kernel(x_ref, att_ref, cst_ref, h0_ref, n1g_ref, win_ref, cw_ref, cb_ref, bd_ref,
                         ba_ref, bx_ref, lam_ref, woa_ref, wor_ref, wout_ref,
                         x1_ref, cnew_ref, hnew_ref):
    nb = h0_ref.shape[0]
    nt = x_ref.shape[0] // nb
    d = x_ref.shape[1]
    x = x_ref[...]
    xn = _rms(x, n1g_ref[...]).astype(BF16)
    xr = _dot(xn, win_ref[:, REST0:REST0 + d])
    prev = cst_ref[...]
    slabs = [prev[k * nb:(k + 1) * nb] for k in range(CONV_W - 1)]
    slabs += [xr[k * nb:(k + 1) * nb] for k in range(nt)]
    xc = jnp.concatenate(
        [cb_ref[...] + sum(cw_ref[j:j + 1, :] * slabs[ti + j] for j in range(CONV_W))
         for ti in range(nt)], axis=0)
    cnew_ref[...] = jnp.concatenate(slabs[-(CONV_W - 1):], axis=0)

    a, bterm = _rglru_gates(xc, bd_ref, ba_ref[...], bx_ref[...], lam_ref[...])
    h = h0_ref[...]
    hs = []
    for ti in range(nt):
        h = a[ti * nb:(ti + 1) * nb] * h + bterm[ti * nb:(ti + 1) * nb]
        hs.append(h)
    hnew_ref[...] = h
    x1_ref[...] = _merge(x, xn, jnp.concatenate(hs, axis=0), att_ref[...],
                         win_ref, woa_ref, wor_ref, wout_ref)


def _decode_mixer(x, att, cst, h0, w):
    m, d = x.shape
    consts = [w['n1g'], w['win'], w['cw'], w['cb'], w['bd'], w['ba'], w['bx'], w['lam'],
              w['woa'], w['wor'], w['wout']]
    ins = [x, att, cst, h0] + consts
    return pl.pallas_call(
        _decode_mixer_kernel,
        out_shape=(jax.ShapeDtypeStruct((m, d), F32),
                   jax.ShapeDtypeStruct(cst.shape, F32),
                   jax.ShapeDtypeStruct(h0.shape, F32)),
        grid=(1,),
        in_specs=[_const_spec(c.shape) for c in ins],
        out_specs=(pl.BlockSpec((m, d), lambda i: (0, 0)),
                   pl.BlockSpec(cst.shape, lambda i: (0, 0)),
                   pl.BlockSpec(h0.shape, lambda i: (0, 0))),
        compiler_params=pltpu.CompilerParams(
            dimension_semantics=("arbitrary",), vmem_limit_bytes=VMEM_LIMIT),
        name="decode_mixer",
    )(*ins)


def _rel_bucket(dist):
    n = np.maximum(dist, 0)
    max_exact = REL_BUCKETS // 2
    nf = np.maximum(n, 1).astype(np.float32)
    large = max_exact + (np.log(nf / max_exact) / math.log(REL_MAX_DIST / max_exact)
                         * (REL_BUCKETS - max_exact)).astype(np.int32)
    large = np.minimum(large, REL_BUCKETS - 1)
    return np.where(n < max_exact, n, large)


def _bias_rows(rel_bias, dist, mask):
    tb = jnp.where(mask[:, :, None], rel_bias[_rel_bucket(dist)].astype(F32), NEG_INF)
    tq, tk = dist.shape
    return jnp.transpose(tb, (2, 0, 1)).reshape(N_HEADS * tq, tk)


def _prompt_bias(rel_bias):
    span = 3 * WINDOW
    k = np.arange(span)
    dist = WINDOW + np.where(k < WINDOW, k, k - span)
    valid = (dist >= 0) & (dist <= WINDOW)
    u = jnp.where(valid[:, None], rel_bias[_rel_bucket(dist)].astype(F32), NEG_INF).T
    n_keys = 2 * WINDOW
    skew = jnp.tile(u, (1, n_keys))[:, :n_keys * (span - 1)].reshape(N_HEADS, n_keys, span - 1)
    return skew[:, :, :WINDOW]


def _gate_blocks(rg_wa, rg_wx, n_grp):
    per = rg_wa.shape[0] // n_grp
    both = jnp.stack([rg_wa, rg_wx]).reshape(2, n_grp, per, RNN_BS, RNN_BS)
    on_diag = np.eye(per, dtype=bool)[None, None, :, None, :, None]
    blocks = jnp.where(on_diag, both[:, :, :, :, None, :], 0.0)
    blocks = jnp.transpose(blocks, (1, 2, 3, 0, 4, 5))
    return blocks.reshape(n_grp, per * RNN_BS, 2 * per * RNN_BS).astype(BF16)


def _head_avg(width):
    idx = np.arange(width) // HEAD_DIM
    return jnp.asarray((idx[:, None] == idx[None, :]).astype(np.float32) / HEAD_DIM, BF16)


def _prepare(rel_bias, norm1_g, w_in, q_norm_g, k_norm_g, sinks, w_o_attn, conv_w, conv_b,
             rg_wa, rg_ba, rg_wx, rg_bx, rg_lambda, w_o_rnn, w_out, norm2_g, w_up, w_down,
             ple_norm_g, w_ple_gate, w_ple, n_dec):
    d = w_in.shape[0]
    order = [g * GROUP + j for j in range(GROUP) for g in range(KV_HEADS)]
    perm = np.concatenate([np.arange(h * HEAD_DIM, (h + 1) * HEAD_DIM) for h in order])
    row = lambda a: a.reshape(1, -1).astype(F32)
    w = {}
    w['n1g'] = row(norm1_g)
    w['win'] = w_in.astype(BF16)
    w['wqkv'] = jnp.concatenate([w['win'][:, :Q_W][:, perm], w['win'][:, Q_W:REST0]], axis=1)
    w['gq'] = row(jnp.tile(q_norm_g, N_HEADS)) * (HEAD_DIM ** -0.5)
    w['gk'] = row(jnp.tile(k_norm_g, KV_HEADS))
    w['oq'] = _head_avg(Q_W)
    w['ok'] = _head_avg(KV_W)
    w['woa'] = w_o_attn[perm, :].astype(BF16)
    w['cw'] = conv_w.astype(F32)
    w['cb'] = row(conv_b)
    w['bd'] = _gate_blocks(rg_wa, rg_wx, d // MXU_DIM)
    w['ba'] = row(rg_ba)
    w['bx'] = row(rg_bx)
    w['lam'] = row(rg_lambda)
    w['wor'] = w_o_rnn.astype(BF16)
    w['wout'] = w_out.astype(BF16)
    w['n2g'] = row(norm2_g)
    w['wup'] = w_up.astype(BF16)
    w['wdn'] = w_down.astype(BF16)
    w['pg'] = row(ple_norm_g)
    w['wpg'] = w_ple_gate.astype(BF16)
    w['wple'] = w_ple.astype(BF16)

    bias = _prompt_bias(rel_bias)
    first = jnp.where((np.arange(2 * WINDOW) >= WINDOW)[:, None], bias, NEG_INF)
    by_group = lambda a: jnp.concatenate(
        [a.reshape(KV_HEADS, GROUP, 2 * WINDOW, WINDOW)[:, j] for j in range(GROUP)], axis=-1)
    w['bias_p'] = jnp.stack([by_group(bias), by_group(first)])
    sink_rows = sinks.astype(F32).reshape(KV_HEADS, GROUP, 1)
    w['sink_p'] = jnp.repeat(sink_rows, WINDOW, axis=2).reshape(KV_HEADS, 1, GROUP * WINDOW)

    sb = SEQ_BLOCK
    ti = np.arange(n_dec)[:, None]
    dist_c = ti + WINDOW - np.arange(WINDOW)[None, :]
    dist_n = ti - np.arange(n_dec)[None, :]
    per_head = lambda a: a.reshape(N_HEADS, n_dec, -1)
    bias_c = jnp.transpose(per_head(_bias_rows(rel_bias, dist_c, dist_c <= WINDOW)), (2, 0, 1))
    w['bias_sc'] = jnp.broadcast_to(bias_c[:, :, None, :], (WINDOW, N_HEADS, sb, n_dec)
                                    ).reshape(WINDOW, N_HEADS * sb * n_dec)
    bias_n = jnp.transpose(per_head(_bias_rows(rel_bias, dist_n, dist_n >= 0)), (2, 0, 1))
    same_seq = np.eye(sb, dtype=bool)[:, None, None, :, None]
    w['bias_sn'] = jnp.where(same_seq, bias_n[None, :, :, None, :], NEG_INF
                             ).reshape(sb * n_dec, N_HEADS * sb * n_dec)
    w['sink_s'] = jnp.broadcast_to(sinks.astype(F32)[:, None, None], (N_HEADS, sb, n_dec)
                                   ).reshape(1, N_HEADS * sb * n_dec)
    return w


def kernel(x_prompt, x_sample, cache_k_win, cache_v_win, state_conv, state_h, p_prompt, p_sample, rel_bias, norm1_g, w_in, q_norm_g, k_norm_g, sinks, w_o_attn, conv_w, conv_b, rg_wa, rg_ba, rg_wx, rg_bx, rg_lambda, w_o_rnn, w_out, norm2_g, w_up, w_down, ple_norm_g, w_ple_gate, w_ple):
    depth = w_in.shape[0]
    assert depth == 1, "single-layer step"
    b, t, d = x_prompt.shape
    nb, nt, _ = x_sample.shape
    w = _prepare(rel_bias, norm1_g[0], w_in[0], q_norm_g[0], k_norm_g[0], sinks[0], w_o_attn[0],
                 conv_w[0], conv_b[0], rg_wa[0], rg_ba[0], rg_wx[0], rg_bx[0], rg_lambda[0],
                 w_o_rnn[0], w_out[0], norm2_g[0], w_up[0], w_down[0], ple_norm_g[0],
                 w_ple_gate[0], w_ple[0], nt)

    x1p, kp, vp, cp, hp = _prompt_mixer(x_prompt, w)
    yp = _mlp(x1p.reshape(b * t, d), p_prompt[0].reshape(b * t, -1), w).reshape(b, t, d)

    att, ks, vs = _decode_attn(x_sample.reshape(nb * nt, d),
                               cache_k_win[0].reshape(nb, WINDOW, KV_W),
                               cache_v_win[0].reshape(nb, WINDOW, KV_W), w)
    to_tm = lambda a: jnp.swapaxes(a, 0, 1).reshape(-1, a.shape[-1])
    x1s, cs, hs = _decode_mixer(to_tm(x_sample), to_tm(att.reshape(nb, nt, Q_W)),
                                to_tm(state_conv[0]), state_h[0], w)
    ys = _mlp(x1s, to_tm(p_sample[0]), w)
    from_tm = lambda a, n: jnp.swapaxes(a.reshape(n, nb, a.shape[-1]), 0, 1)

    kv_shape = (1, -1, WINDOW, KV_HEADS, HEAD_DIM)
    return (yp, from_tm(ys, nt),
            kp.reshape(kv_shape), vp.reshape(kv_shape), cp[None], hp.reshape(1, b, d),
            ks.reshape(kv_shape), vs.reshape(kv_shape), from_tm(cs, CONV_W - 1)[None], hs[None])
```

```python
import functools
import math

import numpy as np
import jax
import jax.numpy as jnp
from jax import lax
from jax.experimental import pallas as pl
from jax.experimental.pallas import tpu as pltpu

F32 = jnp.float32
BF16 = jnp.bfloat16

N_HEADS = 8
KV_HEADS = 2
GROUP = N_HEADS // KV_HEADS
HEAD_DIM = 64
Q_W = N_HEADS * HEAD_DIM
KV_W = KV_HEADS * HEAD_DIM
REST0 = Q_W + 2 * KV_W
WINDOW = 128
REL_BUCKETS = 32
REL_MAX_DIST = 128
RNN_BS = 64
CONV_W = 4
RG_C = 8.0
EPS = 1e-6
NEG_INF = -1e30
EXPM1_SERIES_BELOW = 2.0 ** -11
SQRT_FLOOR = 1e-30

LANES = 128
SUBLANES = 8
MXU_DIM = 256
VMEM_LIMIT = 56 * 1024 * 1024

TM_MIX = 256
TM_MLP = 512
FF_CHUNK = 1024
SEQ_BLOCK = 16


def _dot(a, b):
    return jnp.dot(a, b, preferred_element_type=F32)


def _dot_nt(a, b):
    return lax.dot_general(a, b, (((1,), (1,)), ((), ())), preferred_element_type=F32)


def _rms(x, g):
    ms = jnp.mean(x * x, axis=-1, keepdims=True)
    return x * lax.rsqrt(ms + EPS) * g


def _head_rms(x, ones_ref, g):
    ms = _dot((x * x).astype(BF16), ones_ref[...])
    return x * lax.rsqrt(ms + EPS) * g


def _lower_half(shape):
    return lax.broadcasted_iota(jnp.int32, shape, len(shape) - 1) < HEAD_DIM


def _scan_rows(a, b, h0):
    n, d = a.shape
    a3 = a.reshape(n // SUBLANES, SUBLANES, d)
    b3 = b.reshape(n // SUBLANES, SUBLANES, d)
    sub = lax.broadcasted_iota(jnp.int32, (1, SUBLANES, d), 1)
    step = 1
    while step < SUBLANES:
        keep = sub >= step
        b3 = jnp.where(keep, a3 * pltpu.roll(b3, step, axis=1) + b3, b3)
        a3 = jnp.where(keep, a3 * pltpu.roll(a3, step, axis=1), a3)
        step *= 2
    hs = []
    for k in range(n // SUBLANES):
        hk = a3[k] * h0 + b3[k]
        hs.append(hk)
        h0 = hk[SUBLANES - 1:SUBLANES, :]
    return jnp.concatenate(hs, axis=0)


def _shift_rows(x, prev, k):
    rolled = pltpu.roll(x, k, axis=0)
    sub = lax.broadcasted_iota(jnp.int32, prev.shape, 0)
    head = jnp.where(sub < k, pltpu.roll(prev, k, axis=0), rolled[0:SUBLANES])
    return jnp.concatenate([head, rolled[SUBLANES:]], axis=0)


def _neg_expm1_2x(x, exp_x):
    return jnp.where(x > -EXPM1_SERIES_BELOW, (-2.0 * x) * (1.0 + x), 1.0 - exp_x * exp_x)


def _gate_math(ya, yx, xc, ba, bx, lam):
    r = jax.nn.sigmoid(ya + ba)
    i = jax.nn.sigmoid(yx + bx)
    log_a = (-RG_C * jax.nn.softplus(-lam)) * r
    a = jnp.exp(log_a)
    y = _neg_expm1_2x(log_a, a)
    bterm = (y * lax.rsqrt(jnp.maximum(y, SQRT_FLOOR))) * (i * xc)
    return a, bterm


def _rglru_gates(xc, bd_ref, ba, bx, lam):
    xcb = xc.astype(BF16)
    ya, yx = [], []
    for gi in range(xc.shape[1] // MXU_DIM):
        y = _dot(xcb[:, gi * MXU_DIM:(gi + 1) * MXU_DIM], bd_ref[gi])
        ya.append(y[:, :MXU_DIM])
        yx.append(y[:, MXU_DIM:])
    return _gate_math(jnp.concatenate(ya, axis=1), jnp.concatenate(yx, axis=1), xc, ba, bx, lam)


def _prompt_mixer_kernel(x_ref, n1g_ref, wqkv_ref, win_ref, gq_ref, gk_ref, oq_ref, ok_ref,
                         bias_ref, sink_ref, cw_ref, cb_ref, bd_ref, ba_ref, bx_ref, lam_ref,
                         hg, att, gates, kwin_ref, vwin_ref, cst_ref, hst_ref,
                         proj, kbuf, vbuf, xp, hc, *, blocks_per_seq):
    tm, d = x_ref.shape
    n = pl.program_id(0)

    @pl.when(n == 0)
    def _():
        proj[1] = jnp.zeros(proj.shape[1:], F32)
        kbuf[...] = jnp.zeros(kbuf.shape, BF16)
        vbuf[...] = jnp.zeros(vbuf.shape, BF16)
        xp[...] = jnp.zeros(xp.shape, F32)
        hc[...] = jnp.zeros(hc.shape, F32)

    wslot = lax.rem(n, 2)
    rslot = 1 - wslot
    fresh = lax.rem(n + blocks_per_seq - 1, blocks_per_seq) == 0

    xn = _rms(x_ref[...], n1g_ref[...]).astype(BF16)
    n_qkv = wqkv_ref.shape[1]

    def project(c):
        lo = c * MXU_DIM
        if lo < n_qkv:
            res = _dot(xn, wqkv_ref[:, lo:lo + MXU_DIM])
        else:
            res = _dot(xn, win_ref[:, lo:lo + MXU_DIM])
        proj[wslot, :, lo:lo + MXU_DIM] = res

    pending = list(range(win_ref.shape[1] // MXU_DIM))

    def emit_stage1(count):
        for _ in range(min(count, len(pending))):
            project(pending.pop(0))

    pr = proj.at[rslot]
    emit_stage1(2)
    qn = _head_rms(pr[:, 0:Q_W], oq_ref, gq_ref[...]).astype(BF16)
    kn = _head_rms(pr[:, Q_W:Q_W + KV_W], ok_ref, gk_ref[...])
    v = pr[:, Q_W + KV_W:n_qkv]
    kbuf[0:WINDOW, :] = jnp.where(fresh, jnp.zeros((WINDOW, KV_W), BF16), kbuf[tm:tm + WINDOW, :])
    vbuf[:, 0:WINDOW] = jnp.where(fresh, jnp.zeros((KV_W, WINDOW), BF16), vbuf[:, tm:tm + WINDOW])
    kbuf[WINDOW:WINDOW + tm, :] = kn.astype(BF16)
    vbuf[:, WINDOW:WINDOW + tm] = v.T.astype(BF16)
    kwin_ref[...] = kn[tm - WINDOW:, :]
    vwin_ref[...] = v[tm - WINDOW:, :]

    first = jnp.where(fresh, 1, 0)
    lower = _lower_half((WINDOW, LANES))
    zero = jnp.zeros((WINDOW, LANES), BF16)

    def attend(s):
        rows = slice(s * WINDOW, (s + 1) * WINDOW)
        keys = kbuf[s * WINDOW:(s + 2) * WINDOW, :]
        vals_t = vbuf[:, s * WINDOW:(s + 2) * WINDOW]
        outs = []
        for g in range(KV_HEADS):
            qs = []
            for j in range(GROUP):
                slab = qn[rows, j * LANES:(j + 1) * LANES]
                qs.append(jnp.where(lower, slab, zero) if g == 0 else jnp.where(lower, zero, slab))
            logit = _dot_nt(keys, jnp.concatenate(qs, axis=0))
            yield
            logit = logit + (bias_ref[first, g] if s == 0 else bias_ref[0, g])
            sink = sink_ref[g]
            m = jnp.maximum(jnp.max(logit, axis=0, keepdims=True), sink)
            e = jnp.exp(logit - m)
            den = jnp.sum(e, axis=0, keepdims=True) + jnp.exp(sink - m)
            outs.append(_dot(vals_t, e.astype(BF16)) / den)
            yield
        for j in range(GROUP):
            blk = slice(j * WINDOW, (j + 1) * WINDOW)
            both = jnp.concatenate([outs[0][:HEAD_DIM, blk], outs[1][HEAD_DIM:, blk]], axis=0)
            att[rows, j * LANES:(j + 1) * LANES] = both.T.astype(BF16)

    def recur(gi):
        cols = slice(gi * MXU_DIM, (gi + 1) * MXU_DIM)
        xr = pr[:, REST0 + gi * MXU_DIM:REST0 + (gi + 1) * MXU_DIM]
        prev = jnp.where(fresh, 0.0, xp[:, cols])
        xc = cb_ref[:, cols] + cw_ref[CONV_W - 1:CONV_W, cols] * xr
        for j in range(CONV_W - 1):
            xc = xc + cw_ref[j:j + 1, cols] * _shift_rows(xr, prev, CONV_W - 1 - j)
        xp[:, cols] = xr[tm - SUBLANES:, :]
        cst_ref[:, cols] = xr[tm - (CONV_W - 1):, :]
        y = _dot(xc.astype(BF16), bd_ref[gi])
        yield
        a, bterm = _gate_math(y[:, :MXU_DIM], y[:, MXU_DIM:], xc,
                              ba_ref[:, cols], bx_ref[:, cols], lam_ref[:, cols])
        h = _scan_rows(a, bterm, jnp.where(fresh, 0.0, hc[:, cols]))
        hc[:, cols] = h[tm - 1:tm, :]
        hst_ref[:, cols] = h[tm - 1:tm, :]
        gr = pr[:, REST0 + d + gi * MXU_DIM:REST0 + d + (gi + 1) * MXU_DIM]
        hg[:, cols] = (h * jax.nn.gelu(gr)).astype(BF16)
        yield

    def hand_off(gi):
        cols = slice(gi * MXU_DIM, (gi + 1) * MXU_DIM)
        gates[:, cols] = pr[:, REST0 + 2 * d + gi * MXU_DIM:REST0 + 2 * d + (gi + 1) * MXU_DIM].astype(BF16)
        yield

    n_sub = tm // WINDOW
    n_grp = d // MXU_DIM
    pieces = []
    for u in range(max(n_sub, n_grp)):
        if u < n_sub:
            pieces.append(attend(u))
        if u < n_grp:
            pieces.append(recur(u))
    pieces += [hand_off(gi) for gi in range(2 * n_grp)]
    for piece in pieces:
        for _ in piece:
            emit_stage1(1)
    emit_stage1(len(pending))


def _const_spec(shape):
    nd = len(shape)
    return pl.BlockSpec(shape, lambda *_: (0,) * nd, pipeline_mode=pl.Buffered(1))


def _prompt_mixer(x, w):
    b, t, d = x.shape
    tm = TM_MIX
    nt = t // tm
    n_blocks = b * nt
    consts = [w['n1g'], w['wqkv'], w['win'], w['gq'], w['gk'], w['oq'], w['ok'],
              w['bias_p'], w['sink_p'], w['cw'], w['cb'], w['bd'], w['ba'], w['bx'], w['lam']]
    out_shape = (
        jax.ShapeDtypeStruct((b, t, d), BF16),
        jax.ShapeDtypeStruct((b, t, Q_W), BF16),
        jax.ShapeDtypeStruct((b, t, 2 * d), BF16),
        jax.ShapeDtypeStruct((b, WINDOW, KV_W), F32),
        jax.ShapeDtypeStruct((b, WINDOW, KV_W), F32),
        jax.ShapeDtypeStruct((b, CONV_W - 1, d), F32),
        jax.ShapeDtypeStruct((b, 1, d), F32),
    )
    def cur(n):
        i = jnp.minimum(n, n_blocks - 1)
        return (i // nt, i % nt, 0)

    def prv(n):
        i = jnp.maximum(n - 1, 0)
        return (i // nt, i % nt, 0)

    per_seq = lambda n: (jnp.maximum(n - 1, 0) // nt, 0, 0)
    n_proj = w['win'].shape[1]
    return pl.pallas_call(
        functools.partial(_prompt_mixer_kernel, blocks_per_seq=nt),
        out_shape=out_shape,
        grid=(n_blocks + 1,),
        in_specs=[pl.BlockSpec((None, tm, d), cur)]
                 + [_const_spec(c.shape) for c in consts],
        out_specs=(
            pl.BlockSpec((None, tm, d), prv),
            pl.BlockSpec((None, tm, Q_W), prv),
            pl.BlockSpec((None, tm, 2 * d), prv),
            pl.BlockSpec((None, WINDOW, KV_W), per_seq),
            pl.BlockSpec((None, WINDOW, KV_W), per_seq),
            pl.BlockSpec((None, CONV_W - 1, d), per_seq),
            pl.BlockSpec((None, 1, d), per_seq),
        ),
        scratch_shapes=[
            pltpu.VMEM((2, tm, n_proj), F32),
            pltpu.VMEM((WINDOW + tm, KV_W), BF16),
            pltpu.VMEM((KV_W, WINDOW + tm), BF16),
            pltpu.VMEM((SUBLANES, d), F32),
            pltpu.VMEM((1, d), F32),
        ],
        compiler_params=pltpu.CompilerParams(
            dimension_semantics=("arbitrary",), vmem_limit_bytes=VMEM_LIMIT),
        name="prompt_mixer",
    )(x, *consts)


def _channel_kernel(x_ref, hg_ref, att_ref, gates_ref, p_ref, woa_ref, wor_ref, wout_ref,
                    n2g_ref, wup_ref, wdn_ref, pg_ref, wpg_ref, wple_ref, o_ref):
    d = x_ref.shape[1]
    rnn = _dot(hg_ref[...], wor_ref[...])
    atto = _dot(att_ref[...], woa_ref[...])
    gates = gates_ref[...].astype(F32)
    mix = (jax.nn.sigmoid(gates[:, :d]) * atto + jax.nn.sigmoid(gates[:, d:]) * rnn).astype(BF16)
    x = x_ref[...] + _dot(mix, wout_ref[...])
    xn = _rms(x, n2g_ref[...]).astype(BF16)
    acc = x
    for c in range(wup_ref.shape[1] // FF_CHUNK):
        cols = slice(c * FF_CHUNK, (c + 1) * FF_CHUNK)
        hmid = jnp.maximum(_dot(xn, wup_ref[:, cols]), 0.0)
        acc = acc + _dot((hmid * hmid).astype(BF16), wdn_ref[cols, :])
    gate = jax.nn.sigmoid(_dot(_rms(acc, pg_ref[...]).astype(BF16), wpg_ref[...]))
    o_ref[...] = acc + gate * _dot(p_ref[...].astype(BF16), wple_ref[...])


def _channel(x, hg, att, gates, p, w):
    m, d = x.shape
    tm = min(TM_MLP, m)
    consts = [w['woa'], w['wor'], w['wout'], w['n2g'], w['wup'], w['wdn'], w['pg'], w['wpg'], w['wple']]
    rows = lambda a: pl.BlockSpec((tm, a.shape[1]), lambda i: (i, 0))
    return pl.pallas_call(
        _channel_kernel,
        out_shape=jax.ShapeDtypeStruct((m, d), F32),
        grid=(m // tm,),
        in_specs=[rows(a) for a in (x, hg, att, gates, p)] + [_const_spec(c.shape) for c in consts],
        out_specs=rows(x),
        compiler_params=pltpu.CompilerParams(
            dimension_semantics=("arbitrary",), vmem_limit_bytes=VMEM_LIMIT),
        name="merge_mlp_ple",
    )(x, hg, att, gates, p, *consts)


def _decode_attn_kernel(x_ref, ck_ref, cv_ref, n1g_ref, wqkv_ref, gq_ref, gk_ref, oq_ref, ok_ref,
                        bias_c_ref, bias_n_ref, sink_ref,
                        att_ref, kwin_ref, vwin_ref):
    sb = ck_ref.shape[0]
    nt = x_ref.shape[0] // sb
    xn = _rms(x_ref[...], n1g_ref[...]).astype(BF16)
    qkv = _dot(xn, wqkv_ref[...])
    qn = _head_rms(qkv[:, :Q_W], oq_ref, gq_ref[...])
    kn = _head_rms(qkv[:, Q_W:Q_W + KV_W], ok_ref, gk_ref[...])
    v = qkv[:, Q_W + KV_W:]

    n_rows = sb * nt
    ck = ck_ref[...]
    cv = cv_ref[...]
    for b in range(sb):
        rows = slice(b * nt, (b + 1) * nt)
        kwin_ref[b, 0:WINDOW - nt, :] = ck[b, nt:, :]
        vwin_ref[b, 0:WINDOW - nt, :] = cv[b, nt:, :]
        kwin_ref[b, WINDOW - nt:WINDOW, :] = kn[rows]
        vwin_ref[b, WINDOW - nt:WINDOW, :] = v[rows]

    qb = qn.astype(BF16)
    lower = _lower_half((n_rows, LANES))
    zero = jnp.zeros((n_rows, LANES), BF16)
    q_all = jnp.concatenate(
        [jnp.where(lower, qb[:, j * LANES:(j + 1) * LANES], zero) if g == 0
         else jnp.where(lower, zero, qb[:, j * LANES:(j + 1) * LANES])
         for g in range(KV_HEADS) for j in range(GROUP)], axis=0)
    n_col = q_all.shape[0]
    col_seq = (lax.broadcasted_iota(jnp.int32, (1, n_col), 1) // nt) % sb

    def own(big):
        out = big[0:WINDOW]
        for b in range(1, sb):
            out = jnp.where(col_seq == b, big[b * WINDOW:(b + 1) * WINDOW], out)
        return out

    st = own(_dot_nt(ck.reshape(sb * WINDOW, KV_W).astype(BF16), q_all)) + bias_c_ref[...]
    stx = _dot_nt(kn.astype(BF16), q_all) + bias_n_ref[...]
    sink = sink_ref[...]
    m = jnp.maximum(jnp.maximum(jnp.max(st, axis=0, keepdims=True),
                                jnp.max(stx, axis=0, keepdims=True)), sink)
    e = jnp.exp(st - m)
    ex = jnp.exp(stx - m)
    den = (jnp.sum(e, axis=0, keepdims=True) + jnp.sum(ex, axis=0, keepdims=True)
           + jnp.exp(sink - m))
    cv_t = jnp.concatenate([cv[b].T for b in range(sb)], axis=0).astype(BF16)
    out = own(_dot(cv_t, e.astype(BF16)))
    pad = WINDOW - n_rows
    v_t = jnp.concatenate([v, jnp.zeros((pad, KV_W), F32)], axis=0).T.astype(BF16)
    ex_pad = jnp.concatenate([ex, jnp.zeros((pad, n_col), F32)], axis=0).astype(BF16)
    out = (out + _dot(v_t, ex_pad)) / den
    half = n_col // KV_HEADS
    both_t = jnp.concatenate([out[:HEAD_DIM, :half], out[HEAD_DIM:, half:]], axis=0).T
    for j in range(GROUP):
        att_ref[:, j * LANES:(j + 1) * LANES] = both_t[j * n_rows:(j + 1) * n_rows].astype(att_ref.dtype)


def _decode_attn(x, ck, cv, w):
    m, d = x.shape
    nb = ck.shape[0]
    nt = m // nb
    sb = SEQ_BLOCK
    consts = [w['n1g'], w['wqkv'], w['gq'], w['gk'], w['oq'], w['ok'],
              w['bias_sc'], w['bias_sn'], w['sink_s']]
    cache_spec = pl.BlockSpec((sb, WINDOW, KV_W), lambda i: (i, 0, 0))
    return pl.pallas_call(
        _decode_attn_kernel,
        out_shape=(jax.ShapeDtypeStruct((m, Q_W), BF16),
                   jax.ShapeDtypeStruct(ck.shape, F32),
                   jax.ShapeDtypeStruct(cv.shape, F32)),
        grid=(nb // sb,),
        in_specs=[pl.BlockSpec((sb * nt, d), lambda i: (i, 0)), cache_spec, cache_spec]
                 + [_const_spec(c.shape) for c in consts],
        out_specs=(pl.BlockSpec((sb * nt, Q_W), lambda i: (i, 0)), cache_spec, cache_spec),
        compiler_params=pltpu.CompilerParams(
            dimension_semantics=("arbitrary",), vmem_limit_bytes=VMEM_LIMIT),
        name="decode_attn",
    )(x, ck, cv, *consts)


def _decode_mixer_kernel(x_ref, cst_ref, h0_ref, n1g_ref, win_ref, cw_ref, cb_ref, bd_ref,
                         ba_ref, bx_ref, lam_ref,
                         hg_ref, gates_ref, cnew_ref, hnew_ref):
    nb = h0_ref.shape[0]
    nt = x_ref.shape[0] // nb
    d = x_ref.shape[1]
    xn = _rms(x_ref[...], n1g_ref[...]).astype(BF16)
    xr = _dot(xn, win_ref[:, REST0:REST0 + d])
    prev = cst_ref[...]
    slabs = [prev[k * nb:(k + 1) * nb] for k in range(CONV_W - 1)]
    slabs += [xr[k * nb:(k + 1) * nb] for k in range(nt)]
    xc = jnp.concatenate(
        [cb_ref[...] + sum(cw_ref[j:j + 1, :] * slabs[ti + j] for j in range(CONV_W))
         for ti in range(nt)], axis=0)
    cnew_ref[...] = jnp.concatenate(slabs[-(CONV_W - 1):], axis=0)

    a, bterm = _rglru_gates(xc, bd_ref, ba_ref[...], bx_ref[...], lam_ref[...])
    h = h0_ref[...]
    hs = []
    for ti in range(nt):
        h = a[ti * nb:(ti + 1) * nb] * h + bterm[ti * nb:(ti + 1) * nb]
        hs.append(h)
    hnew_ref[...] = h
    gr = _dot(xn, win_ref[:, REST0 + d:REST0 + 2 * d])
    hg_ref[...] = (jnp.concatenate(hs, axis=0) * jax.nn.gelu(gr)).astype(BF16)
    gates_ref[...] = _dot(xn, win_ref[:, REST0 + 2 * d:REST0 + 4 * d]).astype(BF16)


def _decode_mixer(x, cst, h0, w):
    m, d = x.shape
    consts = [w['n1g'], w['win'], w['cw'], w['cb'], w['bd'], w['ba'], w['bx'], w['lam']]
    ins = [x, cst, h0] + consts
    whole = lambda shape: pl.BlockSpec(shape, lambda i: (0,) * len(shape))
    return pl.pallas_call(
        _decode_mixer_kernel,
        out_shape=(jax.ShapeDtypeStruct((m, d), BF16),
                   jax.ShapeDtypeStruct((m, 2 * d), BF16),
                   jax.ShapeDtypeStruct(cst.shape, F32),
                   jax.ShapeDtypeStruct(h0.shape, F32)),
        grid=(1,),
        in_specs=[_const_spec(c.shape) for c in ins],
        out_specs=(whole((m, d)), whole((m, 2 * d)), whole(cst.shape), whole(h0.shape)),
        compiler_params=pltpu.CompilerParams(
            dimension_semantics=("arbitrary",), vmem_limit_bytes=VMEM_LIMIT),
        name="decode_mixer",
    )(*ins)


def _rel_bucket(dist):
    n = np.maximum(dist, 0)
    max_exact = REL_BUCKETS // 2
    nf = np.maximum(n, 1).astype(np.float32)
    large = max_exact + (np.log(nf / max_exact) / math.log(REL_MAX_DIST / max_exact)
                         * (REL_BUCKETS - max_exact)).astype(np.int32)
    large = np.minimum(large, REL_BUCKETS - 1)
    return np.where(n < max_exact, n, large)


def _bias_rows(rel_bias, dist, mask):
    tb = jnp.where(mask[:, :, None], rel_bias[_rel_bucket(dist)].astype(F32), NEG_INF)
    tq, tk = dist.shape
    return jnp.transpose(tb, (2, 0, 1)).reshape(N_HEADS * tq, tk)


def _prompt_bias(rel_bias):
    span = 3 * WINDOW
    k = np.arange(span)
    dist = WINDOW + np.where(k < WINDOW, k, k - span)
    valid = (dist >= 0) & (dist <= WINDOW)
    u = jnp.where(valid[:, None], rel_bias[_rel_bucket(dist)].astype(F32), NEG_INF).T
    n_keys = 2 * WINDOW
    skew = jnp.tile(u, (1, n_keys))[:, :n_keys * (span - 1)].reshape(N_HEADS, n_keys, span - 1)
    return skew[:, :, :WINDOW]


def _gate_blocks(rg_wa, rg_wx, n_grp):
    per = rg_wa.shape[0] // n_grp
    both = jnp.stack([rg_wa, rg_wx]).reshape(2, n_grp, per, RNN_BS, RNN_BS)
    on_diag = np.eye(per, dtype=bool)[None, None, :, None, :, None]
    blocks = jnp.where(on_diag, both[:, :, :, :, None, :], 0.0)
    blocks = jnp.transpose(blocks, (1, 2, 3, 0, 4, 5))
    return blocks.reshape(n_grp, per * RNN_BS, 2 * per * RNN_BS).astype(BF16)


def _head_avg(width):
    idx = np.arange(width) // HEAD_DIM
    return jnp.asarray((idx[:, None] == idx[None, :]).astype(np.float32) / HEAD_DIM, BF16)


def _prepare(rel_bias, norm1_g, w_in, q_norm_g, k_norm_g, sinks, w_o_attn, conv_w, conv_b,
             rg_wa, rg_ba, rg_wx, rg_bx, rg_lambda, w_o_rnn, w_out, norm2_g, w_up, w_down,
             ple_norm_g, w_ple_gate, w_ple, n_dec):
    d = w_in.shape[0]
    order = [g * GROUP + j for j in range(GROUP) for g in range(KV_HEADS)]
    perm = np.concatenate([np.arange(h * HEAD_DIM, (h + 1) * HEAD_DIM) for h in order])
    row = lambda a: a.reshape(1, -1).astype(F32)
    w = {}
    w['n1g'] = row(norm1_g)
    w['win'] = w_in.astype(BF16)
    w['wqkv'] = jnp.concatenate([w['win'][:, :Q_W][:, perm], w['win'][:, Q_W:REST0]], axis=1)
    w['gq'] = row(jnp.tile(q_norm_g, N_HEADS)) * (HEAD_DIM ** -0.5)
    w['gk'] = row(jnp.tile(k_norm_g, KV_HEADS))
    w['oq'] = _head_avg(Q_W)
    w['ok'] = _head_avg(KV_W)
    w['woa'] = w_o_attn[perm, :].astype(BF16)
    w['cw'] = conv_w.astype(F32)
    w['cb'] = row(conv_b)
    w['bd'] = _gate_blocks(rg_wa, rg_wx, d // MXU_DIM)
    w['ba'] = row(rg_ba)
    w['bx'] = row(rg_bx)
    w['lam'] = row(rg_lambda)
    w['wor'] = w_o_rnn.astype(BF16)
    w['wout'] = w_out.astype(BF16)
    w['n2g'] = row(norm2_g)
    w['wup'] = w_up.astype(BF16)
    w['wdn'] = w_down.astype(BF16)
    w['pg'] = row(ple_norm_g)
    w['wpg'] = w_ple_gate.astype(BF16)
    w['wple'] = w_ple.astype(BF16)

    bias = _prompt_bias(rel_bias)
    first = jnp.where((np.arange(2 * WINDOW) >= WINDOW)[:, None], bias, NEG_INF)
    by_group = lambda a: jnp.concatenate(
        [a.reshape(KV_HEADS, GROUP, 2 * WINDOW, WINDOW)[:, j] for j in range(GROUP)], axis=-1)
    w['bias_p'] = jnp.stack([by_group(bias), by_group(first)])
    sink_rows = sinks.astype(F32).reshape(KV_HEADS, GROUP, 1)
    w['sink_p'] = jnp.repeat(sink_rows, WINDOW, axis=2).reshape(KV_HEADS, 1, GROUP * WINDOW)

    sb = SEQ_BLOCK
    ti = np.arange(n_dec)[:, None]
    dist_c = ti + WINDOW - np.arange(WINDOW)[None, :]
    dist_n = ti - np.arange(n_dec)[None, :]
    per_head = lambda a: a.reshape(N_HEADS, n_dec, -1)
    bias_c = jnp.transpose(per_head(_bias_rows(rel_bias, dist_c, dist_c <= WINDOW)), (2, 0, 1))
    w['bias_sc'] = jnp.broadcast_to(bias_c[:, :, None, :], (WINDOW, N_HEADS, sb, n_dec)
                                    ).reshape(WINDOW, N_HEADS * sb * n_dec)
    bias_n = jnp.transpose(per_head(_bias_rows(rel_bias, dist_n, dist_n >= 0)), (2, 0, 1))
    same_seq = np.eye(sb, dtype=bool)[:, None, None, :, None]
    w['bias_sn'] = jnp.where(same_seq, bias_n[None, :, :, None, :], NEG_INF
                             ).reshape(sb * n_dec, N_HEADS * sb * n_dec)
    w['sink_s'] = jnp.broadcast_to(sinks.astype(F32)[:, None, None], (N_HEADS, sb, n_dec)
                                   ).reshape(1, N_HEADS * sb * n_dec)
    return w


def kernel(x_prompt, x_sample, cache_k_win, cache_v_win, state_conv, state_h, p_prompt, p_sample, rel_bias, norm1_g, w_in, q_norm_g, k_norm_g, sinks, w_o_attn, conv_w, conv_b, rg_wa, rg_ba, rg_wx, rg_bx, rg_lambda, w_o_rnn, w_out, norm2_g, w_up, w_down, ple_norm_g, w_ple_gate, w_ple):
    depth = w_in.shape[0]
    assert depth == 1, "single-layer step"
    b, t, d = x_prompt.shape
    nb, nt, _ = x_sample.shape
    w = _prepare(rel_bias, norm1_g[0], w_in[0], q_norm_g[0], k_norm_g[0], sinks[0], w_o_attn[0],
                 conv_w[0], conv_b[0], rg_wa[0], rg_ba[0], rg_wx[0], rg_bx[0], rg_lambda[0],
                 w_o_rnn[0], w_out[0], norm2_g[0], w_up[0], w_down[0], ple_norm_g[0],
                 w_ple_gate[0], w_ple[0], nt)

    hgp, attp, gatesp, kp, vp, cp, hp = _prompt_mixer(x_prompt, w)
    flat = lambda a: a.reshape(b * t, a.shape[-1])
    yp = _channel(flat(x_prompt), flat(hgp), flat(attp), flat(gatesp), flat(p_prompt[0]), w).reshape(b, t, d)

    att, ks, vs = _decode_attn(x_sample.reshape(nb * nt, d),
                               cache_k_win[0].reshape(nb, WINDOW, KV_W),
                               cache_v_win[0].reshape(nb, WINDOW, KV_W), w)
    to_tm = lambda a: jnp.swapaxes(a, 0, 1).reshape(-1, a.shape[-1])
    xs_tm = to_tm(x_sample)
    hgs, gatess, cs, hs = _decode_mixer(xs_tm, to_tm(state_conv[0]), state_h[0], w)
    ys = _channel(xs_tm, hgs, to_tm(att.reshape(nb, nt, Q_W)), gatess, to_tm(p_sample[0]), w)
    from_tm = lambda a, n: jnp.swapaxes(a.reshape(n, nb, a.shape[-1]), 0, 1)

    kv_shape = (1, -1, WINDOW, KV_HEADS, HEAD_DIM)
    return (yp, from_tm(ys, nt),
            kp.reshape(kv_shape), vp.reshape(kv_shape), cp[None], hp.reshape(1, b, d),
            ks.reshape(kv_shape), vs.reshape(kv_shape), from_tm(cs, CONV_W - 1)[None], hs[None])
```

```python
import functools
import math

import numpy as np
import jax
import jax.numpy as jnp
from jax import lax
from jax.experimental import pallas as pl
from jax.experimental.pallas import tpu as pltpu

F32 = jnp.float32
BF16 = jnp.bfloat16

N_HEADS = 8
KV_HEADS = 2
GROUP = N_HEADS // KV_HEADS
HEAD_DIM = 64
Q_W = N_HEADS * HEAD_DIM
KV_W = KV_HEADS * HEAD_DIM
REST0 = Q_W + 2 * KV_W
WINDOW = 128
REL_BUCKETS = 32
REL_MAX_DIST = 128
RNN_BS = 64
CONV_W = 4
RG_C = 8.0
EPS = 1e-6
NEG_INF = -1e30
EXPM1_SERIES_BELOW = 2.0 ** -11
SQRT_FLOOR = 1e-30

LANES = 128
SUBLANES = 8
MXU_DIM = 256
VMEM_LIMIT = 56 * 1024 * 1024

TM_MIX = 512
TM_MLP = 512
FF_CHUNK = 1024
SEQ_BLOCK = 16
PACE_LAG = 2
SEG_PAD = 4


def _dot(a, b):
    return jnp.dot(a, b, preferred_element_type=F32)


def _dot_nt(a, b):
    return lax.dot_general(a, b, (((1,), (1,)), ((), ())), preferred_element_type=F32)


def _rms(x, g):
    ms = jnp.mean(x * x, axis=-1, keepdims=True)
    return x * lax.rsqrt(ms + EPS) * g


def _head_rms(x, ones_ref, g):
    ms = _dot((x * x).astype(BF16), ones_ref[...])
    return x * lax.rsqrt(ms + EPS) * g


def _lower_half(shape):
    return lax.broadcasted_iota(jnp.int32, shape, len(shape) - 1) < HEAD_DIM


def _scan_rows(a, b, h0, sub=None):
    n, d = a.shape
    a3 = a.reshape(n // SUBLANES, SUBLANES, d)
    b3 = b.reshape(n // SUBLANES, SUBLANES, d)
    if sub is None:
        sub = lax.broadcasted_iota(jnp.int32, (1, SUBLANES, d), 1)
    step = 1
    while step < SUBLANES:
        keep = sub >= step
        b3 = jnp.where(keep, a3 * pltpu.roll(b3, step, axis=1) + b3, b3)
        a3 = jnp.where(keep, a3 * pltpu.roll(a3, step, axis=1), a3)
        step *= 2
    hs = []
    for k in range(n // SUBLANES):
        hk = a3[k] * h0 + b3[k]
        hs.append(hk)
        h0 = hk[SUBLANES - 1:SUBLANES, :]
    return jnp.concatenate(hs, axis=0), h0


def _shift_rows(x, prev, k):
    rolled = pltpu.roll(x, k, axis=0)
    sub = lax.broadcasted_iota(jnp.int32, prev.shape, 0)
    head = jnp.where(sub < k, pltpu.roll(prev, k, axis=0), rolled[0:SUBLANES])
    return jnp.concatenate([head, rolled[SUBLANES:]], axis=0)


def _neg_expm1_2x(x, exp_x):
    return jnp.where(x > -EXPM1_SERIES_BELOW, (-2.0 * x) * (1.0 + x), 1.0 - exp_x * exp_x)


def _log_decay(lam):
    return -RG_C * jax.nn.softplus(-lam)


def _gate_math(ya, yx, xc, ba, bx, log_decay):
    r = jax.nn.sigmoid(ya + ba)
    i = jax.nn.sigmoid(yx + bx)
    log_a = log_decay * r
    a = jnp.exp(log_a)
    y = _neg_expm1_2x(log_a, a)
    bterm = (y * lax.rsqrt(jnp.maximum(y, SQRT_FLOOR))) * (i * xc)
    return a, bterm


def _rglru_gates(xc, bd_ref, ba, bx, lam):
    xcb = xc.astype(BF16)
    ya, yx = [], []
    for gi in range(xc.shape[1] // MXU_DIM):
        y = _dot(xcb[:, gi * MXU_DIM:(gi + 1) * MXU_DIM], bd_ref[gi])
        ya.append(y[:, :MXU_DIM])
        yx.append(y[:, MXU_DIM:])
    return _gate_math(jnp.concatenate(ya, axis=1), jnp.concatenate(yx, axis=1), xc, ba, bx,
                      _log_decay(lam))


def _prompt_mixer_kernel(x_ref, n1g_ref, wqkv_ref, win_ref, gq_ref, gk_ref, oq_ref, ok_ref,
                         bias_ref, sink_ref, cw8_ref, cb8_ref, bd_ref, ba8_ref, bx8_ref, lam8_ref,
                         hg, att, gates, kwin_ref, vwin_ref, cst_ref, hst_ref,
                         proj, projx, kbuf, vbuf, xp, hc, logit_s, e_s, xc_s, y_s, hl_s, ac_s, hgn_s,
                         *, blocks_per_seq):
    tm, d = x_ref.shape
    n = pl.program_id(0)

    @pl.when(n == 0)
    def _():
        proj[1] = jnp.zeros(proj.shape[1:], F32)
        projx[1] = jnp.zeros(projx.shape[1:], F32)
        kbuf[...] = jnp.zeros(kbuf.shape, BF16)
        vbuf[...] = jnp.zeros(vbuf.shape, BF16)
        xp[...] = jnp.zeros(xp.shape, F32)
        hc[...] = jnp.zeros(hc.shape, F32)

    wslot = lax.rem(n, 2)
    rslot = 1 - wslot
    fresh = lax.rem(n + blocks_per_seq - 1, blocks_per_seq) == 0

    xn = _rms(x_ref[...], n1g_ref[...]).astype(BF16)
    n_qkv = wqkv_ref.shape[1]

    tokens = []

    def project(c):
        seg, pitch = tm // SUBLANES, tm // SUBLANES + SEG_PAD
        lo = c * MXU_DIM
        res = _dot(xn, (wqkv_ref if lo < n_qkv else win_ref)[:, lo:lo + MXU_DIM])
        tokens.append(res[0:1, :])
        if lo < n_qkv:
            for i in range(MXU_DIM // LANES):
                proj[wslot, lo // LANES + i] = res[:, i * LANES:(i + 1) * LANES]
        elif lo < REST0 + 2 * d:
            for i in range(MXU_DIM // LANES):
                for sgm in range(SUBLANES):
                    projx[wslot, (lo - REST0) // LANES + i, sgm * pitch:sgm * pitch + seg, :] = (
                        res[sgm * seg:(sgm + 1) * seg, i * LANES:(i + 1) * LANES])
        else:
            gates[:, lo - REST0 - 2 * d:lo - REST0 - 2 * d + MXU_DIM] = res.astype(BF16)

    pending = list(range(win_ref.shape[1] // MXU_DIM))

    def pace():
        if len(tokens) < PACE_LAG:
            return jnp.zeros((1, MXU_DIM), F32)
        bits = lax.bitcast_convert_type(tokens[-PACE_LAG], jnp.uint32)
        return lax.bitcast_convert_type((bits >> 16) >> 16, F32)

    def emit_stage1(count):
        for _ in range(min(count, len(pending))):
            project(pending.pop(0))

    def kept(lo, hi, rows=slice(None)):
        return jnp.concatenate([proj[rslot, t, rows, :] for t in range(lo // LANES, hi // LANES)], axis=1)

    emit_stage1(2)
    qn = _head_rms(kept(0, Q_W), oq_ref, gq_ref[...]).astype(BF16)
    kn = _head_rms(kept(Q_W, Q_W + KV_W), ok_ref, gk_ref[...])
    v = kept(Q_W + KV_W, n_qkv)
    kbuf[0:WINDOW, :] = jnp.where(fresh, jnp.zeros((WINDOW, KV_W), BF16), kbuf[tm:tm + WINDOW, :])
    vbuf[:, 0:WINDOW] = jnp.where(fresh, jnp.zeros((KV_W, WINDOW), BF16), vbuf[:, tm:tm + WINDOW])
    kbuf[WINDOW:WINDOW + tm, :] = kn.astype(BF16)
    vbuf[:, WINDOW:WINDOW + tm] = v.T.astype(BF16)
    kwin_ref[...] = kn[tm - WINDOW:, :]
    vwin_ref[...] = v[tm - WINDOW:, :]

    first = jnp.where(fresh, 1, 0)
    lower = _lower_half((WINDOW, LANES))
    zero = jnp.zeros((WINDOW, LANES), BF16)

    def attend(s):
        rows = slice(s * WINDOW, (s + 1) * WINDOW)
        keys = kbuf[s * WINDOW:(s + 2) * WINDOW, :]
        vals_t = vbuf[:, s * WINDOW:(s + 2) * WINDOW]
        outs = []
        for g in range(KV_HEADS):
            qs = []
            for j in range(GROUP):
                slab = qn[rows, j * LANES:(j + 1) * LANES]
                qs.append(jnp.where(lower, slab, zero) if g == 0 else jnp.where(lower, zero, slab))
            logit_s[g] = _dot_nt(keys, jnp.concatenate(qs, axis=0))
            yield
            inv = []
            for j in range(GROUP):
                blk = slice(j * WINDOW, (j + 1) * WINDOW)
                bias = bias_ref[first, g, :, blk] if s == 0 else bias_ref[0, g, :, blk]
                logit = logit_s[g, :, blk] + bias
                sink = sink_ref[g, :, blk] + pace()[:, :WINDOW]
                m = jnp.maximum(jnp.max(logit, axis=0, keepdims=True), sink)
                e = jnp.exp(logit - m)
                inv.append(1.0 / (jnp.sum(e, axis=0, keepdims=True) + jnp.exp(sink - m)))
                e_s[g, :, blk] = e.astype(BF16)
                yield
            outs.append(_dot(vals_t, e_s[g]) * jnp.concatenate(inv, axis=1))
        for j in range(GROUP):
            blk = slice(j * WINDOW, (j + 1) * WINDOW)
            both = jnp.concatenate([outs[0][:HEAD_DIM, blk], outs[1][HEAD_DIM:, blk]], axis=0)
            att[rows, j * LANES:(j + 1) * LANES] = both.T.astype(BF16)
        yield

    seg = tm // SUBLANES
    pitch = seg + SEG_PAD
    sub8 = lax.broadcasted_iota(jnp.int32, (SUBLANES, MXU_DIM), 0)

    def recur(gi):
        cols = slice(gi * MXU_DIM, (gi + 1) * MXU_DIM)
        tiles = range(gi * (MXU_DIM // LANES), (gi + 1) * (MXU_DIM // LANES))
        by_segment = lambda k: pl.ds(k, SUBLANES, stride=pitch)
        x_at = lambda k: jnp.concatenate([projx[rslot, t, by_segment(k), :] for t in tiles], axis=1)
        last = SUBLANES * pitch - SEG_PAD
        x_tail = lambda n_rows: jnp.concatenate(
            [projx[rslot, t, last - n_rows:last, :] for t in tiles], axis=1)
        prev = jnp.where(fresh, 0.0, xp[:, cols])
        window = [jnp.where(sub8 == 0, prev[SUBLANES - j:SUBLANES - j + 1, :],
                            pltpu.roll(x_at(seg - j), 1, axis=0)) for j in range(CONV_W - 1, 0, -1)]
        for k in range(seg):
            window.append(x_at(k))
            xc_s[gi, k * SUBLANES:(k + 1) * SUBLANES, :] = cb8_ref[:, cols] + sum(
                cw8_ref[j, :, cols] * window[j] for j in range(CONV_W))
            window.pop(0)
            if k % 16 == 15:
                yield
        xp[:, cols] = x_tail(SUBLANES)
        cst_ref[:, cols] = x_tail(CONV_W - 1)
        y_s[gi] = _dot(xc_s[gi].astype(BF16), bd_ref[gi])
        yield
        ba, bx = ba8_ref[:, cols], bx8_ref[:, cols]
        log_decay = _log_decay(lam8_ref[:, cols])
        h_loc = a_cum = None
        for k in range(seg):
            r = slice(k * SUBLANES, (k + 1) * SUBLANES)
            a, bterm = _gate_math(y_s[gi, r, :MXU_DIM], y_s[gi, r, MXU_DIM:], xc_s[gi, r, :] + pace(),
                                  ba, bx, log_decay)
            h_loc = bterm if k == 0 else a * h_loc + bterm
            a_cum = a if k == 0 else a * a_cum
            hl_s[gi, r, :] = h_loc
            ac_s[gi, r, :] = a_cum
            if k % 4 == 3:
                yield
        h_in = jnp.where(fresh, 0.0, hc[:, cols])
        carry_in = []
        for sgm in range(SUBLANES):
            carry_in.append(h_in)
            h_in = h_loc[sgm:sgm + 1, :] + a_cum[sgm:sgm + 1, :] * h_in
        hc[:, cols] = h_in
        hst_ref[:, cols] = h_in
        carry_in = jnp.concatenate(carry_in, axis=0)
        n_x = d // LANES
        for k in range(seg):
            r = slice(k * SUBLANES, (k + 1) * SUBLANES)
            h = hl_s[gi, r, :] + ac_s[gi, r, :] * carry_in
            gr = jnp.concatenate([projx[rslot, n_x + t, by_segment(k), :] for t in tiles], axis=1)
            hgn = h * jax.nn.gelu(gr)
            for i in range(MXU_DIM // LANES):
                hgn_s[gi, i, by_segment(k), :] = hgn[:, i * LANES:(i + 1) * LANES]
            if k % 8 == 7:
                yield
        hg[:, cols] = jnp.concatenate(
            [jnp.concatenate([hgn_s[gi, i, sgm * pitch:sgm * pitch + seg, :] for sgm in range(SUBLANES)], axis=0)
             for i in range(MXU_DIM // LANES)], axis=1).astype(BF16)
        yield

    n_sub = tm // WINDOW
    n_grp = d // MXU_DIM
    pieces = []
    for u in range(max(n_sub, n_grp)):
        if u < n_sub:
            pieces.append(attend(u))
        if u < n_grp:
            pieces.append(recur(u))
    n_spots = n_sub * (KV_HEADS * (1 + GROUP) + 1) + n_grp * (2 + seg // 16 + seg // 4 + seg // 8)
    per_spot = len(pending) / n_spots
    due = 0.0
    for piece in pieces:
        for _ in piece:
            due += per_spot
            emit_stage1(int(due))
            due -= int(due)
    emit_stage1(len(pending))


def _const_spec(shape):
    nd = len(shape)
    return pl.BlockSpec(shape, lambda *_: (0,) * nd, pipeline_mode=pl.Buffered(1))


def _prompt_mixer(x, w):
    b, t, d = x.shape
    tm = TM_MIX
    nt = t // tm
    n_blocks = b * nt
    consts = [w['n1g'], w['wqkv'], w['win'], w['gq'], w['gk'], w['oq'], w['ok'],
              w['bias_p'], w['sink_p'], w['cw8'], w['cb8'], w['bd'], w['ba8'], w['bx8'], w['lam8']]
    out_shape = (
        jax.ShapeDtypeStruct((b, t, d), BF16),
        jax.ShapeDtypeStruct((b, t, Q_W), BF16),
        jax.ShapeDtypeStruct((b, t, 2 * d), BF16),
        jax.ShapeDtypeStruct((b, WINDOW, KV_W), F32),
        jax.ShapeDtypeStruct((b, WINDOW, KV_W), F32),
        jax.ShapeDtypeStruct((b, CONV_W - 1, d), F32),
        jax.ShapeDtypeStruct((b, 1, d), F32),
    )
    def cur(n):
        i = jnp.minimum(n, n_blocks - 1)
        return (i // nt, i % nt, 0)

    def prv(n):
        i = jnp.maximum(n - 1, 0)
        return (i // nt, i % nt, 0)

    per_seq = lambda n: (jnp.maximum(n - 1, 0) // nt, 0, 0)
    return pl.pallas_call(
        functools.partial(_prompt_mixer_kernel, blocks_per_seq=nt),
        out_shape=out_shape,
        grid=(n_blocks + 1,),
        in_specs=[pl.BlockSpec((None, tm, d), cur)]
                 + [_const_spec(c.shape) for c in consts],
        out_specs=(
            pl.BlockSpec((None, tm, d), prv),
            pl.BlockSpec((None, tm, Q_W), prv),
            pl.BlockSpec((None, tm, 2 * d), cur),
            pl.BlockSpec((None, WINDOW, KV_W), per_seq),
            pl.BlockSpec((None, WINDOW, KV_W), per_seq),
            pl.BlockSpec((None, CONV_W - 1, d), per_seq),
            pl.BlockSpec((None, 1, d), per_seq),
        ),
        scratch_shapes=[
            pltpu.VMEM((2, REST0 // LANES, tm, LANES), F32),
            pltpu.VMEM((2, 2 * d // LANES, SUBLANES * (tm // SUBLANES + SEG_PAD), LANES), F32),
            pltpu.VMEM((WINDOW + tm, KV_W), BF16),
            pltpu.VMEM((KV_W, WINDOW + tm), BF16),
            pltpu.VMEM((SUBLANES, d), F32),
            pltpu.VMEM((1, d), F32),
            pltpu.VMEM((KV_HEADS, 2 * WINDOW, GROUP * WINDOW), F32),
            pltpu.VMEM((KV_HEADS, 2 * WINDOW, GROUP * WINDOW), BF16),
            pltpu.VMEM((d // MXU_DIM, tm, MXU_DIM), F32),
            pltpu.VMEM((d // MXU_DIM, tm, 2 * MXU_DIM), F32),
            pltpu.VMEM((d // MXU_DIM, tm, MXU_DIM), F32),
            pltpu.VMEM((d // MXU_DIM, tm, MXU_DIM), F32),
            pltpu.VMEM((d // MXU_DIM, MXU_DIM // LANES, SUBLANES * (tm // SUBLANES + SEG_PAD), LANES), F32),
        ],
        compiler_params=pltpu.CompilerParams(
            dimension_semantics=("arbitrary",), vmem_limit_bytes=VMEM_LIMIT),
        name="prompt_mixer",
    )(x, *consts)


def _channel_kernel(x_ref, hg_ref, att_ref, gates_ref, p_ref, woa_ref, wor_ref, wout_ref,
                    n2g_ref, wup_ref, wdn_ref, pg_ref, wpg_ref, wple_ref, o_ref):
    d = x_ref.shape[1]
    rnn = _dot(hg_ref[...], wor_ref[...])
    atto = _dot(att_ref[...], woa_ref[...])
    gates = gates_ref[...].astype(F32)
    mix = (jax.nn.sigmoid(gates[:, :d]) * atto + jax.nn.sigmoid(gates[:, d:]) * rnn).astype(BF16)
    x = x_ref[...] + _dot(mix, wout_ref[...])
    xn = _rms(x, n2g_ref[...]).astype(BF16)
    acc = x
    for c in range(wup_ref.shape[1] // FF_CHUNK):
        cols = slice(c * FF_CHUNK, (c + 1) * FF_CHUNK)
        hmid = jnp.maximum(_dot(xn, wup_ref[:, cols]), 0.0)
        acc = acc + _dot((hmid * hmid).astype(BF16), wdn_ref[cols, :])
    gate = jax.nn.sigmoid(_dot(_rms(acc, pg_ref[...]).astype(BF16), wpg_ref[...]))
    o_ref[...] = acc + gate * _dot(p_ref[...].astype(BF16), wple_ref[...])


def _channel(x, hg, att, gates, p, w):
    m, d = x.shape
    tm = min(TM_MLP, m)
    consts = [w['woa'], w['wor'], w['wout'], w['n2g'], w['wup'], w['wdn'], w['pg'], w['wpg'], w['wple']]
    rows = lambda a: pl.BlockSpec((tm, a.shape[1]), lambda i: (i, 0))
    return pl.pallas_call(
        _channel_kernel,
        out_shape=jax.ShapeDtypeStruct((m, d), F32),
        grid=(m // tm,),
        in_specs=[rows(a) for a in (x, hg, att, gates, p)] + [_const_spec(c.shape) for c in consts],
        out_specs=rows(x),
        compiler_params=pltpu.CompilerParams(
            dimension_semantics=("arbitrary",), vmem_limit_bytes=VMEM_LIMIT),
        name="merge_mlp_ple",
    )(x, hg, att, gates, p, *consts)


def _decode_attn_kernel(x_ref, ck_ref, cv_ref, n1g_ref, wqkv_ref, gq_ref, gk_ref, oq_ref, ok_ref,
                        bias_c_ref, bias_n_ref, sink_ref,
                        att_ref, kwin_ref, vwin_ref):
    sb = ck_ref.shape[0]
    nt = x_ref.shape[0] // sb
    xn = _rms(x_ref[...], n1g_ref[...]).astype(BF16)
    qkv = _dot(xn, wqkv_ref[...])
    qn = _head_rms(qkv[:, :Q_W], oq_ref, gq_ref[...])
    kn = _head_rms(qkv[:, Q_W:Q_W + KV_W], ok_ref, gk_ref[...])
    v = qkv[:, Q_W + KV_W:]

    n_rows = sb * nt
    ck = ck_ref[...]
    cv = cv_ref[...]
    for b in range(sb):
        rows = slice(b * nt, (b + 1) * nt)
        kwin_ref[b, 0:WINDOW - nt, :] = ck[b, nt:, :]
        vwin_ref[b, 0:WINDOW - nt, :] = cv[b, nt:, :]
        kwin_ref[b, WINDOW - nt:WINDOW, :] = kn[rows]
        vwin_ref[b, WINDOW - nt:WINDOW, :] = v[rows]

    qb = qn.astype(BF16)
    lower = _lower_half((n_rows, LANES))
    zero = jnp.zeros((n_rows, LANES), BF16)
    q_all = jnp.concatenate(
        [jnp.where(lower, qb[:, j * LANES:(j + 1) * LANES], zero) if g == 0
         else jnp.where(lower, zero, qb[:, j * LANES:(j + 1) * LANES])
         for g in range(KV_HEADS) for j in range(GROUP)], axis=0)
    n_col = q_all.shape[0]
    col_seq = (lax.broadcasted_iota(jnp.int32, (1, n_col), 1) // nt) % sb

    def own(big):
        out = big[0:WINDOW]
        for b in range(1, sb):
            out = jnp.where(col_seq == b, big[b * WINDOW:(b + 1) * WINDOW], out)
        return out

    st = own(_dot_nt(ck.reshape(sb * WINDOW, KV_W).astype(BF16), q_all)) + bias_c_ref[...]
    stx = _dot_nt(kn.astype(BF16), q_all) + bias_n_ref[...]
    sink = sink_ref[...]
    m = jnp.maximum(jnp.maximum(jnp.max(st, axis=0, keepdims=True),
                                jnp.max(stx, axis=0, keepdims=True)), sink)
    e = jnp.exp(st - m)
    ex = jnp.exp(stx - m)
    den = (jnp.sum(e, axis=0, keepdims=True) + jnp.sum(ex, axis=0, keepdims=True)
           + jnp.exp(sink - m))
    cv_t = jnp.concatenate([cv[b].T for b in range(sb)], axis=0).astype(BF16)
    out = own(_dot(cv_t, e.astype(BF16)))
    pad = WINDOW - n_rows
    v_t = jnp.concatenate([v, jnp.zeros((pad, KV_W), F32)], axis=0).T.astype(BF16)
    ex_pad = jnp.concatenate([ex, jnp.zeros((pad, n_col), F32)], axis=0).astype(BF16)
    out = (out + _dot(v_t, ex_pad)) / den
    half = n_col // KV_HEADS
    both_t = jnp.concatenate([out[:HEAD_DIM, :half], out[HEAD_DIM:, half:]], axis=0).T
    for j in range(GROUP):
        att_ref[:, j * LANES:(j + 1) * LANES] = both_t[j * n_rows:(j + 1) * n_rows].astype(att_ref.dtype)


def _decode_attn(x, ck, cv, w):
    m, d = x.shape
    nb = ck.shape[0]
    nt = m // nb
    sb = SEQ_BLOCK
    consts = [w['n1g'], w['wqkv'], w['gq'], w['gk'], w['oq'], w['ok'],
              w['bias_sc'], w['bias_sn'], w['sink_s']]
    cache_spec = pl.BlockSpec((sb, WINDOW, KV_W), lambda i: (i, 0, 0))
    return pl.pallas_call(
        _decode_attn_kernel,
        out_shape=(jax.ShapeDtypeStruct((m, Q_W), BF16),
                   jax.ShapeDtypeStruct(ck.shape, F32),
                   jax.ShapeDtypeStruct(cv.shape, F32)),
        grid=(nb // sb,),
        in_specs=[pl.BlockSpec((sb * nt, d), lambda i: (i, 0)), cache_spec, cache_spec]
                 + [_const_spec(c.shape) for c in consts],
        out_specs=(pl.BlockSpec((sb * nt, Q_W), lambda i: (i, 0)), cache_spec, cache_spec),
        compiler_params=pltpu.CompilerParams(
            dimension_semantics=("arbitrary",), vmem_limit_bytes=VMEM_LIMIT),
        name="decode_attn",
    )(x, ck, cv, *consts)


def _decode_mixer_kernel(x_ref, cst_ref, h0_ref, n1g_ref, win_ref, cw_ref, cb_ref, bd_ref,
                         ba_ref, bx_ref, lam_ref,
                         hg_ref, gates_ref, cnew_ref, hnew_ref):
    nb = h0_ref.shape[0]
    nt = x_ref.shape[0] // nb
    d = x_ref.shape[1]
    xn = _rms(x_ref[...], n1g_ref[...]).astype(BF16)
    xr = _dot(xn, win_ref[:, REST0:REST0 + d])
    prev = cst_ref[...]
    slabs = [prev[k * nb:(k + 1) * nb] for k in range(CONV_W - 1)]
    slabs += [xr[k * nb:(k + 1) * nb] for k in range(nt)]
    xc = jnp.concatenate(
        [cb_ref[...] + sum(cw_ref[j:j + 1, :] * slabs[ti + j] for j in range(CONV_W))
         for ti in range(nt)], axis=0)
    cnew_ref[...] = jnp.concatenate(slabs[-(CONV_W - 1):], axis=0)

    a, bterm = _rglru_gates(xc, bd_ref, ba_ref[...], bx_ref[...], lam_ref[...])
    h = h0_ref[...]
    hs = []
    for ti in range(nt):
        h = a[ti * nb:(ti + 1) * nb] * h + bterm[ti * nb:(ti + 1) * nb]
        hs.append(h)
    hnew_ref[...] = h
    gr = _dot(xn, win_ref[:, REST0 + d:REST0 + 2 * d])
    hg_ref[...] = (jnp.concatenate(hs, axis=0) * jax.nn.gelu(gr)).astype(BF16)
    gates_ref[...] = _dot(xn, win_ref[:, REST0 + 2 * d:REST0 + 4 * d]).astype(BF16)


def _decode_mixer(x, cst, h0, w):
    m, d = x.shape
    consts = [w['n1g'], w['win'], w['cw'], w['cb'], w['bd'], w['ba'], w['bx'], w['lam']]
    ins = [x, cst, h0] + consts
    whole = lambda shape: pl.BlockSpec(shape, lambda i: (0,) * len(shape))
    return pl.pallas_call(
        _decode_mixer_kernel,
        out_shape=(jax.ShapeDtypeStruct((m, d), BF16),
                   jax.ShapeDtypeStruct((m, 2 * d), BF16),
                   jax.ShapeDtypeStruct(cst.shape, F32),
                   jax.ShapeDtypeStruct(h0.shape, F32)),
        grid=(1,),
        in_specs=[_const_spec(c.shape) for c in ins],
        out_specs=(whole((m, d)), whole((m, 2 * d)), whole(cst.shape), whole(h0.shape)),
        compiler_params=pltpu.CompilerParams(
            dimension_semantics=("arbitrary",), vmem_limit_bytes=VMEM_LIMIT),
        name="decode_mixer",
    )(*ins)


def _rel_bucket(dist):
    n = np.maximum(dist, 0)
    max_exact = REL_BUCKETS // 2
    nf = np.maximum(n, 1).astype(np.float32)
    large = max_exact + (np.log(nf / max_exact) / math.log(REL_MAX_DIST / max_exact)
                         * (REL_BUCKETS - max_exact)).astype(np.int32)
    large = np.minimum(large, REL_BUCKETS - 1)
    return np.where(n < max_exact, n, large)


def _bias_rows(rel_bias, dist, mask):
    tb = jnp.where(mask[:, :, None], rel_bias[_rel_bucket(dist)].astype(F32), NEG_INF)
    tq, tk = dist.shape
    return jnp.transpose(tb, (2, 0, 1)).reshape(N_HEADS * tq, tk)


def _prompt_bias(rel_bias):
    span = 3 * WINDOW
    k = np.arange(span)
    dist = WINDOW + np.where(k < WINDOW, k, k - span)
    valid = (dist >= 0) & (dist <= WINDOW)
    u = jnp.where(valid[:, None], rel_bias[_rel_bucket(dist)].astype(F32), NEG_INF).T
    n_keys = 2 * WINDOW
    skew = jnp.tile(u, (1, n_keys))[:, :n_keys * (span - 1)].reshape(N_HEADS, n_keys, span - 1)
    return skew[:, :, :WINDOW]


def _gate_blocks(rg_wa, rg_wx, n_grp):
    per = rg_wa.shape[0] // n_grp
    both = jnp.stack([rg_wa, rg_wx]).reshape(2, n_grp, per, RNN_BS, RNN_BS)
    on_diag = np.eye(per, dtype=bool)[None, None, :, None, :, None]
    blocks = jnp.where(on_diag, both[:, :, :, :, None, :], 0.0)
    blocks = jnp.transpose(blocks, (1, 2, 3, 0, 4, 5))
    return blocks.reshape(n_grp, per * RNN_BS, 2 * per * RNN_BS).astype(BF16)


def _head_avg(width):
    idx = np.arange(width) // HEAD_DIM
    return jnp.asarray((idx[:, None] == idx[None, :]).astype(np.float32) / HEAD_DIM, BF16)


def _prepare(rel_bias, norm1_g, w_in, q_norm_g, k_norm_g, sinks, w_o_attn, conv_w, conv_b,
             rg_wa, rg_ba, rg_wx, rg_bx, rg_lambda, w_o_rnn, w_out, norm2_g, w_up, w_down,
             ple_norm_g, w_ple_gate, w_ple, n_dec):
    d = w_in.shape[0]
    order = [g * GROUP + j for j in range(GROUP) for g in range(KV_HEADS)]
    perm = np.concatenate([np.arange(h * HEAD_DIM, (h + 1) * HEAD_DIM) for h in order])
    row = lambda a: a.reshape(1, -1).astype(F32)
    w = {}
    w['n1g'] = row(norm1_g)
    w['win'] = w_in.astype(BF16)
    w['wqkv'] = jnp.concatenate([w['win'][:, :Q_W][:, perm], w['win'][:, Q_W:REST0]], axis=1)
    w['gq'] = row(jnp.tile(q_norm_g, N_HEADS)) * (HEAD_DIM ** -0.5)
    w['gk'] = row(jnp.tile(k_norm_g, KV_HEADS))
    w['oq'] = _head_avg(Q_W)
    w['ok'] = _head_avg(KV_W)
    w['woa'] = w_o_attn[perm, :].astype(BF16)
    w['cw'] = conv_w.astype(F32)
    w['cb'] = row(conv_b)
    w['bd'] = _gate_blocks(rg_wa, rg_wx, d // MXU_DIM)
    w['ba'] = row(rg_ba)
    w['bx'] = row(rg_bx)
    w['lam'] = row(rg_lambda)
    rows8 = lambda a: jnp.broadcast_to(a, (SUBLANES, a.shape[-1]))
    w['cw8'] = jnp.broadcast_to(w['cw'][:, None, :], (CONV_W, SUBLANES, d))
    w['cb8'], w['ba8'], w['bx8'], w['lam8'] = (rows8(w[k]) for k in ('cb', 'ba', 'bx', 'lam'))
    w['wor'] = w_o_rnn.astype(BF16)
    w['wout'] = w_out.astype(BF16)
    w['n2g'] = row(norm2_g)
    w['wup'] = w_up.astype(BF16)
    w['wdn'] = w_down.astype(BF16)
    w['pg'] = row(ple_norm_g)
    w['wpg'] = w_ple_gate.astype(BF16)
    w['wple'] = w_ple.astype(BF16)

    bias = _prompt_bias(rel_bias)
    first = jnp.where((np.arange(2 * WINDOW) >= WINDOW)[:, None], bias, NEG_INF)
    by_group = lambda a: jnp.concatenate(
        [a.reshape(KV_HEADS, GROUP, 2 * WINDOW, WINDOW)[:, j] for j in range(GROUP)], axis=-1)
    w['bias_p'] = jnp.stack([by_group(bias), by_group(first)])
    sink_rows = sinks.astype(F32).reshape(KV_HEADS, GROUP, 1)
    w['sink_p'] = jnp.repeat(sink_rows, WINDOW, axis=2).reshape(KV_HEADS, 1, GROUP * WINDOW)

    sb = SEQ_BLOCK
    ti = np.arange(n_dec)[:, None]
    dist_c = ti + WINDOW - np.arange(WINDOW)[None, :]
    dist_n = ti - np.arange(n_dec)[None, :]
    per_head = lambda a: a.reshape(N_HEADS, n_dec, -1)
    bias_c = jnp.transpose(per_head(_bias_rows(rel_bias, dist_c, dist_c <= WINDOW)), (2, 0, 1))
    w['bias_sc'] = jnp.broadcast_to(bias_c[:, :, None, :], (WINDOW, N_HEADS, sb, n_dec)
                                    ).reshape(WINDOW, N_HEADS * sb * n_dec)
    bias_n = jnp.transpose(per_head(_bias_rows(rel_bias, dist_n, dist_n >= 0)), (2, 0, 1))
    same_seq = np.eye(sb, dtype=bool)[:, None, None, :, None]
    w['bias_sn'] = jnp.where(same_seq, bias_n[None, :, :, None, :], NEG_INF
                             ).reshape(sb * n_dec, N_HEADS * sb * n_dec)
    w['sink_s'] = jnp.broadcast_to(sinks.astype(F32)[:, None, None], (N_HEADS, sb, n_dec)
                                   ).reshape(1, N_HEADS * sb * n_dec)
    return w


def kernel(x_prompt, x_sample, cache_k_win, cache_v_win, state_conv, state_h, p_prompt, p_sample, rel_bias, norm1_g, w_in, q_norm_g, k_norm_g, sinks, w_o_attn, conv_w, conv_b, rg_wa, rg_ba, rg_wx, rg_bx, rg_lambda, w_o_rnn, w_out, norm2_g, w_up, w_down, ple_norm_g, w_ple_gate, w_ple):
    depth = w_in.shape[0]
    assert depth == 1, "single-layer step"
    b, t, d = x_prompt.shape
    nb, nt, _ = x_sample.shape
    w = _prepare(rel_bias, norm1_g[0], w_in[0], q_norm_g[0], k_norm_g[0], sinks[0], w_o_attn[0],
                 conv_w[0], conv_b[0], rg_wa[0], rg_ba[0], rg_wx[0], rg_bx[0], rg_lambda[0],
                 w_o_rnn[0], w_out[0], norm2_g[0], w_up[0], w_down[0], ple_norm_g[0],
                 w_ple_gate[0], w_ple[0], nt)

    hgp, attp, gatesp, kp, vp, cp, hp = _prompt_mixer(x_prompt, w)
    flat = lambda a: a.reshape(b * t, a.shape[-1])
    yp = _channel(flat(x_prompt), flat(hgp), flat(attp), flat(gatesp), flat(p_prompt[0]), w).reshape(b, t, d)

    att, ks, vs = _decode_attn(x_sample.reshape(nb * nt, d),
                               cache_k_win[0].reshape(nb, WINDOW, KV_W),
                               cache_v_win[0].reshape(nb, WINDOW, KV_W), w)
    to_tm = lambda a: jnp.swapaxes(a, 0, 1).reshape(-1, a.shape[-1])
    xs_tm = to_tm(x_sample)
    hgs, gatess, cs, hs = _decode_mixer(xs_tm, to_tm(state_conv[0]), state_h[0], w)
    ys = _channel(xs_tm, hgs, to_tm(att.reshape(nb, nt, Q_W)), gatess, to_tm(p_sample[0]), w)
    from_tm = lambda a, n: jnp.swapaxes(a.reshape(n, nb, a.shape[-1]), 0, 1)

    kv_shape = (1, -1, WINDOW, KV_HEADS, HEAD_DIM)
    return (yp, from_tm(ys, nt),
            kp.reshape(kv_shape), vp.reshape(kv_shape), cp[None], hp.reshape(1, b, d),
            ks.reshape(kv_shape), vs.reshape(kv_shape), from_tm(cs, CONV_W - 1)[None], hs[None])
```

```python
import functools
import math

import numpy as np
import jax
import jax.numpy as jnp
from jax import lax
from jax.experimental import pallas as pl
from jax.experimental.pallas import tpu as pltpu

F32 = jnp.float32
BF16 = jnp.bfloat16

N_HEADS = 8
KV_HEADS = 2
GROUP = N_HEADS // KV_HEADS
HEAD_DIM = 64
Q_W = N_HEADS * HEAD_DIM
KV_W = KV_HEADS * HEAD_DIM
REST0 = Q_W + 2 * KV_W
WINDOW = 128
REL_BUCKETS = 32
REL_MAX_DIST = 128
RNN_BS = 64
CONV_W = 4
RG_C = 8.0
EPS = 1e-6
NEG_INF = -1e30
EXPM1_SERIES_BELOW = 2.0 ** -11
SQRT_FLOOR = 1e-30

LANES = 128
SUBLANES = 8
MXU_DIM = 256
VMEM_LIMIT = 56 * 1024 * 1024

TM_MIX = 512
TM_MLP = 512
FF_CHUNK = 1024
SEQ_BLOCK = 16
PACE_LAG = 2
OUT_CHUNK = 256
SEG_PAD = 4


def _dot(a, b):
    return jnp.dot(a, b, preferred_element_type=F32)


def _dot_nt(a, b):
    return lax.dot_general(a, b, (((1,), (1,)), ((), ())), preferred_element_type=F32)


def _rms(x, g):
    ms = jnp.mean(x * x, axis=-1, keepdims=True)
    return x * lax.rsqrt(ms + EPS) * g


def _head_rms(x, ones_ref, g):
    ms = _dot((x * x).astype(BF16), ones_ref[...])
    return x * lax.rsqrt(ms + EPS) * g


def _lower_half(shape):
    return lax.broadcasted_iota(jnp.int32, shape, len(shape) - 1) < HEAD_DIM


def _scan_rows(a, b, h0, sub=None):
    n, d = a.shape
    a3 = a.reshape(n // SUBLANES, SUBLANES, d)
    b3 = b.reshape(n // SUBLANES, SUBLANES, d)
    if sub is None:
        sub = lax.broadcasted_iota(jnp.int32, (1, SUBLANES, d), 1)
    step = 1
    while step < SUBLANES:
        keep = sub >= step
        b3 = jnp.where(keep, a3 * pltpu.roll(b3, step, axis=1) + b3, b3)
        a3 = jnp.where(keep, a3 * pltpu.roll(a3, step, axis=1), a3)
        step *= 2
    hs = []
    for k in range(n // SUBLANES):
        hk = a3[k] * h0 + b3[k]
        hs.append(hk)
        h0 = hk[SUBLANES - 1:SUBLANES, :]
    return jnp.concatenate(hs, axis=0), h0


def _shift_rows(x, prev, k):
    rolled = pltpu.roll(x, k, axis=0)
    sub = lax.broadcasted_iota(jnp.int32, prev.shape, 0)
    head = jnp.where(sub < k, pltpu.roll(prev, k, axis=0), rolled[0:SUBLANES])
    return jnp.concatenate([head, rolled[SUBLANES:]], axis=0)


def _neg_expm1_2x(x, exp_x):
    return jnp.where(x > -EXPM1_SERIES_BELOW, (-2.0 * x) * (1.0 + x), 1.0 - exp_x * exp_x)


def _log_decay(lam):
    return -RG_C * jax.nn.softplus(-lam)


def _gate_math(ya, yx, xc, ba, bx, log_decay):
    r = jax.nn.sigmoid(ya + ba)
    i = jax.nn.sigmoid(yx + bx)
    log_a = log_decay * r
    a = jnp.exp(log_a)
    y = _neg_expm1_2x(log_a, a)
    bterm = (y * lax.rsqrt(jnp.maximum(y, SQRT_FLOOR))) * (i * xc)
    return a, bterm


def _rglru_gates(xc, bd_ref, ba, bx, lam):
    xcb = xc.astype(BF16)
    ya, yx = [], []
    for gi in range(xc.shape[1] // MXU_DIM):
        y = _dot(xcb[:, gi * MXU_DIM:(gi + 1) * MXU_DIM], bd_ref[gi])
        ya.append(y[:, :MXU_DIM])
        yx.append(y[:, MXU_DIM:])
    return _gate_math(jnp.concatenate(ya, axis=1), jnp.concatenate(yx, axis=1), xc, ba, bx,
                      _log_decay(lam))


def _prompt_mixer_kernel(x_ref, n1g_ref, wqkv_ref, win_ref, gq_ref, gk_ref, oq_ref, ok_ref,
                         bias_rows_ref, sink_ref, cw8_ref, cb8_ref, bd_ref, ba8_ref, bx8_ref, lam8_ref,
                         hg, att, gates, kwin_ref, vwin_ref, cst_ref, hst_ref,
                         bias_ref, proj, projx, kbuf, vbuf, xp, hc, logit_s, e_s, xc_s, y_s, hl_s, ac_s, hgn_s,
                         *, blocks_per_seq):
    tm, d = x_ref.shape
    n = pl.program_id(0)

    @pl.when(n == 0)
    def _():
        proj[1] = jnp.zeros(proj.shape[1:], F32)
        projx[1] = jnp.zeros(projx.shape[1:], F32)
        kbuf[...] = jnp.zeros(kbuf.shape, BF16)
        vbuf[...] = jnp.zeros(vbuf.shape, BF16)
        xp[...] = jnp.zeros(xp.shape, F32)
        hc[...] = jnp.zeros(hc.shape, F32)
        has_prev = lax.broadcasted_iota(jnp.int32, (2 * WINDOW, WINDOW), 0) >= WINDOW
        for h in range(N_HEADS):
            row = jnp.broadcast_to(bias_rows_ref[h:h + 1, :], (2 * WINDOW, bias_rows_ref.shape[1]))
            table = pltpu.roll(row, 0, 1, stride=1, stride_axis=0)[:, :WINDOW]
            at = (h // GROUP, slice(None), slice((h % GROUP) * WINDOW, (h % GROUP + 1) * WINDOW))
            bias_ref[(0,) + at] = table
            bias_ref[(1,) + at] = jnp.where(has_prev, table, NEG_INF)

    wslot = lax.rem(n, 2)
    rslot = 1 - wslot
    fresh = lax.rem(n + blocks_per_seq - 1, blocks_per_seq) == 0

    xn = _rms(x_ref[...], n1g_ref[...]).astype(BF16)
    n_qkv = wqkv_ref.shape[1]

    tokens = []

    def project(lo, width):
        seg, pitch = tm // SUBLANES, tm // SUBLANES + SEG_PAD
        res = _dot(xn, (wqkv_ref if lo < n_qkv else win_ref)[:, lo:lo + width])
        tokens.append(res[0:1, :MXU_DIM])
        if lo < n_qkv:
            for i in range(width // LANES):
                proj[wslot, lo // LANES + i] = res[:, i * LANES:(i + 1) * LANES]
        elif lo < REST0 + d:
            for i in range(width // LANES):
                for sgm in range(SUBLANES):
                    projx[wslot, (lo - REST0) // LANES + i, sgm * pitch:sgm * pitch + seg, :] = (
                        res[sgm * seg:(sgm + 1) * seg, i * LANES:(i + 1) * LANES])
        else:
            gates[:, lo - REST0 - d:lo - REST0 - d + width] = res.astype(BF16)

    pending = ([(lo, MXU_DIM) for lo in range(0, REST0 + d, MXU_DIM)]
               + [(lo, OUT_CHUNK) for lo in range(REST0 + d, win_ref.shape[1], OUT_CHUNK)])

    def pace():
        if len(tokens) < PACE_LAG:
            return jnp.zeros((1, MXU_DIM), F32)
        bits = lax.bitcast_convert_type(tokens[-PACE_LAG], jnp.uint32)
        return lax.bitcast_convert_type((bits >> 16) >> 16, F32)

    def emit_stage1(count):
        for _ in range(min(count, len(pending))):
            project(*pending.pop(0))

    def kept(lo, hi, rows=slice(None)):
        return jnp.concatenate([proj[rslot, t, rows, :] for t in range(lo // LANES, hi // LANES)], axis=1)

    emit_stage1(2)
    qn = _head_rms(kept(0, Q_W), oq_ref, gq_ref[...]).astype(BF16)
    kn = _head_rms(kept(Q_W, Q_W + KV_W), ok_ref, gk_ref[...])
    v = kept(Q_W + KV_W, n_qkv)
    kbuf[0:WINDOW, :] = jnp.where(fresh, jnp.zeros((WINDOW, KV_W), BF16), kbuf[tm:tm + WINDOW, :])
    vbuf[:, 0:WINDOW] = jnp.where(fresh, jnp.zeros((KV_W, WINDOW), BF16), vbuf[:, tm:tm + WINDOW])
    kbuf[WINDOW:WINDOW + tm, :] = kn.astype(BF16)
    vbuf[:, WINDOW:WINDOW + tm] = v.T.astype(BF16)
    kwin_ref[...] = kn[tm - WINDOW:, :]
    vwin_ref[...] = v[tm - WINDOW:, :]

    first = jnp.where(fresh, 1, 0)
    lower = _lower_half((WINDOW, LANES))
    zero = jnp.zeros((WINDOW, LANES), BF16)

    def attend(s):
        rows = slice(s * WINDOW, (s + 1) * WINDOW)
        keys = kbuf[s * WINDOW:(s + 2) * WINDOW, :]
        vals_t = vbuf[:, s * WINDOW:(s + 2) * WINDOW]
        outs = []
        for g in range(KV_HEADS):
            qs = []
            for j in range(GROUP):
                slab = qn[rows, j * LANES:(j + 1) * LANES]
                qs.append(jnp.where(lower, slab, zero) if g == 0 else jnp.where(lower, zero, slab))
            logit_s[g] = _dot_nt(keys, jnp.concatenate(qs, axis=0))
            yield
            inv = []
            for j in range(GROUP):
                blk = slice(j * WINDOW, (j + 1) * WINDOW)
                bias = bias_ref[first, g, :, blk] if s == 0 else bias_ref[0, g, :, blk]
                logit = logit_s[g, :, blk] + bias
                sink = sink_ref[g, :, blk] + pace()[:, :WINDOW]
                m = jnp.maximum(jnp.max(logit, axis=0, keepdims=True), sink)
                e = jnp.exp(logit - m)
                inv.append(1.0 / (jnp.sum(e, axis=0, keepdims=True) + jnp.exp(sink - m)))
                e_s[g, :, blk] = e.astype(BF16)
                yield
            outs.append(_dot(vals_t, e_s[g]) * jnp.concatenate(inv, axis=1))
        for j in range(GROUP):
            blk = slice(j * WINDOW, (j + 1) * WINDOW)
            both = jnp.concatenate([outs[0][:HEAD_DIM, blk], outs[1][HEAD_DIM:, blk]], axis=0)
            att[rows, j * LANES:(j + 1) * LANES] = both.T.astype(BF16)
        yield

    seg = tm // SUBLANES
    pitch = seg + SEG_PAD
    sub8 = lax.broadcasted_iota(jnp.int32, (SUBLANES, MXU_DIM), 0)

    def recur(gi):
        cols = slice(gi * MXU_DIM, (gi + 1) * MXU_DIM)
        tiles = range(gi * (MXU_DIM // LANES), (gi + 1) * (MXU_DIM // LANES))
        by_segment = lambda k: pl.ds(k, SUBLANES, stride=pitch)
        x_at = lambda k: jnp.concatenate([projx[rslot, t, by_segment(k), :] for t in tiles], axis=1)
        last = SUBLANES * pitch - SEG_PAD
        x_tail = lambda n_rows: jnp.concatenate(
            [projx[rslot, t, last - n_rows:last, :] for t in tiles], axis=1)
        prev = jnp.where(fresh, 0.0, xp[:, cols])
        window = [jnp.where(sub8 == 0, prev[SUBLANES - j:SUBLANES - j + 1, :],
                            pltpu.roll(x_at(seg - j), 1, axis=0)) for j in range(CONV_W - 1, 0, -1)]
        for k in range(seg):
            window.append(x_at(k))
            xc_s[gi, k * SUBLANES:(k + 1) * SUBLANES, :] = cb8_ref[:, cols] + sum(
                cw8_ref[j, :, cols] * window[j] for j in range(CONV_W))
            window.pop(0)
            if k % 16 == 15:
                yield
        xp[:, cols] = x_tail(SUBLANES)
        cst_ref[:, cols] = x_tail(CONV_W - 1)
        y_s[gi] = _dot(xc_s[gi].astype(BF16), bd_ref[gi])
        yield
        ba, bx = ba8_ref[:, cols], bx8_ref[:, cols]
        log_decay = _log_decay(lam8_ref[:, cols])
        h_loc = a_cum = None
        for k in range(seg):
            r = slice(k * SUBLANES, (k + 1) * SUBLANES)
            a, bterm = _gate_math(y_s[gi, r, :MXU_DIM], y_s[gi, r, MXU_DIM:], xc_s[gi, r, :] + pace(),
                                  ba, bx, log_decay)
            h_loc = bterm if k == 0 else a * h_loc + bterm
            a_cum = a if k == 0 else a * a_cum
            hl_s[gi, r, :] = h_loc
            ac_s[gi, r, :] = a_cum
            if k % 4 == 3:
                yield
        h_in = jnp.where(fresh, 0.0, hc[:, cols])
        carry_in = []
        for sgm in range(SUBLANES):
            carry_in.append(h_in)
            h_in = h_loc[sgm:sgm + 1, :] + a_cum[sgm:sgm + 1, :] * h_in
        hc[:, cols] = h_in
        hst_ref[:, cols] = h_in
        carry_in = jnp.concatenate(carry_in, axis=0)
        for k in range(seg):
            r = slice(k * SUBLANES, (k + 1) * SUBLANES)
            h = hl_s[gi, r, :] + ac_s[gi, r, :] * carry_in
            for i in range(MXU_DIM // LANES):
                hgn_s[gi, i, by_segment(k), :] = h[:, i * LANES:(i + 1) * LANES]
            if k % 8 == 7:
                yield
        hg[:, cols] = jnp.concatenate(
            [jnp.concatenate([hgn_s[gi, i, sgm * pitch:sgm * pitch + seg, :] for sgm in range(SUBLANES)], axis=0)
             for i in range(MXU_DIM // LANES)], axis=1).astype(BF16)
        yield

    n_sub = tm // WINDOW
    n_grp = d // MXU_DIM
    pieces = []
    for u in range(max(n_sub, n_grp)):
        if u < n_sub:
            pieces.append(attend(u))
        if u < n_grp:
            pieces.append(recur(u))
    n_spots = n_sub * (KV_HEADS * (1 + GROUP) + 1) + n_grp * (2 + seg // 16 + seg // 4 + seg // 8)
    per_spot = len(pending) / n_spots
    due = 0.0
    for piece in pieces:
        for _ in piece:
            due += per_spot
            emit_stage1(int(due))
            due -= int(due)
    emit_stage1(len(pending))


def _const_spec(shape):
    nd = len(shape)
    return pl.BlockSpec(shape, lambda *_: (0,) * nd, pipeline_mode=pl.Buffered(1))


def _prompt_mixer(x, w):
    b, t, d = x.shape
    tm = TM_MIX
    nt = t // tm
    n_blocks = b * nt
    consts = [w['n1g'], w['wqkv'], w['win'], w['gq'], w['gk'], w['oq'], w['ok'],
              w['bias_p'], w['sink_p'], w['cw8'], w['cb8'], w['bd'], w['ba8'], w['bx8'], w['lam8']]
    out_shape = (
        jax.ShapeDtypeStruct((b, t, d), BF16),
        jax.ShapeDtypeStruct((b, t, Q_W), BF16),
        jax.ShapeDtypeStruct((b, t, 3 * d), BF16),
        jax.ShapeDtypeStruct((b, WINDOW, KV_W), F32),
        jax.ShapeDtypeStruct((b, WINDOW, KV_W), F32),
        jax.ShapeDtypeStruct((b, CONV_W - 1, d), F32),
        jax.ShapeDtypeStruct((b, 1, d), F32),
    )
    def cur(n):
        i = jnp.minimum(n, n_blocks - 1)
        return (i // nt, i % nt, 0)

    def prv(n):
        i = jnp.maximum(n - 1, 0)
        return (i // nt, i % nt, 0)

    per_seq = lambda n: (jnp.maximum(n - 1, 0) // nt, 0, 0)
    return pl.pallas_call(
        functools.partial(_prompt_mixer_kernel, blocks_per_seq=nt),
        out_shape=out_shape,
        grid=(n_blocks + 1,),
        in_specs=[pl.BlockSpec((None, tm, d), cur)]
                 + [_const_spec(c.shape) for c in consts],
        out_specs=(
            pl.BlockSpec((None, tm, d), prv),
            pl.BlockSpec((None, tm, Q_W), prv),
            pl.BlockSpec((None, tm, 3 * d), cur),
            pl.BlockSpec((None, WINDOW, KV_W), per_seq),
            pl.BlockSpec((None, WINDOW, KV_W), per_seq),
            pl.BlockSpec((None, CONV_W - 1, d), per_seq),
            pl.BlockSpec((None, 1, d), per_seq),
        ),
        scratch_shapes=[
            pltpu.VMEM((2, KV_HEADS, 2 * WINDOW, GROUP * WINDOW), F32),
            pltpu.VMEM((2, REST0 // LANES, tm, LANES), F32),
            pltpu.VMEM((2, d // LANES, SUBLANES * (tm // SUBLANES + SEG_PAD), LANES), F32),
            pltpu.VMEM((WINDOW + tm, KV_W), BF16),
            pltpu.VMEM((KV_W, WINDOW + tm), BF16),
            pltpu.VMEM((SUBLANES, d), F32),
            pltpu.VMEM((1, d), F32),
            pltpu.VMEM((KV_HEADS, 2 * WINDOW, GROUP * WINDOW), F32),
            pltpu.VMEM((KV_HEADS, 2 * WINDOW, GROUP * WINDOW), BF16),
            pltpu.VMEM((d // MXU_DIM, tm, MXU_DIM), F32),
            pltpu.VMEM((d // MXU_DIM, tm, 2 * MXU_DIM), F32),
            pltpu.VMEM((d // MXU_DIM, tm, MXU_DIM), F32),
            pltpu.VMEM((d // MXU_DIM, tm, MXU_DIM), F32),
            pltpu.VMEM((d // MXU_DIM, MXU_DIM // LANES, SUBLANES * (tm // SUBLANES + SEG_PAD), LANES), F32),
        ],
        compiler_params=pltpu.CompilerParams(
            dimension_semantics=("arbitrary",), vmem_limit_bytes=VMEM_LIMIT),
        name="prompt_mixer",
    )(x, *consts)


def _channel_kernel(x_ref, h_ref, att_ref, pre_ref, p_ref, woa_ref, wor_ref, wout_ref,
                    n2g_ref, wup_ref, wdn_ref, pg_ref, wpg_ref, wple_ref, o_ref):
    d = x_ref.shape[1]
    pre = pre_ref[...].astype(F32)
    hg = (h_ref[...].astype(F32) * jax.nn.gelu(pre[:, :d])).astype(BF16)
    rnn = _dot(hg, wor_ref[...])
    atto = _dot(att_ref[...], woa_ref[...])
    mix = (jax.nn.sigmoid(pre[:, d:2 * d]) * atto + jax.nn.sigmoid(pre[:, 2 * d:]) * rnn).astype(BF16)
    x = x_ref[...] + _dot(mix, wout_ref[...])
    xn = _rms(x, n2g_ref[...]).astype(BF16)
    acc = x
    for c in range(wup_ref.shape[1] // FF_CHUNK):
        cols = slice(c * FF_CHUNK, (c + 1) * FF_CHUNK)
        hmid = jnp.maximum(_dot(xn, wup_ref[:, cols]), 0.0)
        acc = acc + _dot((hmid * hmid).astype(BF16), wdn_ref[cols, :])
    gate = jax.nn.sigmoid(_dot(_rms(acc, pg_ref[...]).astype(BF16), wpg_ref[...]))
    o_ref[...] = acc + gate * _dot(p_ref[...].astype(BF16), wple_ref[...])


def _channel(x, hg, att, gates, p, w):
    m, d = x.shape
    tm = min(TM_MLP, m)
    consts = [w['woa'], w['wor'], w['wout'], w['n2g'], w['wup'], w['wdn'], w['pg'], w['wpg'], w['wple']]
    rows = lambda a: pl.BlockSpec((tm, a.shape[1]), lambda i: (i, 0))
    return pl.pallas_call(
        _channel_kernel,
        out_shape=jax.ShapeDtypeStruct((m, d), F32),
        grid=(m // tm,),
        in_specs=[rows(a) for a in (x, hg, att, gates, p)] + [_const_spec(c.shape) for c in consts],
        out_specs=rows(x),
        compiler_params=pltpu.CompilerParams(
            dimension_semantics=("arbitrary",), vmem_limit_bytes=VMEM_LIMIT),
        name="merge_mlp_ple",
    )(x, hg, att, gates, p, *consts)


def _decode_attn_kernel(x_ref, ck_ref, cv_ref, n1g_ref, wqkv_ref, gq_ref, gk_ref, oq_ref, ok_ref,
                        bias_c_ref, bias_n_ref, sink_ref,
                        att_ref, kwin_ref, vwin_ref):
    sb = ck_ref.shape[0]
    nt = x_ref.shape[0] // sb
    xn = _rms(x_ref[...], n1g_ref[...]).astype(BF16)
    qkv = _dot(xn, wqkv_ref[...])
    qn = _head_rms(qkv[:, :Q_W], oq_ref, gq_ref[...])
    kn = _head_rms(qkv[:, Q_W:Q_W + KV_W], ok_ref, gk_ref[...])
    v = qkv[:, Q_W + KV_W:]

    n_rows = sb * nt
    ck = ck_ref[...]
    cv = cv_ref[...]
    for b in range(sb):
        rows = slice(b * nt, (b + 1) * nt)
        kwin_ref[b, 0:WINDOW - nt, :] = ck[b, nt:, :]
        vwin_ref[b, 0:WINDOW - nt, :] = cv[b, nt:, :]
        kwin_ref[b, WINDOW - nt:WINDOW, :] = kn[rows]
        vwin_ref[b, WINDOW - nt:WINDOW, :] = v[rows]

    qb = qn.astype(BF16)
    lower = _lower_half((n_rows, LANES))
    zero = jnp.zeros((n_rows, LANES), BF16)
    q_all = jnp.concatenate(
        [jnp.where(lower, qb[:, j * LANES:(j + 1) * LANES], zero) if g == 0
         else jnp.where(lower, zero, qb[:, j * LANES:(j + 1) * LANES])
         for g in range(KV_HEADS) for j in range(GROUP)], axis=0)
    n_col = q_all.shape[0]
    col_seq = (lax.broadcasted_iota(jnp.int32, (1, n_col), 1) // nt) % sb

    def own(big):
        out = big[0:WINDOW]
        for b in range(1, sb):
            out = jnp.where(col_seq == b, big[b * WINDOW:(b + 1) * WINDOW], out)
        return out

    st = own(_dot_nt(ck.reshape(sb * WINDOW, KV_W).astype(BF16), q_all)) + bias_c_ref[...]
    stx = _dot_nt(kn.astype(BF16), q_all) + bias_n_ref[...]
    sink = sink_ref[...]
    m = jnp.maximum(jnp.maximum(jnp.max(st, axis=0, keepdims=True),
                                jnp.max(stx, axis=0, keepdims=True)), sink)
    e = jnp.exp(st - m)
    ex = jnp.exp(stx - m)
    den = (jnp.sum(e, axis=0, keepdims=True) + jnp.sum(ex, axis=0, keepdims=True)
           + jnp.exp(sink - m))
    cv_t = jnp.concatenate([cv[b].T for b in range(sb)], axis=0).astype(BF16)
    out = own(_dot(cv_t, e.astype(BF16)))
    pad = WINDOW - n_rows
    v_t = jnp.concatenate([v, jnp.zeros((pad, KV_W), F32)], axis=0).T.astype(BF16)
    ex_pad = jnp.concatenate([ex, jnp.zeros((pad, n_col), F32)], axis=0).astype(BF16)
    out = (out + _dot(v_t, ex_pad)) / den
    half = n_col // KV_HEADS
    both_t = jnp.concatenate([out[:HEAD_DIM, :half], out[HEAD_DIM:, half:]], axis=0).T
    for j in range(GROUP):
        att_ref[:, j * LANES:(j + 1) * LANES] = both_t[j * n_rows:(j + 1) * n_rows].astype(att_ref.dtype)


def _decode_attn(x, ck, cv, w):
    m, d = x.shape
    nb = ck.shape[0]
    nt = m // nb
    sb = SEQ_BLOCK
    consts = [w['n1g'], w['wqkv'], w['gq'], w['gk'], w['oq'], w['ok'],
              w['bias_sc'], w['bias_sn'], w['sink_s']]
    cache_spec = pl.BlockSpec((sb, WINDOW, KV_W), lambda i: (i, 0, 0))
    return pl.pallas_call(
        _decode_attn_kernel,
        out_shape=(jax.ShapeDtypeStruct((m, Q_W), BF16),
                   jax.ShapeDtypeStruct(ck.shape, F32),
                   jax.ShapeDtypeStruct(cv.shape, F32)),
        grid=(nb // sb,),
        in_specs=[pl.BlockSpec((sb * nt, d), lambda i: (i, 0)), cache_spec, cache_spec]
                 + [_const_spec(c.shape) for c in consts],
        out_specs=(pl.BlockSpec((sb * nt, Q_W), lambda i: (i, 0)), cache_spec, cache_spec),
        compiler_params=pltpu.CompilerParams(
            dimension_semantics=("arbitrary",), vmem_limit_bytes=VMEM_LIMIT),
        name="decode_attn",
    )(x, ck, cv, *consts)


def _decode_mixer_kernel(x_ref, cst_ref, h0_ref, n1g_ref, win_ref, cw_ref, cb_ref, bd_ref,
                         ba_ref, bx_ref, lam_ref,
                         hg_ref, gates_ref, cnew_ref, hnew_ref):
    nb = h0_ref.shape[0]
    nt = x_ref.shape[0] // nb
    d = x_ref.shape[1]
    xn = _rms(x_ref[...], n1g_ref[...]).astype(BF16)
    xr = _dot(xn, win_ref[:, REST0:REST0 + d])
    prev = cst_ref[...]
    slabs = [prev[k * nb:(k + 1) * nb] for k in range(CONV_W - 1)]
    slabs += [xr[k * nb:(k + 1) * nb] for k in range(nt)]
    xc = jnp.concatenate(
        [cb_ref[...] + sum(cw_ref[j:j + 1, :] * slabs[ti + j] for j in range(CONV_W))
         for ti in range(nt)], axis=0)
    cnew_ref[...] = jnp.concatenate(slabs[-(CONV_W - 1):], axis=0)

    a, bterm = _rglru_gates(xc, bd_ref, ba_ref[...], bx_ref[...], lam_ref[...])
    h = h0_ref[...]
    hs = []
    for ti in range(nt):
        h = a[ti * nb:(ti + 1) * nb] * h + bterm[ti * nb:(ti + 1) * nb]
        hs.append(h)
    hnew_ref[...] = h
    hg_ref[...] = jnp.concatenate(hs, axis=0).astype(BF16)
    gates_ref[...] = _dot(xn, win_ref[:, REST0 + d:REST0 + 4 * d]).astype(BF16)


def _decode_mixer(x, cst, h0, w):
    m, d = x.shape
    consts = [w['n1g'], w['win'], w['cw'], w['cb'], w['bd'], w['ba'], w['bx'], w['lam']]
    ins = [x, cst, h0] + consts
    whole = lambda shape: pl.BlockSpec(shape, lambda i: (0,) * len(shape))
    return pl.pallas_call(
        _decode_mixer_kernel,
        out_shape=(jax.ShapeDtypeStruct((m, d), BF16),
                   jax.ShapeDtypeStruct((m, 3 * d), BF16),
                   jax.ShapeDtypeStruct(cst.shape, F32),
                   jax.ShapeDtypeStruct(h0.shape, F32)),
        grid=(1,),
        in_specs=[_const_spec(c.shape) for c in ins],
        out_specs=(whole((m, d)), whole((m, 3 * d)), whole(cst.shape), whole(h0.shape)),
        compiler_params=pltpu.CompilerParams(
            dimension_semantics=("arbitrary",), vmem_limit_bytes=VMEM_LIMIT),
        name="decode_mixer",
    )(*ins)


def _rel_bucket(dist):
    n = np.maximum(dist, 0)
    max_exact = REL_BUCKETS // 2
    nf = np.maximum(n, 1).astype(np.float32)
    large = max_exact + (np.log(nf / max_exact) / math.log(REL_MAX_DIST / max_exact)
                         * (REL_BUCKETS - max_exact)).astype(np.int32)
    large = np.minimum(large, REL_BUCKETS - 1)
    return np.where(n < max_exact, n, large)


def _bias_rows(rel_bias, dist, mask):
    tb = jnp.where(mask[:, :, None], rel_bias[_rel_bucket(dist)].astype(F32), NEG_INF)
    tq, tk = dist.shape
    return jnp.transpose(tb, (2, 0, 1)).reshape(N_HEADS * tq, tk)


def _prompt_bias_rows(rel_bias):
    span = 3 * WINDOW
    k = np.arange(span)
    dist = WINDOW + np.where(k < WINDOW, k, k - span)
    valid = (dist >= 0) & (dist <= WINDOW)
    return jnp.where(valid[:, None], rel_bias[_rel_bucket(dist)].astype(F32), NEG_INF).T


def _gate_blocks(rg_wa, rg_wx, n_grp):
    per = rg_wa.shape[0] // n_grp
    both = jnp.stack([rg_wa, rg_wx]).reshape(2, n_grp, per, RNN_BS, RNN_BS)
    on_diag = np.eye(per, dtype=bool)[None, None, :, None, :, None]
    blocks = jnp.where(on_diag, both[:, :, :, :, None, :], 0.0)
    blocks = jnp.transpose(blocks, (1, 2, 3, 0, 4, 5))
    return blocks.reshape(n_grp, per * RNN_BS, 2 * per * RNN_BS).astype(BF16)


def _head_avg(width):
    idx = np.arange(width) // HEAD_DIM
    return jnp.asarray((idx[:, None] == idx[None, :]).astype(np.float32) / HEAD_DIM, BF16)


def _prepare(rel_bias, norm1_g, w_in, q_norm_g, k_norm_g, sinks, w_o_attn, conv_w, conv_b,
             rg_wa, rg_ba, rg_wx, rg_bx, rg_lambda, w_o_rnn, w_out, norm2_g, w_up, w_down,
             ple_norm_g, w_ple_gate, w_ple, n_dec):
    d = w_in.shape[0]
    order = [g * GROUP + j for j in range(GROUP) for g in range(KV_HEADS)]
    perm = np.concatenate([np.arange(h * HEAD_DIM, (h + 1) * HEAD_DIM) for h in order])
    row = lambda a: a.reshape(1, -1).astype(F32)
    w = {}
    w['n1g'] = row(norm1_g)
    w['win'] = w_in.astype(BF16)
    w['wqkv'] = jnp.concatenate([w['win'][:, :Q_W][:, perm], w['win'][:, Q_W:REST0]], axis=1)
    w['gq'] = row(jnp.tile(q_norm_g, N_HEADS)) * (HEAD_DIM ** -0.5)
    w['gk'] = row(jnp.tile(k_norm_g, KV_HEADS))
    w['oq'] = _head_avg(Q_W)
    w['ok'] = _head_avg(KV_W)
    w['woa'] = w_o_attn[perm, :].astype(BF16)
    w['cw'] = conv_w.astype(F32)
    w['cb'] = row(conv_b)
    w['bd'] = _gate_blocks(rg_wa, rg_wx, d // MXU_DIM)
    w['ba'] = row(rg_ba)
    w['bx'] = row(rg_bx)
    w['lam'] = row(rg_lambda)
    rows8 = lambda a: jnp.broadcast_to(a, (SUBLANES, a.shape[-1]))
    w['cw8'] = jnp.broadcast_to(w['cw'][:, None, :], (CONV_W, SUBLANES, d))
    w['cb8'], w['ba8'], w['bx8'], w['lam8'] = (rows8(w[k]) for k in ('cb', 'ba', 'bx', 'lam'))
    w['wor'] = w_o_rnn.astype(BF16)
    w['wout'] = w_out.astype(BF16)
    w['n2g'] = row(norm2_g)
    w['wup'] = w_up.astype(BF16)
    w['wdn'] = w_down.astype(BF16)
    w['pg'] = row(ple_norm_g)
    w['wpg'] = w_ple_gate.astype(BF16)
    w['wple'] = w_ple.astype(BF16)

    w['bias_p'] = _prompt_bias_rows(rel_bias)
    sink_rows = sinks.astype(F32).reshape(KV_HEADS, GROUP, 1)
    w['sink_p'] = jnp.repeat(sink_rows, WINDOW, axis=2).reshape(KV_HEADS, 1, GROUP * WINDOW)

    sb = SEQ_BLOCK
    ti = np.arange(n_dec)[:, None]
    dist_c = ti + WINDOW - np.arange(WINDOW)[None, :]
    dist_n = ti - np.arange(n_dec)[None, :]
    n_col = N_HEADS * sb * n_dec
    col = np.arange(n_col)
    col_head_t = col // (sb * n_dec) * n_dec + col % n_dec
    spread = (np.arange(N_HEADS * n_dec)[:, None] == col_head_t[None, :]).astype(np.float32)
    expand = lambda a: jnp.dot(a.T, spread, precision=lax.Precision.HIGHEST)
    w['bias_sc'] = expand(_bias_rows(rel_bias, dist_c, dist_c <= WINDOW))
    bias_n = expand(_bias_rows(rel_bias, dist_n, dist_n >= 0))
    own = (np.arange(sb * n_dec) // n_dec)[:, None] == (col // n_dec % sb)[None, :]
    w['bias_sn'] = jnp.where(own, jnp.tile(bias_n, (sb, 1)), NEG_INF)
    w['sink_s'] = jnp.repeat(sinks.astype(F32), sb * n_dec)[None, :]
    return w


def kernel(x_prompt, x_sample, cache_k_win, cache_v_win, state_conv, state_h, p_prompt, p_sample, rel_bias, norm1_g, w_in, q_norm_g, k_norm_g, sinks, w_o_attn, conv_w, conv_b, rg_wa, rg_ba, rg_wx, rg_bx, rg_lambda, w_o_rnn, w_out, norm2_g, w_up, w_down, ple_norm_g, w_ple_gate, w_ple):
    depth = w_in.shape[0]
    assert depth == 1, "single-layer step"
    b, t, d = x_prompt.shape
    nb, nt, _ = x_sample.shape
    w = _prepare(rel_bias, norm1_g[0], w_in[0], q_norm_g[0], k_norm_g[0], sinks[0], w_o_attn[0],
                 conv_w[0], conv_b[0], rg_wa[0], rg_ba[0], rg_wx[0], rg_bx[0], rg_lambda[0],
                 w_o_rnn[0], w_out[0], norm2_g[0], w_up[0], w_down[0], ple_norm_g[0],
                 w_ple_gate[0], w_ple[0], nt)

    hgp, attp, gatesp, kp, vp, cp, hp = _prompt_mixer(x_prompt, w)
    flat = lambda a: a.reshape(b * t, a.shape[-1])
    yp = _channel(flat(x_prompt), flat(hgp), flat(attp), flat(gatesp), flat(p_prompt[0]), w).reshape(b, t, d)

    att, ks, vs = _decode_attn(x_sample.reshape(nb * nt, d),
                               cache_k_win[0].reshape(nb, WINDOW, KV_W),
                               cache_v_win[0].reshape(nb, WINDOW, KV_W), w)
    to_tm = lambda a: jnp.swapaxes(a, 0, 1).reshape(-1, a.shape[-1])
    xs_tm = to_tm(x_sample)
    hgs, gatess, cs, hs = _decode_mixer(xs_tm, to_tm(state_conv[0]), state_h[0], w)
    ys = _channel(xs_tm, hgs, to_tm(att.reshape(nb, nt, Q_W)), gatess, to_tm(p_sample[0]), w)
    from_tm = lambda a, n: jnp.swapaxes(a.reshape(n, nb, a.shape[-1]), 0, 1)

    kv_shape = (1, -1, WINDOW, KV_HEADS, HEAD_DIM)
    return (yp, from_tm(ys, nt),
            kp.reshape(kv_shape), vp.reshape(kv_shape), cp[None], hp.reshape(1, b, d),
            ks.reshape(kv_shape), vs.reshape(kv_shape), from_tm(cs, CONV_W - 1)[None], hs[None])
```

```python
import functools
import math

import numpy as np
import jax
import jax.numpy as jnp
from jax import lax
from jax.experimental import pallas as pl
from jax.experimental.pallas import tpu as pltpu

F32 = jnp.float32
BF16 = jnp.bfloat16

N_HEADS = 8
KV_HEADS = 2
GROUP = N_HEADS // KV_HEADS
HEAD_DIM = 64
Q_W = N_HEADS * HEAD_DIM
KV_W = KV_HEADS * HEAD_DIM
REST0 = Q_W + 2 * KV_W
WINDOW = 128
REL_BUCKETS = 32
REL_MAX_DIST = 128
RNN_BS = 64
CONV_W = 4
RG_C = 8.0
EPS = 1e-6
NEG_INF = -1e30
EXPM1_SERIES_BELOW = 2.0 ** -11
SQRT_FLOOR = 1e-30

LANES = 128
SUBLANES = 8
MXU_DIM = 256
VMEM_LIMIT = 56 * 1024 * 1024

TM_MIX = 512
TM_MLP = 512
FF_CHUNK = 1024
SEQ_BLOCK = 16
PACE_LAG = 2
OUT_CHUNK = 256
SEG_PAD = 4


def _dot(a, b):
    return jnp.dot(a, b, preferred_element_type=F32)


def _dot_nt(a, b):
    return lax.dot_general(a, b, (((1,), (1,)), ((), ())), preferred_element_type=F32)


def _rms(x, g):
    ms = jnp.mean(x * x, axis=-1, keepdims=True)
    return x * lax.rsqrt(ms + EPS) * g


def _head_rms(x, ones_ref, g):
    ms = _dot((x * x).astype(BF16), ones_ref[...])
    return x * lax.rsqrt(ms + EPS) * g


def _lower_half(shape):
    return lax.broadcasted_iota(jnp.int32, shape, len(shape) - 1) < HEAD_DIM


def _scan_rows(a, b, h0, sub=None):
    n, d = a.shape
    a3 = a.reshape(n // SUBLANES, SUBLANES, d)
    b3 = b.reshape(n // SUBLANES, SUBLANES, d)
    if sub is None:
        sub = lax.broadcasted_iota(jnp.int32, (1, SUBLANES, d), 1)
    step = 1
    while step < SUBLANES:
        keep = sub >= step
        b3 = jnp.where(keep, a3 * pltpu.roll(b3, step, axis=1) + b3, b3)
        a3 = jnp.where(keep, a3 * pltpu.roll(a3, step, axis=1), a3)
        step *= 2
    hs = []
    for k in range(n // SUBLANES):
        hk = a3[k] * h0 + b3[k]
        hs.append(hk)
        h0 = hk[SUBLANES - 1:SUBLANES, :]
    return jnp.concatenate(hs, axis=0), h0


def _shift_rows(x, prev, k):
    rolled = pltpu.roll(x, k, axis=0)
    sub = lax.broadcasted_iota(jnp.int32, prev.shape, 0)
    head = jnp.where(sub < k, pltpu.roll(prev, k, axis=0), rolled[0:SUBLANES])
    return jnp.concatenate([head, rolled[SUBLANES:]], axis=0)


def _neg_expm1_2x(x, exp_x):
    return jnp.where(x > -EXPM1_SERIES_BELOW, (-2.0 * x) * (1.0 + x), 1.0 - exp_x * exp_x)


def _log_decay(lam):
    return -RG_C * jax.nn.softplus(-lam)


def _gate_math(ya, yx, xc, ba, bx, log_decay):
    r = jax.nn.sigmoid(ya + ba)
    i = jax.nn.sigmoid(yx + bx)
    log_a = log_decay * r
    a = jnp.exp(log_a)
    y = _neg_expm1_2x(log_a, a)
    bterm = (y * lax.rsqrt(jnp.maximum(y, SQRT_FLOOR))) * (i * xc)
    return a, bterm


def _rglru_gates(xc, bd_ref, ba, bx, lam):
    xcb = xc.astype(BF16)
    ya, yx = [], []
    for gi in range(xc.shape[1] // MXU_DIM):
        y = _dot(xcb[:, gi * MXU_DIM:(gi + 1) * MXU_DIM], bd_ref[gi])
        ya.append(y[:, :MXU_DIM])
        yx.append(y[:, MXU_DIM:])
    return _gate_math(jnp.concatenate(ya, axis=1), jnp.concatenate(yx, axis=1), xc, ba, bx,
                      _log_decay(lam))


def _prompt_mixer_kernel(x_ref, n1g_ref, wqkv_ref, win_ref, cw8_ref, cb8_ref, bd_ref, ba8_ref, bx8_ref,
                         lam8_ref,
                         hg, qkv_out, gates, cst_ref, hst_ref,
                         projx, xp, hc, xc_s, y_s, hl_s, ac_s, hgn_s, *, blocks_per_seq):
    tm, d = x_ref.shape
    n = pl.program_id(0)

    @pl.when(n == 0)
    def _():
        projx[1] = jnp.zeros(projx.shape[1:], F32)
        xp[...] = jnp.zeros(xp.shape, F32)
        hc[...] = jnp.zeros(hc.shape, F32)

    wslot = lax.rem(n, 2)
    rslot = 1 - wslot
    fresh = lax.rem(n + blocks_per_seq - 1, blocks_per_seq) == 0

    xn = _rms(x_ref[...], n1g_ref[...]).astype(BF16)
    n_qkv = wqkv_ref.shape[1]

    tokens = []

    def project(lo, width):
        seg, pitch = tm // SUBLANES, tm // SUBLANES + SEG_PAD
        res = _dot(xn, (wqkv_ref if lo < n_qkv else win_ref)[:, lo:lo + width])
        tokens.append(res[0:1, :MXU_DIM])
        if lo < n_qkv:
            qkv_out[:, lo:lo + width] = res
        elif lo < REST0 + d:
            for i in range(width // LANES):
                for sgm in range(SUBLANES):
                    projx[wslot, (lo - REST0) // LANES + i, sgm * pitch:sgm * pitch + seg, :] = (
                        res[sgm * seg:(sgm + 1) * seg, i * LANES:(i + 1) * LANES])
        else:
            gates[:, lo - REST0 - d:lo - REST0 - d + width] = res.astype(BF16)

    pending = ([(lo, MXU_DIM) for lo in range(0, REST0 + d, MXU_DIM)]
               + [(lo, OUT_CHUNK) for lo in range(REST0 + d, win_ref.shape[1], OUT_CHUNK)])

    def pace():
        if len(tokens) < PACE_LAG:
            return jnp.zeros((1, MXU_DIM), F32)
        bits = lax.bitcast_convert_type(tokens[-PACE_LAG], jnp.uint32)
        return lax.bitcast_convert_type((bits >> 16) >> 16, F32)

    def emit_stage1(count):
        for _ in range(min(count, len(pending))):
            project(*pending.pop(0))

    emit_stage1(2)
    seg = tm // SUBLANES
    pitch = seg + SEG_PAD
    sub8 = lax.broadcasted_iota(jnp.int32, (SUBLANES, MXU_DIM), 0)

    def recur(gi):
        cols = slice(gi * MXU_DIM, (gi + 1) * MXU_DIM)
        tiles = range(gi * (MXU_DIM // LANES), (gi + 1) * (MXU_DIM // LANES))
        by_segment = lambda k: pl.ds(k, SUBLANES, stride=pitch)
        x_at = lambda k: jnp.concatenate([projx[rslot, t, by_segment(k), :] for t in tiles], axis=1)
        last = SUBLANES * pitch - SEG_PAD
        x_tail = lambda n_rows: jnp.concatenate(
            [projx[rslot, t, last - n_rows:last, :] for t in tiles], axis=1)
        prev = jnp.where(fresh, 0.0, xp[:, cols])
        window = [jnp.where(sub8 == 0, prev[SUBLANES - j:SUBLANES - j + 1, :],
                            pltpu.roll(x_at(seg - j), 1, axis=0)) for j in range(CONV_W - 1, 0, -1)]
        for k in range(seg):
            window.append(x_at(k))
            xc_s[gi, k * SUBLANES:(k + 1) * SUBLANES, :] = cb8_ref[:, cols] + sum(
                cw8_ref[j, :, cols] * window[j] for j in range(CONV_W))
            window.pop(0)
            if k % 16 == 15:
                yield
        xp[:, cols] = x_tail(SUBLANES)
        cst_ref[:, cols] = x_tail(CONV_W - 1)
        y_s[gi] = _dot(xc_s[gi].astype(BF16), bd_ref[gi])
        yield
        ba, bx = ba8_ref[:, cols], bx8_ref[:, cols]
        log_decay = _log_decay(lam8_ref[:, cols])
        h_loc = a_cum = None
        for k in range(seg):
            r = slice(k * SUBLANES, (k + 1) * SUBLANES)
            a, bterm = _gate_math(y_s[gi, r, :MXU_DIM], y_s[gi, r, MXU_DIM:], xc_s[gi, r, :] + pace(),
                                  ba, bx, log_decay)
            h_loc = bterm if k == 0 else a * h_loc + bterm
            a_cum = a if k == 0 else a * a_cum
            hl_s[gi, r, :] = h_loc
            ac_s[gi, r, :] = a_cum
            if k % 4 == 3:
                yield
        h_in = jnp.where(fresh, 0.0, hc[:, cols])
        carry_in = []
        for sgm in range(SUBLANES):
            carry_in.append(h_in)
            h_in = h_loc[sgm:sgm + 1, :] + a_cum[sgm:sgm + 1, :] * h_in
        hc[:, cols] = h_in
        hst_ref[:, cols] = h_in
        carry_in = jnp.concatenate(carry_in, axis=0)
        for k in range(seg):
            r = slice(k * SUBLANES, (k + 1) * SUBLANES)
            h = hl_s[gi, r, :] + ac_s[gi, r, :] * carry_in
            for i in range(MXU_DIM // LANES):
                hgn_s[gi, i, by_segment(k), :] = h[:, i * LANES:(i + 1) * LANES]
            if k % 8 == 7:
                yield
        hg[:, cols] = jnp.concatenate(
            [jnp.concatenate([hgn_s[gi, i, sgm * pitch:sgm * pitch + seg, :] for sgm in range(SUBLANES)], axis=0)
             for i in range(MXU_DIM // LANES)], axis=1).astype(BF16)
        yield

    n_grp = d // MXU_DIM
    pieces = [recur(gi) for gi in range(n_grp)]
    n_spots = n_grp * (2 + seg // 16 + seg // 4 + seg // 8)
    per_spot = len(pending) / n_spots
    due = 0.0
    for piece in pieces:
        for _ in piece:
            due += per_spot
            emit_stage1(int(due))
            due -= int(due)
    emit_stage1(len(pending))


def _const_spec(shape):
    nd = len(shape)
    return pl.BlockSpec(shape, lambda *_: (0,) * nd, pipeline_mode=pl.Buffered(1))


def _prompt_mixer(x, w):
    b, t, d = x.shape
    tm = TM_MIX
    nt = t // tm
    n_blocks = b * nt
    consts = [w['n1g'], w['wqkv'], w['win'], w['cw8'], w['cb8'], w['bd'], w['ba8'], w['bx8'], w['lam8']]
    out_shape = (
        jax.ShapeDtypeStruct((b, t, d), BF16),
        jax.ShapeDtypeStruct((b, t, REST0), F32),
        jax.ShapeDtypeStruct((b, t, 3 * d), BF16),
        jax.ShapeDtypeStruct((b, CONV_W - 1, d), F32),
        jax.ShapeDtypeStruct((b, 1, d), F32),
    )
    def cur(n):
        i = jnp.minimum(n, n_blocks - 1)
        return (i // nt, i % nt, 0)

    def prv(n):
        i = jnp.maximum(n - 1, 0)
        return (i // nt, i % nt, 0)

    per_seq = lambda n: (jnp.maximum(n - 1, 0) // nt, 0, 0)
    return pl.pallas_call(
        functools.partial(_prompt_mixer_kernel, blocks_per_seq=nt),
        out_shape=out_shape,
        grid=(n_blocks + 1,),
        in_specs=[pl.BlockSpec((None, tm, d), cur)]
                 + [_const_spec(c.shape) for c in consts],
        out_specs=(
            pl.BlockSpec((None, tm, d), prv),
            pl.BlockSpec((None, tm, REST0), cur),
            pl.BlockSpec((None, tm, 3 * d), cur),
            pl.BlockSpec((None, CONV_W - 1, d), per_seq),
            pl.BlockSpec((None, 1, d), per_seq),
        ),
        scratch_shapes=[
            pltpu.VMEM((2, d // LANES, SUBLANES * (tm // SUBLANES + SEG_PAD), LANES), F32),
            pltpu.VMEM((SUBLANES, d), F32),
            pltpu.VMEM((1, d), F32),
            pltpu.VMEM((d // MXU_DIM, tm, MXU_DIM), F32),
            pltpu.VMEM((d // MXU_DIM, tm, 2 * MXU_DIM), F32),
            pltpu.VMEM((d // MXU_DIM, tm, MXU_DIM), F32),
            pltpu.VMEM((d // MXU_DIM, tm, MXU_DIM), F32),
            pltpu.VMEM((d // MXU_DIM, MXU_DIM // LANES, SUBLANES * (tm // SUBLANES + SEG_PAD), LANES), F32),
        ],
        compiler_params=pltpu.CompilerParams(
            dimension_semantics=("arbitrary",), vmem_limit_bytes=VMEM_LIMIT),
        name="prompt_mixer",
    )(x, *consts)


def _gelu_gated(h, pre):
    d = h.shape[1]
    return (h.astype(F32) * jax.nn.gelu(pre[:, :d].astype(F32))).astype(BF16)


def _merge_mlp_ple(x, rnn, att, pre, p, woa_ref, wout_ref, n2g_ref, wup_ref, wdn_ref,
                   pg_ref, wpg_ref, wple_ref):
    d = x.shape[1]
    pre = pre.astype(F32)
    atto = _dot(att, woa_ref[...])
    mix = (jax.nn.sigmoid(pre[:, d:2 * d]) * atto + jax.nn.sigmoid(pre[:, 2 * d:]) * rnn).astype(BF16)
    x = x + _dot(mix, wout_ref[...])
    xn = _rms(x, n2g_ref[...]).astype(BF16)
    acc = x
    for c in range(wup_ref.shape[1] // FF_CHUNK):
        cols = slice(c * FF_CHUNK, (c + 1) * FF_CHUNK)
        hmid = jnp.maximum(_dot(xn, wup_ref[:, cols]), 0.0)
        acc = acc + _dot((hmid * hmid).astype(BF16), wdn_ref[cols, :])
    gate = jax.nn.sigmoid(_dot(_rms(acc, pg_ref[...]).astype(BF16), wpg_ref[...]))
    return acc + gate * _dot(p.astype(BF16), wple_ref[...])


def _channel_kernel(x_ref, h_ref, att_ref, pre_ref, p_ref, woa_ref, wor_ref, *refs):
    *weights, o_ref = refs
    rnn = _dot(_gelu_gated(h_ref[...], pre_ref[...]), wor_ref[...])
    o_ref[...] = _merge_mlp_ple(x_ref[...], rnn, att_ref[...], pre_ref[...], p_ref[...], woa_ref, *weights)


def _prompt_channel_kernel(x_ref, h_ref, qkv_ref, pre_ref, p_ref, gq_ref, gk_ref, oq_ref, ok_ref,
                           bias_rows_ref, sink_ref, *refs, blocks_per_seq):
    woa_ref, wor_ref, *weights, o_ref, kwin_ref, vwin_ref, bias_ref, kbuf, vbuf, logit_s, e_s, att = refs
    tm = x_ref.shape[0]
    i = pl.program_id(0)

    @pl.when(i == 0)
    def _():
        kbuf[...] = jnp.zeros(kbuf.shape, BF16)
        vbuf[...] = jnp.zeros(vbuf.shape, BF16)
        has_prev = lax.broadcasted_iota(jnp.int32, (2 * WINDOW, WINDOW), 0) >= WINDOW
        for h in range(N_HEADS):
            row = jnp.broadcast_to(bias_rows_ref[h:h + 1, :], (2 * WINDOW, bias_rows_ref.shape[1]))
            table = pltpu.roll(row, 0, 1, stride=1, stride_axis=0)[:, :WINDOW]
            at = (h // GROUP, slice(None), slice((h % GROUP) * WINDOW, (h % GROUP + 1) * WINDOW))
            bias_ref[(0,) + at] = table
            bias_ref[(1,) + at] = jnp.where(has_prev, table, NEG_INF)

    fresh = lax.rem(i, blocks_per_seq) == 0
    qkv = qkv_ref[...]
    qn = _head_rms(qkv[:, :Q_W], oq_ref, gq_ref[...]).astype(BF16)
    kn = _head_rms(qkv[:, Q_W:Q_W + KV_W], ok_ref, gk_ref[...])
    v = qkv[:, Q_W + KV_W:]
    kbuf[0:WINDOW, :] = jnp.where(fresh, jnp.zeros((WINDOW, KV_W), BF16), kbuf[tm:tm + WINDOW, :])
    vbuf[:, 0:WINDOW] = jnp.where(fresh, jnp.zeros((KV_W, WINDOW), BF16), vbuf[:, tm:tm + WINDOW])
    kbuf[WINDOW:WINDOW + tm, :] = kn.astype(BF16)
    vbuf[:, WINDOW:WINDOW + tm] = v.T.astype(BF16)
    kwin_ref[...] = kn[tm - WINDOW:, :]
    vwin_ref[...] = v[tm - WINDOW:, :]

    first = jnp.where(fresh, 1, 0)
    lower = _lower_half((WINDOW, LANES))
    zero = jnp.zeros((WINDOW, LANES), BF16)
    hg = _gelu_gated(h_ref[...], pre_ref[...])
    n_sub = tm // WINDOW
    rnn_cols = wor_ref.shape[1] // n_sub
    rnn = []
    for s in range(n_sub):
        rows = slice(s * WINDOW, (s + 1) * WINDOW)
        keys = kbuf[s * WINDOW:(s + 2) * WINDOW, :]
        vals_t = vbuf[:, s * WINDOW:(s + 2) * WINDOW]
        outs = []
        for g in range(KV_HEADS):
            qs = []
            for j in range(GROUP):
                slab = qn[rows, j * LANES:(j + 1) * LANES]
                qs.append(jnp.where(lower, slab, zero) if g == 0 else jnp.where(lower, zero, slab))
            logit_s[g] = _dot_nt(keys, jnp.concatenate(qs, axis=0))
            inv = []
            for j in range(GROUP):
                blk = slice(j * WINDOW, (j + 1) * WINDOW)
                bias = bias_ref[first, g, :, blk] if s == 0 else bias_ref[0, g, :, blk]
                logit = logit_s[g, :, blk] + bias
                sink = sink_ref[g, :, blk]
                m = jnp.maximum(jnp.max(logit, axis=0, keepdims=True), sink)
                e = jnp.exp(logit - m)
                inv.append(1.0 / (jnp.sum(e, axis=0, keepdims=True) + jnp.exp(sink - m)))
                e_s[g, :, blk] = e.astype(BF16)
            outs.append(_dot(vals_t, e_s[g]) * jnp.concatenate(inv, axis=1))
        for j in range(GROUP):
            blk = slice(j * WINDOW, (j + 1) * WINDOW)
            both = jnp.concatenate([outs[0][:HEAD_DIM, blk], outs[1][HEAD_DIM:, blk]], axis=0)
            att[rows, j * LANES:(j + 1) * LANES] = both.T.astype(BF16)
        rnn.append(_dot(hg, wor_ref[:, s * rnn_cols:(s + 1) * rnn_cols]))

    o_ref[...] = _merge_mlp_ple(x_ref[...], jnp.concatenate(rnn, axis=1), att[...], pre_ref[...],
                                p_ref[...], woa_ref, *weights)


def _channel_weights(w):
    return [w['woa'], w['wor'], w['wout'], w['n2g'], w['wup'], w['wdn'], w['pg'], w['wpg'], w['wple']]


def _channel(x, h, att, pre, p, w):
    m, d = x.shape
    tm = min(TM_MLP, m)
    consts = _channel_weights(w)
    rows = lambda a: pl.BlockSpec((tm, a.shape[1]), lambda i: (i, 0))
    return pl.pallas_call(
        _channel_kernel,
        out_shape=jax.ShapeDtypeStruct((m, d), F32),
        grid=(m // tm,),
        in_specs=[rows(a) for a in (x, h, att, pre, p)] + [_const_spec(c.shape) for c in consts],
        out_specs=rows(x),
        compiler_params=pltpu.CompilerParams(
            dimension_semantics=("arbitrary",), vmem_limit_bytes=VMEM_LIMIT),
        name="merge_mlp_ple",
    )(x, h, att, pre, p, *consts)


def _prompt_channel(x, h, qkv, pre, p, w, n_seq):
    m, d = x.shape
    tm = TM_MLP
    bps = m // n_seq // tm
    consts = [w['gq'], w['gk'], w['oq'], w['ok'], w['bias_p'], w['sink_p']] + _channel_weights(w)
    rows = lambda a: pl.BlockSpec((tm, a.shape[1]), lambda i: (i, 0))
    per_seq = pl.BlockSpec((None, WINDOW, KV_W), lambda i: (i // bps, 0, 0))
    return pl.pallas_call(
        functools.partial(_prompt_channel_kernel, blocks_per_seq=bps),
        out_shape=(jax.ShapeDtypeStruct((m, d), F32),
                   jax.ShapeDtypeStruct((n_seq, WINDOW, KV_W), F32),
                   jax.ShapeDtypeStruct((n_seq, WINDOW, KV_W), F32)),
        grid=(m // tm,),
        in_specs=[rows(a) for a in (x, h, qkv, pre, p)] + [_const_spec(c.shape) for c in consts],
        out_specs=(rows(x), per_seq, per_seq),
        scratch_shapes=[
            pltpu.VMEM((2, KV_HEADS, 2 * WINDOW, GROUP * WINDOW), F32),
            pltpu.VMEM((WINDOW + tm, KV_W), BF16),
            pltpu.VMEM((KV_W, WINDOW + tm), BF16),
            pltpu.VMEM((KV_HEADS, 2 * WINDOW, GROUP * WINDOW), F32),
            pltpu.VMEM((KV_HEADS, 2 * WINDOW, GROUP * WINDOW), BF16),
            pltpu.VMEM((tm, Q_W), BF16),
        ],
        compiler_params=pltpu.CompilerParams(
            dimension_semantics=("arbitrary",), vmem_limit_bytes=VMEM_LIMIT),
        name="attn_merge_mlp_ple",
    )(x, h, qkv, pre, p, *consts)


def _decode_attn_kernel(x_ref, ck_ref, cv_ref, n1g_ref, wqkv_ref, gq_ref, gk_ref, oq_ref, ok_ref,
                        bias_c_ref, bias_n_ref, sink_ref,
                        att_ref, kwin_ref, vwin_ref):
    sb = ck_ref.shape[0]
    nt = x_ref.shape[0] // sb
    xn = _rms(x_ref[...], n1g_ref[...]).astype(BF16)
    qkv = _dot(xn, wqkv_ref[...])
    qn = _head_rms(qkv[:, :Q_W], oq_ref, gq_ref[...])
    kn = _head_rms(qkv[:, Q_W:Q_W + KV_W], ok_ref, gk_ref[...])
    v = qkv[:, Q_W + KV_W:]

    n_rows = sb * nt
    ck = ck_ref[...]
    cv = cv_ref[...]
    for b in range(sb):
        rows = slice(b * nt, (b + 1) * nt)
        kwin_ref[b, 0:WINDOW - nt, :] = ck[b, nt:, :]
        vwin_ref[b, 0:WINDOW - nt, :] = cv[b, nt:, :]
        kwin_ref[b, WINDOW - nt:WINDOW, :] = kn[rows]
        vwin_ref[b, WINDOW - nt:WINDOW, :] = v[rows]

    qb = qn.astype(BF16)
    lower = _lower_half((n_rows, LANES))
    zero = jnp.zeros((n_rows, LANES), BF16)
    q_all = jnp.concatenate(
        [jnp.where(lower, qb[:, j * LANES:(j + 1) * LANES], zero) if g == 0
         else jnp.where(lower, zero, qb[:, j * LANES:(j + 1) * LANES])
         for g in range(KV_HEADS) for j in range(GROUP)], axis=0)
    n_col = q_all.shape[0]
    col_seq = (lax.broadcasted_iota(jnp.int32, (1, n_col), 1) // nt) % sb

    def own(big):
        out = big[0:WINDOW]
        for b in range(1, sb):
            out = jnp.where(col_seq == b, big[b * WINDOW:(b + 1) * WINDOW], out)
        return out

    st = own(_dot_nt(ck.reshape(sb * WINDOW, KV_W).astype(BF16), q_all)) + bias_c_ref[...]
    stx = _dot_nt(kn.astype(BF16), q_all) + bias_n_ref[...]
    sink = sink_ref[...]
    m = jnp.maximum(jnp.maximum(jnp.max(st, axis=0, keepdims=True),
                                jnp.max(stx, axis=0, keepdims=True)), sink)
    e = jnp.exp(st - m)
    ex = jnp.exp(stx - m)
    den = (jnp.sum(e, axis=0, keepdims=True) + jnp.sum(ex, axis=0, keepdims=True)
           + jnp.exp(sink - m))
    cv_t = jnp.concatenate([cv[b].T for b in range(sb)], axis=0).astype(BF16)
    out = own(_dot(cv_t, e.astype(BF16)))
    pad = WINDOW - n_rows
    v_t = jnp.concatenate([v, jnp.zeros((pad, KV_W), F32)], axis=0).T.astype(BF16)
    ex_pad = jnp.concatenate([ex, jnp.zeros((pad, n_col), F32)], axis=0).astype(BF16)
    out = (out + _dot(v_t, ex_pad)) / den
    half = n_col // KV_HEADS
    both_t = jnp.concatenate([out[:HEAD_DIM, :half], out[HEAD_DIM:, half:]], axis=0).T
    for j in range(GROUP):
        att_ref[:, j * LANES:(j + 1) * LANES] = both_t[j * n_rows:(j + 1) * n_rows].astype(att_ref.dtype)


def _decode_attn(x, ck, cv, w):
    m, d = x.shape
    nb = ck.shape[0]
    nt = m // nb
    sb = SEQ_BLOCK
    consts = [w['n1g'], w['wqkv'], w['gq'], w['gk'], w['oq'], w['ok'],
              w['bias_sc'], w['bias_sn'], w['sink_s']]
    cache_spec = pl.BlockSpec((sb, WINDOW, KV_W), lambda i: (i, 0, 0))
    return pl.pallas_call(
        _decode_attn_kernel,
        out_shape=(jax.ShapeDtypeStruct((m, Q_W), BF16),
                   jax.ShapeDtypeStruct(ck.shape, F32),
                   jax.ShapeDtypeStruct(cv.shape, F32)),
        grid=(nb // sb,),
        in_specs=[pl.BlockSpec((sb * nt, d), lambda i: (i, 0)), cache_spec, cache_spec]
                 + [_const_spec(c.shape) for c in consts],
        out_specs=(pl.BlockSpec((sb * nt, Q_W), lambda i: (i, 0)), cache_spec, cache_spec),
        compiler_params=pltpu.CompilerParams(
            dimension_semantics=("arbitrary",), vmem_limit_bytes=VMEM_LIMIT),
        name="decode_attn",
    )(x, ck, cv, *consts)


def _decode_mixer_kernel(x_ref, cst_ref, h0_ref, n1g_ref, win_ref, cw_ref, cb_ref, bd_ref,
                         ba_ref, bx_ref, lam_ref,
                         hg_ref, gates_ref, cnew_ref, hnew_ref):
    nb = h0_ref.shape[0]
    nt = x_ref.shape[0] // nb
    d = x_ref.shape[1]
    xn = _rms(x_ref[...], n1g_ref[...]).astype(BF16)
    xr = _dot(xn, win_ref[:, REST0:REST0 + d])
    prev = cst_ref[...]
    slabs = [prev[k * nb:(k + 1) * nb] for k in range(CONV_W - 1)]
    slabs += [xr[k * nb:(k + 1) * nb] for k in range(nt)]
    xc = jnp.concatenate(
        [cb_ref[...] + sum(cw_ref[j:j + 1, :] * slabs[ti + j] for j in range(CONV_W))
         for ti in range(nt)], axis=0)
    cnew_ref[...] = jnp.concatenate(slabs[-(CONV_W - 1):], axis=0)

    a, bterm = _rglru_gates(xc, bd_ref, ba_ref[...], bx_ref[...], lam_ref[...])
    h = h0_ref[...]
    hs = []
    for ti in range(nt):
        h = a[ti * nb:(ti + 1) * nb] * h + bterm[ti * nb:(ti + 1) * nb]
        hs.append(h)
    hnew_ref[...] = h
    hg_ref[...] = jnp.concatenate(hs, axis=0).astype(BF16)
    gates_ref[...] = _dot(xn, win_ref[:, REST0 + d:REST0 + 4 * d]).astype(BF16)


def _decode_mixer(x, cst, h0, w):
    m, d = x.shape
    consts = [w['n1g'], w['win'], w['cw'], w['cb'], w['bd'], w['ba'], w['bx'], w['lam']]
    ins = [x, cst, h0] + consts
    whole = lambda shape: pl.BlockSpec(shape, lambda i: (0,) * len(shape))
    return pl.pallas_call(
        _decode_mixer_kernel,
        out_shape=(jax.ShapeDtypeStruct((m, d), BF16),
                   jax.ShapeDtypeStruct((m, 3 * d), BF16),
                   jax.ShapeDtypeStruct(cst.shape, F32),
                   jax.ShapeDtypeStruct(h0.shape, F32)),
        grid=(1,),
        in_specs=[_const_spec(c.shape) for c in ins],
        out_specs=(whole((m, d)), whole((m, 3 * d)), whole(cst.shape), whole(h0.shape)),
        compiler_params=pltpu.CompilerParams(
            dimension_semantics=("arbitrary",), vmem_limit_bytes=VMEM_LIMIT),
        name="decode_mixer",
    )(*ins)


def _rel_bucket(dist):
    n = np.maximum(dist, 0)
    max_exact = REL_BUCKETS // 2
    nf = np.maximum(n, 1).astype(np.float32)
    large = max_exact + (np.log(nf / max_exact) / math.log(REL_MAX_DIST / max_exact)
                         * (REL_BUCKETS - max_exact)).astype(np.int32)
    large = np.minimum(large, REL_BUCKETS - 1)
    return np.where(n < max_exact, n, large)


def _bias_rows(rel_bias, dist, mask):
    tb = jnp.where(mask[:, :, None], rel_bias[_rel_bucket(dist)].astype(F32), NEG_INF)
    tq, tk = dist.shape
    return jnp.transpose(tb, (2, 0, 1)).reshape(N_HEADS * tq, tk)


def _prompt_bias_rows(rel_bias):
    span = 3 * WINDOW
    k = np.arange(span)
    dist = WINDOW + np.where(k < WINDOW, k, k - span)
    valid = (dist >= 0) & (dist <= WINDOW)
    return jnp.where(valid[:, None], rel_bias[_rel_bucket(dist)].astype(F32), NEG_INF).T


def _gate_blocks(rg_wa, rg_wx, n_grp):
    per = rg_wa.shape[0] // n_grp
    both = jnp.stack([rg_wa, rg_wx]).reshape(2, n_grp, per, RNN_BS, RNN_BS)
    on_diag = np.eye(per, dtype=bool)[None, None, :, None, :, None]
    blocks = jnp.where(on_diag, both[:, :, :, :, None, :], 0.0)
    blocks = jnp.transpose(blocks, (1, 2, 3, 0, 4, 5))
    return blocks.reshape(n_grp, per * RNN_BS, 2 * per * RNN_BS).astype(BF16)


def _head_avg(width):
    idx = np.arange(width) // HEAD_DIM
    return jnp.asarray((idx[:, None] == idx[None, :]).astype(np.float32) / HEAD_DIM, BF16)


def _prepare(rel_bias, norm1_g, w_in, q_norm_g, k_norm_g, sinks, w_o_attn, conv_w, conv_b,
             rg_wa, rg_ba, rg_wx, rg_bx, rg_lambda, w_o_rnn, w_out, norm2_g, w_up, w_down,
             ple_norm_g, w_ple_gate, w_ple, n_dec):
    d = w_in.shape[0]
    order = [g * GROUP + j for j in range(GROUP) for g in range(KV_HEADS)]
    perm = np.concatenate([np.arange(h * HEAD_DIM, (h + 1) * HEAD_DIM) for h in order])
    row = lambda a: a.reshape(1, -1).astype(F32)
    w = {}
    w['n1g'] = row(norm1_g)
    w['win'] = w_in.astype(BF16)
    w['wqkv'] = jnp.concatenate([w['win'][:, :Q_W][:, perm], w['win'][:, Q_W:REST0]], axis=1)
    w['gq'] = row(jnp.tile(q_norm_g, N_HEADS)) * (HEAD_DIM ** -0.5)
    w['gk'] = row(jnp.tile(k_norm_g, KV_HEADS))
    w['oq'] = _head_avg(Q_W)
    w['ok'] = _head_avg(KV_W)
    w['woa'] = w_o_attn[perm, :].astype(BF16)
    w['cw'] = conv_w.astype(F32)
    w['cb'] = row(conv_b)
    w['bd'] = _gate_blocks(rg_wa, rg_wx, d // MXU_DIM)
    w['ba'] = row(rg_ba)
    w['bx'] = row(rg_bx)
    w['lam'] = row(rg_lambda)
    rows8 = lambda a: jnp.broadcast_to(a, (SUBLANES, a.shape[-1]))
    w['cw8'] = jnp.broadcast_to(w['cw'][:, None, :], (CONV_W, SUBLANES, d))
    w['cb8'], w['ba8'], w['bx8'], w['lam8'] = (rows8(w[k]) for k in ('cb', 'ba', 'bx', 'lam'))
    w['wor'] = w_o_rnn.astype(BF16)
    w['wout'] = w_out.astype(BF16)
    w['n2g'] = row(norm2_g)
    w['wup'] = w_up.astype(BF16)
    w['wdn'] = w_down.astype(BF16)
    w['pg'] = row(ple_norm_g)
    w['wpg'] = w_ple_gate.astype(BF16)
    w['wple'] = w_ple.astype(BF16)

    w['bias_p'] = _prompt_bias_rows(rel_bias)
    sink_rows = sinks.astype(F32).reshape(KV_HEADS, GROUP, 1)
    w['sink_p'] = jnp.repeat(sink_rows, WINDOW, axis=2).reshape(KV_HEADS, 1, GROUP * WINDOW)

    sb = SEQ_BLOCK
    ti = np.arange(n_dec)[:, None]
    dist_c = ti + WINDOW - np.arange(WINDOW)[None, :]
    dist_n = ti - np.arange(n_dec)[None, :]
    n_col = N_HEADS * sb * n_dec
    col = np.arange(n_col)
    col_head_t = col // (sb * n_dec) * n_dec + col % n_dec
    spread = (np.arange(N_HEADS * n_dec)[:, None] == col_head_t[None, :]).astype(np.float32)
    expand = lambda a: jnp.dot(a.T, spread, precision=lax.Precision.HIGHEST)
    w['bias_sc'] = expand(_bias_rows(rel_bias, dist_c, dist_c <= WINDOW))
    bias_n = expand(_bias_rows(rel_bias, dist_n, dist_n >= 0))
    own = (np.arange(sb * n_dec) // n_dec)[:, None] == (col // n_dec % sb)[None, :]
    w['bias_sn'] = jnp.where(own, jnp.tile(bias_n, (sb, 1)), NEG_INF)
    w['sink_s'] = jnp.repeat(sinks.astype(F32), sb * n_dec)[None, :]
    return w


def kernel(x_prompt, x_sample, cache_k_win, cache_v_win, state_conv, state_h, p_prompt, p_sample, rel_bias, norm1_g, w_in, q_norm_g, k_norm_g, sinks, w_o_attn, conv_w, conv_b, rg_wa, rg_ba, rg_wx, rg_bx, rg_lambda, w_o_rnn, w_out, norm2_g, w_up, w_down, ple_norm_g, w_ple_gate, w_ple):
    depth = w_in.shape[0]
    assert depth == 1, "single-layer step"
    b, t, d = x_prompt.shape
    nb, nt, _ = x_sample.shape
    w = _prepare(rel_bias, norm1_g[0], w_in[0], q_norm_g[0], k_norm_g[0], sinks[0], w_o_attn[0],
                 conv_w[0], conv_b[0], rg_wa[0], rg_ba[0], rg_wx[0], rg_bx[0], rg_lambda[0],
                 w_o_rnn[0], w_out[0], norm2_g[0], w_up[0], w_down[0], ple_norm_g[0],
                 w_ple_gate[0], w_ple[0], nt)

    hp_rows, qkvp, prep, cp, hp = _prompt_mixer(x_prompt, w)
    flat = lambda a: a.reshape(b * t, a.shape[-1])
    yp, kp, vp = _prompt_channel(flat(x_prompt), flat(hp_rows), flat(qkvp), flat(prep),
                                 flat(p_prompt[0]), w, b)
    yp = yp.reshape(b, t, d)

    att, ks, vs = _decode_attn(x_sample.reshape(nb * nt, d),
                               cache_k_win[0].reshape(nb, WINDOW, KV_W),
                               cache_v_win[0].reshape(nb, WINDOW, KV_W), w)
    to_tm = lambda a: jnp.swapaxes(a, 0, 1).reshape(-1, a.shape[-1])
    xs_tm = to_tm(x_sample)
    hgs, gatess, cs, hs = _decode_mixer(xs_tm, to_tm(state_conv[0]), state_h[0], w)
    ys = _channel(xs_tm, hgs, to_tm(att.reshape(nb, nt, Q_W)), gatess, to_tm(p_sample[0]), w)
    from_tm = lambda a, n: jnp.swapaxes(a.reshape(n, nb, a.shape[-1]), 0, 1)

    kv_shape = (1, -1, WINDOW, KV_HEADS, HEAD_DIM)
    return (yp, from_tm(ys, nt),
            kp.reshape(kv_shape), vp.reshape(kv_shape), cp[None], hp.reshape(1, b, d),
            ks.reshape(kv_shape), vs.reshape(kv_shape), from_tm(cs, CONV_W - 1)[None], hs[None])
```

```python
import functools
import math

import numpy as np
import jax
import jax.numpy as jnp
from jax import lax
from jax.experimental import pallas as pl
from jax.experimental.pallas import tpu as pltpu

F32 = jnp.float32
BF16 = jnp.bfloat16

N_HEADS = 8
KV_HEADS = 2
GROUP = N_HEADS // KV_HEADS
HEAD_DIM = 64
Q_W = N_HEADS * HEAD_DIM
KV_W = KV_HEADS * HEAD_DIM
REST0 = Q_W + 2 * KV_W
WINDOW = 128
REL_BUCKETS = 32
REL_MAX_DIST = 128
RNN_BS = 64
CONV_W = 4
RG_C = 8.0
EPS = 1e-6
NEG_INF = -1e30
EXPM1_SERIES_BELOW = 2.0 ** -11
SQRT_FLOOR = 1e-30

LANES = 128
SUBLANES = 8
MXU_DIM = 256
VMEM_LIMIT = 56 * 1024 * 1024

TM_MIX = 512
TM_MLP = 512
FF_CHUNK = 1024
SEQ_BLOCK = 16
PACE_LAG = 2
SEG_PAD = 4


def _dot(a, b):
    return jnp.dot(a, b, preferred_element_type=F32)


def _dot_nt(a, b):
    return lax.dot_general(a, b, (((1,), (1,)), ((), ())), preferred_element_type=F32)


def _rms(x, g):
    ms = jnp.mean(x * x, axis=-1, keepdims=True)
    return x * lax.rsqrt(ms + EPS) * g


def _head_rms(x, ones_ref, g):
    ms = _dot((x * x).astype(BF16), ones_ref[...])
    return x * lax.rsqrt(ms + EPS) * g


def _lower_half(shape):
    return lax.broadcasted_iota(jnp.int32, shape, len(shape) - 1) < HEAD_DIM


def _neg_expm1_2x(x, exp_x):
    return jnp.where(x > -EXPM1_SERIES_BELOW, (-2.0 * x) * (1.0 + x), 1.0 - exp_x * exp_x)


def _log_decay(lam):
    return -RG_C * jax.nn.softplus(-lam)


def _gate_math(ya, yx, xc, ba, bx, log_decay):
    r = jax.nn.sigmoid(ya + ba)
    i = jax.nn.sigmoid(yx + bx)
    log_a = log_decay * r
    a = jnp.exp(log_a)
    y = _neg_expm1_2x(log_a, a)
    bterm = (y * lax.rsqrt(jnp.maximum(y, SQRT_FLOOR))) * (i * xc)
    return a, bterm


def _rglru_gates(xc, bd_ref, ba, bx, lam):
    xcb = xc.astype(BF16)
    ya, yx = [], []
    for gi in range(xc.shape[1] // MXU_DIM):
        y = _dot(xcb[:, gi * MXU_DIM:(gi + 1) * MXU_DIM], bd_ref[gi])
        ya.append(y[:, :MXU_DIM])
        yx.append(y[:, MXU_DIM:])
    return _gate_math(jnp.concatenate(ya, axis=1), jnp.concatenate(yx, axis=1), xc, ba, bx,
                      _log_decay(lam))


def _prompt_mixer_kernel(x_ref, n1g_ref, wqkv_ref, win_ref, gq_ref, gk_ref, oq_ref, ok_ref,
                         bias_rows_ref, sink_ref, cw8_ref, cb8_ref, bd_ref, ba8_ref, bx8_ref, lam8_ref,
                         h_out, att, pre_out, kwin_ref, vwin_ref, cst_ref, hst_ref,
                         bias_ref, proj, projx, kbuf, vbuf, xp, hc, logit_s, e_s, xc_s, y_s, hl_s, ac_s, hn_s,
                         *, blocks_per_seq):
    tm, d = x_ref.shape
    n = pl.program_id(0)

    @pl.when(n == 0)
    def _():
        proj[1] = jnp.zeros(proj.shape[1:], F32)
        projx[1] = jnp.zeros(projx.shape[1:], F32)
        kbuf[...] = jnp.zeros(kbuf.shape, BF16)
        vbuf[...] = jnp.zeros(vbuf.shape, BF16)
        xp[...] = jnp.zeros(xp.shape, F32)
        hc[...] = jnp.zeros(hc.shape, F32)
        has_prev = lax.broadcasted_iota(jnp.int32, (2 * WINDOW, WINDOW), 0) >= WINDOW
        for h in range(N_HEADS):
            row = jnp.broadcast_to(bias_rows_ref[h:h + 1, :], (2 * WINDOW, bias_rows_ref.shape[1]))
            table = pltpu.roll(row, 0, 1, stride=1, stride_axis=0)[:, :WINDOW]
            at = (h // GROUP, slice(None), slice((h % GROUP) * WINDOW, (h % GROUP + 1) * WINDOW))
            bias_ref[(0,) + at] = table
            bias_ref[(1,) + at] = jnp.where(has_prev, table, NEG_INF)

    wslot = lax.rem(n, 2)
    rslot = 1 - wslot
    fresh = lax.rem(n + blocks_per_seq - 1, blocks_per_seq) == 0
    seg = tm // SUBLANES
    pitch = seg + SEG_PAD

    xn = _rms(x_ref[...], n1g_ref[...]).astype(BF16)
    n_qkv = wqkv_ref.shape[1]
    tokens = []

    def project(lo):
        res = _dot(xn, (wqkv_ref if lo < n_qkv else win_ref)[:, lo:lo + MXU_DIM])
        tokens.append(res[0:1, :])
        if lo < n_qkv:
            for i in range(MXU_DIM // LANES):
                proj[wslot, lo // LANES + i] = res[:, i * LANES:(i + 1) * LANES]
        elif lo < REST0 + d:
            for i in range(MXU_DIM // LANES):
                for sgm in range(SUBLANES):
                    projx[wslot, (lo - REST0) // LANES + i, sgm * pitch:sgm * pitch + seg, :] = (
                        res[sgm * seg:(sgm + 1) * seg, i * LANES:(i + 1) * LANES])
        else:
            pre_out[:, lo - REST0 - d:lo - REST0 - d + MXU_DIM] = res.astype(BF16)

    pending = list(range(0, win_ref.shape[1], MXU_DIM))

    def pace():
        if len(tokens) < PACE_LAG:
            return jnp.zeros((1, MXU_DIM), F32)
        bits = lax.bitcast_convert_type(tokens[-PACE_LAG], jnp.uint32)
        return lax.bitcast_convert_type((bits >> 16) >> 16, F32)

    def emit_stage1(count):
        for _ in range(min(count, len(pending))):
            project(pending.pop(0))

    def kept(lo, hi):
        return jnp.concatenate([proj[rslot, t] for t in range(lo // LANES, hi // LANES)], axis=1)

    emit_stage1(2)
    qn = _head_rms(kept(0, Q_W), oq_ref, gq_ref[...]).astype(BF16)
    kn = _head_rms(kept(Q_W, Q_W + KV_W), ok_ref, gk_ref[...])
    v = kept(Q_W + KV_W, n_qkv)
    kbuf[0:WINDOW, :] = jnp.where(fresh, jnp.zeros((WINDOW, KV_W), BF16), kbuf[tm:tm + WINDOW, :])
    vbuf[:, 0:WINDOW] = jnp.where(fresh, jnp.zeros((KV_W, WINDOW), BF16), vbuf[:, tm:tm + WINDOW])
    kbuf[WINDOW:WINDOW + tm, :] = kn.astype(BF16)
    vbuf[:, WINDOW:WINDOW + tm] = v.T.astype(BF16)
    kwin_ref[...] = kn[tm - WINDOW:, :]
    vwin_ref[...] = v[tm - WINDOW:, :]

    first = jnp.where(fresh, 1, 0)
    lower = _lower_half((WINDOW, LANES))
    zero = jnp.zeros((WINDOW, LANES), BF16)

    def attend(s):
        rows = slice(s * WINDOW, (s + 1) * WINDOW)
        keys = kbuf[s * WINDOW:(s + 2) * WINDOW, :]
        vals_t = vbuf[:, s * WINDOW:(s + 2) * WINDOW]
        outs = []
        for g in range(KV_HEADS):
            qs = []
            for j in range(GROUP):
                slab = qn[rows, j * LANES:(j + 1) * LANES]
                qs.append(jnp.where(lower, slab, zero) if g == 0 else jnp.where(lower, zero, slab))
            logit_s[g] = _dot_nt(keys, jnp.concatenate(qs, axis=0))
            yield
            inv = []
            for j in range(GROUP):
                blk = slice(j * WINDOW, (j + 1) * WINDOW)
                bias = bias_ref[first, g, :, blk] if s == 0 else bias_ref[0, g, :, blk]
                logit = logit_s[g, :, blk] + bias
                sink = sink_ref[g, :, blk] + pace()[:, :WINDOW]
                m = jnp.maximum(jnp.max(logit, axis=0, keepdims=True), sink)
                e = jnp.exp(logit - m)
                inv.append(1.0 / (jnp.sum(e, axis=0, keepdims=True) + jnp.exp(sink - m)))
                e_s[g, :, blk] = e.astype(BF16)
                yield
            outs.append(_dot(vals_t, e_s[g]) * jnp.concatenate(inv, axis=1))
        for j in range(GROUP):
            blk = slice(j * WINDOW, (j + 1) * WINDOW)
            both = jnp.concatenate([outs[0][:HEAD_DIM, blk], outs[1][HEAD_DIM:, blk]], axis=0)
            att[rows, j * LANES:(j + 1) * LANES] = both.T.astype(BF16)
        yield

    sub8 = lax.broadcasted_iota(jnp.int32, (SUBLANES, MXU_DIM), 0)

    def recur(gi):
        cols = slice(gi * MXU_DIM, (gi + 1) * MXU_DIM)
        tiles = range(gi * (MXU_DIM // LANES), (gi + 1) * (MXU_DIM // LANES))
        by_segment = lambda k: pl.ds(k, SUBLANES, stride=pitch)
        x_at = lambda k: jnp.concatenate([projx[rslot, t, by_segment(k), :] for t in tiles], axis=1)
        last = SUBLANES * pitch - SEG_PAD
        x_tail = lambda n_rows: jnp.concatenate(
            [projx[rslot, t, last - n_rows:last, :] for t in tiles], axis=1)
        prev = jnp.where(fresh, 0.0, xp[:, cols])
        window = [jnp.where(sub8 == 0, prev[SUBLANES - j:SUBLANES - j + 1, :],
                            pltpu.roll(x_at(seg - j), 1, axis=0)) for j in range(CONV_W - 1, 0, -1)]
        for k in range(seg):
            window.append(x_at(k))
            xc_s[gi, k * SUBLANES:(k + 1) * SUBLANES, :] = cb8_ref[:, cols] + sum(
                cw8_ref[j, :, cols] * window[j] for j in range(CONV_W))
            window.pop(0)
            if k % 16 == 15:
                yield
        xp[:, cols] = x_tail(SUBLANES)
        cst_ref[:, cols] = x_tail(CONV_W - 1)
        y_s[gi] = _dot(xc_s[gi].astype(BF16), bd_ref[gi])
        yield
        ba, bx = ba8_ref[:, cols], bx8_ref[:, cols]
        log_decay = _log_decay(lam8_ref[:, cols])
        h_loc = a_cum = None
        for k in range(seg):
            r = slice(k * SUBLANES, (k + 1) * SUBLANES)
            a, bterm = _gate_math(y_s[gi, r, :MXU_DIM], y_s[gi, r, MXU_DIM:], xc_s[gi, r, :] + pace(),
                                  ba, bx, log_decay)
            h_loc = bterm if k == 0 else a * h_loc + bterm
            a_cum = a if k == 0 else a * a_cum
            hl_s[gi, r, :] = h_loc
            ac_s[gi, r, :] = a_cum
            if k % 4 == 3:
                yield
        h_in = jnp.where(fresh, 0.0, hc[:, cols])
        carry_in = []
        for sgm in range(SUBLANES):
            carry_in.append(h_in)
            h_in = h_loc[sgm:sgm + 1, :] + a_cum[sgm:sgm + 1, :] * h_in
        hc[:, cols] = h_in
        hst_ref[:, cols] = h_in
        carry_in = jnp.concatenate(carry_in, axis=0)
        for k in range(seg):
            r = slice(k * SUBLANES, (k + 1) * SUBLANES)
            h = hl_s[gi, r, :] + ac_s[gi, r, :] * carry_in
            for i in range(MXU_DIM // LANES):
                hn_s[gi, i, by_segment(k), :] = h[:, i * LANES:(i + 1) * LANES]
            if k % 8 == 7:
                yield
        h_out[:, cols] = jnp.concatenate(
            [jnp.concatenate([hn_s[gi, i, sgm * pitch:sgm * pitch + seg, :] for sgm in range(SUBLANES)], axis=0)
             for i in range(MXU_DIM // LANES)], axis=1).astype(BF16)
        yield

    n_sub = tm // WINDOW
    n_grp = d // MXU_DIM
    pieces = []
    for u in range(max(n_sub, n_grp)):
        if u < n_sub:
            pieces.append(attend(u))
        if u < n_grp:
            pieces.append(recur(u))
    n_spots = n_sub * (KV_HEADS * (1 + GROUP) + 1) + n_grp * (2 + seg // 16 + seg // 4 + seg // 8)
    per_spot = len(pending) / n_spots
    due = 0.0
    for piece in pieces:
        for _ in piece:
            due += per_spot
            emit_stage1(int(due))
            due -= int(due)
    emit_stage1(len(pending))


def _const_spec(shape):
    nd = len(shape)
    return pl.BlockSpec(shape, lambda *_: (0,) * nd, pipeline_mode=pl.Buffered(1))


def _prompt_mixer(x, w):
    b, t, d = x.shape
    tm = TM_MIX
    nt = t // tm
    n_blocks = b * nt
    padded_rows = SUBLANES * (tm // SUBLANES + SEG_PAD)
    consts = [w['n1g'], w['wqkv'], w['win'], w['gq'], w['gk'], w['oq'], w['ok'],
              w['bias_p'], w['sink_p'], w['cw8'], w['cb8'], w['bd'], w['ba8'], w['bx8'], w['lam8']]
    out_shape = (
        jax.ShapeDtypeStruct((b, t, d), BF16),
        jax.ShapeDtypeStruct((b, t, Q_W), BF16),
        jax.ShapeDtypeStruct((b, t, 3 * d), BF16),
        jax.ShapeDtypeStruct((b, WINDOW, KV_W), F32),
        jax.ShapeDtypeStruct((b, WINDOW, KV_W), F32),
        jax.ShapeDtypeStruct((b, CONV_W - 1, d), F32),
        jax.ShapeDtypeStruct((b, 1, d), F32),
    )

    def cur(n):
        i = jnp.minimum(n, n_blocks - 1)
        return (i // nt, i % nt, 0)

    def prv(n):
        i = jnp.maximum(n - 1, 0)
        return (i // nt, i % nt, 0)

    per_seq = lambda n: (jnp.maximum(n - 1, 0) // nt, 0, 0)
    return pl.pallas_call(
        functools.partial(_prompt_mixer_kernel, blocks_per_seq=nt),
        out_shape=out_shape,
        grid=(n_blocks + 1,),
        in_specs=[pl.BlockSpec((None, tm, d), cur)]
                 + [_const_spec(c.shape) for c in consts],
        out_specs=(
            pl.BlockSpec((None, tm, d), prv),
            pl.BlockSpec((None, tm, Q_W), prv),
            pl.BlockSpec((None, tm, 3 * d), cur),
            pl.BlockSpec((None, WINDOW, KV_W), per_seq),
            pl.BlockSpec((None, WINDOW, KV_W), per_seq),
            pl.BlockSpec((None, CONV_W - 1, d), per_seq),
            pl.BlockSpec((None, 1, d), per_seq),
        ),
        scratch_shapes=[
            pltpu.VMEM((2, KV_HEADS, 2 * WINDOW, GROUP * WINDOW), F32),
            pltpu.VMEM((2, REST0 // LANES, tm, LANES), F32),
            pltpu.VMEM((2, d // LANES, padded_rows, LANES), F32),
            pltpu.VMEM((WINDOW + tm, KV_W), BF16),
            pltpu.VMEM((KV_W, WINDOW + tm), BF16),
            pltpu.VMEM((SUBLANES, d), F32),
            pltpu.VMEM((1, d), F32),
            pltpu.VMEM((KV_HEADS, 2 * WINDOW, GROUP * WINDOW), F32),
            pltpu.VMEM((KV_HEADS, 2 * WINDOW, GROUP * WINDOW), BF16),
            pltpu.VMEM((d // MXU_DIM, tm, MXU_DIM), F32),
            pltpu.VMEM((d // MXU_DIM, tm, 2 * MXU_DIM), F32),
            pltpu.VMEM((d // MXU_DIM, tm, MXU_DIM), F32),
            pltpu.VMEM((d // MXU_DIM, tm, MXU_DIM), F32),
            pltpu.VMEM((d // MXU_DIM, MXU_DIM // LANES, padded_rows, LANES), F32),
        ],
        compiler_params=pltpu.CompilerParams(
            dimension_semantics=("arbitrary",), vmem_limit_bytes=VMEM_LIMIT),
        name="prompt_mixer",
    )(x, *consts)


def _channel_kernel(x_ref, h_ref, att_ref, pre_ref, p_ref, woa_ref, wor_ref, wout_ref,
                    n2g_ref, wup_ref, wdn_ref, pg_ref, wpg_ref, wple_ref, o_ref):
    d = x_ref.shape[1]
    pre = pre_ref[...].astype(F32)
    hg = (h_ref[...].astype(F32) * jax.nn.gelu(pre[:, :d])).astype(BF16)
    rnn = _dot(hg, wor_ref[...])
    atto = _dot(att_ref[...], woa_ref[...])
    mix = (jax.nn.sigmoid(pre[:, d:2 * d]) * atto + jax.nn.sigmoid(pre[:, 2 * d:]) * rnn).astype(BF16)
    x = x_ref[...] + _dot(mix, wout_ref[...])
    xn = _rms(x, n2g_ref[...]).astype(BF16)
    acc = x
    for c in range(wup_ref.shape[1] // FF_CHUNK):
        cols = slice(c * FF_CHUNK, (c + 1) * FF_CHUNK)
        hmid = jnp.maximum(_dot(xn, wup_ref[:, cols]), 0.0)
        acc = acc + _dot((hmid * hmid).astype(BF16), wdn_ref[cols, :])
    gate = jax.nn.sigmoid(_dot(_rms(acc, pg_ref[...]).astype(BF16), wpg_ref[...]))
    o_ref[...] = acc + gate * _dot(p_ref[...].astype(BF16), wple_ref[...])


def _channel(x, h, att, pre, p, w):
    m, d = x.shape
    tm = min(TM_MLP, m)
    consts = [w['woa'], w['wor'], w['wout'], w['n2g'], w['wup'], w['wdn'], w['pg'], w['wpg'], w['wple']]
    rows = lambda a: pl.BlockSpec((tm, a.shape[1]), lambda i: (i, 0))
    return pl.pallas_call(
        _channel_kernel,
        out_shape=jax.ShapeDtypeStruct((m, d), F32),
        grid=(m // tm,),
        in_specs=[rows(a) for a in (x, h, att, pre, p)] + [_const_spec(c.shape) for c in consts],
        out_specs=rows(x),
        compiler_params=pltpu.CompilerParams(
            dimension_semantics=("arbitrary",), vmem_limit_bytes=VMEM_LIMIT),
        name="merge_mlp_ple",
    )(x, h, att, pre, p, *consts)


def _decode_attn_kernel(x_ref, ck_ref, cv_ref, n1g_ref, wqkv_ref, gq_ref, gk_ref, oq_ref, ok_ref,
                        bias_c_ref, bias_n_ref, sink_ref,
                        att_ref, kwin_ref, vwin_ref):
    sb = ck_ref.shape[0]
    nt = x_ref.shape[0] // sb
    xn = _rms(x_ref[...], n1g_ref[...]).astype(BF16)
    qkv = _dot(xn, wqkv_ref[...])
    qn = _head_rms(qkv[:, :Q_W], oq_ref, gq_ref[...])
    kn = _head_rms(qkv[:, Q_W:Q_W + KV_W], ok_ref, gk_ref[...])
    v = qkv[:, Q_W + KV_W:]

    n_rows = sb * nt
    ck = ck_ref[...]
    cv = cv_ref[...]
    for b in range(sb):
        rows = slice(b * nt, (b + 1) * nt)
        kwin_ref[b, 0:WINDOW - nt, :] = ck[b, nt:, :]
        vwin_ref[b, 0:WINDOW - nt, :] = cv[b, nt:, :]
        kwin_ref[b, WINDOW - nt:WINDOW, :] = kn[rows]
        vwin_ref[b, WINDOW - nt:WINDOW, :] = v[rows]

    qb = qn.astype(BF16)
    lower = _lower_half((n_rows, LANES))
    zero = jnp.zeros((n_rows, LANES), BF16)
    q_all = jnp.concatenate(
        [jnp.where(lower, qb[:, j * LANES:(j + 1) * LANES], zero) if g == 0
         else jnp.where(lower, zero, qb[:, j * LANES:(j + 1) * LANES])
         for g in range(KV_HEADS) for j in range(GROUP)], axis=0)
    n_col = q_all.shape[0]
    col_seq = (lax.broadcasted_iota(jnp.int32, (1, n_col), 1) // nt) % sb

    def own(big):
        out = big[0:WINDOW]
        for b in range(1, sb):
            out = jnp.where(col_seq == b, big[b * WINDOW:(b + 1) * WINDOW], out)
        return out

    st = own(_dot_nt(ck.reshape(sb * WINDOW, KV_W).astype(BF16), q_all)) + bias_c_ref[...]
    stx = _dot_nt(kn.astype(BF16), q_all) + bias_n_ref[...]
    sink = sink_ref[...]
    m = jnp.maximum(jnp.maximum(jnp.max(st, axis=0, keepdims=True),
                                jnp.max(stx, axis=0, keepdims=True)), sink)
    e = jnp.exp(st - m)
    ex = jnp.exp(stx - m)
    den = (jnp.sum(e, axis=0, keepdims=True) + jnp.sum(ex, axis=0, keepdims=True)
           + jnp.exp(sink - m))
    cv_t = jnp.concatenate([cv[b].T for b in range(sb)], axis=0).astype(BF16)
    out = own(_dot(cv_t, e.astype(BF16)))
    pad = WINDOW - n_rows
    v_t = jnp.concatenate([v, jnp.zeros((pad, KV_W), F32)], axis=0).T.astype(BF16)
    ex_pad = jnp.concatenate([ex, jnp.zeros((pad, n_col), F32)], axis=0).astype(BF16)
    out = (out + _dot(v_t, ex_pad)) / den
    half = n_col // KV_HEADS
    both_t = jnp.concatenate([out[:HEAD_DIM, :half], out[HEAD_DIM:, half:]], axis=0).T
    for j in range(GROUP):
        att_ref[:, j * LANES:(j + 1) * LANES] = both_t[j * n_rows:(j + 1) * n_rows].astype(att_ref.dtype)


def _decode_attn(x, ck, cv, w):
    m, d = x.shape
    nb = ck.shape[0]
    nt = m // nb
    sb = SEQ_BLOCK
    consts = [w['n1g'], w['wqkv'], w['gq'], w['gk'], w['oq'], w['ok'],
              w['bias_sc'], w['bias_sn'], w['sink_s']]
    cache_spec = pl.BlockSpec((sb, WINDOW, KV_W), lambda i: (i, 0, 0))
    return pl.pallas_call(
        _decode_attn_kernel,
        out_shape=(jax.ShapeDtypeStruct((m, Q_W), BF16),
                   jax.ShapeDtypeStruct(ck.shape, F32),
                   jax.ShapeDtypeStruct(cv.shape, F32)),
        grid=(nb // sb,),
        in_specs=[pl.BlockSpec((sb * nt, d), lambda i: (i, 0)), cache_spec, cache_spec]
                 + [_const_spec(c.shape) for c in consts],
        out_specs=(pl.BlockSpec((sb * nt, Q_W), lambda i: (i, 0)), cache_spec, cache_spec),
        compiler_params=pltpu.CompilerParams(
            dimension_semantics=("arbitrary",), vmem_limit_bytes=VMEM_LIMIT),
        name="decode_attn",
    )(x, ck, cv, *consts)


def _decode_mixer_kernel(x_ref, cst_ref, h0_ref, n1g_ref, win_ref, cw_ref, cb_ref, bd_ref,
                         ba_ref, bx_ref, lam_ref,
                         h_out, pre_out, cnew_ref, hnew_ref):
    nb = h0_ref.shape[0]
    nt = x_ref.shape[0] // nb
    d = x_ref.shape[1]
    xn = _rms(x_ref[...], n1g_ref[...]).astype(BF16)
    xr = _dot(xn, win_ref[:, REST0:REST0 + d])
    prev = cst_ref[...]
    slabs = [prev[k * nb:(k + 1) * nb] for k in range(CONV_W - 1)]
    slabs += [xr[k * nb:(k + 1) * nb] for k in range(nt)]
    xc = jnp.concatenate(
        [cb_ref[...] + sum(cw_ref[j:j + 1, :] * slabs[ti + j] for j in range(CONV_W))
         for ti in range(nt)], axis=0)
    cnew_ref[...] = jnp.concatenate(slabs[-(CONV_W - 1):], axis=0)

    a, bterm = _rglru_gates(xc, bd_ref, ba_ref[...], bx_ref[...], lam_ref[...])
    h = h0_ref[...]
    hs = []
    for ti in range(nt):
        h = a[ti * nb:(ti + 1) * nb] * h + bterm[ti * nb:(ti + 1) * nb]
        hs.append(h)
    hnew_ref[...] = h
    h_out[...] = jnp.concatenate(hs, axis=0).astype(BF16)
    pre_out[...] = _dot(xn, win_ref[:, REST0 + d:REST0 + 4 * d]).astype(BF16)


def _decode_mixer(x, cst, h0, w):
    m, d = x.shape
    consts = [w['n1g'], w['win'], w['cw'], w['cb'], w['bd'], w['ba'], w['bx'], w['lam']]
    ins = [x, cst, h0] + consts
    whole = lambda shape: pl.BlockSpec(shape, lambda i: (0,) * len(shape))
    return pl.pallas_call(
        _decode_mixer_kernel,
        out_shape=(jax.ShapeDtypeStruct((m, d), BF16),
                   jax.ShapeDtypeStruct((m, 3 * d), BF16),
                   jax.ShapeDtypeStruct(cst.shape, F32),
                   jax.ShapeDtypeStruct(h0.shape, F32)),
        grid=(1,),
        in_specs=[_const_spec(c.shape) for c in ins],
        out_specs=(whole((m, d)), whole((m, 3 * d)), whole(cst.shape), whole(h0.shape)),
        compiler_params=pltpu.CompilerParams(
            dimension_semantics=("arbitrary",), vmem_limit_bytes=VMEM_LIMIT),
        name="decode_mixer",
    )(*ins)


def _rel_bucket(dist):
    n = np.maximum(dist, 0)
    max_exact = REL_BUCKETS // 2
    nf = np.maximum(n, 1).astype(np.float32)
    large = max_exact + (np.log(nf / max_exact) / math.log(REL_MAX_DIST / max_exact)
                         * (REL_BUCKETS - max_exact)).astype(np.int32)
    large = np.minimum(large, REL_BUCKETS - 1)
    return np.where(n < max_exact, n, large)


def _bias_rows(rel_bias, dist, mask):
    tb = jnp.where(mask[:, :, None], rel_bias[_rel_bucket(dist)].astype(F32), NEG_INF)
    tq, tk = dist.shape
    return jnp.transpose(tb, (2, 0, 1)).reshape(N_HEADS * tq, tk)


def _prompt_bias_rows(rel_bias):
    span = 3 * WINDOW
    k = np.arange(span)
    dist = WINDOW + np.where(k < WINDOW, k, k - span)
    valid = (dist >= 0) & (dist <= WINDOW)
    return jnp.where(valid[:, None], rel_bias[_rel_bucket(dist)].astype(F32), NEG_INF).T


def _gate_blocks(rg_wa, rg_wx, n_grp):
    per = rg_wa.shape[0] // n_grp
    both = jnp.stack([rg_wa, rg_wx]).reshape(2, n_grp, per, RNN_BS, RNN_BS)
    on_diag = np.eye(per, dtype=bool)[None, None, :, None, :, None]
    blocks = jnp.where(on_diag, both[:, :, :, :, None, :], 0.0)
    blocks = jnp.transpose(blocks, (1, 2, 3, 0, 4, 5))
    return blocks.reshape(n_grp, per * RNN_BS, 2 * per * RNN_BS).astype(BF16)


def _head_avg(width):
    idx = np.arange(width) // HEAD_DIM
    return jnp.asarray((idx[:, None] == idx[None, :]).astype(np.float32) / HEAD_DIM, BF16)


def _prepare(rel_bias, norm1_g, w_in, q_norm_g, k_norm_g, sinks, w_o_attn, conv_w, conv_b,
             rg_wa, rg_ba, rg_wx, rg_bx, rg_lambda, w_o_rnn, w_out, norm2_g, w_up, w_down,
             ple_norm_g, w_ple_gate, w_ple, n_dec):
    d = w_in.shape[0]
    order = [g * GROUP + j for j in range(GROUP) for g in range(KV_HEADS)]
    perm = np.concatenate([np.arange(h * HEAD_DIM, (h + 1) * HEAD_DIM) for h in order])
    row = lambda a: a.reshape(1, -1).astype(F32)
    w = {}
    w['n1g'] = row(norm1_g)
    w['win'] = w_in.astype(BF16)
    w['wqkv'] = jnp.concatenate([w['win'][:, :Q_W][:, perm], w['win'][:, Q_W:REST0]], axis=1)
    w['gq'] = row(jnp.tile(q_norm_g, N_HEADS)) * (HEAD_DIM ** -0.5)
    w['gk'] = row(jnp.tile(k_norm_g, KV_HEADS))
    w['oq'] = _head_avg(Q_W)
    w['ok'] = _head_avg(KV_W)
    w['woa'] = w_o_attn[perm, :].astype(BF16)
    w['cw'] = conv_w.astype(F32)
    w['cb'] = row(conv_b)
    w['bd'] = _gate_blocks(rg_wa, rg_wx, d // MXU_DIM)
    w['ba'] = row(rg_ba)
    w['bx'] = row(rg_bx)
    w['lam'] = row(rg_lambda)
    rows8 = lambda a: jnp.broadcast_to(a, (SUBLANES, a.shape[-1]))
    w['cw8'] = jnp.broadcast_to(w['cw'][:, None, :], (CONV_W, SUBLANES, d))
    w['cb8'], w['ba8'], w['bx8'], w['lam8'] = (rows8(w[k]) for k in ('cb', 'ba', 'bx', 'lam'))
    w['wor'] = w_o_rnn.astype(BF16)
    w['wout'] = w_out.astype(BF16)
    w['n2g'] = row(norm2_g)
    w['wup'] = w_up.astype(BF16)
    w['wdn'] = w_down.astype(BF16)
    w['pg'] = row(ple_norm_g)
    w['wpg'] = w_ple_gate.astype(BF16)
    w['wple'] = w_ple.astype(BF16)

    w['bias_p'] = _prompt_bias_rows(rel_bias)
    sink_rows = sinks.astype(F32).reshape(KV_HEADS, GROUP, 1)
    w['sink_p'] = jnp.repeat(sink_rows, WINDOW, axis=2).reshape(KV_HEADS, 1, GROUP * WINDOW)

    sb = SEQ_BLOCK
    ti = np.arange(n_dec)[:, None]
    dist_c = ti + WINDOW - np.arange(WINDOW)[None, :]
    dist_n = ti - np.arange(n_dec)[None, :]
    n_col = N_HEADS * sb * n_dec
    col = np.arange(n_col)
    col_head_t = col // (sb * n_dec) * n_dec + col % n_dec
    spread = (np.arange(N_HEADS * n_dec)[:, None] == col_head_t[None, :]).astype(np.float32)
    expand = lambda a: jnp.dot(a.T, spread, precision=lax.Precision.HIGHEST)
    w['bias_sc'] = expand(_bias_rows(rel_bias, dist_c, dist_c <= WINDOW))
    bias_n = expand(_bias_rows(rel_bias, dist_n, dist_n >= 0))
    own = (np.arange(sb * n_dec) // n_dec)[:, None] == (col // n_dec % sb)[None, :]
    w['bias_sn'] = jnp.where(own, jnp.tile(bias_n, (sb, 1)), NEG_INF)
    w['sink_s'] = jnp.repeat(sinks.astype(F32), sb * n_dec)[None, :]
    return w


def kernel(x_prompt, x_sample, cache_k_win, cache_v_win, state_conv, state_h, p_prompt, p_sample, rel_bias, norm1_g, w_in, q_norm_g, k_norm_g, sinks, w_o_attn, conv_w, conv_b, rg_wa, rg_ba, rg_wx, rg_bx, rg_lambda, w_o_rnn, w_out, norm2_g, w_up, w_down, ple_norm_g, w_ple_gate, w_ple):
    depth = w_in.shape[0]
    assert depth == 1, "single-layer step"
    b, t, d = x_prompt.shape
    nb, nt, _ = x_sample.shape
    w = _prepare(rel_bias, norm1_g[0], w_in[0], q_norm_g[0], k_norm_g[0], sinks[0], w_o_attn[0],
                 conv_w[0], conv_b[0], rg_wa[0], rg_ba[0], rg_wx[0], rg_bx[0], rg_lambda[0],
                 w_o_rnn[0], w_out[0], norm2_g[0], w_up[0], w_down[0], ple_norm_g[0],
                 w_ple_gate[0], w_ple[0], nt)

    hp_rows, attp, prep, kp, vp, cp, hp = _prompt_mixer(x_prompt, w)
    flat = lambda a: a.reshape(b * t, a.shape[-1])
    yp = _channel(flat(x_prompt), flat(hp_rows), flat(attp), flat(prep), flat(p_prompt[0]), w).reshape(b, t, d)

    att, ks, vs = _decode_attn(x_sample.reshape(nb * nt, d),
                               cache_k_win[0].reshape(nb, WINDOW, KV_W),
                               cache_v_win[0].reshape(nb, WINDOW, KV_W), w)
    to_tm = lambda a: jnp.swapaxes(a, 0, 1).reshape(-1, a.shape[-1])
    xs_tm = to_tm(x_sample)
    hs_rows, pres, cs, hs = _decode_mixer(xs_tm, to_tm(state_conv[0]), state_h[0], w)
    ys = _channel(xs_tm, hs_rows, to_tm(att.reshape(nb, nt, Q_W)), pres, to_tm(p_sample[0]), w)
    from_tm = lambda a, n: jnp.swapaxes(a.reshape(n, nb, a.shape[-1]), 0, 1)

    kv_shape = (1, -1, WINDOW, KV_HEADS, HEAD_DIM)
    return (yp, from_tm(ys, nt),
            kp.reshape(kv_shape), vp.reshape(kv_shape), cp[None], hp.reshape(1, b, d),
            ks.reshape(kv_shape), vs.reshape(kv_shape), from_tm(cs, CONV_W - 1)[None], hs[None])
```

```python
import functools
import math

import numpy as np
import jax
import jax.numpy as jnp
from jax import lax
from jax.experimental import pallas as pl
from jax.experimental.pallas import tpu as pltpu

F32 = jnp.float32
BF16 = jnp.bfloat16

N_HEADS = 8
KV_HEADS = 2
GROUP = N_HEADS // KV_HEADS
HEAD_DIM = 64
Q_W = N_HEADS * HEAD_DIM
KV_W = KV_HEADS * HEAD_DIM
REST0 = Q_W + 2 * KV_W
WINDOW = 128
REL_BUCKETS = 32
REL_MAX_DIST = 128
RNN_BS = 64
CONV_W = 4
RG_C = 8.0
EPS = 1e-6
NEG_INF = -1e30
EXPM1_SERIES_BELOW = 2.0 ** -11
SQRT_FLOOR = 1e-30

LANES = 128
SUBLANES = 8
MXU_DIM = 256
VMEM_LIMIT = 56 * 1024 * 1024

TM_MIX = 512
TM_MLP = 512
FF_CHUNK = 1024
SEQ_BLOCK = 16
PACE_LAG = 2
SEG_PAD = 4


def _dot(a, b):
    return jnp.dot(a, b, preferred_element_type=F32)


def _dot_nt(a, b):
    return lax.dot_general(a, b, (((1,), (1,)), ((), ())), preferred_element_type=F32)


def _rms(x, g):
    ms = jnp.mean(x * x, axis=-1, keepdims=True)
    return x * lax.rsqrt(ms + EPS) * g


def _head_rms(x, ones_ref, g):
    ms = _dot((x * x).astype(BF16), ones_ref[...])
    return x * lax.rsqrt(ms + EPS) * g


def _lower_half(shape):
    return lax.broadcasted_iota(jnp.int32, shape, len(shape) - 1) < HEAD_DIM


def _neg_expm1_2x(x, exp_x):
    return jnp.where(x > -EXPM1_SERIES_BELOW, (-2.0 * x) * (1.0 + x), 1.0 - exp_x * exp_x)


def _log_decay(lam):
    return -RG_C * jax.nn.softplus(-lam)


def _gate_math(ya, yx, xc, ba, bx, log_decay):
    r = jax.nn.sigmoid(ya + ba)
    i = jax.nn.sigmoid(yx + bx)
    log_a = log_decay * r
    a = jnp.exp(log_a)
    y = _neg_expm1_2x(log_a, a)
    bterm = (y * lax.rsqrt(jnp.maximum(y, SQRT_FLOOR))) * (i * xc)
    return a, bterm


def _rglru_gates(xc, bd_ref, ba, bx, lam):
    xcb = xc.astype(BF16)
    ya, yx = [], []
    for gi in range(xc.shape[1] // MXU_DIM):
        y = _dot(xcb[:, gi * MXU_DIM:(gi + 1) * MXU_DIM], bd_ref[gi])
        ya.append(y[:, :MXU_DIM])
        yx.append(y[:, MXU_DIM:])
    return _gate_math(jnp.concatenate(ya, axis=1), jnp.concatenate(yx, axis=1), xc, ba, bx,
                      _log_decay(lam))


def _prompt_mixer_kernel(x_ref, n1g_ref, wqkv_ref, win_ref, gq_ref, gk_ref, oq_ref, ok_ref,
                         bias_rows_ref, sink_ref, cw8_ref, cb8_ref, bd_ref, ba8_ref, bx8_ref, lam8_ref,
                         h_out, att, pre_out, kwin_ref, vwin_ref, cst_ref, hst_ref,
                         bias_ref, proj, projx, kbuf, vbuf, xp, hc, logit_s, e_s, xc_s, y_s, hl_s, ac_s, hn_s,
                         *, blocks_per_seq):
    tm, d = x_ref.shape
    n = pl.program_id(0)

    @pl.when(n == 0)
    def _():
        proj[1] = jnp.zeros(proj.shape[1:], F32)
        projx[1] = jnp.zeros(projx.shape[1:], F32)
        kbuf[...] = jnp.zeros(kbuf.shape, BF16)
        vbuf[...] = jnp.zeros(vbuf.shape, BF16)
        xp[...] = jnp.zeros(xp.shape, F32)
        hc[...] = jnp.zeros(hc.shape, F32)
        has_prev = lax.broadcasted_iota(jnp.int32, (2 * WINDOW, WINDOW), 0) >= WINDOW
        for h in range(N_HEADS):
            row = jnp.broadcast_to(bias_rows_ref[h:h + 1, :], (2 * WINDOW, bias_rows_ref.shape[1]))
            table = pltpu.roll(row, 0, 1, stride=1, stride_axis=0)[:, :WINDOW]
            at = (h // GROUP, slice(None), slice((h % GROUP) * WINDOW, (h % GROUP + 1) * WINDOW))
            bias_ref[(0,) + at] = table
            bias_ref[(1,) + at] = jnp.where(has_prev, table, NEG_INF)

    wslot = lax.rem(n, 2)
    rslot = 1 - wslot
    fresh = lax.rem(n + blocks_per_seq - 1, blocks_per_seq) == 0
    seg = tm // SUBLANES
    pitch = seg + SEG_PAD

    xn = _rms(x_ref[...], n1g_ref[...]).astype(BF16)
    n_qkv = wqkv_ref.shape[1]
    tokens = []

    def project(lo):
        res = _dot(xn, (wqkv_ref if lo < n_qkv else win_ref)[:, lo:lo + MXU_DIM])
        tokens.append(res[0:1, :])
        if lo < n_qkv:
            for i in range(MXU_DIM // LANES):
                proj[wslot, lo // LANES + i] = res[:, i * LANES:(i + 1) * LANES]
        elif lo < REST0 + d:
            for i in range(MXU_DIM // LANES):
                for sgm in range(SUBLANES):
                    projx[wslot, (lo - REST0) // LANES + i, sgm * pitch:sgm * pitch + seg, :] = (
                        res[sgm * seg:(sgm + 1) * seg, i * LANES:(i + 1) * LANES])
        else:
            pre_out[:, lo - REST0 - d:lo - REST0 - d + MXU_DIM] = res.astype(BF16)

    pending = list(range(0, win_ref.shape[1], MXU_DIM))

    def pace():
        if len(tokens) < PACE_LAG:
            return jnp.zeros((1, MXU_DIM), F32)
        bits = lax.bitcast_convert_type(tokens[-PACE_LAG], jnp.uint32)
        return lax.bitcast_convert_type((bits >> 16) >> 16, F32)

    def emit_stage1(count):
        for _ in range(min(count, len(pending))):
            project(pending.pop(0))

    def kept(lo, hi):
        return jnp.concatenate([proj[rslot, t] for t in range(lo // LANES, hi // LANES)], axis=1)

    emit_stage1(2)
    qn = _head_rms(kept(0, Q_W), oq_ref, gq_ref[...]).astype(BF16)
    kn = _head_rms(kept(Q_W, Q_W + KV_W), ok_ref, gk_ref[...])
    v = kept(Q_W + KV_W, n_qkv)
    kbuf[0:WINDOW, :] = jnp.where(fresh, jnp.zeros((WINDOW, KV_W), BF16), kbuf[tm:tm + WINDOW, :])
    vbuf[:, 0:WINDOW] = jnp.where(fresh, jnp.zeros((KV_W, WINDOW), BF16), vbuf[:, tm:tm + WINDOW])
    kbuf[WINDOW:WINDOW + tm, :] = kn.astype(BF16)
    vbuf[:, WINDOW:WINDOW + tm] = v.T.astype(BF16)
    kwin_ref[...] = kn[tm - WINDOW:, :]
    vwin_ref[...] = v[tm - WINDOW:, :]

    first = jnp.where(fresh, 1, 0)
    lower = _lower_half((WINDOW, LANES))
    zero = jnp.zeros((WINDOW, LANES), BF16)

    def attend(s):
        rows = slice(s * WINDOW, (s + 1) * WINDOW)
        keys = kbuf[s * WINDOW:(s + 2) * WINDOW, :]
        vals_t = vbuf[:, s * WINDOW:(s + 2) * WINDOW]
        outs = []
        for g in range(KV_HEADS):
            qs = []
            for j in range(GROUP):
                slab = qn[rows, j * LANES:(j + 1) * LANES]
                qs.append(jnp.where(lower, slab, zero) if g == 0 else jnp.where(lower, zero, slab))
            logit_s[g] = _dot_nt(keys, jnp.concatenate(qs, axis=0))
            yield
            inv = []
            for j in range(GROUP):
                blk = slice(j * WINDOW, (j + 1) * WINDOW)
                bias = bias_ref[first, g, :, blk] if s == 0 else bias_ref[0, g, :, blk]
                logit = logit_s[g, :, blk] + bias
                sink = sink_ref[g, :, blk] + pace()[:, :WINDOW]
                m = jnp.maximum(jnp.max(logit, axis=0, keepdims=True), sink)
                e = jnp.exp(logit - m)
                inv.append(1.0 / (jnp.sum(e, axis=0, keepdims=True) + jnp.exp(sink - m)))
                e_s[g, :, blk] = e.astype(BF16)
                yield
            outs.append(_dot(vals_t, e_s[g]) * jnp.concatenate(inv, axis=1))
        for j in range(GROUP):
            blk = slice(j * WINDOW, (j + 1) * WINDOW)
            both = jnp.concatenate([outs[0][:HEAD_DIM, blk], outs[1][HEAD_DIM:, blk]], axis=0)
            att[rows, j * LANES:(j + 1) * LANES] = both.T.astype(BF16)
        yield

    sub8 = lax.broadcasted_iota(jnp.int32, (SUBLANES, MXU_DIM), 0)

    def recur(gi):
        cols = slice(gi * MXU_DIM, (gi + 1) * MXU_DIM)
        tiles = range(gi * (MXU_DIM // LANES), (gi + 1) * (MXU_DIM // LANES))
        by_segment = lambda k: pl.ds(k, SUBLANES, stride=pitch)
        x_at = lambda k: jnp.concatenate([projx[rslot, t, by_segment(k), :] for t in tiles], axis=1)
        last = SUBLANES * pitch - SEG_PAD
        x_tail = lambda n_rows: jnp.concatenate(
            [projx[rslot, t, last - n_rows:last, :] for t in tiles], axis=1)
        prev = jnp.where(fresh, 0.0, xp[:, cols])
        window = [jnp.where(sub8 == 0, prev[SUBLANES - j:SUBLANES - j + 1, :],
                            pltpu.roll(x_at(seg - j), 1, axis=0)) for j in range(CONV_W - 1, 0, -1)]
        for k in range(seg):
            window.append(x_at(k))
            xc_s[gi, k * SUBLANES:(k + 1) * SUBLANES, :] = cb8_ref[:, cols] + sum(
                cw8_ref[j, :, cols] * window[j] for j in range(CONV_W))
            window.pop(0)
            if k % 16 == 15:
                yield
        xp[:, cols] = x_tail(SUBLANES)
        cst_ref[:, cols] = x_tail(CONV_W - 1)
        y_s[gi] = _dot(xc_s[gi].astype(BF16), bd_ref[gi])
        yield
        ba, bx = ba8_ref[:, cols], bx8_ref[:, cols]
        log_decay = _log_decay(lam8_ref[:, cols])
        h_loc = a_cum = None
        for k in range(seg):
            r = slice(k * SUBLANES, (k + 1) * SUBLANES)
            a, bterm = _gate_math(y_s[gi, r, :MXU_DIM], y_s[gi, r, MXU_DIM:], xc_s[gi, r, :] + pace(),
                                  ba, bx, log_decay)
            h_loc = bterm if k == 0 else a * h_loc + bterm
            a_cum = a if k == 0 else a * a_cum
            hl_s[gi, r, :] = h_loc
            ac_s[gi, r, :] = a_cum
            if k % 4 == 3:
                yield
        h_in = jnp.where(fresh, 0.0, hc[:, cols])
        carry_in = []
        for sgm in range(SUBLANES):
            carry_in.append(h_in)
            h_in = h_loc[sgm:sgm + 1, :] + a_cum[sgm:sgm + 1, :] * h_in
        hc[:, cols] = h_in
        hst_ref[:, cols] = h_in
        carry_in = jnp.concatenate(carry_in, axis=0)
        for k in range(seg):
            r = slice(k * SUBLANES, (k + 1) * SUBLANES)
            h = hl_s[gi, r, :] + ac_s[gi, r, :] * carry_in
            for i in range(MXU_DIM // LANES):
                hn_s[gi, i, by_segment(k), :] = h[:, i * LANES:(i + 1) * LANES]
            if k % 8 == 7:
                yield
        h_out[:, cols] = jnp.concatenate(
            [jnp.concatenate([hn_s[gi, i, sgm * pitch:sgm * pitch + seg, :] for sgm in range(SUBLANES)], axis=0)
             for i in range(MXU_DIM // LANES)], axis=1).astype(BF16)
        yield

    n_sub = tm // WINDOW
    n_grp = d // MXU_DIM
    pieces = []
    for u in range(max(n_sub, n_grp)):
        if u < n_sub:
            pieces.append(attend(u))
        if u < n_grp:
            pieces.append(recur(u))
    n_spots = n_sub * (KV_HEADS * (1 + GROUP) + 1) + n_grp * (2 + seg // 16 + seg // 4 + seg // 8)
    per_spot = len(pending) / n_spots
    due = 0.0
    for piece in pieces:
        for _ in piece:
            due += per_spot
            emit_stage1(int(due))
            due -= int(due)
    emit_stage1(len(pending))


def _const_spec(shape):
    nd = len(shape)
    return pl.BlockSpec(shape, lambda *_: (0,) * nd, pipeline_mode=pl.Buffered(1))


def _prompt_mixer(x, w):
    b, t, d = x.shape
    tm = TM_MIX
    nt = t // tm
    n_blocks = b * nt
    padded_rows = SUBLANES * (tm // SUBLANES + SEG_PAD)
    consts = [w['n1g'], w['wqkv'], w['win'], w['gq'], w['gk'], w['oq'], w['ok'],
              w['bias_p'], w['sink_p'], w['cw8'], w['cb8'], w['bd'], w['ba8'], w['bx8'], w['lam8']]
    out_shape = (
        jax.ShapeDtypeStruct((b, t, d), BF16),
        jax.ShapeDtypeStruct((b, t, Q_W), BF16),
        jax.ShapeDtypeStruct((b, t, 3 * d), BF16),
        jax.ShapeDtypeStruct((b, WINDOW, KV_W), F32),
        jax.ShapeDtypeStruct((b, WINDOW, KV_W), F32),
        jax.ShapeDtypeStruct((b, CONV_W - 1, d), F32),
        jax.ShapeDtypeStruct((b, 1, d), F32),
    )

    def cur(n):
        i = jnp.minimum(n, n_blocks - 1)
        return (i // nt, i % nt, 0)

    def prv(n):
        i = jnp.maximum(n - 1, 0)
        return (i // nt, i % nt, 0)

    per_seq = lambda n: (jnp.maximum(n - 1, 0) // nt, 0, 0)
    return pl.pallas_call(
        functools.partial(_prompt_mixer_kernel, blocks_per_seq=nt),
        out_shape=out_shape,
        grid=(n_blocks + 1,),
        in_specs=[pl.BlockSpec((None, tm, d), cur)]
                 + [_const_spec(c.shape) for c in consts],
        out_specs=(
            pl.BlockSpec((None, tm, d), prv),
            pl.BlockSpec((None, tm, Q_W), prv),
            pl.BlockSpec((None, tm, 3 * d), cur),
            pl.BlockSpec((None, WINDOW, KV_W), per_seq),
            pl.BlockSpec((None, WINDOW, KV_W), per_seq),
            pl.BlockSpec((None, CONV_W - 1, d), per_seq),
            pl.BlockSpec((None, 1, d), per_seq),
        ),
        scratch_shapes=[
            pltpu.VMEM((2, KV_HEADS, 2 * WINDOW, GROUP * WINDOW), F32),
            pltpu.VMEM((2, REST0 // LANES, tm, LANES), F32),
            pltpu.VMEM((2, d // LANES, padded_rows, LANES), F32),
            pltpu.VMEM((WINDOW + tm, KV_W), BF16),
            pltpu.VMEM((KV_W, WINDOW + tm), BF16),
            pltpu.VMEM((SUBLANES, d), F32),
            pltpu.VMEM((1, d), F32),
            pltpu.VMEM((KV_HEADS, 2 * WINDOW, GROUP * WINDOW), F32),
            pltpu.VMEM((KV_HEADS, 2 * WINDOW, GROUP * WINDOW), BF16),
            pltpu.VMEM((d // MXU_DIM, tm, MXU_DIM), F32),
            pltpu.VMEM((d // MXU_DIM, tm, 2 * MXU_DIM), F32),
            pltpu.VMEM((d // MXU_DIM, tm, MXU_DIM), F32),
            pltpu.VMEM((d // MXU_DIM, tm, MXU_DIM), F32),
            pltpu.VMEM((d // MXU_DIM, MXU_DIM // LANES, padded_rows, LANES), F32),
        ],
        compiler_params=pltpu.CompilerParams(
            dimension_semantics=("arbitrary",), vmem_limit_bytes=VMEM_LIMIT),
        name="prompt_mixer",
    )(x, *consts)


def _channel_rows(x_ref, h_ref, att_ref, pre_ref, p_ref, woa_ref, wor_ref, wout_ref,
                  n2g_ref, wup_ref, wdn_ref, pg_ref, wpg_ref, wple_ref, o_ref):
    d = x_ref.shape[1]
    pre = pre_ref[...].astype(F32)
    hg = (h_ref[...].astype(F32) * jax.nn.gelu(pre[:, :d])).astype(BF16)
    rnn = _dot(hg, wor_ref[...])
    atto = _dot(att_ref[...], woa_ref[...])
    mix = (jax.nn.sigmoid(pre[:, d:2 * d]) * atto + jax.nn.sigmoid(pre[:, 2 * d:]) * rnn).astype(BF16)
    x = x_ref[...] + _dot(mix, wout_ref[...])
    xn = _rms(x, n2g_ref[...]).astype(BF16)
    acc = x
    for c in range(wup_ref.shape[1] // FF_CHUNK):
        cols = slice(c * FF_CHUNK, (c + 1) * FF_CHUNK)
        hmid = jnp.maximum(_dot(xn, wup_ref[:, cols]), 0.0)
        acc = acc + _dot((hmid * hmid).astype(BF16), wdn_ref[cols, :])
    gate = jax.nn.sigmoid(_dot(_rms(acc, pg_ref[...]).astype(BF16), wpg_ref[...]))
    o_ref[...] = acc + gate * _dot(p_ref[...].astype(BF16), wple_ref[...])


N_ROW_INPUTS = 5


def _channel_kernel(*refs, first_steps):
    first, second = refs[:N_ROW_INPUTS], refs[N_ROW_INPUTS:2 * N_ROW_INPUTS]
    consts, (o_first, o_second) = refs[2 * N_ROW_INPUTS:-2], refs[-2:]
    i = pl.program_id(0)

    @pl.when(i < first_steps)
    def _():
        _channel_rows(*first, *consts, o_first)

    @pl.when(i >= first_steps)
    def _():
        _channel_rows(*second, *consts, o_second)


def _channel(first, second, w):
    m1, d = first[0].shape
    m2 = second[0].shape[0]
    tm = min(TM_MLP, m1, m2)
    n1, n2 = m1 // tm, m2 // tm
    consts = [w['woa'], w['wor'], w['wout'], w['n2g'], w['wup'], w['wdn'], w['pg'], w['wpg'], w['wple']]
    rows1 = lambda a: pl.BlockSpec((tm, a.shape[1]), lambda i: (jnp.minimum(i, n1 - 1), 0))
    rows2 = lambda a: pl.BlockSpec((tm, a.shape[1]), lambda i: (jnp.maximum(i - n1, 0), 0),
                                   pipeline_mode=pl.Buffered(1))
    return pl.pallas_call(
        functools.partial(_channel_kernel, first_steps=n1),
        out_shape=(jax.ShapeDtypeStruct((m1, d), F32), jax.ShapeDtypeStruct((m2, d), F32)),
        grid=(n1 + n2,),
        in_specs=[rows1(a) for a in first] + [rows2(a) for a in second]
                 + [_const_spec(c.shape) for c in consts],
        out_specs=(rows1(first[0]), rows2(second[0])),
        compiler_params=pltpu.CompilerParams(
            dimension_semantics=("arbitrary",), vmem_limit_bytes=VMEM_LIMIT),
        name="merge_mlp_ple",
    )(*first, *second, *consts)


def _decode_attn_kernel(x_ref, ck_ref, cv_ref, n1g_ref, wqkv_ref, gq_ref, gk_ref, oq_ref, ok_ref,
                        bias_c_ref, bias_n_ref, sink_ref,
                        att_ref, kwin_ref, vwin_ref):
    sb = ck_ref.shape[0]
    nt = x_ref.shape[0] // sb
    xn = _rms(x_ref[...], n1g_ref[...]).astype(BF16)
    qkv = _dot(xn, wqkv_ref[...])
    qn = _head_rms(qkv[:, :Q_W], oq_ref, gq_ref[...])
    kn = _head_rms(qkv[:, Q_W:Q_W + KV_W], ok_ref, gk_ref[...])
    v = qkv[:, Q_W + KV_W:]

    n_rows = sb * nt
    ck = ck_ref[...]
    cv = cv_ref[...]
    for b in range(sb):
        rows = slice(b * nt, (b + 1) * nt)
        kwin_ref[b, 0:WINDOW - nt, :] = ck[b, nt:, :]
        vwin_ref[b, 0:WINDOW - nt, :] = cv[b, nt:, :]
        kwin_ref[b, WINDOW - nt:WINDOW, :] = kn[rows]
        vwin_ref[b, WINDOW - nt:WINDOW, :] = v[rows]

    qb = qn.astype(BF16)
    lower = _lower_half((n_rows, LANES))
    zero = jnp.zeros((n_rows, LANES), BF16)
    q_all = jnp.concatenate(
        [jnp.where(lower, qb[:, j * LANES:(j + 1) * LANES], zero) if g == 0
         else jnp.where(lower, zero, qb[:, j * LANES:(j + 1) * LANES])
         for g in range(KV_HEADS) for j in range(GROUP)], axis=0)
    n_col = q_all.shape[0]
    col_seq = (lax.broadcasted_iota(jnp.int32, (1, n_col), 1) // nt) % sb

    def own(big):
        out = big[0:WINDOW]
        for b in range(1, sb):
            out = jnp.where(col_seq == b, big[b * WINDOW:(b + 1) * WINDOW], out)
        return out

    st = own(_dot_nt(ck.reshape(sb * WINDOW, KV_W).astype(BF16), q_all)) + bias_c_ref[...]
    stx = _dot_nt(kn.astype(BF16), q_all) + bias_n_ref[...]
    sink = sink_ref[...]
    m = jnp.maximum(jnp.maximum(jnp.max(st, axis=0, keepdims=True),
                                jnp.max(stx, axis=0, keepdims=True)), sink)
    e = jnp.exp(st - m)
    ex = jnp.exp(stx - m)
    den = (jnp.sum(e, axis=0, keepdims=True) + jnp.sum(ex, axis=0, keepdims=True)
           + jnp.exp(sink - m))
    cv_t = jnp.concatenate([cv[b].T for b in range(sb)], axis=0).astype(BF16)
    out = own(_dot(cv_t, e.astype(BF16)))
    pad = WINDOW - n_rows
    v_t = jnp.concatenate([v, jnp.zeros((pad, KV_W), F32)], axis=0).T.astype(BF16)
    ex_pad = jnp.concatenate([ex, jnp.zeros((pad, n_col), F32)], axis=0).astype(BF16)
    out = (out + _dot(v_t, ex_pad)) / den
    half = n_col // KV_HEADS
    both_t = jnp.concatenate([out[:HEAD_DIM, :half], out[HEAD_DIM:, half:]], axis=0).T
    for j in range(GROUP):
        att_ref[:, j * LANES:(j + 1) * LANES] = both_t[j * n_rows:(j + 1) * n_rows].astype(att_ref.dtype)


def _decode_attn(x, ck, cv, w):
    m, d = x.shape
    nb = ck.shape[0]
    nt = m // nb
    sb = SEQ_BLOCK
    consts = [w['n1g'], w['wqkv'], w['gq'], w['gk'], w['oq'], w['ok'],
              w['bias_sc'], w['bias_sn'], w['sink_s']]
    cache_spec = pl.BlockSpec((sb, WINDOW, KV_W), lambda i: (i, 0, 0))
    return pl.pallas_call(
        _decode_attn_kernel,
        out_shape=(jax.ShapeDtypeStruct((m, Q_W), BF16),
                   jax.ShapeDtypeStruct(ck.shape, F32),
                   jax.ShapeDtypeStruct(cv.shape, F32)),
        grid=(nb // sb,),
        in_specs=[pl.BlockSpec((sb * nt, d), lambda i: (i, 0)), cache_spec, cache_spec]
                 + [_const_spec(c.shape) for c in consts],
        out_specs=(pl.BlockSpec((sb * nt, Q_W), lambda i: (i, 0)), cache_spec, cache_spec),
        compiler_params=pltpu.CompilerParams(
            dimension_semantics=("arbitrary",), vmem_limit_bytes=VMEM_LIMIT),
        name="decode_attn",
    )(x, ck, cv, *consts)


def _decode_mixer_kernel(x_ref, cst_ref, h0_ref, n1g_ref, win_ref, cw_ref, cb_ref, bd_ref,
                         ba_ref, bx_ref, lam_ref,
                         h_out, pre_out, cnew_ref, hnew_ref):
    nb = h0_ref.shape[0]
    nt = x_ref.shape[0] // nb
    d = x_ref.shape[1]
    xn = _rms(x_ref[...], n1g_ref[...]).astype(BF16)
    xr = _dot(xn, win_ref[:, REST0:REST0 + d])
    prev = cst_ref[...]
    slabs = [prev[k * nb:(k + 1) * nb] for k in range(CONV_W - 1)]
    slabs += [xr[k * nb:(k + 1) * nb] for k in range(nt)]
    xc = jnp.concatenate(
        [cb_ref[...] + sum(cw_ref[j:j + 1, :] * slabs[ti + j] for j in range(CONV_W))
         for ti in range(nt)], axis=0)
    cnew_ref[...] = jnp.concatenate(slabs[-(CONV_W - 1):], axis=0)

    a, bterm = _rglru_gates(xc, bd_ref, ba_ref[...], bx_ref[...], lam_ref[...])
    h = h0_ref[...]
    hs = []
    for ti in range(nt):
        h = a[ti * nb:(ti + 1) * nb] * h + bterm[ti * nb:(ti + 1) * nb]
        hs.append(h)
    hnew_ref[...] = h
    h_out[...] = jnp.concatenate(hs, axis=0).astype(BF16)
    pre_out[...] = _dot(xn, win_ref[:, REST0 + d:REST0 + 4 * d]).astype(BF16)


def _decode_mixer(x, cst, h0, w):
    m, d = x.shape
    consts = [w['n1g'], w['win'], w['cw'], w['cb'], w['bd'], w['ba'], w['bx'], w['lam']]
    ins = [x, cst, h0] + consts
    whole = lambda shape: pl.BlockSpec(shape, lambda i: (0,) * len(shape))
    return pl.pallas_call(
        _decode_mixer_kernel,
        out_shape=(jax.ShapeDtypeStruct((m, d), BF16),
                   jax.ShapeDtypeStruct((m, 3 * d), BF16),
                   jax.ShapeDtypeStruct(cst.shape, F32),
                   jax.ShapeDtypeStruct(h0.shape, F32)),
        grid=(1,),
        in_specs=[_const_spec(c.shape) for c in ins],
        out_specs=(whole((m, d)), whole((m, 3 * d)), whole(cst.shape), whole(h0.shape)),
        compiler_params=pltpu.CompilerParams(
            dimension_semantics=("arbitrary",), vmem_limit_bytes=VMEM_LIMIT),
        name="decode_mixer",
    )(*ins)


def _rel_bucket(dist):
    n = np.maximum(dist, 0)
    max_exact = REL_BUCKETS // 2
    nf = np.maximum(n, 1).astype(np.float32)
    large = max_exact + (np.log(nf / max_exact) / math.log(REL_MAX_DIST / max_exact)
                         * (REL_BUCKETS - max_exact)).astype(np.int32)
    large = np.minimum(large, REL_BUCKETS - 1)
    return np.where(n < max_exact, n, large)


def _bias_rows(rel_bias, dist, mask):
    tb = jnp.where(mask[:, :, None], rel_bias[_rel_bucket(dist)].astype(F32), NEG_INF)
    tq, tk = dist.shape
    return jnp.transpose(tb, (2, 0, 1)).reshape(N_HEADS * tq, tk)


def _prompt_bias_rows(rel_bias):
    span = 3 * WINDOW
    k = np.arange(span)
    dist = WINDOW + np.where(k < WINDOW, k, k - span)
    valid = (dist >= 0) & (dist <= WINDOW)
    return jnp.where(valid[:, None], rel_bias[_rel_bucket(dist)].astype(F32), NEG_INF).T


def _gate_blocks(rg_wa, rg_wx, n_grp):
    per = rg_wa.shape[0] // n_grp
    both = jnp.stack([rg_wa, rg_wx]).reshape(2, n_grp, per, RNN_BS, RNN_BS)
    on_diag = np.eye(per, dtype=bool)[None, None, :, None, :, None]
    blocks = jnp.where(on_diag, both[:, :, :, :, None, :], 0.0)
    blocks = jnp.transpose(blocks, (1, 2, 3, 0, 4, 5))
    return blocks.reshape(n_grp, per * RNN_BS, 2 * per * RNN_BS).astype(BF16)


def _head_avg(width):
    idx = np.arange(width) // HEAD_DIM
    return jnp.asarray((idx[:, None] == idx[None, :]).astype(np.float32) / HEAD_DIM, BF16)


def _prepare(rel_bias, norm1_g, w_in, q_norm_g, k_norm_g, sinks, w_o_attn, conv_w, conv_b,
             rg_wa, rg_ba, rg_wx, rg_bx, rg_lambda, w_o_rnn, w_out, norm2_g, w_up, w_down,
             ple_norm_g, w_ple_gate, w_ple, n_dec):
    d = w_in.shape[0]
    order = [g * GROUP + j for j in range(GROUP) for g in range(KV_HEADS)]
    perm = np.concatenate([np.arange(h * HEAD_DIM, (h + 1) * HEAD_DIM) for h in order])
    row = lambda a: a.reshape(1, -1).astype(F32)
    w = {}
    w['n1g'] = row(norm1_g)
    w['win'] = w_in.astype(BF16)
    w['wqkv'] = jnp.concatenate([w['win'][:, :Q_W][:, perm], w['win'][:, Q_W:REST0]], axis=1)
    w['gq'] = row(jnp.tile(q_norm_g, N_HEADS)) * (HEAD_DIM ** -0.5)
    w['gk'] = row(jnp.tile(k_norm_g, KV_HEADS))
    w['oq'] = _head_avg(Q_W)
    w['ok'] = _head_avg(KV_W)
    w['woa'] = w_o_attn[perm, :].astype(BF16)
    w['cw'] = conv_w.astype(F32)
    w['cb'] = row(conv_b)
    w['bd'] = _gate_blocks(rg_wa, rg_wx, d // MXU_DIM)
    w['ba'] = row(rg_ba)
    w['bx'] = row(rg_bx)
    w['lam'] = row(rg_lambda)
    rows8 = lambda a: jnp.broadcast_to(a, (SUBLANES, a.shape[-1]))
    w['cw8'] = jnp.broadcast_to(w['cw'][:, None, :], (CONV_W, SUBLANES, d))
    w['cb8'], w['ba8'], w['bx8'], w['lam8'] = (rows8(w[k]) for k in ('cb', 'ba', 'bx', 'lam'))
    w['wor'] = w_o_rnn.astype(BF16)
    w['wout'] = w_out.astype(BF16)
    w['n2g'] = row(norm2_g)
    w['wup'] = w_up.astype(BF16)
    w['wdn'] = w_down.astype(BF16)
    w['pg'] = row(ple_norm_g)
    w['wpg'] = w_ple_gate.astype(BF16)
    w['wple'] = w_ple.astype(BF16)

    w['bias_p'] = _prompt_bias_rows(rel_bias)
    sink_rows = sinks.astype(F32).reshape(KV_HEADS, GROUP, 1)
    w['sink_p'] = jnp.repeat(sink_rows, WINDOW, axis=2).reshape(KV_HEADS, 1, GROUP * WINDOW)

    sb = SEQ_BLOCK
    ti = np.arange(n_dec)[:, None]
    dist_c = ti + WINDOW - np.arange(WINDOW)[None, :]
    dist_n = ti - np.arange(n_dec)[None, :]
    n_col = N_HEADS * sb * n_dec
    col = np.arange(n_col)
    col_head_t = col // (sb * n_dec) * n_dec + col % n_dec
    spread = (np.arange(N_HEADS * n_dec)[:, None] == col_head_t[None, :]).astype(np.float32)
    expand = lambda a: jnp.dot(a.T, spread, precision=lax.Precision.HIGHEST)
    w['bias_sc'] = expand(_bias_rows(rel_bias, dist_c, dist_c <= WINDOW))
    bias_n = expand(_bias_rows(rel_bias, dist_n, dist_n >= 0))
    own = (np.arange(sb * n_dec) // n_dec)[:, None] == (col // n_dec % sb)[None, :]
    w['bias_sn'] = jnp.where(own, jnp.tile(bias_n, (sb, 1)), NEG_INF)
    w['sink_s'] = jnp.repeat(sinks.astype(F32), sb * n_dec)[None, :]
    return w


def kernel(x_prompt, x_sample, cache_k_win, cache_v_win, state_conv, state_h, p_prompt, p_sample, rel_bias, norm1_g, w_in, q_norm_g, k_norm_g, sinks, w_o_attn, conv_w, conv_b, rg_wa, rg_ba, rg_wx, rg_bx, rg_lambda, w_o_rnn, w_out, norm2_g, w_up, w_down, ple_norm_g, w_ple_gate, w_ple):
    depth = w_in.shape[0]
    assert depth == 1, "single-layer step"
    b, t, d = x_prompt.shape
    nb, nt, _ = x_sample.shape
    w = _prepare(rel_bias, norm1_g[0], w_in[0], q_norm_g[0], k_norm_g[0], sinks[0], w_o_attn[0],
                 conv_w[0], conv_b[0], rg_wa[0], rg_ba[0], rg_wx[0], rg_bx[0], rg_lambda[0],
                 w_o_rnn[0], w_out[0], norm2_g[0], w_up[0], w_down[0], ple_norm_g[0],
                 w_ple_gate[0], w_ple[0], nt)

    hp_rows, attp, prep, kp, vp, cp, hp = _prompt_mixer(x_prompt, w)
    flat = lambda a: a.reshape(b * t, a.shape[-1])

    att, ks, vs = _decode_attn(x_sample.reshape(nb * nt, d),
                               cache_k_win[0].reshape(nb, WINDOW, KV_W),
                               cache_v_win[0].reshape(nb, WINDOW, KV_W), w)
    to_tm = lambda a: jnp.swapaxes(a, 0, 1).reshape(-1, a.shape[-1])
    xs_tm = to_tm(x_sample)
    hs_rows, pres, cs, hs = _decode_mixer(xs_tm, to_tm(state_conv[0]), state_h[0], w)

    yp, ys = _channel((flat(x_prompt), flat(hp_rows), flat(attp), flat(prep), flat(p_prompt[0])),
                      (xs_tm, hs_rows, to_tm(att.reshape(nb, nt, Q_W)), pres, to_tm(p_sample[0])), w)
    yp = yp.reshape(b, t, d)
    from_tm = lambda a, n: jnp.swapaxes(a.reshape(n, nb, a.shape[-1]), 0, 1)

    kv_shape = (1, -1, WINDOW, KV_HEADS, HEAD_DIM)
    return (yp, from_tm(ys, nt),
            kp.reshape(kv_shape), vp.reshape(kv_shape), cp[None], hp.reshape(1, b, d),
            ks.reshape(kv_shape), vs.reshape(kv_shape), from_tm(cs, CONV_W - 1)[None], hs[None])
```

```python
import functools
import math

import numpy as np
import jax
import jax.numpy as jnp
from jax import lax
from jax.experimental import pallas as pl
from jax.experimental.pallas import tpu as pltpu

F32 = jnp.float32
BF16 = jnp.bfloat16

N_HEADS = 8
KV_HEADS = 2
GROUP = N_HEADS // KV_HEADS
HEAD_DIM = 64
Q_W = N_HEADS * HEAD_DIM
KV_W = KV_HEADS * HEAD_DIM
REST0 = Q_W + 2 * KV_W
WINDOW = 128
REL_BUCKETS = 32
REL_MAX_DIST = 128
RNN_BS = 64
CONV_W = 4
RG_C = 8.0
EPS = 1e-6
NEG_INF = -1e30
EXPM1_SERIES_BELOW = 2.0 ** -11
SQRT_FLOOR = 1e-30

LANES = 128
SUBLANES = 8
MXU_DIM = 256
VMEM_LIMIT = 56 * 1024 * 1024

TM_MIX = 512
TM_MLP = 512
FF_CHUNK = 1024
SEQ_BLOCK = 16
PACE_LAG = 2
SEG_PAD = 4


def _dot(a, b):
    return jnp.dot(a, b, preferred_element_type=F32)


def _dot_nt(a, b):
    return lax.dot_general(a, b, (((1,), (1,)), ((), ())), preferred_element_type=F32)


def _rms(x, g):
    ms = jnp.mean(x * x, axis=-1, keepdims=True)
    return x * lax.rsqrt(ms + EPS) * g


def _head_rms(x, ones_ref, g):
    ms = _dot((x * x).astype(BF16), ones_ref[...])
    return x * lax.rsqrt(ms + EPS) * g


def _lower_half(shape):
    return lax.broadcasted_iota(jnp.int32, shape, len(shape) - 1) < HEAD_DIM


def _neg_expm1_2x(x, exp_x):
    return jnp.where(x > -EXPM1_SERIES_BELOW, (-2.0 * x) * (1.0 + x), 1.0 - exp_x * exp_x)


def _log_decay(lam):
    return -RG_C * jax.nn.softplus(-lam)


def _gate_math(ya, yx, xc, ba, bx, log_decay):
    r = jax.nn.sigmoid(ya + ba)
    i = jax.nn.sigmoid(yx + bx)
    log_a = log_decay * r
    a = jnp.exp(log_a)
    y = _neg_expm1_2x(log_a, a)
    bterm = (y * lax.rsqrt(jnp.maximum(y, SQRT_FLOOR))) * (i * xc)
    return a, bterm


def _rglru_gates(xc, bd_ref, ba, bx, lam):
    xcb = xc.astype(BF16)
    ya, yx = [], []
    for gi in range(xc.shape[1] // MXU_DIM):
        y = _dot(xcb[:, gi * MXU_DIM:(gi + 1) * MXU_DIM], bd_ref[gi])
        ya.append(y[:, :MXU_DIM])
        yx.append(y[:, MXU_DIM:])
    return _gate_math(jnp.concatenate(ya, axis=1), jnp.concatenate(yx, axis=1), xc, ba, bx,
                      _log_decay(lam))


def _prompt_mixer_kernel(x_ref, n1g_ref, wqkv_ref, win_ref, gq_ref, gk_ref, oq_ref, ok_ref,
                         bias_rows_ref, sink_ref, cw8_ref, cb8_ref, bd_ref, ba8_ref, bx8_ref, lam8_ref,
                         wor32_ref, wout32_ref, wup32_ref, wdn32_ref,
                         h_out, att, pre_out, kwin_ref, vwin_ref, cst_ref, hst_ref,
                         wor16_ref, wout16_ref, wup16_ref, wdn16_ref,
                         bias_ref, proj, projx, kbuf, vbuf, xp, hc, logit_s, e_s, xc_s, y_s, hl_s, ac_s, hn_s,
                         *, blocks_per_seq):
    tm, d = x_ref.shape
    n = pl.program_id(0)
    for src, dst in ((wor32_ref, wor16_ref), (wout32_ref, wout16_ref),
                     (wup32_ref, wup16_ref), (wdn32_ref, wdn16_ref)):
        dst[...] = src[...].astype(BF16)

    @pl.when(n == 0)
    def _():
        proj[1] = jnp.zeros(proj.shape[1:], F32)
        projx[1] = jnp.zeros(projx.shape[1:], F32)
        kbuf[...] = jnp.zeros(kbuf.shape, BF16)
        vbuf[...] = jnp.zeros(vbuf.shape, BF16)
        xp[...] = jnp.zeros(xp.shape, F32)
        hc[...] = jnp.zeros(hc.shape, F32)
        has_prev = lax.broadcasted_iota(jnp.int32, (2 * WINDOW, WINDOW), 0) >= WINDOW
        for h in range(N_HEADS):
            row = jnp.broadcast_to(bias_rows_ref[h:h + 1, :], (2 * WINDOW, bias_rows_ref.shape[1]))
            table = pltpu.roll(row, 0, 1, stride=1, stride_axis=0)[:, :WINDOW]
            at = (h // GROUP, slice(None), slice((h % GROUP) * WINDOW, (h % GROUP + 1) * WINDOW))
            bias_ref[(0,) + at] = table
            bias_ref[(1,) + at] = jnp.where(has_prev, table, NEG_INF)

    wslot = lax.rem(n, 2)
    rslot = 1 - wslot
    fresh = lax.rem(n + blocks_per_seq - 1, blocks_per_seq) == 0
    seg = tm // SUBLANES
    pitch = seg + SEG_PAD

    xn = _rms(x_ref[...], n1g_ref[...]).astype(BF16)
    n_qkv = wqkv_ref.shape[1]
    tokens = []

    def project(lo):
        res = _dot(xn, (wqkv_ref if lo < n_qkv else win_ref)[:, lo:lo + MXU_DIM])
        tokens.append(res[0:1, :])
        if lo < n_qkv:
            for i in range(MXU_DIM // LANES):
                proj[wslot, lo // LANES + i] = res[:, i * LANES:(i + 1) * LANES]
        elif lo < REST0 + d:
            for i in range(MXU_DIM // LANES):
                for sgm in range(SUBLANES):
                    projx[wslot, (lo - REST0) // LANES + i, sgm * pitch:sgm * pitch + seg, :] = (
                        res[sgm * seg:(sgm + 1) * seg, i * LANES:(i + 1) * LANES])
        else:
            pre_out[:, lo - REST0 - d:lo - REST0 - d + MXU_DIM] = res.astype(BF16)

    pending = list(range(0, win_ref.shape[1], MXU_DIM))

    def pace():
        if len(tokens) < PACE_LAG:
            return jnp.zeros((1, MXU_DIM), F32)
        bits = lax.bitcast_convert_type(tokens[-PACE_LAG], jnp.uint32)
        return lax.bitcast_convert_type((bits >> 16) >> 16, F32)

    def emit_stage1(count):
        for _ in range(min(count, len(pending))):
            project(pending.pop(0))

    def kept(lo, hi):
        return jnp.concatenate([proj[rslot, t] for t in range(lo // LANES, hi // LANES)], axis=1)

    emit_stage1(2)
    qn = _head_rms(kept(0, Q_W), oq_ref, gq_ref[...]).astype(BF16)
    kn = _head_rms(kept(Q_W, Q_W + KV_W), ok_ref, gk_ref[...])
    v = kept(Q_W + KV_W, n_qkv)
    kbuf[0:WINDOW, :] = jnp.where(fresh, jnp.zeros((WINDOW, KV_W), BF16), kbuf[tm:tm + WINDOW, :])
    vbuf[:, 0:WINDOW] = jnp.where(fresh, jnp.zeros((KV_W, WINDOW), BF16), vbuf[:, tm:tm + WINDOW])
    kbuf[WINDOW:WINDOW + tm, :] = kn.astype(BF16)
    vbuf[:, WINDOW:WINDOW + tm] = v.T.astype(BF16)
    kwin_ref[...] = kn[tm - WINDOW:, :]
    vwin_ref[...] = v[tm - WINDOW:, :]

    first = jnp.where(fresh, 1, 0)
    lower = _lower_half((WINDOW, LANES))
    zero = jnp.zeros((WINDOW, LANES), BF16)

    def attend(s):
        rows = slice(s * WINDOW, (s + 1) * WINDOW)
        keys = kbuf[s * WINDOW:(s + 2) * WINDOW, :]
        vals_t = vbuf[:, s * WINDOW:(s + 2) * WINDOW]
        outs = []
        for g in range(KV_HEADS):
            qs = []
            for j in range(GROUP):
                slab = qn[rows, j * LANES:(j + 1) * LANES]
                qs.append(jnp.where(lower, slab, zero) if g == 0 else jnp.where(lower, zero, slab))
            logit_s[g] = _dot_nt(keys, jnp.concatenate(qs, axis=0))
            yield
            inv = []
            for j in range(GROUP):
                blk = slice(j * WINDOW, (j + 1) * WINDOW)
                bias = bias_ref[first, g, :, blk] if s == 0 else bias_ref[0, g, :, blk]
                logit = logit_s[g, :, blk] + bias
                sink = sink_ref[g, :, blk] + pace()[:, :WINDOW]
                m = jnp.maximum(jnp.max(logit, axis=0, keepdims=True), sink)
                e = jnp.exp(logit - m)
                inv.append(1.0 / (jnp.sum(e, axis=0, keepdims=True) + jnp.exp(sink - m)))
                e_s[g, :, blk] = e.astype(BF16)
                yield
            outs.append(_dot(vals_t, e_s[g]) * jnp.concatenate(inv, axis=1))
        for j in range(GROUP):
            blk = slice(j * WINDOW, (j + 1) * WINDOW)
            both = jnp.concatenate([outs[0][:HEAD_DIM, blk], outs[1][HEAD_DIM:, blk]], axis=0)
            att[rows, j * LANES:(j + 1) * LANES] = both.T.astype(BF16)
        yield

    sub8 = lax.broadcasted_iota(jnp.int32, (SUBLANES, MXU_DIM), 0)

    def recur(gi):
        cols = slice(gi * MXU_DIM, (gi + 1) * MXU_DIM)
        tiles = range(gi * (MXU_DIM // LANES), (gi + 1) * (MXU_DIM // LANES))
        by_segment = lambda k: pl.ds(k, SUBLANES, stride=pitch)
        x_at = lambda k: jnp.concatenate([projx[rslot, t, by_segment(k), :] for t in tiles], axis=1)
        last = SUBLANES * pitch - SEG_PAD
        x_tail = lambda n_rows: jnp.concatenate(
            [projx[rslot, t, last - n_rows:last, :] for t in tiles], axis=1)
        prev = jnp.where(fresh, 0.0, xp[:, cols])
        window = [jnp.where(sub8 == 0, prev[SUBLANES - j:SUBLANES - j + 1, :],
                            pltpu.roll(x_at(seg - j), 1, axis=0)) for j in range(CONV_W - 1, 0, -1)]
        for k in range(seg):
            window.append(x_at(k))
            xc_s[gi, k * SUBLANES:(k + 1) * SUBLANES, :] = cb8_ref[:, cols] + sum(
                cw8_ref[j, :, cols] * window[j] for j in range(CONV_W))
            window.pop(0)
            if k % 16 == 15:
                yield
        xp[:, cols] = x_tail(SUBLANES)
        cst_ref[:, cols] = x_tail(CONV_W - 1)
        y_s[gi] = _dot(xc_s[gi].astype(BF16), bd_ref[gi])
        yield
        ba, bx = ba8_ref[:, cols], bx8_ref[:, cols]
        log_decay = _log_decay(lam8_ref[:, cols])
        h_loc = a_cum = None
        for k in range(seg):
            r = slice(k * SUBLANES, (k + 1) * SUBLANES)
            a, bterm = _gate_math(y_s[gi, r, :MXU_DIM], y_s[gi, r, MXU_DIM:], xc_s[gi, r, :] + pace(),
                                  ba, bx, log_decay)
            h_loc = bterm if k == 0 else a * h_loc + bterm
            a_cum = a if k == 0 else a * a_cum
            hl_s[gi, r, :] = h_loc
            ac_s[gi, r, :] = a_cum
            if k % 4 == 3:
                yield
        h_in = jnp.where(fresh, 0.0, hc[:, cols])
        carry_in = []
        for sgm in range(SUBLANES):
            carry_in.append(h_in)
            h_in = h_loc[sgm:sgm + 1, :] + a_cum[sgm:sgm + 1, :] * h_in
        hc[:, cols] = h_in
        hst_ref[:, cols] = h_in
        carry_in = jnp.concatenate(carry_in, axis=0)
        for k in range(seg):
            r = slice(k * SUBLANES, (k + 1) * SUBLANES)
            h = hl_s[gi, r, :] + ac_s[gi, r, :] * carry_in
            for i in range(MXU_DIM // LANES):
                hn_s[gi, i, by_segment(k), :] = h[:, i * LANES:(i + 1) * LANES]
            if k % 8 == 7:
                yield
        h_out[:, cols] = jnp.concatenate(
            [jnp.concatenate([hn_s[gi, i, sgm * pitch:sgm * pitch + seg, :] for sgm in range(SUBLANES)], axis=0)
             for i in range(MXU_DIM // LANES)], axis=1).astype(BF16)
        yield

    n_sub = tm // WINDOW
    n_grp = d // MXU_DIM
    pieces = []
    for u in range(max(n_sub, n_grp)):
        if u < n_sub:
            pieces.append(attend(u))
        if u < n_grp:
            pieces.append(recur(u))
    n_spots = n_sub * (KV_HEADS * (1 + GROUP) + 1) + n_grp * (2 + seg // 16 + seg // 4 + seg // 8)
    per_spot = len(pending) / n_spots
    due = 0.0
    for piece in pieces:
        for _ in piece:
            due += per_spot
            emit_stage1(int(due))
            due -= int(due)
    emit_stage1(len(pending))


def _const_spec(shape):
    nd = len(shape)
    return pl.BlockSpec(shape, lambda *_: (0,) * nd, pipeline_mode=pl.Buffered(1))


def _prompt_mixer(x, w):
    b, t, d = x.shape
    tm = TM_MIX
    nt = t // tm
    n_blocks = b * nt
    padded_rows = SUBLANES * (tm // SUBLANES + SEG_PAD)
    consts = [w['n1g'], w['wqkv'], w['win'], w['gq'], w['gk'], w['oq'], w['ok'],
              w['bias_p'], w['sink_p'], w['cw8'], w['cb8'], w['bd'], w['ba8'], w['bx8'], w['lam8']]
    to_round = [w['wor32'], w['wout32'], w['wup32'], w['wdn32']]
    bf16_rows = 2 * SUBLANES
    assert all(a.shape[0] % (n_blocks * bf16_rows) == 0 for a in to_round)
    out_shape = (
        jax.ShapeDtypeStruct((b, t, d), BF16),
        jax.ShapeDtypeStruct((b, t, Q_W), BF16),
        jax.ShapeDtypeStruct((b, t, 3 * d), BF16),
        jax.ShapeDtypeStruct((b, WINDOW, KV_W), F32),
        jax.ShapeDtypeStruct((b, WINDOW, KV_W), F32),
        jax.ShapeDtypeStruct((b, CONV_W - 1, d), F32),
        jax.ShapeDtypeStruct((b, 1, d), F32),
    ) + tuple(jax.ShapeDtypeStruct(a.shape, BF16) for a in to_round)

    def cur(n):
        i = jnp.minimum(n, n_blocks - 1)
        return (i // nt, i % nt, 0)

    def prv(n):
        i = jnp.maximum(n - 1, 0)
        return (i // nt, i % nt, 0)

    per_seq = lambda n: (jnp.maximum(n - 1, 0) // nt, 0, 0)
    chunk = lambda a: pl.BlockSpec((a.shape[0] // n_blocks, a.shape[1]),
                                   lambda n: (jnp.minimum(n, n_blocks - 1), 0))
    return pl.pallas_call(
        functools.partial(_prompt_mixer_kernel, blocks_per_seq=nt),
        out_shape=out_shape,
        grid=(n_blocks + 1,),
        in_specs=[pl.BlockSpec((None, tm, d), cur)]
                 + [_const_spec(c.shape) for c in consts]
                 + [chunk(a) for a in to_round],
        out_specs=(
            pl.BlockSpec((None, tm, d), prv),
            pl.BlockSpec((None, tm, Q_W), prv),
            pl.BlockSpec((None, tm, 3 * d), cur),
            pl.BlockSpec((None, WINDOW, KV_W), per_seq),
            pl.BlockSpec((None, WINDOW, KV_W), per_seq),
            pl.BlockSpec((None, CONV_W - 1, d), per_seq),
            pl.BlockSpec((None, 1, d), per_seq),
        ) + tuple(chunk(a) for a in to_round),
        scratch_shapes=[
            pltpu.VMEM((2, KV_HEADS, 2 * WINDOW, GROUP * WINDOW), F32),
            pltpu.VMEM((2, REST0 // LANES, tm, LANES), F32),
            pltpu.VMEM((2, d // LANES, padded_rows, LANES), F32),
            pltpu.VMEM((WINDOW + tm, KV_W), BF16),
            pltpu.VMEM((KV_W, WINDOW + tm), BF16),
            pltpu.VMEM((SUBLANES, d), F32),
            pltpu.VMEM((1, d), F32),
            pltpu.VMEM((KV_HEADS, 2 * WINDOW, GROUP * WINDOW), F32),
            pltpu.VMEM((KV_HEADS, 2 * WINDOW, GROUP * WINDOW), BF16),
            pltpu.VMEM((d // MXU_DIM, tm, MXU_DIM), F32),
            pltpu.VMEM((d // MXU_DIM, tm, 2 * MXU_DIM), F32),
            pltpu.VMEM((d // MXU_DIM, tm, MXU_DIM), F32),
            pltpu.VMEM((d // MXU_DIM, tm, MXU_DIM), F32),
            pltpu.VMEM((d // MXU_DIM, MXU_DIM // LANES, padded_rows, LANES), F32),
        ],
        compiler_params=pltpu.CompilerParams(
            dimension_semantics=("arbitrary",), vmem_limit_bytes=VMEM_LIMIT),
        name="prompt_mixer",
    )(x, *consts, *to_round)


def _channel_rows(x_ref, h_ref, att_ref, pre_ref, p_ref, woa_ref, wor_ref, wout_ref,
                  n2g_ref, wup_ref, wdn_ref, pg_ref, wpg_ref, wple_ref, o_ref):
    d = x_ref.shape[1]
    pre = pre_ref[...].astype(F32)
    hg = (h_ref[...].astype(F32) * jax.nn.gelu(pre[:, :d])).astype(BF16)
    rnn = _dot(hg, wor_ref[...])
    atto = _dot(att_ref[...], woa_ref[...])
    mix = (jax.nn.sigmoid(pre[:, d:2 * d]) * atto + jax.nn.sigmoid(pre[:, 2 * d:]) * rnn).astype(BF16)
    x = x_ref[...] + _dot(mix, wout_ref[...])
    xn = _rms(x, n2g_ref[...]).astype(BF16)
    acc = x
    for c in range(wup_ref.shape[1] // FF_CHUNK):
        cols = slice(c * FF_CHUNK, (c + 1) * FF_CHUNK)
        hmid = jnp.maximum(_dot(xn, wup_ref[:, cols]), 0.0)
        acc = acc + _dot((hmid * hmid).astype(BF16), wdn_ref[cols, :])
    gate = jax.nn.sigmoid(_dot(_rms(acc, pg_ref[...]).astype(BF16), wpg_ref[...]))
    o_ref[...] = acc + gate * _dot(p_ref[...].astype(BF16), wple_ref[...])


N_ROW_INPUTS = 5


def _channel_kernel(*refs, first_steps):
    first, second = refs[:N_ROW_INPUTS], refs[N_ROW_INPUTS:2 * N_ROW_INPUTS]
    consts, (o_first, o_second) = refs[2 * N_ROW_INPUTS:-2], refs[-2:]
    i = pl.program_id(0)

    @pl.when(i < first_steps)
    def _():
        _channel_rows(*first, *consts, o_first)

    @pl.when(i >= first_steps)
    def _():
        _channel_rows(*second, *consts, o_second)


def _channel(first, second, w):
    m1, d = first[0].shape
    m2 = second[0].shape[0]
    tm = min(TM_MLP, m1, m2)
    n1, n2 = m1 // tm, m2 // tm
    consts = [w['woa'], w['wor'], w['wout'], w['n2g'], w['wup'], w['wdn'], w['pg'], w['wpg'], w['wple']]
    rows1 = lambda a: pl.BlockSpec((tm, a.shape[1]), lambda i: (jnp.minimum(i, n1 - 1), 0))
    rows2 = lambda a: pl.BlockSpec((tm, a.shape[1]), lambda i: (jnp.maximum(i - n1, 0), 0),
                                   pipeline_mode=pl.Buffered(1))
    return pl.pallas_call(
        functools.partial(_channel_kernel, first_steps=n1),
        out_shape=(jax.ShapeDtypeStruct((m1, d), F32), jax.ShapeDtypeStruct((m2, d), F32)),
        grid=(n1 + n2,),
        in_specs=[rows1(a) for a in first] + [rows2(a) for a in second]
                 + [_const_spec(c.shape) for c in consts],
        out_specs=(rows1(first[0]), rows2(second[0])),
        compiler_params=pltpu.CompilerParams(
            dimension_semantics=("arbitrary",), vmem_limit_bytes=VMEM_LIMIT),
        name="merge_mlp_ple",
    )(*first, *second, *consts)


def _decode_attn_kernel(x_ref, ck_ref, cv_ref, n1g_ref, wqkv_ref, gq_ref, gk_ref, oq_ref, ok_ref,
                        bias_c_ref, bias_n_ref, sink_ref,
                        att_ref, kwin_ref, vwin_ref):
    sb = ck_ref.shape[0]
    nt = x_ref.shape[0] // sb
    xn = _rms(x_ref[...], n1g_ref[...]).astype(BF16)
    qkv = _dot(xn, wqkv_ref[...])
    qn = _head_rms(qkv[:, :Q_W], oq_ref, gq_ref[...])
    kn = _head_rms(qkv[:, Q_W:Q_W + KV_W], ok_ref, gk_ref[...])
    v = qkv[:, Q_W + KV_W:]

    n_rows = sb * nt
    ck = ck_ref[...]
    cv = cv_ref[...]
    for b in range(sb):
        rows = slice(b * nt, (b + 1) * nt)
        kwin_ref[b, 0:WINDOW - nt, :] = ck[b, nt:, :]
        vwin_ref[b, 0:WINDOW - nt, :] = cv[b, nt:, :]
        kwin_ref[b, WINDOW - nt:WINDOW, :] = kn[rows]
        vwin_ref[b, WINDOW - nt:WINDOW, :] = v[rows]

    qb = qn.astype(BF16)
    lower = _lower_half((n_rows, LANES))
    zero = jnp.zeros((n_rows, LANES), BF16)
    q_all = jnp.concatenate(
        [jnp.where(lower, qb[:, j * LANES:(j + 1) * LANES], zero) if g == 0
         else jnp.where(lower, zero, qb[:, j * LANES:(j + 1) * LANES])
         for g in range(KV_HEADS) for j in range(GROUP)], axis=0)
    n_col = q_all.shape[0]
    col_seq = (lax.broadcasted_iota(jnp.int32, (1, n_col), 1) // nt) % sb

    def own(big):
        out = big[0:WINDOW]
        for b in range(1, sb):
            out = jnp.where(col_seq == b, big[b * WINDOW:(b + 1) * WINDOW], out)
        return out

    st = own(_dot_nt(ck.reshape(sb * WINDOW, KV_W).astype(BF16), q_all)) + bias_c_ref[...]
    stx = _dot_nt(kn.astype(BF16), q_all) + bias_n_ref[...]
    sink = sink_ref[...]
    m = jnp.maximum(jnp.maximum(jnp.max(st, axis=0, keepdims=True),
                                jnp.max(stx, axis=0, keepdims=True)), sink)
    e = jnp.exp(st - m)
    ex = jnp.exp(stx - m)
    den = (jnp.sum(e, axis=0, keepdims=True) + jnp.sum(ex, axis=0, keepdims=True)
           + jnp.exp(sink - m))
    cv_t = jnp.concatenate([cv[b].T for b in range(sb)], axis=0).astype(BF16)
    out = own(_dot(cv_t, e.astype(BF16)))
    pad = WINDOW - n_rows
    v_t = jnp.concatenate([v, jnp.zeros((pad, KV_W), F32)], axis=0).T.astype(BF16)
    ex_pad = jnp.concatenate([ex, jnp.zeros((pad, n_col), F32)], axis=0).astype(BF16)
    out = (out + _dot(v_t, ex_pad)) / den
    half = n_col // KV_HEADS
    both_t = jnp.concatenate([out[:HEAD_DIM, :half], out[HEAD_DIM:, half:]], axis=0).T
    for j in range(GROUP):
        att_ref[:, j * LANES:(j + 1) * LANES] = both_t[j * n_rows:(j + 1) * n_rows].astype(att_ref.dtype)


def _decode_attn(x, ck, cv, w):
    m, d = x.shape
    nb = ck.shape[0]
    nt = m // nb
    sb = SEQ_BLOCK
    consts = [w['n1g'], w['wqkv'], w['gq'], w['gk'], w['oq'], w['ok'],
              w['bias_sc'], w['bias_sn'], w['sink_s']]
    cache_spec = pl.BlockSpec((sb, WINDOW, KV_W), lambda i: (i, 0, 0))
    return pl.pallas_call(
        _decode_attn_kernel,
        out_shape=(jax.ShapeDtypeStruct((m, Q_W), BF16),
                   jax.ShapeDtypeStruct(ck.shape, F32),
                   jax.ShapeDtypeStruct(cv.shape, F32)),
        grid=(nb // sb,),
        in_specs=[pl.BlockSpec((sb * nt, d), lambda i: (i, 0)), cache_spec, cache_spec]
                 + [_const_spec(c.shape) for c in consts],
        out_specs=(pl.BlockSpec((sb * nt, Q_W), lambda i: (i, 0)), cache_spec, cache_spec),
        compiler_params=pltpu.CompilerParams(
            dimension_semantics=("arbitrary",), vmem_limit_bytes=VMEM_LIMIT),
        name="decode_attn",
    )(x, ck, cv, *consts)


def _decode_mixer_kernel(x_ref, cst_ref, h0_ref, n1g_ref, win_ref, cw_ref, cb_ref, bd_ref,
                         ba_ref, bx_ref, lam_ref,
                         h_out, pre_out, cnew_ref, hnew_ref):
    nb = h0_ref.shape[0]
    nt = x_ref.shape[0] // nb
    d = x_ref.shape[1]
    xn = _rms(x_ref[...], n1g_ref[...]).astype(BF16)
    xr = _dot(xn, win_ref[:, REST0:REST0 + d])
    prev = cst_ref[...]
    slabs = [prev[k * nb:(k + 1) * nb] for k in range(CONV_W - 1)]
    slabs += [xr[k * nb:(k + 1) * nb] for k in range(nt)]
    xc = jnp.concatenate(
        [cb_ref[...] + sum(cw_ref[j:j + 1, :] * slabs[ti + j] for j in range(CONV_W))
         for ti in range(nt)], axis=0)
    cnew_ref[...] = jnp.concatenate(slabs[-(CONV_W - 1):], axis=0)

    a, bterm = _rglru_gates(xc, bd_ref, ba_ref[...], bx_ref[...], lam_ref[...])
    h = h0_ref[...]
    hs = []
    for ti in range(nt):
        h = a[ti * nb:(ti + 1) * nb] * h + bterm[ti * nb:(ti + 1) * nb]
        hs.append(h)
    hnew_ref[...] = h
    h_out[...] = jnp.concatenate(hs, axis=0).astype(BF16)
    pre_out[...] = _dot(xn, win_ref[:, REST0 + d:REST0 + 4 * d]).astype(BF16)


def _decode_mixer(x, cst, h0, w):
    m, d = x.shape
    consts = [w['n1g'], w['win'], w['cw'], w['cb'], w['bd'], w['ba'], w['bx'], w['lam']]
    ins = [x, cst, h0] + consts
    whole = lambda shape: pl.BlockSpec(shape, lambda i: (0,) * len(shape))
    return pl.pallas_call(
        _decode_mixer_kernel,
        out_shape=(jax.ShapeDtypeStruct((m, d), BF16),
                   jax.ShapeDtypeStruct((m, 3 * d), BF16),
                   jax.ShapeDtypeStruct(cst.shape, F32),
                   jax.ShapeDtypeStruct(h0.shape, F32)),
        grid=(1,),
        in_specs=[_const_spec(c.shape) for c in ins],
        out_specs=(whole((m, d)), whole((m, 3 * d)), whole(cst.shape), whole(h0.shape)),
        compiler_params=pltpu.CompilerParams(
            dimension_semantics=("arbitrary",), vmem_limit_bytes=VMEM_LIMIT),
        name="decode_mixer",
    )(*ins)


def _rel_bucket(dist):
    n = np.maximum(dist, 0)
    max_exact = REL_BUCKETS // 2
    nf = np.maximum(n, 1).astype(np.float32)
    large = max_exact + (np.log(nf / max_exact) / math.log(REL_MAX_DIST / max_exact)
                         * (REL_BUCKETS - max_exact)).astype(np.int32)
    large = np.minimum(large, REL_BUCKETS - 1)
    return np.where(n < max_exact, n, large)


def _bias_rows(rel_bias, dist, mask):
    tb = jnp.where(mask[:, :, None], rel_bias[_rel_bucket(dist)].astype(F32), NEG_INF)
    tq, tk = dist.shape
    return jnp.transpose(tb, (2, 0, 1)).reshape(N_HEADS * tq, tk)


def _prompt_bias_rows(rel_bias):
    span = 3 * WINDOW
    k = np.arange(span)
    dist = WINDOW + np.where(k < WINDOW, k, k - span)
    valid = (dist >= 0) & (dist <= WINDOW)
    return jnp.where(valid[:, None], rel_bias[_rel_bucket(dist)].astype(F32), NEG_INF).T


def _gate_blocks(rg_wa, rg_wx, n_grp):
    per = rg_wa.shape[0] // n_grp
    both = jnp.stack([rg_wa, rg_wx]).reshape(2, n_grp, per, RNN_BS, RNN_BS)
    on_diag = np.eye(per, dtype=bool)[None, None, :, None, :, None]
    blocks = jnp.where(on_diag, both[:, :, :, :, None, :], 0.0)
    blocks = jnp.transpose(blocks, (1, 2, 3, 0, 4, 5))
    return blocks.reshape(n_grp, per * RNN_BS, 2 * per * RNN_BS).astype(BF16)


def _head_avg(width):
    idx = np.arange(width) // HEAD_DIM
    return jnp.asarray((idx[:, None] == idx[None, :]).astype(np.float32) / HEAD_DIM, BF16)


def _prepare(rel_bias, norm1_g, w_in, q_norm_g, k_norm_g, sinks, w_o_attn, conv_w, conv_b,
             rg_wa, rg_ba, rg_wx, rg_bx, rg_lambda, w_o_rnn, w_out, norm2_g, w_up, w_down,
             ple_norm_g, w_ple_gate, w_ple, n_dec):
    d = w_in.shape[0]
    order = [g * GROUP + j for j in range(GROUP) for g in range(KV_HEADS)]
    perm = np.concatenate([np.arange(h * HEAD_DIM, (h + 1) * HEAD_DIM) for h in order])
    row = lambda a: a.reshape(1, -1).astype(F32)
    w = {}
    w['n1g'] = row(norm1_g)
    w['win'] = w_in.astype(BF16)
    w['wqkv'] = jnp.concatenate([w['win'][:, :Q_W][:, perm], w['win'][:, Q_W:REST0]], axis=1)
    w['gq'] = row(jnp.tile(q_norm_g, N_HEADS)) * (HEAD_DIM ** -0.5)
    w['gk'] = row(jnp.tile(k_norm_g, KV_HEADS))
    w['oq'] = _head_avg(Q_W)
    w['ok'] = _head_avg(KV_W)
    w['woa'] = w_o_attn[perm, :].astype(BF16)
    w['cw'] = conv_w.astype(F32)
    w['cb'] = row(conv_b)
    w['bd'] = _gate_blocks(rg_wa, rg_wx, d // MXU_DIM)
    w['ba'] = row(rg_ba)
    w['bx'] = row(rg_bx)
    w['lam'] = row(rg_lambda)
    rows8 = lambda a: jnp.broadcast_to(a, (SUBLANES, a.shape[-1]))
    w['cw8'] = jnp.broadcast_to(w['cw'][:, None, :], (CONV_W, SUBLANES, d))
    w['cb8'], w['ba8'], w['bx8'], w['lam8'] = (rows8(w[k]) for k in ('cb', 'ba', 'bx', 'lam'))
    w['wor32'], w['wout32'], w['wup32'], w['wdn32'] = w_o_rnn, w_out, w_up, w_down
    w['n2g'] = row(norm2_g)
    w['pg'] = row(ple_norm_g)
    w['wpg'] = w_ple_gate.astype(BF16)
    w['wple'] = w_ple.astype(BF16)

    w['bias_p'] = _prompt_bias_rows(rel_bias)
    sink_rows = sinks.astype(F32).reshape(KV_HEADS, GROUP, 1)
    w['sink_p'] = jnp.repeat(sink_rows, WINDOW, axis=2).reshape(KV_HEADS, 1, GROUP * WINDOW)

    sb = SEQ_BLOCK
    ti = np.arange(n_dec)[:, None]
    dist_c = ti + WINDOW - np.arange(WINDOW)[None, :]
    dist_n = ti - np.arange(n_dec)[None, :]
    n_col = N_HEADS * sb * n_dec
    col = np.arange(n_col)
    col_head_t = col // (sb * n_dec) * n_dec + col % n_dec
    spread = (np.arange(N_HEADS * n_dec)[:, None] == col_head_t[None, :]).astype(np.float32)
    expand = lambda a: jnp.dot(a.T, spread, precision=lax.Precision.HIGHEST)
    w['bias_sc'] = expand(_bias_rows(rel_bias, dist_c, dist_c <= WINDOW))
    bias_n = expand(_bias_rows(rel_bias, dist_n, dist_n >= 0))
    own = (np.arange(sb * n_dec) // n_dec)[:, None] == (col // n_dec % sb)[None, :]
    w['bias_sn'] = jnp.where(own, jnp.tile(bias_n, (sb, 1)), NEG_INF)
    w['sink_s'] = jnp.repeat(sinks.astype(F32), sb * n_dec)[None, :]
    return w


def kernel(x_prompt, x_sample, cache_k_win, cache_v_win, state_conv, state_h, p_prompt, p_sample, rel_bias, norm1_g, w_in, q_norm_g, k_norm_g, sinks, w_o_attn, conv_w, conv_b, rg_wa, rg_ba, rg_wx, rg_bx, rg_lambda, w_o_rnn, w_out, norm2_g, w_up, w_down, ple_norm_g, w_ple_gate, w_ple):
    depth = w_in.shape[0]
    assert depth == 1, "single-layer step"
    b, t, d = x_prompt.shape
    nb, nt, _ = x_sample.shape
    w = _prepare(rel_bias, norm1_g[0], w_in[0], q_norm_g[0], k_norm_g[0], sinks[0], w_o_attn[0],
                 conv_w[0], conv_b[0], rg_wa[0], rg_ba[0], rg_wx[0], rg_bx[0], rg_lambda[0],
                 w_o_rnn[0], w_out[0], norm2_g[0], w_up[0], w_down[0], ple_norm_g[0],
                 w_ple_gate[0], w_ple[0], nt)

    (hp_rows, attp, prep, kp, vp, cp, hp,
     w['wor'], w['wout'], w['wup'], w['wdn']) = _prompt_mixer(x_prompt, w)
    flat = lambda a: a.reshape(b * t, a.shape[-1])

    att, ks, vs = _decode_attn(x_sample.reshape(nb * nt, d),
                               cache_k_win[0].reshape(nb, WINDOW, KV_W),
                               cache_v_win[0].reshape(nb, WINDOW, KV_W), w)
    to_tm = lambda a: jnp.swapaxes(a, 0, 1).reshape(-1, a.shape[-1])
    xs_tm = to_tm(x_sample)
    hs_rows, pres, cs, hs = _decode_mixer(xs_tm, to_tm(state_conv[0]), state_h[0], w)

    yp, ys = _channel((flat(x_prompt), flat(hp_rows), flat(attp), flat(prep), flat(p_prompt[0])),
                      (xs_tm, hs_rows, to_tm(att.reshape(nb, nt, Q_W)), pres, to_tm(p_sample[0])), w)
    yp = yp.reshape(b, t, d)
    from_tm = lambda a, n: jnp.swapaxes(a.reshape(n, nb, a.shape[-1]), 0, 1)

    kv_shape = (1, -1, WINDOW, KV_HEADS, HEAD_DIM)
    return (yp, from_tm(ys, nt),
            kp.reshape(kv_shape), vp.reshape(kv_shape), cp[None], hp.reshape(1, b, d),
            ks.reshape(kv_shape), vs.reshape(kv_shape), from_tm(cs, CONV_W - 1)[None], hs[None])
```

```python
import functools
import math

import numpy as np
import jax
import jax.numpy as jnp
from jax import lax
from jax.experimental import pallas as pl
from jax.experimental.pallas import tpu as pltpu

F32 = jnp.float32
BF16 = jnp.bfloat16

N_HEADS = 8
KV_HEADS = 2
GROUP = N_HEADS // KV_HEADS
HEAD_DIM = 64
Q_W = N_HEADS * HEAD_DIM
KV_W = KV_HEADS * HEAD_DIM
REST0 = Q_W + 2 * KV_W
WINDOW = 128
REL_BUCKETS = 32
REL_MAX_DIST = 128
RNN_BS = 64
CONV_W = 4
RG_C = 8.0
EPS = 1e-6
NEG_INF = -1e30
EXPM1_SERIES_BELOW = 2.0 ** -11
SQRT_FLOOR = 1e-30

LANES = 128
SUBLANES = 8
MXU_DIM = 256
VMEM_LIMIT = 56 * 1024 * 1024

TM_MIX = 512
TM_MLP = 512
FF_CHUNK = 1024
SEQ_BLOCK = 16
PACE_LAG = 2
SEG_PAD = 4


def _dot(a, b):
    return jnp.dot(a, b, preferred_element_type=F32)


def _dot_nt(a, b):
    return lax.dot_general(a, b, (((1,), (1,)), ((), ())), preferred_element_type=F32)


def _rms(x, g):
    ms = jnp.mean(x * x, axis=-1, keepdims=True)
    return x * lax.rsqrt(ms + EPS) * g


def _head_rms(x, ones_ref, g):
    ms = _dot((x * x).astype(BF16), ones_ref[...])
    return x * lax.rsqrt(ms + EPS) * g


def _lower_half(shape):
    return lax.broadcasted_iota(jnp.int32, shape, len(shape) - 1) < HEAD_DIM


def _neg_expm1_2x(x, exp_x):
    return jnp.where(x > -EXPM1_SERIES_BELOW, (-2.0 * x) * (1.0 + x), 1.0 - exp_x * exp_x)


def _log_decay(lam):
    return -RG_C * jax.nn.softplus(-lam)


def _gate_math(ya, yx, xc, ba, bx, log_decay):
    r = jax.nn.sigmoid(ya + ba)
    i = jax.nn.sigmoid(yx + bx)
    log_a = log_decay * r
    a = jnp.exp(log_a)
    y = _neg_expm1_2x(log_a, a)
    bterm = (y * lax.rsqrt(jnp.maximum(y, SQRT_FLOOR))) * (i * xc)
    return a, bterm


def _rglru_gates(xc, bd_ref, ba, bx, lam):
    xcb = xc.astype(BF16)
    ya, yx = [], []
    for gi in range(xc.shape[1] // MXU_DIM):
        y = _dot(xcb[:, gi * MXU_DIM:(gi + 1) * MXU_DIM], bd_ref[gi])
        ya.append(y[:, :MXU_DIM])
        yx.append(y[:, MXU_DIM:])
    return _gate_math(jnp.concatenate(ya, axis=1), jnp.concatenate(yx, axis=1), xc, ba, bx,
                      _log_decay(lam))


def _prompt_mixer_kernel(x_ref, n1g_ref, wqkv_ref, win_ref, gq_ref, gk_ref, oq_ref, ok_ref,
                         bias_rows_ref, sink_ref, cw8_ref, cb8_ref, bd_ref, ba8_ref, bx8_ref, lam8_ref,
                         wor32_ref, wout32_ref, wup32_ref, wdn32_ref,
                         h_out, att, pre_out, kwin_ref, vwin_ref, cst_ref, hst_ref,
                         wor16_ref, wout16_ref, wup16_ref, wdn16_ref,
                         bias_ref, proj, projx, kbuf, vbuf, xp, hc, logit_s, e_s, xc_s, y_s, hl_s, ac_s, hn_s,
                         *, blocks_per_seq):
    tm, d = x_ref.shape
    n = pl.program_id(0)
    for src, dst in ((wor32_ref, wor16_ref), (wout32_ref, wout16_ref),
                     (wup32_ref, wup16_ref), (wdn32_ref, wdn16_ref)):
        dst[...] = src[...].astype(BF16)

    @pl.when(n == 0)
    def _():
        proj[1] = jnp.zeros(proj.shape[1:], F32)
        projx[1] = jnp.zeros(projx.shape[1:], F32)
        kbuf[...] = jnp.zeros(kbuf.shape, BF16)
        vbuf[...] = jnp.zeros(vbuf.shape, BF16)
        xp[...] = jnp.zeros(xp.shape, F32)
        hc[...] = jnp.zeros(hc.shape, F32)
        has_prev = lax.broadcasted_iota(jnp.int32, (2 * WINDOW, WINDOW), 0) >= WINDOW
        for h in range(N_HEADS):
            row = jnp.broadcast_to(bias_rows_ref[h:h + 1, :], (2 * WINDOW, bias_rows_ref.shape[1]))
            table = pltpu.roll(row, 0, 1, stride=1, stride_axis=0)[:, :WINDOW]
            at = (h // GROUP, slice(None), slice((h % GROUP) * WINDOW, (h % GROUP + 1) * WINDOW))
            bias_ref[(0,) + at] = table
            bias_ref[(1,) + at] = jnp.where(has_prev, table, NEG_INF)

    wslot = lax.rem(n, 2)
    rslot = 1 - wslot
    fresh = lax.rem(n + blocks_per_seq - 1, blocks_per_seq) == 0
    seg = tm // SUBLANES
    pitch = seg + SEG_PAD

    xn = _rms(x_ref[...], n1g_ref[...]).astype(BF16)
    n_qkv = wqkv_ref.shape[1]
    tokens = []

    def project(lo):
        res = _dot(xn, (wqkv_ref if lo < n_qkv else win_ref)[:, lo:lo + MXU_DIM])
        tokens.append(res[0:1, :])
        if lo < n_qkv:
            for i in range(MXU_DIM // LANES):
                proj[wslot, lo // LANES + i] = res[:, i * LANES:(i + 1) * LANES]
        elif lo < REST0 + d:
            for i in range(MXU_DIM // LANES):
                for sgm in range(SUBLANES):
                    projx[wslot, (lo - REST0) // LANES + i, sgm * pitch:sgm * pitch + seg, :] = (
                        res[sgm * seg:(sgm + 1) * seg, i * LANES:(i + 1) * LANES])
        else:
            pre_out[:, lo - REST0 - d:lo - REST0 - d + MXU_DIM] = res.astype(BF16)

    pending = list(range(0, win_ref.shape[1], MXU_DIM))

    def pace():
        if len(tokens) < PACE_LAG:
            return jnp.zeros((1, MXU_DIM), F32)
        bits = lax.bitcast_convert_type(tokens[-PACE_LAG], jnp.uint32)
        return lax.bitcast_convert_type((bits >> 16) >> 16, F32)

    def emit_stage1(count):
        for _ in range(min(count, len(pending))):
            project(pending.pop(0))

    def kept(lo, hi):
        return jnp.concatenate([proj[rslot, t] for t in range(lo // LANES, hi // LANES)], axis=1)

    emit_stage1(2)
    qn = _head_rms(kept(0, Q_W), oq_ref, gq_ref[...]).astype(BF16)
    kn = _head_rms(kept(Q_W, Q_W + KV_W), ok_ref, gk_ref[...])
    v = kept(Q_W + KV_W, n_qkv)
    kbuf[0:WINDOW, :] = jnp.where(fresh, jnp.zeros((WINDOW, KV_W), BF16), kbuf[tm:tm + WINDOW, :])
    vbuf[:, 0:WINDOW] = jnp.where(fresh, jnp.zeros((KV_W, WINDOW), BF16), vbuf[:, tm:tm + WINDOW])
    kbuf[WINDOW:WINDOW + tm, :] = kn.astype(BF16)
    vbuf[:, WINDOW:WINDOW + tm] = v.T.astype(BF16)
    kwin_ref[...] = kn[tm - WINDOW:, :]
    vwin_ref[...] = v[tm - WINDOW:, :]

    first = jnp.where(fresh, 1, 0)
    lower = _lower_half((WINDOW, LANES))
    zero = jnp.zeros((WINDOW, LANES), BF16)

    def attend(s):
        rows = slice(s * WINDOW, (s + 1) * WINDOW)
        keys = kbuf[s * WINDOW:(s + 2) * WINDOW, :]
        vals_t = vbuf[:, s * WINDOW:(s + 2) * WINDOW]
        outs = []
        for g in range(KV_HEADS):
            qs = []
            for j in range(GROUP):
                slab = qn[rows, j * LANES:(j + 1) * LANES]
                qs.append(jnp.where(lower, slab, zero) if g == 0 else jnp.where(lower, zero, slab))
            logit_s[g] = _dot_nt(keys, jnp.concatenate(qs, axis=0))
            yield
            inv = []
            for j in range(GROUP):
                blk = slice(j * WINDOW, (j + 1) * WINDOW)
                bias = bias_ref[first, g, :, blk] if s == 0 else bias_ref[0, g, :, blk]
                logit = logit_s[g, :, blk] + bias
                sink = sink_ref[g, :, blk] + pace()[:, :WINDOW]
                m = jnp.maximum(jnp.max(logit, axis=0, keepdims=True), sink)
                e = jnp.exp(logit - m)
                inv.append(1.0 / (jnp.sum(e, axis=0, keepdims=True) + jnp.exp(sink - m)))
                e_s[g, :, blk] = e.astype(BF16)
                yield
            outs.append(_dot(vals_t, e_s[g]) * jnp.concatenate(inv, axis=1))
        for j in range(GROUP):
            blk = slice(j * WINDOW, (j + 1) * WINDOW)
            both = jnp.concatenate([outs[0][:HEAD_DIM, blk], outs[1][HEAD_DIM:, blk]], axis=0)
            att[rows, j * LANES:(j + 1) * LANES] = both.T.astype(BF16)
        yield

    sub8 = lax.broadcasted_iota(jnp.int32, (SUBLANES, MXU_DIM), 0)

    def recur(gi):
        cols = slice(gi * MXU_DIM, (gi + 1) * MXU_DIM)
        tiles = range(gi * (MXU_DIM // LANES), (gi + 1) * (MXU_DIM // LANES))
        by_segment = lambda k: pl.ds(k, SUBLANES, stride=pitch)
        x_at = lambda k: jnp.concatenate([projx[rslot, t, by_segment(k), :] for t in tiles], axis=1)
        last = SUBLANES * pitch - SEG_PAD
        x_tail = lambda n_rows: jnp.concatenate(
            [projx[rslot, t, last - n_rows:last, :] for t in tiles], axis=1)
        prev = jnp.where(fresh, 0.0, xp[:, cols])
        window = [jnp.where(sub8 == 0, prev[SUBLANES - j:SUBLANES - j + 1, :],
                            pltpu.roll(x_at(seg - j), 1, axis=0)) for j in range(CONV_W - 1, 0, -1)]
        for k in range(seg):
            window.append(x_at(k))
            xc_s[gi, k * SUBLANES:(k + 1) * SUBLANES, :] = cb8_ref[:, cols] + sum(
                cw8_ref[j, :, cols] * window[j] for j in range(CONV_W))
            window.pop(0)
            if k % 16 == 15:
                yield
        xp[:, cols] = x_tail(SUBLANES)
        cst_ref[:, cols] = x_tail(CONV_W - 1)
        y_s[gi] = _dot(xc_s[gi].astype(BF16), bd_ref[gi])
        yield
        ba, bx = ba8_ref[:, cols], bx8_ref[:, cols]
        log_decay = _log_decay(lam8_ref[:, cols])
        h_loc = a_cum = None
        for k in range(seg):
            r = slice(k * SUBLANES, (k + 1) * SUBLANES)
            a, bterm = _gate_math(y_s[gi, r, :MXU_DIM], y_s[gi, r, MXU_DIM:], xc_s[gi, r, :] + pace(),
                                  ba, bx, log_decay)
            h_loc = bterm if k == 0 else a * h_loc + bterm
            a_cum = a if k == 0 else a * a_cum
            hl_s[gi, r, :] = h_loc
            ac_s[gi, r, :] = a_cum
            if k % 4 == 3:
                yield
        h_in = jnp.where(fresh, 0.0, hc[:, cols])
        carry_in = []
        for sgm in range(SUBLANES):
            carry_in.append(h_in)
            h_in = h_loc[sgm:sgm + 1, :] + a_cum[sgm:sgm + 1, :] * h_in
        hc[:, cols] = h_in
        hst_ref[:, cols] = h_in
        carry_in = jnp.concatenate(carry_in, axis=0)
        for k in range(seg):
            r = slice(k * SUBLANES, (k + 1) * SUBLANES)
            h = hl_s[gi, r, :] + ac_s[gi, r, :] * carry_in
            for i in range(MXU_DIM // LANES):
                hn_s[gi, i, by_segment(k), :] = h[:, i * LANES:(i + 1) * LANES]
            if k % 8 == 7:
                yield
        h_out[:, cols] = jnp.concatenate(
            [jnp.concatenate([hn_s[gi, i, sgm * pitch:sgm * pitch + seg, :] for sgm in range(SUBLANES)], axis=0)
             for i in range(MXU_DIM // LANES)], axis=1).astype(BF16)
        yield

    n_sub = tm // WINDOW
    n_grp = d // MXU_DIM
    pieces = []
    for u in range(max(n_sub, n_grp)):
        if u < n_sub:
            pieces.append(attend(u))
        if u < n_grp:
            pieces.append(recur(u))
    n_spots = n_sub * (KV_HEADS * (1 + GROUP) + 1) + n_grp * (2 + seg // 16 + seg // 4 + seg // 8)
    per_spot = len(pending) / n_spots
    due = 0.0
    for piece in pieces:
        for _ in piece:
            due += per_spot
            emit_stage1(int(due))
            due -= int(due)
    emit_stage1(len(pending))


def _const_spec(shape):
    nd = len(shape)
    return pl.BlockSpec(shape, lambda *_: (0,) * nd, pipeline_mode=pl.Buffered(1))


def _prompt_mixer(x, w):
    b, t, d = x.shape
    tm = TM_MIX
    nt = t // tm
    n_blocks = b * nt
    padded_rows = SUBLANES * (tm // SUBLANES + SEG_PAD)
    consts = [w['n1g'], w['wqkv'], w['win'], w['gq'], w['gk'], w['oq'], w['ok'],
              w['bias_p'], w['sink_p'], w['cw8'], w['cb8'], w['bd'], w['ba8'], w['bx8'], w['lam8']]
    to_round = [w['wor32'], w['wout32'], w['wup32'], w['wdn32']]
    bf16_rows = 2 * SUBLANES
    assert all(a.shape[0] % (n_blocks * bf16_rows) == 0 for a in to_round)
    out_shape = (
        jax.ShapeDtypeStruct((b, t, d), BF16),
        jax.ShapeDtypeStruct((b, t, Q_W), BF16),
        jax.ShapeDtypeStruct((b, t, 3 * d), BF16),
        jax.ShapeDtypeStruct((b, WINDOW, KV_W), F32),
        jax.ShapeDtypeStruct((b, WINDOW, KV_W), F32),
        jax.ShapeDtypeStruct((b, CONV_W - 1, d), F32),
        jax.ShapeDtypeStruct((b, 1, d), F32),
    ) + tuple(jax.ShapeDtypeStruct(a.shape, BF16) for a in to_round)

    def cur(n):
        i = jnp.minimum(n, n_blocks - 1)
        return (i // nt, i % nt, 0)

    def prv(n):
        i = jnp.maximum(n - 1, 0)
        return (i // nt, i % nt, 0)

    per_seq = lambda n: (jnp.maximum(n - 1, 0) // nt, 0, 0)
    chunk = lambda a: pl.BlockSpec((a.shape[0] // n_blocks, a.shape[1]),
                                   lambda n: (jnp.minimum(n, n_blocks - 1), 0))
    return pl.pallas_call(
        functools.partial(_prompt_mixer_kernel, blocks_per_seq=nt),
        out_shape=out_shape,
        grid=(n_blocks + 1,),
        in_specs=[pl.BlockSpec((None, tm, d), cur)]
                 + [_const_spec(c.shape) for c in consts]
                 + [chunk(a) for a in to_round],
        out_specs=(
            pl.BlockSpec((None, tm, d), prv),
            pl.BlockSpec((None, tm, Q_W), prv),
            pl.BlockSpec((None, tm, 3 * d), cur),
            pl.BlockSpec((None, WINDOW, KV_W), per_seq),
            pl.BlockSpec((None, WINDOW, KV_W), per_seq),
            pl.BlockSpec((None, CONV_W - 1, d), per_seq),
            pl.BlockSpec((None, 1, d), per_seq),
        ) + tuple(chunk(a) for a in to_round),
        scratch_shapes=[
            pltpu.VMEM((2, KV_HEADS, 2 * WINDOW, GROUP * WINDOW), F32),
            pltpu.VMEM((2, REST0 // LANES, tm, LANES), F32),
            pltpu.VMEM((2, d // LANES, padded_rows, LANES), F32),
            pltpu.VMEM((WINDOW + tm, KV_W), BF16),
            pltpu.VMEM((KV_W, WINDOW + tm), BF16),
            pltpu.VMEM((SUBLANES, d), F32),
            pltpu.VMEM((1, d), F32),
            pltpu.VMEM((KV_HEADS, 2 * WINDOW, GROUP * WINDOW), F32),
            pltpu.VMEM((KV_HEADS, 2 * WINDOW, GROUP * WINDOW), BF16),
            pltpu.VMEM((d // MXU_DIM, tm, MXU_DIM), F32),
            pltpu.VMEM((d // MXU_DIM, tm, 2 * MXU_DIM), F32),
            pltpu.VMEM((d // MXU_DIM, tm, MXU_DIM), F32),
            pltpu.VMEM((d // MXU_DIM, tm, MXU_DIM), F32),
            pltpu.VMEM((d // MXU_DIM, MXU_DIM // LANES, padded_rows, LANES), F32),
        ],
        compiler_params=pltpu.CompilerParams(
            dimension_semantics=("arbitrary",), vmem_limit_bytes=VMEM_LIMIT),
        name="prompt_mixer",
    )(x, *consts, *to_round)


def _channel_rows(x_ref, h_ref, att_ref, pre_ref, p_ref, woa_ref, wor_ref, wout_ref,
                  n2g_ref, wup_ref, wdn_ref, pg_ref, wpg_ref, wple_ref, o_ref):
    d = x_ref.shape[1]
    pre = pre_ref[...].astype(F32)
    hg = (h_ref[...].astype(F32) * jax.nn.gelu(pre[:, :d])).astype(BF16)
    rnn = _dot(hg, wor_ref[...])
    atto = _dot(att_ref[...], woa_ref[...])
    mix = (jax.nn.sigmoid(pre[:, d:2 * d]) * atto + jax.nn.sigmoid(pre[:, 2 * d:]) * rnn).astype(BF16)
    x = x_ref[...] + _dot(mix, wout_ref[...])
    xn = _rms(x, n2g_ref[...]).astype(BF16)
    acc = x
    for c in range(wup_ref.shape[1] // FF_CHUNK):
        cols = slice(c * FF_CHUNK, (c + 1) * FF_CHUNK)
        hmid = jnp.maximum(_dot(xn, wup_ref[:, cols]), 0.0)
        acc = acc + _dot((hmid * hmid).astype(BF16), wdn_ref[cols, :])
    gate = jax.nn.sigmoid(_dot(_rms(acc, pg_ref[...]).astype(BF16), wpg_ref[...]))
    o_ref[...] = acc + gate * _dot(p_ref[...].astype(BF16), wple_ref[...])


N_ROW_INPUTS = 5


def _channel_kernel(*refs, first_steps):
    first, second = refs[:N_ROW_INPUTS], refs[N_ROW_INPUTS:2 * N_ROW_INPUTS]
    consts, (o_first, o_second) = refs[2 * N_ROW_INPUTS:-2], refs[-2:]
    i = pl.program_id(0)

    @pl.when(i < first_steps)
    def _():
        _channel_rows(*first, *consts, o_first)

    @pl.when(i >= first_steps)
    def _():
        _channel_rows(*second, *consts, o_second)


def _channel(first, second, w):
    m1, d = first[0].shape
    m2 = second[0].shape[0]
    tm = min(TM_MLP, m1, m2)
    n1, n2 = m1 // tm, m2 // tm
    consts = [w['woa'], w['wor'], w['wout'], w['n2g'], w['wup'], w['wdn'], w['pg'], w['wpg'], w['wple']]
    rows1 = lambda a: pl.BlockSpec((tm, a.shape[1]), lambda i: (jnp.minimum(i, n1 - 1), 0))
    rows2 = lambda a: pl.BlockSpec((tm, a.shape[1]), lambda i: (jnp.maximum(i - n1, 0), 0),
                                   pipeline_mode=pl.Buffered(1))
    return pl.pallas_call(
        functools.partial(_channel_kernel, first_steps=n1),
        out_shape=(jax.ShapeDtypeStruct((m1, d), F32), jax.ShapeDtypeStruct((m2, d), F32)),
        grid=(n1 + n2,),
        in_specs=[rows1(a) for a in first] + [rows2(a) for a in second]
                 + [_const_spec(c.shape) for c in consts],
        out_specs=(rows1(first[0]), rows2(second[0])),
        compiler_params=pltpu.CompilerParams(
            dimension_semantics=("arbitrary",), vmem_limit_bytes=VMEM_LIMIT),
        name="merge_mlp_ple",
    )(*first, *second, *consts)


def _decode_attn_kernel(x_ref, ck_ref, cv_ref, n1g_ref, wqkv_ref, gq_ref, gk_ref, oq_ref, ok_ref,
                        bias_c_ref, bias_n_ref, sink_ref,
                        att_ref, kwin_ref, vwin_ref):
    sb = ck_ref.shape[0]
    nt = x_ref.shape[0] // sb
    xn = _rms(x_ref[...], n1g_ref[...]).astype(BF16)
    qkv = _dot(xn, wqkv_ref[...])
    qn = _head_rms(qkv[:, :Q_W], oq_ref, gq_ref[...])
    kn = _head_rms(qkv[:, Q_W:Q_W + KV_W], ok_ref, gk_ref[...])
    v = qkv[:, Q_W + KV_W:]

    n_rows = sb * nt
    ck = ck_ref[...]
    cv = cv_ref[...]
    for b in range(sb):
        rows = slice(b * nt, (b + 1) * nt)
        kwin_ref[b, 0:WINDOW - nt, :] = ck[b, nt:, :]
        vwin_ref[b, 0:WINDOW - nt, :] = cv[b, nt:, :]
        kwin_ref[b, WINDOW - nt:WINDOW, :] = kn[rows]
        vwin_ref[b, WINDOW - nt:WINDOW, :] = v[rows]

    qb = qn.astype(BF16)
    lower = _lower_half((n_rows, LANES))
    zero = jnp.zeros((n_rows, LANES), BF16)
    q_all = jnp.concatenate(
        [jnp.where(lower, qb[:, j * LANES:(j + 1) * LANES], zero) if g == 0
         else jnp.where(lower, zero, qb[:, j * LANES:(j + 1) * LANES])
         for g in range(KV_HEADS) for j in range(GROUP)], axis=0)
    n_col = q_all.shape[0]
    col_seq = (lax.broadcasted_iota(jnp.int32, (1, n_col), 1) // nt) % sb

    def own(big):
        out = big[0:WINDOW]
        for b in range(1, sb):
            out = jnp.where(col_seq == b, big[b * WINDOW:(b + 1) * WINDOW], out)
        return out

    st = own(_dot_nt(ck.reshape(sb * WINDOW, KV_W).astype(BF16), q_all)) + bias_c_ref[...]
    stx = _dot_nt(kn.astype(BF16), q_all) + bias_n_ref[...]
    sink = sink_ref[...]
    m = jnp.maximum(jnp.maximum(jnp.max(st, axis=0, keepdims=True),
                                jnp.max(stx, axis=0, keepdims=True)), sink)
    e = jnp.exp(st - m)
    ex = jnp.exp(stx - m)
    den = (jnp.sum(e, axis=0, keepdims=True) + jnp.sum(ex, axis=0, keepdims=True)
           + jnp.exp(sink - m))
    cv_t = jnp.concatenate([cv[b].T for b in range(sb)], axis=0).astype(BF16)
    out = own(_dot(cv_t, e.astype(BF16)))
    pad = WINDOW - n_rows
    v_t = jnp.concatenate([v, jnp.zeros((pad, KV_W), F32)], axis=0).T.astype(BF16)
    ex_pad = jnp.concatenate([ex, jnp.zeros((pad, n_col), F32)], axis=0).astype(BF16)
    out = (out + _dot(v_t, ex_pad)) / den
    half = n_col // KV_HEADS
    both_t = jnp.concatenate([out[:HEAD_DIM, :half], out[HEAD_DIM:, half:]], axis=0).T
    for j in range(GROUP):
        att_ref[:, j * LANES:(j + 1) * LANES] = both_t[j * n_rows:(j + 1) * n_rows].astype(att_ref.dtype)


def _decode_attn(x, ck, cv, w):
    m, d = x.shape
    nb = ck.shape[0]
    nt = m // nb
    sb = SEQ_BLOCK
    consts = [w['n1g'], w['wqkv'], w['gq'], w['gk'], w['oq'], w['ok'],
              w['bias_sc'], w['bias_sn'], w['sink_s']]
    cache_spec = pl.BlockSpec((sb, WINDOW, KV_W), lambda i: (i, 0, 0))
    return pl.pallas_call(
        _decode_attn_kernel,
        out_shape=(jax.ShapeDtypeStruct((m, Q_W), BF16),
                   jax.ShapeDtypeStruct(ck.shape, F32),
                   jax.ShapeDtypeStruct(cv.shape, F32)),
        grid=(nb // sb,),
        in_specs=[pl.BlockSpec((sb * nt, d), lambda i: (i, 0)), cache_spec, cache_spec]
                 + [_const_spec(c.shape) for c in consts],
        out_specs=(pl.BlockSpec((sb * nt, Q_W), lambda i: (i, 0)), cache_spec, cache_spec),
        compiler_params=pltpu.CompilerParams(
            dimension_semantics=("arbitrary",), vmem_limit_bytes=VMEM_LIMIT),
        name="decode_attn",
    )(x, ck, cv, *consts)


def _decode_mixer_kernel(x_ref, cst_ref, h0_ref, n1g_ref, win_ref, cw_ref, cb_ref, bd_ref,
                         ba_ref, bx_ref, lam_ref,
                         h_out, pre_out, cnew_ref, hnew_ref,
                         xr_s, c_s, h_s):
    nb = h0_ref.shape[0]
    nt = x_ref.shape[0] // nb
    nc = cst_ref.shape[0] // nb
    d = x_ref.shape[1]
    lanes = [slice(j * LANES, (j + 1) * LANES) for j in range(d // LANES)]
    xn = _rms(x_ref[...], n1g_ref[...]).astype(BF16)
    xr = _dot(xn, win_ref[:, REST0:REST0 + d])
    pre_out[...] = _dot(xn, win_ref[:, REST0 + d:REST0 + 4 * d]).astype(BF16)
    for j, cols in enumerate(lanes):
        xr_s[j] = xr[:, cols]
        c_s[j] = cst_ref[:, cols]

    def step_rows(ref, k, n_steps):
        return jnp.concatenate([ref[j, pl.ds(k, nb, stride=n_steps), :] for j in range(len(lanes))], axis=1)

    def put_step_rows(ref, k, n_steps, val):
        for j, cols in enumerate(lanes):
            ref[j, pl.ds(k, nb, stride=n_steps), :] = val[:, cols]

    slabs = [step_rows(c_s, k, nc) for k in range(nc)] + [step_rows(xr_s, ti, nt) for ti in range(nt)]
    xc = jnp.concatenate(
        [cb_ref[...] + sum(cw_ref[j:j + 1, :] * slabs[ti + j] for j in range(CONV_W))
         for ti in range(nt)], axis=0)
    for k, slab in enumerate(slabs[-nc:]):
        put_step_rows(c_s, k, nc, slab)
    cnew_ref[...] = jnp.concatenate([c_s[j] for j in range(len(lanes))], axis=1)

    a, bterm = _rglru_gates(xc, bd_ref, ba_ref[...], bx_ref[...], lam_ref[...])
    h = h0_ref[...]
    for ti in range(nt):
        h = a[ti * nb:(ti + 1) * nb] * h + bterm[ti * nb:(ti + 1) * nb]
        put_step_rows(h_s, ti, nt, h)
    hnew_ref[...] = h
    h_out[...] = jnp.concatenate([h_s[j] for j in range(len(lanes))], axis=1).astype(BF16)


def _decode_mixer(x, cst, h0, w):
    m, d = x.shape
    consts = [w['n1g'], w['win'], w['cw'], w['cb'], w['bd'], w['ba'], w['bx'], w['lam']]
    ins = [x, cst, h0] + consts
    whole = lambda shape: pl.BlockSpec(shape, lambda i: (0,) * len(shape))
    tiled = lambda rows: pltpu.VMEM((d // LANES, rows, LANES), F32)
    return pl.pallas_call(
        _decode_mixer_kernel,
        out_shape=(jax.ShapeDtypeStruct((m, d), BF16),
                   jax.ShapeDtypeStruct((m, 3 * d), BF16),
                   jax.ShapeDtypeStruct(cst.shape, F32),
                   jax.ShapeDtypeStruct(h0.shape, F32)),
        grid=(1,),
        in_specs=[_const_spec(c.shape) for c in ins],
        out_specs=(whole((m, d)), whole((m, 3 * d)), whole(cst.shape), whole(h0.shape)),
        scratch_shapes=[tiled(m), tiled(cst.shape[0]), tiled(m)],
        compiler_params=pltpu.CompilerParams(
            dimension_semantics=("arbitrary",), vmem_limit_bytes=VMEM_LIMIT),
        name="decode_mixer",
    )(*ins)


def _rel_bucket(dist):
    n = np.maximum(dist, 0)
    max_exact = REL_BUCKETS // 2
    nf = np.maximum(n, 1).astype(np.float32)
    large = max_exact + (np.log(nf / max_exact) / math.log(REL_MAX_DIST / max_exact)
                         * (REL_BUCKETS - max_exact)).astype(np.int32)
    large = np.minimum(large, REL_BUCKETS - 1)
    return np.where(n < max_exact, n, large)


def _bias_rows(rel_bias, dist, mask):
    tb = jnp.where(mask[:, :, None], rel_bias[_rel_bucket(dist)].astype(F32), NEG_INF)
    tq, tk = dist.shape
    return jnp.transpose(tb, (2, 0, 1)).reshape(N_HEADS * tq, tk)


def _prompt_bias_rows(rel_bias):
    span = 3 * WINDOW
    k = np.arange(span)
    dist = WINDOW + np.where(k < WINDOW, k, k - span)
    valid = (dist >= 0) & (dist <= WINDOW)
    return jnp.where(valid[:, None], rel_bias[_rel_bucket(dist)].astype(F32), NEG_INF).T


def _gate_blocks(rg_wa, rg_wx, n_grp):
    per = rg_wa.shape[0] // n_grp
    both = jnp.stack([rg_wa, rg_wx]).reshape(2, n_grp, per, RNN_BS, RNN_BS)
    on_diag = np.eye(per, dtype=bool)[None, None, :, None, :, None]
    blocks = jnp.where(on_diag, both[:, :, :, :, None, :], 0.0)
    blocks = jnp.transpose(blocks, (1, 2, 3, 0, 4, 5))
    return blocks.reshape(n_grp, per * RNN_BS, 2 * per * RNN_BS).astype(BF16)


def _head_avg(width):
    idx = np.arange(width) // HEAD_DIM
    return jnp.asarray((idx[:, None] == idx[None, :]).astype(np.float32) / HEAD_DIM, BF16)


def _prepare(rel_bias, norm1_g, w_in, q_norm_g, k_norm_g, sinks, w_o_attn, conv_w, conv_b,
             rg_wa, rg_ba, rg_wx, rg_bx, rg_lambda, w_o_rnn, w_out, norm2_g, w_up, w_down,
             ple_norm_g, w_ple_gate, w_ple, n_dec):
    d = w_in.shape[0]
    order = [g * GROUP + j for j in range(GROUP) for g in range(KV_HEADS)]
    perm = np.concatenate([np.arange(h * HEAD_DIM, (h + 1) * HEAD_DIM) for h in order])
    row = lambda a: a.reshape(1, -1).astype(F32)
    w = {}
    w['n1g'] = row(norm1_g)
    w['win'] = w_in.astype(BF16)
    w['wqkv'] = jnp.concatenate([w['win'][:, :Q_W][:, perm], w['win'][:, Q_W:REST0]], axis=1)
    w['gq'] = row(jnp.tile(q_norm_g, N_HEADS)) * (HEAD_DIM ** -0.5)
    w['gk'] = row(jnp.tile(k_norm_g, KV_HEADS))
    w['oq'] = _head_avg(Q_W)
    w['ok'] = _head_avg(KV_W)
    w['woa'] = w_o_attn[perm, :].astype(BF16)
    w['cw'] = conv_w.astype(F32)
    w['cb'] = row(conv_b)
    w['bd'] = _gate_blocks(rg_wa, rg_wx, d // MXU_DIM)
    w['ba'] = row(rg_ba)
    w['bx'] = row(rg_bx)
    w['lam'] = row(rg_lambda)
    rows8 = lambda a: jnp.broadcast_to(a, (SUBLANES, a.shape[-1]))
    w['cw8'] = jnp.broadcast_to(w['cw'][:, None, :], (CONV_W, SUBLANES, d))
    w['cb8'], w['ba8'], w['bx8'], w['lam8'] = (rows8(w[k]) for k in ('cb', 'ba', 'bx', 'lam'))
    w['wor32'], w['wout32'], w['wup32'], w['wdn32'] = w_o_rnn, w_out, w_up, w_down
    w['n2g'] = row(norm2_g)
    w['pg'] = row(ple_norm_g)
    w['wpg'] = w_ple_gate.astype(BF16)
    w['wple'] = w_ple.astype(BF16)

    w['bias_p'] = _prompt_bias_rows(rel_bias)
    sink_rows = sinks.astype(F32).reshape(KV_HEADS, GROUP, 1)
    w['sink_p'] = jnp.repeat(sink_rows, WINDOW, axis=2).reshape(KV_HEADS, 1, GROUP * WINDOW)

    sb = SEQ_BLOCK
    ti = np.arange(n_dec)[:, None]
    dist_c = ti + WINDOW - np.arange(WINDOW)[None, :]
    dist_n = ti - np.arange(n_dec)[None, :]
    n_col = N_HEADS * sb * n_dec
    col = np.arange(n_col)
    col_head_t = col // (sb * n_dec) * n_dec + col % n_dec
    spread = (np.arange(N_HEADS * n_dec)[:, None] == col_head_t[None, :]).astype(np.float32)
    expand = lambda a: jnp.dot(a.T, spread, precision=lax.Precision.HIGHEST)
    w['bias_sc'] = expand(_bias_rows(rel_bias, dist_c, dist_c <= WINDOW))
    bias_n = expand(_bias_rows(rel_bias, dist_n, dist_n >= 0))
    own = (np.arange(sb * n_dec) // n_dec)[:, None] == (col // n_dec % sb)[None, :]
    w['bias_sn'] = jnp.where(own, jnp.tile(bias_n, (sb, 1)), NEG_INF)
    w['sink_s'] = jnp.repeat(sinks.astype(F32), sb * n_dec)[None, :]
    return w


def kernel(x_prompt, x_sample, cache_k_win, cache_v_win, state_conv, state_h, p_prompt, p_sample, rel_bias, norm1_g, w_in, q_norm_g, k_norm_g, sinks, w_o_attn, conv_w, conv_b, rg_wa, rg_ba, rg_wx, rg_bx, rg_lambda, w_o_rnn, w_out, norm2_g, w_up, w_down, ple_norm_g, w_ple_gate, w_ple):
    depth = w_in.shape[0]
    assert depth == 1, "single-layer step"
    b, t, d = x_prompt.shape
    nb, nt, _ = x_sample.shape
    w = _prepare(rel_bias, norm1_g[0], w_in[0], q_norm_g[0], k_norm_g[0], sinks[0], w_o_attn[0],
                 conv_w[0], conv_b[0], rg_wa[0], rg_ba[0], rg_wx[0], rg_bx[0], rg_lambda[0],
                 w_o_rnn[0], w_out[0], norm2_g[0], w_up[0], w_down[0], ple_norm_g[0],
                 w_ple_gate[0], w_ple[0], nt)

    (hp_rows, attp, prep, kp, vp, cp, hp,
     w['wor'], w['wout'], w['wup'], w['wdn']) = _prompt_mixer(x_prompt, w)
    flat = lambda a: a.reshape(b * t, a.shape[-1])

    xs_rows = x_sample.reshape(nb * nt, d)
    att, ks, vs = _decode_attn(xs_rows,
                               cache_k_win[0].reshape(nb, WINDOW, KV_W),
                               cache_v_win[0].reshape(nb, WINDOW, KV_W), w)
    hs_rows, pres, cs, hs = _decode_mixer(xs_rows, state_conv[0].reshape(nb * (CONV_W - 1), d),
                                          state_h[0], w)

    yp, ys = _channel((flat(x_prompt), flat(hp_rows), flat(attp), flat(prep), flat(p_prompt[0])),
                      (xs_rows, hs_rows, att, pres, p_sample[0].reshape(nb * nt, -1)), w)

    kv_shape = (1, -1, WINDOW, KV_HEADS, HEAD_DIM)
    return (yp.reshape(b, t, d), ys.reshape(nb, nt, d),
            kp.reshape(kv_shape), vp.reshape(kv_shape), cp[None], hp.reshape(1, b, d),
            ks.reshape(kv_shape), vs.reshape(kv_shape), cs.reshape(1, nb, CONV_W - 1, d), hs[None])
```

```python
import functools
import math

import numpy as np
import jax
import jax.numpy as jnp
from jax import lax
from jax.experimental import pallas as pl
from jax.experimental.pallas import tpu as pltpu

F32 = jnp.float32
BF16 = jnp.bfloat16

N_HEADS = 8
KV_HEADS = 2
GROUP = N_HEADS // KV_HEADS
HEAD_DIM = 64
Q_W = N_HEADS * HEAD_DIM
KV_W = KV_HEADS * HEAD_DIM
REST0 = Q_W + 2 * KV_W
WINDOW = 128
REL_BUCKETS = 32
REL_MAX_DIST = 128
RNN_BS = 64
CONV_W = 4
RG_C = 8.0
EPS = 1e-6
NEG_INF = -1e30
EXPM1_SERIES_BELOW = 2.0 ** -11
SQRT_FLOOR = 1e-30

LANES = 128
SUBLANES = 8
MXU_DIM = 256
VMEM_LIMIT = 56 * 1024 * 1024

TM_MIX = 512
TM_MLP = 512
FF_CHUNK = 1024
SEQ_BLOCK = 16
PACE_LAG = 2
SEG_PAD = 4


def _dot(a, b):
    return jnp.dot(a, b, preferred_element_type=F32)


def _dot_nt(a, b):
    return lax.dot_general(a, b, (((1,), (1,)), ((), ())), preferred_element_type=F32)


def _rms(x, g):
    ms = jnp.mean(x * x, axis=-1, keepdims=True)
    return x * lax.rsqrt(ms + EPS) * g


def _head_rms(x, ones_ref, g):
    ms = _dot((x * x).astype(BF16), ones_ref[...])
    return x * lax.rsqrt(ms + EPS) * g


def _lower_half(shape):
    return lax.broadcasted_iota(jnp.int32, shape, len(shape) - 1) < HEAD_DIM


def _neg_expm1_2x(x, exp_x):
    return jnp.where(x > -EXPM1_SERIES_BELOW, (-2.0 * x) * (1.0 + x), 1.0 - exp_x * exp_x)


def _log_decay(lam):
    return -RG_C * jax.nn.softplus(-lam)


def _gate_math(ya, yx, xc, ba, bx, log_decay):
    r = jax.nn.sigmoid(ya + ba)
    i = jax.nn.sigmoid(yx + bx)
    log_a = log_decay * r
    a = jnp.exp(log_a)
    y = _neg_expm1_2x(log_a, a)
    bterm = (y * lax.rsqrt(jnp.maximum(y, SQRT_FLOOR))) * (i * xc)
    return a, bterm


def _rglru_gates(xc, bd_ref, ba, bx, lam):
    xcb = xc.astype(BF16)
    ya, yx = [], []
    for gi in range(xc.shape[1] // MXU_DIM):
        y = _dot(xcb[:, gi * MXU_DIM:(gi + 1) * MXU_DIM], bd_ref[gi])
        ya.append(y[:, :MXU_DIM])
        yx.append(y[:, MXU_DIM:])
    return _gate_math(jnp.concatenate(ya, axis=1), jnp.concatenate(yx, axis=1), xc, ba, bx,
                      _log_decay(lam))


def _prompt_mixer_kernel(x_ref, n1g_ref, wqkv_ref, win_ref, gq_ref, gk_ref, oq_ref, ok_ref,
                         bias_rows_ref, sink_ref, cw8_ref, cb8_ref, bd_ref, ba8_ref, bx8_ref, lam8_ref,
                         wor32_ref, wout32_ref, wup32_ref, wdn32_ref,
                         h_out, att, pre_out, kwin_ref, vwin_ref, cst_ref, hst_ref,
                         wor16_ref, wout16_ref, wup16_ref, wdn16_ref, gq_copy_ref,
                         bias_ref, proj, projx, kbuf, vbuf, xp, hc, logit_s, e_s, xc_s, y_s, hl_s, ac_s, hn_s,
                         *, blocks_per_seq):
    tm, d = x_ref.shape
    n = pl.program_id(0)
    for src, dst in ((wor32_ref, wor16_ref), (wout32_ref, wout16_ref),
                     (wup32_ref, wup16_ref), (wdn32_ref, wdn16_ref)):
        dst[...] = src[...].astype(BF16)
    gq_copy_ref[...] = gq_ref[...]

    @pl.when(n == 0)
    def _():
        proj[1] = jnp.zeros(proj.shape[1:], F32)
        projx[1] = jnp.zeros(projx.shape[1:], F32)
        kbuf[...] = jnp.zeros(kbuf.shape, BF16)
        vbuf[...] = jnp.zeros(vbuf.shape, BF16)
        xp[...] = jnp.zeros(xp.shape, F32)
        hc[...] = jnp.zeros(hc.shape, F32)
        has_prev = lax.broadcasted_iota(jnp.int32, (2 * WINDOW, WINDOW), 0) >= WINDOW
        for h in range(N_HEADS):
            row = jnp.broadcast_to(bias_rows_ref[h:h + 1, :], (2 * WINDOW, bias_rows_ref.shape[1]))
            table = pltpu.roll(row, 0, 1, stride=1, stride_axis=0)[:, :WINDOW]
            at = (h // GROUP, slice(None), slice((h % GROUP) * WINDOW, (h % GROUP + 1) * WINDOW))
            bias_ref[(0,) + at] = table
            bias_ref[(1,) + at] = jnp.where(has_prev, table, NEG_INF)

    wslot = lax.rem(n, 2)
    rslot = 1 - wslot
    fresh = lax.rem(n + blocks_per_seq - 1, blocks_per_seq) == 0
    seg = tm // SUBLANES
    pitch = seg + SEG_PAD

    xn = _rms(x_ref[...], n1g_ref[...]).astype(BF16)
    n_qkv = wqkv_ref.shape[1]
    tokens = []

    def project(lo):
        res = _dot(xn, (wqkv_ref if lo < n_qkv else win_ref)[:, lo:lo + MXU_DIM])
        tokens.append(res[0:1, :])
        if lo < n_qkv:
            for i in range(MXU_DIM // LANES):
                proj[wslot, lo // LANES + i] = res[:, i * LANES:(i + 1) * LANES]
        elif lo < REST0 + d:
            for i in range(MXU_DIM // LANES):
                for sgm in range(SUBLANES):
                    projx[wslot, (lo - REST0) // LANES + i, sgm * pitch:sgm * pitch + seg, :] = (
                        res[sgm * seg:(sgm + 1) * seg, i * LANES:(i + 1) * LANES])
        else:
            pre_out[:, lo - REST0 - d:lo - REST0 - d + MXU_DIM] = res.astype(BF16)

    pending = list(range(0, win_ref.shape[1], MXU_DIM))

    def pace():
        if len(tokens) < PACE_LAG:
            return jnp.zeros((1, MXU_DIM), F32)
        bits = lax.bitcast_convert_type(tokens[-PACE_LAG], jnp.uint32)
        return lax.bitcast_convert_type((bits >> 16) >> 16, F32)

    def emit_stage1(count):
        for _ in range(min(count, len(pending))):
            project(pending.pop(0))

    def kept(lo, hi):
        return jnp.concatenate([proj[rslot, t] for t in range(lo // LANES, hi // LANES)], axis=1)

    emit_stage1(2)
    qn = _head_rms(kept(0, Q_W), oq_ref, gq_ref[...]).astype(BF16)
    kn = _head_rms(kept(Q_W, Q_W + KV_W), ok_ref, gk_ref[...])
    v = kept(Q_W + KV_W, n_qkv)
    kbuf[0:WINDOW, :] = jnp.where(fresh, jnp.zeros((WINDOW, KV_W), BF16), kbuf[tm:tm + WINDOW, :])
    vbuf[:, 0:WINDOW] = jnp.where(fresh, jnp.zeros((KV_W, WINDOW), BF16), vbuf[:, tm:tm + WINDOW])
    kbuf[WINDOW:WINDOW + tm, :] = kn.astype(BF16)
    vbuf[:, WINDOW:WINDOW + tm] = v.T.astype(BF16)
    kwin_ref[...] = kn[tm - WINDOW:, :]
    vwin_ref[...] = v[tm - WINDOW:, :]

    first = jnp.where(fresh, 1, 0)
    lower = _lower_half((WINDOW, LANES))
    zero = jnp.zeros((WINDOW, LANES), BF16)

    def attend(s):
        rows = slice(s * WINDOW, (s + 1) * WINDOW)
        keys = kbuf[s * WINDOW:(s + 2) * WINDOW, :]
        vals_t = vbuf[:, s * WINDOW:(s + 2) * WINDOW]
        outs = []
        for g in range(KV_HEADS):
            qs = []
            for j in range(GROUP):
                slab = qn[rows, j * LANES:(j + 1) * LANES]
                qs.append(jnp.where(lower, slab, zero) if g == 0 else jnp.where(lower, zero, slab))
            logit_s[g] = _dot_nt(keys, jnp.concatenate(qs, axis=0))
            yield
            inv = []
            for j in range(GROUP):
                blk = slice(j * WINDOW, (j + 1) * WINDOW)
                bias = bias_ref[first, g, :, blk] if s == 0 else bias_ref[0, g, :, blk]
                logit = logit_s[g, :, blk] + bias
                sink = sink_ref[g, :, blk] + pace()[:, :WINDOW]
                m = jnp.maximum(jnp.max(logit, axis=0, keepdims=True), sink)
                e = jnp.exp(logit - m)
                inv.append(1.0 / (jnp.sum(e, axis=0, keepdims=True) + jnp.exp(sink - m)))
                e_s[g, :, blk] = e.astype(BF16)
                yield
            outs.append(_dot(vals_t, e_s[g]) * jnp.concatenate(inv, axis=1))
        for j in range(GROUP):
            blk = slice(j * WINDOW, (j + 1) * WINDOW)
            both = jnp.concatenate([outs[0][:HEAD_DIM, blk], outs[1][HEAD_DIM:, blk]], axis=0)
            att[rows, j * LANES:(j + 1) * LANES] = both.T.astype(BF16)
        yield

    sub8 = lax.broadcasted_iota(jnp.int32, (SUBLANES, MXU_DIM), 0)

    def recur(gi):
        cols = slice(gi * MXU_DIM, (gi + 1) * MXU_DIM)
        tiles = range(gi * (MXU_DIM // LANES), (gi + 1) * (MXU_DIM // LANES))
        by_segment = lambda k: pl.ds(k, SUBLANES, stride=pitch)
        x_at = lambda k: jnp.concatenate([projx[rslot, t, by_segment(k), :] for t in tiles], axis=1)
        last = SUBLANES * pitch - SEG_PAD
        x_tail = lambda n_rows: jnp.concatenate(
            [projx[rslot, t, last - n_rows:last, :] for t in tiles], axis=1)
        prev = jnp.where(fresh, 0.0, xp[:, cols])
        window = [jnp.where(sub8 == 0, prev[SUBLANES - j:SUBLANES - j + 1, :],
                            pltpu.roll(x_at(seg - j), 1, axis=0)) for j in range(CONV_W - 1, 0, -1)]
        for k in range(seg):
            window.append(x_at(k))
            xc_s[gi, k * SUBLANES:(k + 1) * SUBLANES, :] = cb8_ref[:, cols] + sum(
                cw8_ref[j, :, cols] * window[j] for j in range(CONV_W))
            window.pop(0)
            if k % 16 == 15:
                yield
        xp[:, cols] = x_tail(SUBLANES)
        cst_ref[:, cols] = x_tail(CONV_W - 1)
        y_s[gi] = _dot(xc_s[gi].astype(BF16), bd_ref[gi])
        yield
        ba, bx = ba8_ref[:, cols], bx8_ref[:, cols]
        log_decay = _log_decay(lam8_ref[:, cols])
        h_loc = a_cum = None
        for k in range(seg):
            r = slice(k * SUBLANES, (k + 1) * SUBLANES)
            a, bterm = _gate_math(y_s[gi, r, :MXU_DIM], y_s[gi, r, MXU_DIM:], xc_s[gi, r, :] + pace(),
                                  ba, bx, log_decay)
            h_loc = bterm if k == 0 else a * h_loc + bterm
            a_cum = a if k == 0 else a * a_cum
            hl_s[gi, r, :] = h_loc
            ac_s[gi, r, :] = a_cum
            if k % 4 == 3:
                yield
        h_in = jnp.where(fresh, 0.0, hc[:, cols])
        carry_in = []
        for sgm in range(SUBLANES):
            carry_in.append(h_in)
            h_in = h_loc[sgm:sgm + 1, :] + a_cum[sgm:sgm + 1, :] * h_in
        hc[:, cols] = h_in
        hst_ref[:, cols] = h_in
        carry_in = jnp.concatenate(carry_in, axis=0)
        for k in range(seg):
            r = slice(k * SUBLANES, (k + 1) * SUBLANES)
            h = hl_s[gi, r, :] + ac_s[gi, r, :] * carry_in
            for i in range(MXU_DIM // LANES):
                hn_s[gi, i, by_segment(k), :] = h[:, i * LANES:(i + 1) * LANES]
            if k % 8 == 7:
                yield
        h_out[:, cols] = jnp.concatenate(
            [jnp.concatenate([hn_s[gi, i, sgm * pitch:sgm * pitch + seg, :] for sgm in range(SUBLANES)], axis=0)
             for i in range(MXU_DIM // LANES)], axis=1).astype(BF16)
        yield

    n_sub = tm // WINDOW
    n_grp = d // MXU_DIM
    pieces = []
    for u in range(max(n_sub, n_grp)):
        if u < n_sub:
            pieces.append(attend(u))
        if u < n_grp:
            pieces.append(recur(u))
    n_spots = n_sub * (KV_HEADS * (1 + GROUP) + 1) + n_grp * (2 + seg // 16 + seg // 4 + seg // 8)
    per_spot = len(pending) / n_spots
    due = 0.0
    for piece in pieces:
        for _ in piece:
            due += per_spot
            emit_stage1(int(due))
            due -= int(due)
    emit_stage1(len(pending))


def _const_spec(shape):
    nd = len(shape)
    return pl.BlockSpec(shape, lambda *_: (0,) * nd, pipeline_mode=pl.Buffered(1))


def _prompt_mixer(x, w):
    b, t, d = x.shape
    tm = TM_MIX
    nt = t // tm
    n_blocks = b * nt
    padded_rows = SUBLANES * (tm // SUBLANES + SEG_PAD)
    consts = [w['n1g'], w['wqkv'], w['win'], w['gq'], w['gk'], w['oq'], w['ok'],
              w['bias_p'], w['sink_p'], w['cw8'], w['cb8'], w['bd'], w['ba8'], w['bx8'], w['lam8']]
    to_round = [w['wor32'], w['wout32'], w['wup32'], w['wdn32']]
    bf16_rows = 2 * SUBLANES
    assert all(a.shape[0] % (n_blocks * bf16_rows) == 0 for a in to_round)
    out_shape = (
        jax.ShapeDtypeStruct((b, t, d), BF16),
        jax.ShapeDtypeStruct((b, t, Q_W), BF16),
        jax.ShapeDtypeStruct((b, t, 3 * d), BF16),
        jax.ShapeDtypeStruct((b, WINDOW, KV_W), F32),
        jax.ShapeDtypeStruct((b, WINDOW, KV_W), F32),
        jax.ShapeDtypeStruct((b, CONV_W - 1, d), F32),
        jax.ShapeDtypeStruct((b, 1, d), F32),
    ) + tuple(jax.ShapeDtypeStruct(a.shape, BF16) for a in to_round) + (
        jax.ShapeDtypeStruct(w['gq'].shape, F32),)

    def cur(n):
        i = jnp.minimum(n, n_blocks - 1)
        return (i // nt, i % nt, 0)

    def prv(n):
        i = jnp.maximum(n - 1, 0)
        return (i // nt, i % nt, 0)

    per_seq = lambda n: (jnp.maximum(n - 1, 0) // nt, 0, 0)
    chunk = lambda a: pl.BlockSpec((a.shape[0] // n_blocks, a.shape[1]),
                                   lambda n: (jnp.minimum(n, n_blocks - 1), 0))
    return pl.pallas_call(
        functools.partial(_prompt_mixer_kernel, blocks_per_seq=nt),
        out_shape=out_shape,
        grid=(n_blocks + 1,),
        in_specs=[pl.BlockSpec((None, tm, d), cur)]
                 + [_const_spec(c.shape) for c in consts]
                 + [chunk(a) for a in to_round],
        out_specs=(
            pl.BlockSpec((None, tm, d), prv),
            pl.BlockSpec((None, tm, Q_W), prv),
            pl.BlockSpec((None, tm, 3 * d), cur),
            pl.BlockSpec((None, WINDOW, KV_W), per_seq),
            pl.BlockSpec((None, WINDOW, KV_W), per_seq),
            pl.BlockSpec((None, CONV_W - 1, d), per_seq),
            pl.BlockSpec((None, 1, d), per_seq),
        ) + tuple(chunk(a) for a in to_round) + (
            pl.BlockSpec(w['gq'].shape, lambda n: (0, 0)),),
        scratch_shapes=[
            pltpu.VMEM((2, KV_HEADS, 2 * WINDOW, GROUP * WINDOW), F32),
            pltpu.VMEM((2, REST0 // LANES, tm, LANES), F32),
            pltpu.VMEM((2, d // LANES, padded_rows, LANES), F32),
            pltpu.VMEM((WINDOW + tm, KV_W), BF16),
            pltpu.VMEM((KV_W, WINDOW + tm), BF16),
            pltpu.VMEM((SUBLANES, d), F32),
            pltpu.VMEM((1, d), F32),
            pltpu.VMEM((KV_HEADS, 2 * WINDOW, GROUP * WINDOW), F32),
            pltpu.VMEM((KV_HEADS, 2 * WINDOW, GROUP * WINDOW), BF16),
            pltpu.VMEM((d // MXU_DIM, tm, MXU_DIM), F32),
            pltpu.VMEM((d // MXU_DIM, tm, 2 * MXU_DIM), F32),
            pltpu.VMEM((d // MXU_DIM, tm, MXU_DIM), F32),
            pltpu.VMEM((d // MXU_DIM, tm, MXU_DIM), F32),
            pltpu.VMEM((d // MXU_DIM, MXU_DIM // LANES, padded_rows, LANES), F32),
        ],
        compiler_params=pltpu.CompilerParams(
            dimension_semantics=("arbitrary",), vmem_limit_bytes=VMEM_LIMIT),
        name="prompt_mixer",
    )(x, *consts, *to_round)


def _channel_rows(x_ref, h_ref, att_ref, pre_ref, p_ref, woa_ref, wor_ref, wout_ref,
                  n2g_ref, wup_ref, wdn_ref, pg_ref, wpg_ref, wple_ref, o_ref):
    d = x_ref.shape[1]
    pre = pre_ref[...].astype(F32)
    hg = (h_ref[...].astype(F32) * jax.nn.gelu(pre[:, :d])).astype(BF16)
    rnn = _dot(hg, wor_ref[...])
    atto = _dot(att_ref[...], woa_ref[...])
    mix = (jax.nn.sigmoid(pre[:, d:2 * d]) * atto + jax.nn.sigmoid(pre[:, 2 * d:]) * rnn).astype(BF16)
    x = x_ref[...] + _dot(mix, wout_ref[...])
    xn = _rms(x, n2g_ref[...]).astype(BF16)
    acc = x
    for c in range(wup_ref.shape[1] // FF_CHUNK):
        cols = slice(c * FF_CHUNK, (c + 1) * FF_CHUNK)
        hmid = jnp.maximum(_dot(xn, wup_ref[:, cols]), 0.0)
        acc = acc + _dot((hmid * hmid).astype(BF16), wdn_ref[cols, :])
    gate = jax.nn.sigmoid(_dot(_rms(acc, pg_ref[...]).astype(BF16), wpg_ref[...]))
    o_ref[...] = acc + gate * _dot(p_ref[...].astype(BF16), wple_ref[...])


N_ROW_INPUTS = 5


def _channel_kernel(*refs, first_steps):
    first, second = refs[:N_ROW_INPUTS], refs[N_ROW_INPUTS:2 * N_ROW_INPUTS]
    consts, (o_first, o_second) = refs[2 * N_ROW_INPUTS:-2], refs[-2:]
    i = pl.program_id(0)

    @pl.when(i < first_steps)
    def _():
        _channel_rows(*first, *consts, o_first)

    @pl.when(i >= first_steps)
    def _():
        _channel_rows(*second, *consts, o_second)


def _channel(first, second, w):
    m1, d = first[0].shape
    m2 = second[0].shape[0]
    tm = min(TM_MLP, m1, m2)
    n1, n2 = m1 // tm, m2 // tm
    consts = [w['woa'], w['wor'], w['wout'], w['n2g'], w['wup'], w['wdn'], w['pg'], w['wpg'], w['wple']]
    rows1 = lambda a: pl.BlockSpec((tm, a.shape[1]), lambda i: (jnp.minimum(i, n1 - 1), 0))
    rows2 = lambda a: pl.BlockSpec((tm, a.shape[1]), lambda i: (jnp.maximum(i - n1, 0), 0),
                                   pipeline_mode=pl.Buffered(1))
    return pl.pallas_call(
        functools.partial(_channel_kernel, first_steps=n1),
        out_shape=(jax.ShapeDtypeStruct((m1, d), F32), jax.ShapeDtypeStruct((m2, d), F32)),
        grid=(n1 + n2,),
        in_specs=[rows1(a) for a in first] + [rows2(a) for a in second]
                 + [_const_spec(c.shape) for c in consts],
        out_specs=(rows1(first[0]), rows2(second[0])),
        compiler_params=pltpu.CompilerParams(
            dimension_semantics=("arbitrary",), vmem_limit_bytes=VMEM_LIMIT),
        name="merge_mlp_ple",
    )(*first, *second, *consts)


def _decode_attn_kernel(x_ref, ck_ref, cv_ref, n1g_ref, wqkv_ref, gq_ref, gk_ref, oq_ref, ok_ref,
                        bias_c_ref, bias_n_ref, sink_ref,
                        att_ref, kwin_ref, vwin_ref):
    sb = ck_ref.shape[0]
    nt = x_ref.shape[0] // sb
    xn = _rms(x_ref[...], n1g_ref[...]).astype(BF16)
    qkv = _dot(xn, wqkv_ref[...])
    qn = _head_rms(qkv[:, :Q_W], oq_ref, gq_ref[...])
    kn = _head_rms(qkv[:, Q_W:Q_W + KV_W], ok_ref, gk_ref[...])
    v = qkv[:, Q_W + KV_W:]

    n_rows = sb * nt
    ck = ck_ref[...]
    cv = cv_ref[...]
    for b in range(sb):
        rows = slice(b * nt, (b + 1) * nt)
        kwin_ref[b, 0:WINDOW - nt, :] = ck[b, nt:, :]
        vwin_ref[b, 0:WINDOW - nt, :] = cv[b, nt:, :]
        kwin_ref[b, WINDOW - nt:WINDOW, :] = kn[rows]
        vwin_ref[b, WINDOW - nt:WINDOW, :] = v[rows]

    qb = qn.astype(BF16)
    lower = _lower_half((n_rows, LANES))
    zero = jnp.zeros((n_rows, LANES), BF16)
    q_all = jnp.concatenate(
        [jnp.where(lower, qb[:, j * LANES:(j + 1) * LANES], zero) if g == 0
         else jnp.where(lower, zero, qb[:, j * LANES:(j + 1) * LANES])
         for g in range(KV_HEADS) for j in range(GROUP)], axis=0)
    n_col = q_all.shape[0]
    col_seq = (lax.broadcasted_iota(jnp.int32, (1, n_col), 1) // nt) % sb

    def own(big):
        out = big[0:WINDOW]
        for b in range(1, sb):
            out = jnp.where(col_seq == b, big[b * WINDOW:(b + 1) * WINDOW], out)
        return out

    st = own(_dot_nt(ck.reshape(sb * WINDOW, KV_W).astype(BF16), q_all)) + bias_c_ref[...]
    stx = _dot_nt(kn.astype(BF16), q_all) + bias_n_ref[...]
    sink = sink_ref[...]
    m = jnp.maximum(jnp.maximum(jnp.max(st, axis=0, keepdims=True),
                                jnp.max(stx, axis=0, keepdims=True)), sink)
    e = jnp.exp(st - m)
    ex = jnp.exp(stx - m)
    den = (jnp.sum(e, axis=0, keepdims=True) + jnp.sum(ex, axis=0, keepdims=True)
           + jnp.exp(sink - m))
    cv_t = jnp.concatenate([cv[b].T for b in range(sb)], axis=0).astype(BF16)
    out = own(_dot(cv_t, e.astype(BF16)))
    pad = WINDOW - n_rows
    v_t = jnp.concatenate([v, jnp.zeros((pad, KV_W), F32)], axis=0).T.astype(BF16)
    ex_pad = jnp.concatenate([ex, jnp.zeros((pad, n_col), F32)], axis=0).astype(BF16)
    out = (out + _dot(v_t, ex_pad)) / den
    half = n_col // KV_HEADS
    both_t = jnp.concatenate([out[:HEAD_DIM, :half], out[HEAD_DIM:, half:]], axis=0).T
    for j in range(GROUP):
        att_ref[:, j * LANES:(j + 1) * LANES] = both_t[j * n_rows:(j + 1) * n_rows].astype(att_ref.dtype)


def _decode_attn(x, ck, cv, w):
    m, d = x.shape
    nb = ck.shape[0]
    nt = m // nb
    sb = SEQ_BLOCK
    consts = [w['n1g'], w['wqkv'], w['gq'], w['gk'], w['oq'], w['ok'],
              w['bias_sc'], w['bias_sn'], w['sink_s']]
    cache_spec = pl.BlockSpec((sb, WINDOW, KV_W), lambda i: (i, 0, 0))
    return pl.pallas_call(
        _decode_attn_kernel,
        out_shape=(jax.ShapeDtypeStruct((m, Q_W), BF16),
                   jax.ShapeDtypeStruct(ck.shape, F32),
                   jax.ShapeDtypeStruct(cv.shape, F32)),
        grid=(nb // sb,),
        in_specs=[pl.BlockSpec((sb * nt, d), lambda i: (i, 0)), cache_spec, cache_spec]
                 + [_const_spec(c.shape) for c in consts],
        out_specs=(pl.BlockSpec((sb * nt, Q_W), lambda i: (i, 0)), cache_spec, cache_spec),
        compiler_params=pltpu.CompilerParams(
            dimension_semantics=("arbitrary",), vmem_limit_bytes=VMEM_LIMIT),
        name="decode_attn",
    )(x, ck, cv, *consts)


def _decode_mixer_kernel(x_ref, cst_ref, h0_ref, n1g_ref, win_ref, cw_ref, cb_ref, bd_ref,
                         ba_ref, bx_ref, lam_ref,
                         h_out, pre_out, cnew_ref, hnew_ref):
    nb = h0_ref.shape[0]
    nt = x_ref.shape[0] // nb
    d = x_ref.shape[1]
    xn = _rms(x_ref[...], n1g_ref[...]).astype(BF16)
    xr = _dot(xn, win_ref[:, REST0:REST0 + d])
    prev = cst_ref[...]
    slabs = [prev[k * nb:(k + 1) * nb] for k in range(CONV_W - 1)]
    slabs += [xr[k * nb:(k + 1) * nb] for k in range(nt)]
    xc = jnp.concatenate(
        [cb_ref[...] + sum(cw_ref[j:j + 1, :] * slabs[ti + j] for j in range(CONV_W))
         for ti in range(nt)], axis=0)
    cnew_ref[...] = jnp.concatenate(slabs[-(CONV_W - 1):], axis=0)

    a, bterm = _rglru_gates(xc, bd_ref, ba_ref[...], bx_ref[...], lam_ref[...])
    h = h0_ref[...]
    hs = []
    for ti in range(nt):
        h = a[ti * nb:(ti + 1) * nb] * h + bterm[ti * nb:(ti + 1) * nb]
        hs.append(h)
    hnew_ref[...] = h
    h_out[...] = jnp.concatenate(hs, axis=0).astype(BF16)
    pre_out[...] = _dot(xn, win_ref[:, REST0 + d:REST0 + 4 * d]).astype(BF16)


def _decode_mixer(x, cst, h0, w):
    m, d = x.shape
    consts = [w['n1g'], w['win'], w['cw'], w['cb'], w['bd'], w['ba'], w['bx'], w['lam']]
    ins = [x, cst, h0] + consts
    whole = lambda shape: pl.BlockSpec(shape, lambda i: (0,) * len(shape))
    return pl.pallas_call(
        _decode_mixer_kernel,
        out_shape=(jax.ShapeDtypeStruct((m, d), BF16),
                   jax.ShapeDtypeStruct((m, 3 * d), BF16),
                   jax.ShapeDtypeStruct(cst.shape, F32),
                   jax.ShapeDtypeStruct(h0.shape, F32)),
        grid=(1,),
        in_specs=[_const_spec(c.shape) for c in ins],
        out_specs=(whole((m, d)), whole((m, 3 * d)), whole(cst.shape), whole(h0.shape)),
        compiler_params=pltpu.CompilerParams(
            dimension_semantics=("arbitrary",), vmem_limit_bytes=VMEM_LIMIT),
        name="decode_mixer",
    )(*ins)


def _rel_bucket(dist):
    n = np.maximum(dist, 0)
    max_exact = REL_BUCKETS // 2
    nf = np.maximum(n, 1).astype(np.float32)
    large = max_exact + (np.log(nf / max_exact) / math.log(REL_MAX_DIST / max_exact)
                         * (REL_BUCKETS - max_exact)).astype(np.int32)
    large = np.minimum(large, REL_BUCKETS - 1)
    return np.where(n < max_exact, n, large)


def _bias_rows(rel_bias, dist, mask):
    tb = jnp.where(mask[:, :, None], rel_bias[_rel_bucket(dist)].astype(F32), NEG_INF)
    tq, tk = dist.shape
    return jnp.transpose(tb, (2, 0, 1)).reshape(N_HEADS * tq, tk)


def _prompt_bias_rows(rel_bias):
    span = 3 * WINDOW
    k = np.arange(span)
    dist = WINDOW + np.where(k < WINDOW, k, k - span)
    valid = (dist >= 0) & (dist <= WINDOW)
    return jnp.where(valid[:, None], rel_bias[_rel_bucket(dist)].astype(F32), NEG_INF).T


def _gate_blocks(rg_wa, rg_wx, n_grp):
    per = rg_wa.shape[0] // n_grp
    both = jnp.stack([rg_wa, rg_wx]).reshape(2, n_grp, per, RNN_BS, RNN_BS)
    on_diag = np.eye(per, dtype=bool)[None, None, :, None, :, None]
    blocks = jnp.where(on_diag, both[:, :, :, :, None, :], 0.0)
    blocks = jnp.transpose(blocks, (1, 2, 3, 0, 4, 5))
    return blocks.reshape(n_grp, per * RNN_BS, 2 * per * RNN_BS).astype(BF16)


def _head_avg(width):
    idx = np.arange(width) // HEAD_DIM
    return jnp.asarray((idx[:, None] == idx[None, :]).astype(np.float32) / HEAD_DIM, BF16)


def _prepare(rel_bias, norm1_g, w_in, q_norm_g, k_norm_g, sinks, w_o_attn, conv_w, conv_b,
             rg_wa, rg_ba, rg_wx, rg_bx, rg_lambda, w_o_rnn, w_out, norm2_g, w_up, w_down,
             ple_norm_g, w_ple_gate, w_ple, n_dec):
    d = w_in.shape[0]
    order = [g * GROUP + j for j in range(GROUP) for g in range(KV_HEADS)]
    perm = np.concatenate([np.arange(h * HEAD_DIM, (h + 1) * HEAD_DIM) for h in order])
    row = lambda a: a.reshape(1, -1).astype(F32)
    w = {}
    w['n1g'] = row(norm1_g)
    w['win'] = w_in.astype(BF16)
    w['wqkv'] = jnp.concatenate([w['win'][:, :Q_W][:, perm], w['win'][:, Q_W:REST0]], axis=1)
    w['gq'] = row(jnp.tile(q_norm_g, N_HEADS)) * (HEAD_DIM ** -0.5)
    w['gk'] = row(jnp.tile(k_norm_g, KV_HEADS))
    w['oq'] = _head_avg(Q_W)
    w['ok'] = _head_avg(KV_W)
    w['woa'] = w_o_attn[perm, :].astype(BF16)
    w['cw'] = conv_w.astype(F32)
    w['cb'] = row(conv_b)
    w['bd'] = _gate_blocks(rg_wa, rg_wx, d // MXU_DIM)
    w['ba'] = row(rg_ba)
    w['bx'] = row(rg_bx)
    w['lam'] = row(rg_lambda)
    rows8 = lambda a: jnp.broadcast_to(a, (SUBLANES, a.shape[-1]))
    w['cw8'] = jnp.broadcast_to(w['cw'][:, None, :], (CONV_W, SUBLANES, d))
    w['cb8'], w['ba8'], w['bx8'], w['lam8'] = (rows8(w[k]) for k in ('cb', 'ba', 'bx', 'lam'))
    w['wor32'], w['wout32'], w['wup32'], w['wdn32'] = w_o_rnn, w_out, w_up, w_down
    w['n2g'] = row(norm2_g)
    w['pg'] = row(ple_norm_g)
    w['wpg'] = w_ple_gate.astype(BF16)
    w['wple'] = w_ple.astype(BF16)

    w['bias_p'] = _prompt_bias_rows(rel_bias)
    sink_rows = sinks.astype(F32).reshape(KV_HEADS, GROUP, 1)
    w['sink_p'] = jnp.repeat(sink_rows, WINDOW, axis=2).reshape(KV_HEADS, 1, GROUP * WINDOW)

    sb = SEQ_BLOCK
    ti = np.arange(n_dec)[:, None]
    dist_c = ti + WINDOW - np.arange(WINDOW)[None, :]
    dist_n = ti - np.arange(n_dec)[None, :]
    n_col = N_HEADS * sb * n_dec
    col = np.arange(n_col)
    col_head_t = col // (sb * n_dec) * n_dec + col % n_dec
    spread = (np.arange(N_HEADS * n_dec)[:, None] == col_head_t[None, :]).astype(np.float32)
    expand = lambda a: jnp.dot(a.T, spread, precision=lax.Precision.HIGHEST)
    w['bias_sc'] = expand(_bias_rows(rel_bias, dist_c, dist_c <= WINDOW))
    bias_n = expand(_bias_rows(rel_bias, dist_n, dist_n >= 0))
    own = (np.arange(sb * n_dec) // n_dec)[:, None] == (col // n_dec % sb)[None, :]
    w['bias_sn'] = jnp.where(own, jnp.tile(bias_n, (sb, 1)), NEG_INF)
    w['sink_s'] = jnp.repeat(sinks.astype(F32), sb * n_dec)[None, :]
    return w


def kernel(x_prompt, x_sample, cache_k_win, cache_v_win, state_conv, state_h, p_prompt, p_sample, rel_bias, norm1_g, w_in, q_norm_g, k_norm_g, sinks, w_o_attn, conv_w, conv_b, rg_wa, rg_ba, rg_wx, rg_bx, rg_lambda, w_o_rnn, w_out, norm2_g, w_up, w_down, ple_norm_g, w_ple_gate, w_ple):
    depth = w_in.shape[0]
    assert depth == 1, "single-layer step"
    b, t, d = x_prompt.shape
    nb, nt, _ = x_sample.shape
    w = _prepare(rel_bias, norm1_g[0], w_in[0], q_norm_g[0], k_norm_g[0], sinks[0], w_o_attn[0],
                 conv_w[0], conv_b[0], rg_wa[0], rg_ba[0], rg_wx[0], rg_bx[0], rg_lambda[0],
                 w_o_rnn[0], w_out[0], norm2_g[0], w_up[0], w_down[0], ple_norm_g[0],
                 w_ple_gate[0], w_ple[0], nt)

    (hp_rows, attp, prep, kp, vp, cp, hp,
     w['wor'], w['wout'], w['wup'], w['wdn'], gq_after_prompt) = _prompt_mixer(x_prompt, w)
    flat = lambda a: a.reshape(b * t, a.shape[-1])

    att, ks, vs = _decode_attn(x_sample.reshape(nb * nt, d),
                               cache_k_win[0].reshape(nb, WINDOW, KV_W),
                               cache_v_win[0].reshape(nb, WINDOW, KV_W), dict(w, gq=gq_after_prompt))
    to_tm = lambda a: jnp.swapaxes(a, 0, 1).reshape(-1, a.shape[-1])
    xs_tm = to_tm(x_sample)
    hs_rows, pres, cs, hs = _decode_mixer(xs_tm, to_tm(state_conv[0]), state_h[0], w)

    yp, ys = _channel((flat(x_prompt), flat(hp_rows), flat(attp), flat(prep), flat(p_prompt[0])),
                      (xs_tm, hs_rows, to_tm(att.reshape(nb, nt, Q_W)), pres, to_tm(p_sample[0])), w)
    yp = yp.reshape(b, t, d)
    from_tm = lambda a, n: jnp.swapaxes(a.reshape(n, nb, a.shape[-1]), 0, 1)

    kv_shape = (1, -1, WINDOW, KV_HEADS, HEAD_DIM)
    return (yp, from_tm(ys, nt),
            kp.reshape(kv_shape), vp.reshape(kv_shape), cp[None], hp.reshape(1, b, d),
            ks.reshape(kv_shape), vs.reshape(kv_shape), from_tm(cs, CONV_W - 1)[None], hs[None])
```

```python
import functools
import math

import numpy as np
import jax
import jax.numpy as jnp
from jax import lax
from jax.experimental import pallas as pl
from jax.experimental.pallas import tpu as pltpu

F32 = jnp.float32
BF16 = jnp.bfloat16

N_HEADS = 8
KV_HEADS = 2
GROUP = N_HEADS // KV_HEADS
HEAD_DIM = 64
Q_W = N_HEADS * HEAD_DIM
KV_W = KV_HEADS * HEAD_DIM
REST0 = Q_W + 2 * KV_W
WINDOW = 128
REL_BUCKETS = 32
REL_MAX_DIST = 128
RNN_BS = 64
CONV_W = 4
RG_C = 8.0
EPS = 1e-6
NEG_INF = -1e30
EXPM1_SERIES_BELOW = 2.0 ** -11
SQRT_FLOOR = 1e-30

LANES = 128
SUBLANES = 8
MXU_DIM = 256
VMEM_LIMIT = 56 * 1024 * 1024

TM_MIX = 512
TM_MLP = 512
FF_CHUNK = 1024
SEQ_BLOCK = 16
PACE_LAG = 2
SEG_PAD = 4


def _dot(a, b):
    return jnp.dot(a, b, preferred_element_type=F32)


def _dot_nt(a, b):
    return lax.dot_general(a, b, (((1,), (1,)), ((), ())), preferred_element_type=F32)


def _rms(x, g):
    ms = jnp.mean(x * x, axis=-1, keepdims=True)
    return x * lax.rsqrt(ms + EPS) * g


def _head_rms(x, ones_ref, g):
    ms = _dot((x * x).astype(BF16), ones_ref[...])
    return x * lax.rsqrt(ms + EPS) * g


def _lower_half(shape):
    return lax.broadcasted_iota(jnp.int32, shape, len(shape) - 1) < HEAD_DIM


def _neg_expm1_2x(x, exp_x):
    return jnp.where(x > -EXPM1_SERIES_BELOW, (-2.0 * x) * (1.0 + x), 1.0 - exp_x * exp_x)


def _log_decay(lam):
    return -RG_C * jax.nn.softplus(-lam)


def _gate_math(ya, yx, xc, ba, bx, log_decay):
    r = jax.nn.sigmoid(ya + ba)
    i = jax.nn.sigmoid(yx + bx)
    log_a = log_decay * r
    a = jnp.exp(log_a)
    y = _neg_expm1_2x(log_a, a)
    bterm = (y * lax.rsqrt(jnp.maximum(y, SQRT_FLOOR))) * (i * xc)
    return a, bterm


def _rglru_gates(xc, bd_ref, ba, bx, lam):
    xcb = xc.astype(BF16)
    ya, yx = [], []
    for gi in range(xc.shape[1] // MXU_DIM):
        y = _dot(xcb[:, gi * MXU_DIM:(gi + 1) * MXU_DIM], bd_ref[gi])
        ya.append(y[:, :MXU_DIM])
        yx.append(y[:, MXU_DIM:])
    return _gate_math(jnp.concatenate(ya, axis=1), jnp.concatenate(yx, axis=1), xc, ba, bx,
                      _log_decay(lam))


def _prompt_mixer_kernel(x_ref, n1g_ref, wqkv_ref, win_ref, gq_ref, gk_ref, oq_ref, ok_ref,
                         bias_rows_ref, sink_ref, cw_ref, cb_ref, bd_ref, ba_ref, bx_ref, lam_ref,
                         wor32_ref, wout32_ref, wup32_ref, wdn32_ref, woa32_ref, wpg32_ref, wple32_ref,
                         h_out, att, pre_out, kwin_ref, vwin_ref, cst_ref, hst_ref,
                         wor16_ref, wout16_ref, wup16_ref, wdn16_ref, woa16_ref, wpg16_ref, wple16_ref,
                         bias_ref, cw8_ref, cb8_ref, ba8_ref, bx8_ref, lam8_ref,
                         proj, projx, kbuf, vbuf, xp, hc, logit_s, e_s, xc_s, y_s, hl_s, ac_s, hn_s,
                         *, blocks_per_seq):
    tm, d = x_ref.shape
    n = pl.program_id(0)
    for src, dst in ((wor32_ref, wor16_ref), (wout32_ref, wout16_ref), (wup32_ref, wup16_ref),
                     (wdn32_ref, wdn16_ref), (woa32_ref, woa16_ref), (wpg32_ref, wpg16_ref),
                     (wple32_ref, wple16_ref)):
        dst[...] = src[...].astype(BF16)

    @pl.when(n == 0)
    def _():
        for j in range(CONV_W):
            cw8_ref[j] = jnp.broadcast_to(cw_ref[j:j + 1, :], cw8_ref.shape[1:])
        for row, rows8 in ((cb_ref, cb8_ref), (ba_ref, ba8_ref), (bx_ref, bx8_ref), (lam_ref, lam8_ref)):
            rows8[...] = jnp.broadcast_to(row[...], rows8.shape)
        proj[1] = jnp.zeros(proj.shape[1:], F32)
        projx[1] = jnp.zeros(projx.shape[1:], F32)
        kbuf[...] = jnp.zeros(kbuf.shape, BF16)
        vbuf[...] = jnp.zeros(vbuf.shape, BF16)
        xp[...] = jnp.zeros(xp.shape, F32)
        hc[...] = jnp.zeros(hc.shape, F32)
        has_prev = lax.broadcasted_iota(jnp.int32, (2 * WINDOW, WINDOW), 0) >= WINDOW
        for h in range(N_HEADS):
            row = jnp.broadcast_to(bias_rows_ref[h:h + 1, :], (2 * WINDOW, bias_rows_ref.shape[1]))
            table = pltpu.roll(row, 0, 1, stride=1, stride_axis=0)[:, :WINDOW]
            at = (h // GROUP, slice(None), slice((h % GROUP) * WINDOW, (h % GROUP + 1) * WINDOW))
            bias_ref[(0,) + at] = table
            bias_ref[(1,) + at] = jnp.where(has_prev, table, NEG_INF)

    wslot = lax.rem(n, 2)
    rslot = 1 - wslot
    fresh = lax.rem(n + blocks_per_seq - 1, blocks_per_seq) == 0
    seg = tm // SUBLANES
    pitch = seg + SEG_PAD

    xn = _rms(x_ref[...], n1g_ref[...]).astype(BF16)
    n_qkv = wqkv_ref.shape[1]
    tokens = []

    def project(lo):
        res = _dot(xn, (wqkv_ref if lo < n_qkv else win_ref)[:, lo:lo + MXU_DIM])
        tokens.append(res[0:1, :])
        if lo < n_qkv:
            for i in range(MXU_DIM // LANES):
                proj[wslot, lo // LANES + i] = res[:, i * LANES:(i + 1) * LANES]
        elif lo < REST0 + d:
            for i in range(MXU_DIM // LANES):
                for sgm in range(SUBLANES):
                    projx[wslot, (lo - REST0) // LANES + i, sgm * pitch:sgm * pitch + seg, :] = (
                        res[sgm * seg:(sgm + 1) * seg, i * LANES:(i + 1) * LANES])
        else:
            pre_out[:, lo - REST0 - d:lo - REST0 - d + MXU_DIM] = res.astype(BF16)

    pending = list(range(0, win_ref.shape[1], MXU_DIM))

    def pace():
        if len(tokens) < PACE_LAG:
            return jnp.zeros((1, MXU_DIM), F32)
        bits = lax.bitcast_convert_type(tokens[-PACE_LAG], jnp.uint32)
        return lax.bitcast_convert_type((bits >> 16) >> 16, F32)

    def emit_stage1(count):
        for _ in range(min(count, len(pending))):
            project(pending.pop(0))

    def kept(lo, hi):
        return jnp.concatenate([proj[rslot, t] for t in range(lo // LANES, hi // LANES)], axis=1)

    emit_stage1(2)
    qn = _head_rms(kept(0, Q_W), oq_ref, gq_ref[...]).astype(BF16)
    kn = _head_rms(kept(Q_W, Q_W + KV_W), ok_ref, gk_ref[...])
    v = kept(Q_W + KV_W, n_qkv)
    kbuf[0:WINDOW, :] = jnp.where(fresh, jnp.zeros((WINDOW, KV_W), BF16), kbuf[tm:tm + WINDOW, :])
    vbuf[:, 0:WINDOW] = jnp.where(fresh, jnp.zeros((KV_W, WINDOW), BF16), vbuf[:, tm:tm + WINDOW])
    kbuf[WINDOW:WINDOW + tm, :] = kn.astype(BF16)
    vbuf[:, WINDOW:WINDOW + tm] = v.T.astype(BF16)
    kwin_ref[...] = kn[tm - WINDOW:, :]
    vwin_ref[...] = v[tm - WINDOW:, :]

    first = jnp.where(fresh, 1, 0)
    lower = _lower_half((WINDOW, LANES))
    zero = jnp.zeros((WINDOW, LANES), BF16)

    def attend(s):
        rows = slice(s * WINDOW, (s + 1) * WINDOW)
        keys = kbuf[s * WINDOW:(s + 2) * WINDOW, :]
        vals_t = vbuf[:, s * WINDOW:(s + 2) * WINDOW]
        outs = []
        for g in range(KV_HEADS):
            qs = []
            for j in range(GROUP):
                slab = qn[rows, j * LANES:(j + 1) * LANES]
                qs.append(jnp.where(lower, slab, zero) if g == 0 else jnp.where(lower, zero, slab))
            logit_s[g] = _dot_nt(keys, jnp.concatenate(qs, axis=0))
            yield
            inv = []
            for j in range(GROUP):
                blk = slice(j * WINDOW, (j + 1) * WINDOW)
                bias = bias_ref[first, g, :, blk] if s == 0 else bias_ref[0, g, :, blk]
                logit = logit_s[g, :, blk] + bias
                sink = sink_ref[g, :, blk] + pace()[:, :WINDOW]
                m = jnp.maximum(jnp.max(logit, axis=0, keepdims=True), sink)
                e = jnp.exp(logit - m)
                inv.append(1.0 / (jnp.sum(e, axis=0, keepdims=True) + jnp.exp(sink - m)))
                e_s[g, :, blk] = e.astype(BF16)
                yield
            outs.append(_dot(vals_t, e_s[g]) * jnp.concatenate(inv, axis=1))
        for j in range(GROUP):
            blk = slice(j * WINDOW, (j + 1) * WINDOW)
            both = jnp.concatenate([outs[0][:HEAD_DIM, blk], outs[1][HEAD_DIM:, blk]], axis=0)
            att[rows, j * LANES:(j + 1) * LANES] = both.T.astype(BF16)
        yield

    sub8 = lax.broadcasted_iota(jnp.int32, (SUBLANES, MXU_DIM), 0)

    def recur(gi):
        cols = slice(gi * MXU_DIM, (gi + 1) * MXU_DIM)
        tiles = range(gi * (MXU_DIM // LANES), (gi + 1) * (MXU_DIM // LANES))
        by_segment = lambda k: pl.ds(k, SUBLANES, stride=pitch)
        x_at = lambda k: jnp.concatenate([projx[rslot, t, by_segment(k), :] for t in tiles], axis=1)
        last = SUBLANES * pitch - SEG_PAD
        x_tail = lambda n_rows: jnp.concatenate(
            [projx[rslot, t, last - n_rows:last, :] for t in tiles], axis=1)
        prev = jnp.where(fresh, 0.0, xp[:, cols])
        window = [jnp.where(sub8 == 0, prev[SUBLANES - j:SUBLANES - j + 1, :],
                            pltpu.roll(x_at(seg - j), 1, axis=0)) for j in range(CONV_W - 1, 0, -1)]
        for k in range(seg):
            window.append(x_at(k))
            xc_s[gi, k * SUBLANES:(k + 1) * SUBLANES, :] = cb8_ref[:, cols] + sum(
                cw8_ref[j, :, cols] * window[j] for j in range(CONV_W))
            window.pop(0)
            if k % 16 == 15:
                yield
        xp[:, cols] = x_tail(SUBLANES)
        cst_ref[:, cols] = x_tail(CONV_W - 1)
        y_s[gi] = _dot(xc_s[gi].astype(BF16), bd_ref[gi])
        yield
        ba, bx = ba8_ref[:, cols], bx8_ref[:, cols]
        log_decay = _log_decay(lam8_ref[:, cols])
        h_loc = a_cum = None
        for k in range(seg):
            r = slice(k * SUBLANES, (k + 1) * SUBLANES)
            a, bterm = _gate_math(y_s[gi, r, :MXU_DIM], y_s[gi, r, MXU_DIM:], xc_s[gi, r, :] + pace(),
                                  ba, bx, log_decay)
            h_loc = bterm if k == 0 else a * h_loc + bterm
            a_cum = a if k == 0 else a * a_cum
            hl_s[gi, r, :] = h_loc
            ac_s[gi, r, :] = a_cum
            if k % 4 == 3:
                yield
        h_in = jnp.where(fresh, 0.0, hc[:, cols])
        carry_in = []
        for sgm in range(SUBLANES):
            carry_in.append(h_in)
            h_in = h_loc[sgm:sgm + 1, :] + a_cum[sgm:sgm + 1, :] * h_in
        hc[:, cols] = h_in
        hst_ref[:, cols] = h_in
        carry_in = jnp.concatenate(carry_in, axis=0)
        for k in range(seg):
            r = slice(k * SUBLANES, (k + 1) * SUBLANES)
            h = hl_s[gi, r, :] + ac_s[gi, r, :] * carry_in
            for i in range(MXU_DIM // LANES):
                hn_s[gi, i, by_segment(k), :] = h[:, i * LANES:(i + 1) * LANES]
            if k % 8 == 7:
                yield
        h_out[:, cols] = jnp.concatenate(
            [jnp.concatenate([hn_s[gi, i, sgm * pitch:sgm * pitch + seg, :] for sgm in range(SUBLANES)], axis=0)
             for i in range(MXU_DIM // LANES)], axis=1).astype(BF16)
        yield

    n_sub = tm // WINDOW
    n_grp = d // MXU_DIM
    pieces = []
    for u in range(max(n_sub, n_grp)):
        if u < n_sub:
            pieces.append(attend(u))
        if u < n_grp:
            pieces.append(recur(u))
    n_spots = n_sub * (KV_HEADS * (1 + GROUP) + 1) + n_grp * (2 + seg // 16 + seg // 4 + seg // 8)
    per_spot = len(pending) / n_spots
    due = 0.0
    for piece in pieces:
        for _ in piece:
            due += per_spot
            emit_stage1(int(due))
            due -= int(due)
    emit_stage1(len(pending))


def _const_spec(shape):
    nd = len(shape)
    return pl.BlockSpec(shape, lambda *_: (0,) * nd, pipeline_mode=pl.Buffered(1))


def _prompt_mixer(x, w):
    b, t, d = x.shape
    tm = TM_MIX
    nt = t // tm
    n_blocks = b * nt
    padded_rows = SUBLANES * (tm // SUBLANES + SEG_PAD)
    consts = [w['n1g'], w['wqkv'], w['win'], w['gq'], w['gk'], w['oq'], w['ok'],
              w['bias_p'], w['sink_p'], w['cw'], w['cb'], w['bd'], w['ba'], w['bx'], w['lam']]
    to_round = [w['wor32'], w['wout32'], w['wup32'], w['wdn32'], w['woa32'], w['wpg32'], w['wple32']]
    bf16_rows = 2 * SUBLANES

    n_chunks = lambda a: min(n_blocks, a.shape[0] // bf16_rows)

    def chunks(a, src_block=lambda c: c):
        assert a.shape[0] % (n_chunks(a) * bf16_rows) == 0
        block = (a.shape[0] // n_chunks(a), a.shape[1])
        at = lambda n: jnp.minimum(n, n_chunks(a) - 1)
        return (pl.BlockSpec(block, lambda n: (src_block(at(n)), 0)),
                pl.BlockSpec(block, lambda n: (at(n), 0)))

    per_head = n_chunks(w['woa32']) // N_HEADS
    assert per_head >= 1 and per_head * N_HEADS == n_chunks(w['woa32'])

    def woa_source(c):
        slot = c // per_head
        return ((slot % KV_HEADS) * GROUP + slot // KV_HEADS) * per_head + c % per_head

    round_specs = [chunks(a, woa_source if a is w['woa32'] else (lambda c: c)) for a in to_round]
    out_shape = (
        jax.ShapeDtypeStruct((b, t, d), BF16),
        jax.ShapeDtypeStruct((b, t, Q_W), BF16),
        jax.ShapeDtypeStruct((b, t, 3 * d), BF16),
        jax.ShapeDtypeStruct((b, WINDOW, KV_W), F32),
        jax.ShapeDtypeStruct((b, WINDOW, KV_W), F32),
        jax.ShapeDtypeStruct((b, CONV_W - 1, d), F32),
        jax.ShapeDtypeStruct((b, 1, d), F32),
    ) + tuple(jax.ShapeDtypeStruct(a.shape, BF16) for a in to_round)

    def cur(n):
        i = jnp.minimum(n, n_blocks - 1)
        return (i // nt, i % nt, 0)

    def prv(n):
        i = jnp.maximum(n - 1, 0)
        return (i // nt, i % nt, 0)

    per_seq = lambda n: (jnp.maximum(n - 1, 0) // nt, 0, 0)
    return pl.pallas_call(
        functools.partial(_prompt_mixer_kernel, blocks_per_seq=nt),
        out_shape=out_shape,
        grid=(n_blocks + 1,),
        in_specs=[pl.BlockSpec((None, tm, d), cur)]
                 + [_const_spec(c.shape) for c in consts]
                 + [src for src, _ in round_specs],
        out_specs=(
            pl.BlockSpec((None, tm, d), prv),
            pl.BlockSpec((None, tm, Q_W), prv),
            pl.BlockSpec((None, tm, 3 * d), cur),
            pl.BlockSpec((None, WINDOW, KV_W), per_seq),
            pl.BlockSpec((None, WINDOW, KV_W), per_seq),
            pl.BlockSpec((None, CONV_W - 1, d), per_seq),
            pl.BlockSpec((None, 1, d), per_seq),
        ) + tuple(dst for _, dst in round_specs),
        scratch_shapes=[
            pltpu.VMEM((2, KV_HEADS, 2 * WINDOW, GROUP * WINDOW), F32),
            pltpu.VMEM((CONV_W, SUBLANES, d), F32),
            pltpu.VMEM((SUBLANES, d), F32),
            pltpu.VMEM((SUBLANES, d), F32),
            pltpu.VMEM((SUBLANES, d), F32),
            pltpu.VMEM((SUBLANES, d), F32),
            pltpu.VMEM((2, REST0 // LANES, tm, LANES), F32),
            pltpu.VMEM((2, d // LANES, padded_rows, LANES), F32),
            pltpu.VMEM((WINDOW + tm, KV_W), BF16),
            pltpu.VMEM((KV_W, WINDOW + tm), BF16),
            pltpu.VMEM((SUBLANES, d), F32),
            pltpu.VMEM((1, d), F32),
            pltpu.VMEM((KV_HEADS, 2 * WINDOW, GROUP * WINDOW), F32),
            pltpu.VMEM((KV_HEADS, 2 * WINDOW, GROUP * WINDOW), BF16),
            pltpu.VMEM((d // MXU_DIM, tm, MXU_DIM), F32),
            pltpu.VMEM((d // MXU_DIM, tm, 2 * MXU_DIM), F32),
            pltpu.VMEM((d // MXU_DIM, tm, MXU_DIM), F32),
            pltpu.VMEM((d // MXU_DIM, tm, MXU_DIM), F32),
            pltpu.VMEM((d // MXU_DIM, MXU_DIM // LANES, padded_rows, LANES), F32),
        ],
        compiler_params=pltpu.CompilerParams(
            dimension_semantics=("arbitrary",), vmem_limit_bytes=VMEM_LIMIT),
        name="prompt_mixer",
    )(x, *consts, *to_round)


def _channel_rows(x_ref, h_ref, att_ref, pre_ref, p_ref, woa_ref, wor_ref, wout_ref,
                  n2g_ref, wup_ref, wdn_ref, pg_ref, wpg_ref, wple_ref, o_ref):
    d = x_ref.shape[1]
    pre = pre_ref[...].astype(F32)
    hg = (h_ref[...].astype(F32) * jax.nn.gelu(pre[:, :d])).astype(BF16)
    rnn = _dot(hg, wor_ref[...])
    atto = _dot(att_ref[...], woa_ref[...])
    mix = (jax.nn.sigmoid(pre[:, d:2 * d]) * atto + jax.nn.sigmoid(pre[:, 2 * d:]) * rnn).astype(BF16)
    x = x_ref[...] + _dot(mix, wout_ref[...])
    xn = _rms(x, n2g_ref[...]).astype(BF16)
    acc = x
    for c in range(wup_ref.shape[1] // FF_CHUNK):
        cols = slice(c * FF_CHUNK, (c + 1) * FF_CHUNK)
        hmid = jnp.maximum(_dot(xn, wup_ref[:, cols]), 0.0)
        acc = acc + _dot((hmid * hmid).astype(BF16), wdn_ref[cols, :])
    gate = jax.nn.sigmoid(_dot(_rms(acc, pg_ref[...]).astype(BF16), wpg_ref[...]))
    o_ref[...] = acc + gate * _dot(p_ref[...].astype(BF16), wple_ref[...])


N_ROW_INPUTS = 5


def _channel_kernel(*refs, first_steps):
    first, second = refs[:N_ROW_INPUTS], refs[N_ROW_INPUTS:2 * N_ROW_INPUTS]
    consts, (o_first, o_second) = refs[2 * N_ROW_INPUTS:-2], refs[-2:]
    i = pl.program_id(0)

    @pl.when(i < first_steps)
    def _():
        _channel_rows(*first, *consts, o_first)

    @pl.when(i >= first_steps)
    def _():
        _channel_rows(*second, *consts, o_second)


def _channel(first, second, w):
    m1, d = first[0].shape
    m2 = second[0].shape[0]
    tm = min(TM_MLP, m1, m2)
    n1, n2 = m1 // tm, m2 // tm
    consts = [w['woa'], w['wor'], w['wout'], w['n2g'], w['wup'], w['wdn'], w['pg'], w['wpg'], w['wple']]
    rows1 = lambda a: pl.BlockSpec((tm, a.shape[1]), lambda i: (jnp.minimum(i, n1 - 1), 0))
    rows2 = lambda a: pl.BlockSpec((tm, a.shape[1]), lambda i: (jnp.maximum(i - n1, 0), 0),
                                   pipeline_mode=pl.Buffered(1))
    return pl.pallas_call(
        functools.partial(_channel_kernel, first_steps=n1),
        out_shape=(jax.ShapeDtypeStruct((m1, d), F32), jax.ShapeDtypeStruct((m2, d), F32)),
        grid=(n1 + n2,),
        in_specs=[rows1(a) for a in first] + [rows2(a) for a in second]
                 + [_const_spec(c.shape) for c in consts],
        out_specs=(rows1(first[0]), rows2(second[0])),
        compiler_params=pltpu.CompilerParams(
            dimension_semantics=("arbitrary",), vmem_limit_bytes=VMEM_LIMIT),
        name="merge_mlp_ple",
    )(*first, *second, *consts)


def _decode_attn_kernel(x_ref, ck_ref, cv_ref, n1g_ref, wqkv_ref, gq_ref, gk_ref, oq_ref, ok_ref,
                        bias_c_ref, bias_n_ref, sink_ref,
                        att_ref, kwin_ref, vwin_ref):
    sb = ck_ref.shape[0]
    nt = x_ref.shape[0] // sb
    xn = _rms(x_ref[...], n1g_ref[...]).astype(BF16)
    qkv = _dot(xn, wqkv_ref[...])
    qn = _head_rms(qkv[:, :Q_W], oq_ref, gq_ref[...])
    kn = _head_rms(qkv[:, Q_W:Q_W + KV_W], ok_ref, gk_ref[...])
    v = qkv[:, Q_W + KV_W:]

    n_rows = sb * nt
    ck = ck_ref[...]
    cv = cv_ref[...]
    for b in range(sb):
        rows = slice(b * nt, (b + 1) * nt)
        kwin_ref[b, 0:WINDOW - nt, :] = ck[b, nt:, :]
        vwin_ref[b, 0:WINDOW - nt, :] = cv[b, nt:, :]
        kwin_ref[b, WINDOW - nt:WINDOW, :] = kn[rows]
        vwin_ref[b, WINDOW - nt:WINDOW, :] = v[rows]

    qb = qn.astype(BF16)
    lower = _lower_half((n_rows, LANES))
    zero = jnp.zeros((n_rows, LANES), BF16)
    q_all = jnp.concatenate(
        [jnp.where(lower, qb[:, j * LANES:(j + 1) * LANES], zero) if g == 0
         else jnp.where(lower, zero, qb[:, j * LANES:(j + 1) * LANES])
         for g in range(KV_HEADS) for j in range(GROUP)], axis=0)
    n_col = q_all.shape[0]
    col_seq = (lax.broadcasted_iota(jnp.int32, (1, n_col), 1) // nt) % sb

    def own(big):
        out = big[0:WINDOW]
        for b in range(1, sb):
            out = jnp.where(col_seq == b, big[b * WINDOW:(b + 1) * WINDOW], out)
        return out

    st = own(_dot_nt(ck.reshape(sb * WINDOW, KV_W).astype(BF16), q_all)) + bias_c_ref[...]
    stx = _dot_nt(kn.astype(BF16), q_all) + bias_n_ref[...]
    sink = sink_ref[...]
    m = jnp.maximum(jnp.maximum(jnp.max(st, axis=0, keepdims=True),
                                jnp.max(stx, axis=0, keepdims=True)), sink)
    e = jnp.exp(st - m)
    ex = jnp.exp(stx - m)
    den = (jnp.sum(e, axis=0, keepdims=True) + jnp.sum(ex, axis=0, keepdims=True)
           + jnp.exp(sink - m))
    cv_t = jnp.concatenate([cv[b].T for b in range(sb)], axis=0).astype(BF16)
    out = own(_dot(cv_t, e.astype(BF16)))
    pad = WINDOW - n_rows
    v_t = jnp.concatenate([v, jnp.zeros((pad, KV_W), F32)], axis=0).T.astype(BF16)
    ex_pad = jnp.concatenate([ex, jnp.zeros((pad, n_col), F32)], axis=0).astype(BF16)
    out = (out + _dot(v_t, ex_pad)) / den
    half = n_col // KV_HEADS
    both_t = jnp.concatenate([out[:HEAD_DIM, :half], out[HEAD_DIM:, half:]], axis=0).T
    for j in range(GROUP):
        att_ref[:, j * LANES:(j + 1) * LANES] = both_t[j * n_rows:(j + 1) * n_rows].astype(att_ref.dtype)


def _decode_attn(x, ck, cv, w):
    m, d = x.shape
    nb = ck.shape[0]
    nt = m // nb
    sb = SEQ_BLOCK
    consts = [w['n1g'], w['wqkv'], w['gq'], w['gk'], w['oq'], w['ok'],
              w['bias_sc'], w['bias_sn'], w['sink_s']]
    cache_spec = pl.BlockSpec((sb, WINDOW, KV_W), lambda i: (i, 0, 0))
    return pl.pallas_call(
        _decode_attn_kernel,
        out_shape=(jax.ShapeDtypeStruct((m, Q_W), BF16),
                   jax.ShapeDtypeStruct(ck.shape, F32),
                   jax.ShapeDtypeStruct(cv.shape, F32)),
        grid=(nb // sb,),
        in_specs=[pl.BlockSpec((sb * nt, d), lambda i: (i, 0)), cache_spec, cache_spec]
                 + [_const_spec(c.shape) for c in consts],
        out_specs=(pl.BlockSpec((sb * nt, Q_W), lambda i: (i, 0)), cache_spec, cache_spec),
        compiler_params=pltpu.CompilerParams(
            dimension_semantics=("arbitrary",), vmem_limit_bytes=VMEM_LIMIT),
        name="decode_attn",
    )(x, ck, cv, *consts)


def _decode_mixer_kernel(x_ref, cst_ref, h0_ref, n1g_ref, win_ref, cw_ref, cb_ref, bd_ref,
                         ba_ref, bx_ref, lam_ref,
                         h_out, pre_out, cnew_ref, hnew_ref):
    nb = h0_ref.shape[0]
    nt = x_ref.shape[0] // nb
    d = x_ref.shape[1]
    xn = _rms(x_ref[...], n1g_ref[...]).astype(BF16)
    xr = _dot(xn, win_ref[:, REST0:REST0 + d])
    prev = cst_ref[...]
    slabs = [prev[k * nb:(k + 1) * nb] for k in range(CONV_W - 1)]
    slabs += [xr[k * nb:(k + 1) * nb] for k in range(nt)]
    xc = jnp.concatenate(
        [cb_ref[...] + sum(cw_ref[j:j + 1, :] * slabs[ti + j] for j in range(CONV_W))
         for ti in range(nt)], axis=0)
    cnew_ref[...] = jnp.concatenate(slabs[-(CONV_W - 1):], axis=0)

    a, bterm = _rglru_gates(xc, bd_ref, ba_ref[...], bx_ref[...], lam_ref[...])
    h = h0_ref[...]
    hs = []
    for ti in range(nt):
        h = a[ti * nb:(ti + 1) * nb] * h + bterm[ti * nb:(ti + 1) * nb]
        hs.append(h)
    hnew_ref[...] = h
    h_out[...] = jnp.concatenate(hs, axis=0).astype(BF16)
    pre_out[...] = _dot(xn, win_ref[:, REST0 + d:REST0 + 4 * d]).astype(BF16)


def _decode_mixer(x, cst, h0, w):
    m, d = x.shape
    consts = [w['n1g'], w['win'], w['cw'], w['cb'], w['bd'], w['ba'], w['bx'], w['lam']]
    ins = [x, cst, h0] + consts
    whole = lambda shape: pl.BlockSpec(shape, lambda i: (0,) * len(shape))
    return pl.pallas_call(
        _decode_mixer_kernel,
        out_shape=(jax.ShapeDtypeStruct((m, d), BF16),
                   jax.ShapeDtypeStruct((m, 3 * d), BF16),
                   jax.ShapeDtypeStruct(cst.shape, F32),
                   jax.ShapeDtypeStruct(h0.shape, F32)),
        grid=(1,),
        in_specs=[_const_spec(c.shape) for c in ins],
        out_specs=(whole((m, d)), whole((m, 3 * d)), whole(cst.shape), whole(h0.shape)),
        compiler_params=pltpu.CompilerParams(
            dimension_semantics=("arbitrary",), vmem_limit_bytes=VMEM_LIMIT),
        name="decode_mixer",
    )(*ins)


def _rel_bucket(dist):
    n = np.maximum(dist, 0)
    max_exact = REL_BUCKETS // 2
    nf = np.maximum(n, 1).astype(np.float32)
    large = max_exact + (np.log(nf / max_exact) / math.log(REL_MAX_DIST / max_exact)
                         * (REL_BUCKETS - max_exact)).astype(np.int32)
    large = np.minimum(large, REL_BUCKETS - 1)
    return np.where(n < max_exact, n, large)


def _bias_rows(rel_bias, dist, mask):
    tb = jnp.where(mask[:, :, None], rel_bias[_rel_bucket(dist)].astype(F32), NEG_INF)
    tq, tk = dist.shape
    return jnp.transpose(tb, (2, 0, 1)).reshape(N_HEADS * tq, tk)


def _prompt_bias_rows(rel_bias):
    span = 3 * WINDOW
    k = np.arange(span)
    dist = WINDOW + np.where(k < WINDOW, k, k - span)
    valid = (dist >= 0) & (dist <= WINDOW)
    return jnp.where(valid[:, None], rel_bias[_rel_bucket(dist)].astype(F32), NEG_INF).T


def _gate_blocks(rg_wa, rg_wx, n_grp):
    per = rg_wa.shape[0] // n_grp
    both = jnp.stack([rg_wa, rg_wx]).reshape(2, n_grp, per, RNN_BS, RNN_BS)
    on_diag = np.eye(per, dtype=bool)[None, None, :, None, :, None]
    blocks = jnp.where(on_diag, both[:, :, :, :, None, :], 0.0)
    blocks = jnp.transpose(blocks, (1, 2, 3, 0, 4, 5))
    return blocks.reshape(n_grp, per * RNN_BS, 2 * per * RNN_BS).astype(BF16)


def _head_avg(width):
    idx = np.arange(width) // HEAD_DIM
    return jnp.asarray((idx[:, None] == idx[None, :]).astype(np.float32) / HEAD_DIM, BF16)


def _prepare(rel_bias, norm1_g, w_in, q_norm_g, k_norm_g, sinks, w_o_attn, conv_w, conv_b,
             rg_wa, rg_ba, rg_wx, rg_bx, rg_lambda, w_o_rnn, w_out, norm2_g, w_up, w_down,
             ple_norm_g, w_ple_gate, w_ple, n_dec):
    d = w_in.shape[0]
    order = [g * GROUP + j for j in range(GROUP) for g in range(KV_HEADS)]
    perm = np.concatenate([np.arange(h * HEAD_DIM, (h + 1) * HEAD_DIM) for h in order])
    row = lambda a: a.reshape(1, -1).astype(F32)
    w = {}
    w['n1g'] = row(norm1_g)
    w['win'] = w_in.astype(BF16)
    w['wqkv'] = jnp.concatenate([w['win'][:, :Q_W][:, perm], w['win'][:, Q_W:REST0]], axis=1)
    w['gq'] = row(jnp.tile(q_norm_g, N_HEADS)) * (HEAD_DIM ** -0.5)
    w['gk'] = row(jnp.tile(k_norm_g, KV_HEADS))
    w['oq'] = _head_avg(Q_W)
    w['ok'] = _head_avg(KV_W)
    w['cw'] = conv_w.astype(F32)
    w['cb'] = row(conv_b)
    w['bd'] = _gate_blocks(rg_wa, rg_wx, d // MXU_DIM)
    w['ba'] = row(rg_ba)
    w['bx'] = row(rg_bx)
    w['lam'] = row(rg_lambda)
    w['wor32'], w['wout32'], w['wup32'], w['wdn32'] = w_o_rnn, w_out, w_up, w_down
    w['woa32'], w['wpg32'], w['wple32'] = w_o_attn, w_ple_gate, w_ple
    w['n2g'] = row(norm2_g)
    w['pg'] = row(ple_norm_g)

    w['bias_p'] = _prompt_bias_rows(rel_bias)
    sink_rows = sinks.astype(F32).reshape(KV_HEADS, GROUP, 1)
    w['sink_p'] = jnp.repeat(sink_rows, WINDOW, axis=2).reshape(KV_HEADS, 1, GROUP * WINDOW)

    sb = SEQ_BLOCK
    ti = np.arange(n_dec)[:, None]
    dist_c = ti + WINDOW - np.arange(WINDOW)[None, :]
    dist_n = ti - np.arange(n_dec)[None, :]
    n_col = N_HEADS * sb * n_dec
    col = np.arange(n_col)
    col_head_t = col // (sb * n_dec) * n_dec + col % n_dec
    spread = (np.arange(N_HEADS * n_dec)[:, None] == col_head_t[None, :]).astype(np.float32)
    expand = lambda a: jnp.dot(a.T, spread, precision=lax.Precision.HIGHEST)
    w['bias_sc'] = expand(_bias_rows(rel_bias, dist_c, dist_c <= WINDOW))
    bias_n = expand(_bias_rows(rel_bias, dist_n, dist_n >= 0))
    own = (np.arange(sb * n_dec) // n_dec)[:, None] == (col // n_dec % sb)[None, :]
    w['bias_sn'] = jnp.where(own, jnp.tile(bias_n, (sb, 1)), NEG_INF)
    w['sink_s'] = jnp.repeat(sinks.astype(F32), sb * n_dec)[None, :]
    return w


def kernel(x_prompt, x_sample, cache_k_win, cache_v_win, state_conv, state_h, p_prompt, p_sample, rel_bias, norm1_g, w_in, q_norm_g, k_norm_g, sinks, w_o_attn, conv_w, conv_b, rg_wa, rg_ba, rg_wx, rg_bx, rg_lambda, w_o_rnn, w_out, norm2_g, w_up, w_down, ple_norm_g, w_ple_gate, w_ple):
    depth = w_in.shape[0]
    assert depth == 1, "single-layer step"
    b, t, d = x_prompt.shape
    nb, nt, _ = x_sample.shape
    w = _prepare(rel_bias, norm1_g[0], w_in[0], q_norm_g[0], k_norm_g[0], sinks[0], w_o_attn[0],
                 conv_w[0], conv_b[0], rg_wa[0], rg_ba[0], rg_wx[0], rg_bx[0], rg_lambda[0],
                 w_o_rnn[0], w_out[0], norm2_g[0], w_up[0], w_down[0], ple_norm_g[0],
                 w_ple_gate[0], w_ple[0], nt)

    (hp_rows, attp, prep, kp, vp, cp, hp,
     w['wor'], w['wout'], w['wup'], w['wdn'], w['woa'], w['wpg'], w['wple']) = _prompt_mixer(x_prompt, w)
    flat = lambda a: a.reshape(b * t, a.shape[-1])

    att, ks, vs = _decode_attn(x_sample.reshape(nb * nt, d),
                               cache_k_win[0].reshape(nb, WINDOW, KV_W),
                               cache_v_win[0].reshape(nb, WINDOW, KV_W), w)
    to_tm = lambda a: jnp.swapaxes(a, 0, 1).reshape(-1, a.shape[-1])
    xs_tm = to_tm(x_sample)
    hs_rows, pres, cs, hs = _decode_mixer(xs_tm, to_tm(state_conv[0]), state_h[0], w)

    yp, ys = _channel((flat(x_prompt), flat(hp_rows), flat(attp), flat(prep), flat(p_prompt[0])),
                      (xs_tm, hs_rows, to_tm(att.reshape(nb, nt, Q_W)), pres, to_tm(p_sample[0])), w)
    yp = yp.reshape(b, t, d)
    from_tm = lambda a, n: jnp.swapaxes(a.reshape(n, nb, a.shape[-1]), 0, 1)

    kv_shape = (1, -1, WINDOW, KV_HEADS, HEAD_DIM)
    return (yp, from_tm(ys, nt),
            kp.reshape(kv_shape), vp.reshape(kv_shape), cp[None], hp.reshape(1, b, d),
            ks.reshape(kv_shape), vs.reshape(kv_shape), from_tm(cs, CONV_W - 1)[None], hs[None])
```

```python
import functools
import math

import numpy as np
import jax
import jax.numpy as jnp
from jax import lax
from jax.experimental import pallas as pl
from jax.experimental.pallas import tpu as pltpu

F32 = jnp.float32
BF16 = jnp.bfloat16

N_HEADS = 8
KV_HEADS = 2
GROUP = N_HEADS // KV_HEADS
HEAD_DIM = 64
Q_W = N_HEADS * HEAD_DIM
KV_W = KV_HEADS * HEAD_DIM
REST0 = Q_W + 2 * KV_W
WINDOW = 128
REL_BUCKETS = 32
REL_MAX_DIST = 128
RNN_BS = 64
CONV_W = 4
RG_C = 8.0
EPS = 1e-6
NEG_INF = -1e30
EXPM1_SERIES_BELOW = 2.0 ** -11
SQRT_FLOOR = 1e-30

LANES = 128
SUBLANES = 8
MXU_DIM = 256
VMEM_LIMIT = 56 * 1024 * 1024

TM_MIX = 512
TM_MLP = 512
FF_CHUNK = 1024
SEQ_BLOCK = 16
PACE_LAG = 2
SEG_PAD = 4


def _dot(a, b):
    return jnp.dot(a, b, preferred_element_type=F32)


def _dot_nt(a, b):
    return lax.dot_general(a, b, (((1,), (1,)), ((), ())), preferred_element_type=F32)


def _rms(x, g):
    ms = jnp.mean(x * x, axis=-1, keepdims=True)
    return x * lax.rsqrt(ms + EPS) * g


def _head_rms(x, ones_ref, g):
    ms = _dot((x * x).astype(BF16), ones_ref[...])
    return x * lax.rsqrt(ms + EPS) * g


def _lower_half(shape):
    return lax.broadcasted_iota(jnp.int32, shape, len(shape) - 1) < HEAD_DIM


def _neg_expm1_2x(x, exp_x):
    return jnp.where(x > -EXPM1_SERIES_BELOW, (-2.0 * x) * (1.0 + x), 1.0 - exp_x * exp_x)


def _log_decay(lam):
    return -RG_C * jax.nn.softplus(-lam)


def _gate_math(ya, yx, xc, ba, bx, log_decay):
    r = jax.nn.sigmoid(ya + ba)
    i = jax.nn.sigmoid(yx + bx)
    log_a = log_decay * r
    a = jnp.exp(log_a)
    y = _neg_expm1_2x(log_a, a)
    bterm = (y * lax.rsqrt(jnp.maximum(y, SQRT_FLOOR))) * (i * xc)
    return a, bterm


def _rglru_gates(xc, bd_ref, ba, bx, lam):
    xcb = xc.astype(BF16)
    ya, yx = [], []
    for gi in range(xc.shape[1] // MXU_DIM):
        y = _dot(xcb[:, gi * MXU_DIM:(gi + 1) * MXU_DIM], bd_ref[gi])
        ya.append(y[:, :MXU_DIM])
        yx.append(y[:, MXU_DIM:])
    return _gate_math(jnp.concatenate(ya, axis=1), jnp.concatenate(yx, axis=1), xc, ba, bx,
                      _log_decay(lam))


def _prompt_mixer_kernel(x_ref, n1g_ref, wqkv_ref, win_ref, gq_ref, gk_ref, oq_ref, ok_ref,
                         bias_rows_ref, sink_ref, cw8_ref, cb8_ref, bd_ref, ba8_ref, bx8_ref, lam8_ref,
                         wor32_ref, wout32_ref, wup32_ref, wdn32_ref, woa32_ref, wpg32_ref, wple32_ref,
                         h_out, att, pre_out, kwin_ref, vwin_ref, cst_ref, hst_ref,
                         wor16_ref, wout16_ref, wup16_ref, wdn16_ref, woa16_ref, wpg16_ref, wple16_ref,
                         bias_ref, proj, projx, kbuf, vbuf, xp, hc, logit_s, e_s, xc_s, y_s, hl_s, ac_s, hn_s,
                         *, blocks_per_seq):
    tm, d = x_ref.shape
    n = pl.program_id(0)
    for src, dst in ((wor32_ref, wor16_ref), (wout32_ref, wout16_ref), (wup32_ref, wup16_ref),
                     (wdn32_ref, wdn16_ref), (woa32_ref, woa16_ref), (wpg32_ref, wpg16_ref),
                     (wple32_ref, wple16_ref)):
        dst[...] = src[...].astype(BF16)

    @pl.when(n == 0)
    def _():
        proj[1] = jnp.zeros(proj.shape[1:], F32)
        projx[1] = jnp.zeros(projx.shape[1:], F32)
        kbuf[...] = jnp.zeros(kbuf.shape, BF16)
        vbuf[...] = jnp.zeros(vbuf.shape, BF16)
        xp[...] = jnp.zeros(xp.shape, F32)
        hc[...] = jnp.zeros(hc.shape, F32)
        has_prev = lax.broadcasted_iota(jnp.int32, (2 * WINDOW, WINDOW), 0) >= WINDOW
        for h in range(N_HEADS):
            row = jnp.broadcast_to(bias_rows_ref[h:h + 1, :], (2 * WINDOW, bias_rows_ref.shape[1]))
            table = pltpu.roll(row, 0, 1, stride=1, stride_axis=0)[:, :WINDOW]
            at = (h // GROUP, slice(None), slice((h % GROUP) * WINDOW, (h % GROUP + 1) * WINDOW))
            bias_ref[(0,) + at] = table
            bias_ref[(1,) + at] = jnp.where(has_prev, table, NEG_INF)

    wslot = lax.rem(n, 2)
    rslot = 1 - wslot
    fresh = lax.rem(n + blocks_per_seq - 1, blocks_per_seq) == 0
    seg = tm // SUBLANES
    pitch = seg + SEG_PAD

    xn = _rms(x_ref[...], n1g_ref[...]).astype(BF16)
    n_qkv = wqkv_ref.shape[1]
    tokens = []

    def project(lo):
        res = _dot(xn, (wqkv_ref if lo < n_qkv else win_ref)[:, lo:lo + MXU_DIM])
        tokens.append(res[0:1, :])
        if lo < n_qkv:
            for i in range(MXU_DIM // LANES):
                proj[wslot, lo // LANES + i] = res[:, i * LANES:(i + 1) * LANES]
        elif lo < REST0 + d:
            for i in range(MXU_DIM // LANES):
                for sgm in range(SUBLANES):
                    projx[wslot, (lo - REST0) // LANES + i, sgm * pitch:sgm * pitch + seg, :] = (
                        res[sgm * seg:(sgm + 1) * seg, i * LANES:(i + 1) * LANES])
        else:
            pre_out[:, lo - REST0 - d:lo - REST0 - d + MXU_DIM] = res.astype(BF16)

    pending = list(range(0, win_ref.shape[1], MXU_DIM))

    def pace():
        if len(tokens) < PACE_LAG:
            return jnp.zeros((1, MXU_DIM), F32)
        bits = lax.bitcast_convert_type(tokens[-PACE_LAG], jnp.uint32)
        return lax.bitcast_convert_type((bits >> 16) >> 16, F32)

    def emit_stage1(count):
        for _ in range(min(count, len(pending))):
            project(pending.pop(0))

    def kept(lo, hi):
        return jnp.concatenate([proj[rslot, t] for t in range(lo // LANES, hi // LANES)], axis=1)

    emit_stage1(2)
    qn = _head_rms(kept(0, Q_W), oq_ref, gq_ref[...]).astype(BF16)
    kn = _head_rms(kept(Q_W, Q_W + KV_W), ok_ref, gk_ref[...])
    v = kept(Q_W + KV_W, n_qkv)
    kbuf[0:WINDOW, :] = jnp.where(fresh, jnp.zeros((WINDOW, KV_W), BF16), kbuf[tm:tm + WINDOW, :])
    vbuf[:, 0:WINDOW] = jnp.where(fresh, jnp.zeros((KV_W, WINDOW), BF16), vbuf[:, tm:tm + WINDOW])
    kbuf[WINDOW:WINDOW + tm, :] = kn.astype(BF16)
    vbuf[:, WINDOW:WINDOW + tm] = v.T.astype(BF16)
    kwin_ref[...] = kn[tm - WINDOW:, :]
    vwin_ref[...] = v[tm - WINDOW:, :]

    first = jnp.where(fresh, 1, 0)
    lower = _lower_half((WINDOW, LANES))
    zero = jnp.zeros((WINDOW, LANES), BF16)

    def attend(s):
        rows = slice(s * WINDOW, (s + 1) * WINDOW)
        keys = kbuf[s * WINDOW:(s + 2) * WINDOW, :]
        vals_t = vbuf[:, s * WINDOW:(s + 2) * WINDOW]
        outs = []
        for g in range(KV_HEADS):
            qs = []
            for j in range(GROUP):
                slab = qn[rows, j * LANES:(j + 1) * LANES]
                qs.append(jnp.where(lower, slab, zero) if g == 0 else jnp.where(lower, zero, slab))
            logit_s[g] = _dot_nt(keys, jnp.concatenate(qs, axis=0))
            yield
            inv = []
            for j in range(GROUP):
                blk = slice(j * WINDOW, (j + 1) * WINDOW)
                bias = bias_ref[first, g, :, blk] if s == 0 else bias_ref[0, g, :, blk]
                logit = logit_s[g, :, blk] + bias
                sink = sink_ref[g, :, blk] + pace()[:, :WINDOW]
                m = jnp.maximum(jnp.max(logit, axis=0, keepdims=True), sink)
                e = jnp.exp(logit - m)
                inv.append(1.0 / (jnp.sum(e, axis=0, keepdims=True) + jnp.exp(sink - m)))
                e_s[g, :, blk] = e.astype(BF16)
                yield
            outs.append(_dot(vals_t, e_s[g]) * jnp.concatenate(inv, axis=1))
        for j in range(GROUP):
            blk = slice(j * WINDOW, (j + 1) * WINDOW)
            both = jnp.concatenate([outs[0][:HEAD_DIM, blk], outs[1][HEAD_DIM:, blk]], axis=0)
            att[rows, j * LANES:(j + 1) * LANES] = both.T.astype(BF16)
        yield

    sub8 = lax.broadcasted_iota(jnp.int32, (SUBLANES, MXU_DIM), 0)

    def recur(gi):
        cols = slice(gi * MXU_DIM, (gi + 1) * MXU_DIM)
        tiles = range(gi * (MXU_DIM // LANES), (gi + 1) * (MXU_DIM // LANES))
        by_segment = lambda k: pl.ds(k, SUBLANES, stride=pitch)
        x_at = lambda k: jnp.concatenate([projx[rslot, t, by_segment(k), :] for t in tiles], axis=1)
        last = SUBLANES * pitch - SEG_PAD
        x_tail = lambda n_rows: jnp.concatenate(
            [projx[rslot, t, last - n_rows:last, :] for t in tiles], axis=1)
        prev = jnp.where(fresh, 0.0, xp[:, cols])
        window = [jnp.where(sub8 == 0, prev[SUBLANES - j:SUBLANES - j + 1, :],
                            pltpu.roll(x_at(seg - j), 1, axis=0)) for j in range(CONV_W - 1, 0, -1)]
        for k in range(seg):
            window.append(x_at(k))
            xc_s[gi, k * SUBLANES:(k + 1) * SUBLANES, :] = cb8_ref[:, cols] + sum(
                cw8_ref[j, :, cols] * window[j] for j in range(CONV_W))
            window.pop(0)
            if k % 16 == 15:
                yield
        xp[:, cols] = x_tail(SUBLANES)
        cst_ref[:, cols] = x_tail(CONV_W - 1)
        y_s[gi] = _dot(xc_s[gi].astype(BF16), bd_ref[gi])
        yield
        ba, bx = ba8_ref[:, cols], bx8_ref[:, cols]
        log_decay = _log_decay(lam8_ref[:, cols])
        h_loc = a_cum = None
        for k in range(seg):
            r = slice(k * SUBLANES, (k + 1) * SUBLANES)
            a, bterm = _gate_math(y_s[gi, r, :MXU_DIM], y_s[gi, r, MXU_DIM:], xc_s[gi, r, :] + pace(),
                                  ba, bx, log_decay)
            h_loc = bterm if k == 0 else a * h_loc + bterm
            a_cum = a if k == 0 else a * a_cum
            hl_s[gi, r, :] = h_loc
            ac_s[gi, r, :] = a_cum
            if k % 4 == 3:
                yield
        h_in = jnp.where(fresh, 0.0, hc[:, cols])
        carry_in = []
        for sgm in range(SUBLANES):
            carry_in.append(h_in)
            h_in = h_loc[sgm:sgm + 1, :] + a_cum[sgm:sgm + 1, :] * h_in
        hc[:, cols] = h_in
        hst_ref[:, cols] = h_in
        carry_in = jnp.concatenate(carry_in, axis=0)
        for k in range(seg):
            r = slice(k * SUBLANES, (k + 1) * SUBLANES)
            h = hl_s[gi, r, :] + ac_s[gi, r, :] * carry_in
            for i in range(MXU_DIM // LANES):
                hn_s[gi, i, by_segment(k), :] = h[:, i * LANES:(i + 1) * LANES]
            if k % 8 == 7:
                yield
        h_out[:, cols] = jnp.concatenate(
            [jnp.concatenate([hn_s[gi, i, sgm * pitch:sgm * pitch + seg, :] for sgm in range(SUBLANES)], axis=0)
             for i in range(MXU_DIM // LANES)], axis=1).astype(BF16)
        yield

    n_sub = tm // WINDOW
    n_grp = d // MXU_DIM
    pieces = []
    for u in range(max(n_sub, n_grp)):
        if u < n_sub:
            pieces.append(attend(u))
        if u < n_grp:
            pieces.append(recur(u))
    n_spots = n_sub * (KV_HEADS * (1 + GROUP) + 1) + n_grp * (2 + seg // 16 + seg // 4 + seg // 8)
    per_spot = len(pending) / n_spots
    due = 0.0
    for piece in pieces:
        for _ in piece:
            due += per_spot
            emit_stage1(int(due))
            due -= int(due)
    emit_stage1(len(pending))


def _const_spec(shape):
    nd = len(shape)
    return pl.BlockSpec(shape, lambda *_: (0,) * nd, pipeline_mode=pl.Buffered(1))


def _prompt_mixer(x, w):
    b, t, d = x.shape
    tm = TM_MIX
    nt = t // tm
    n_blocks = b * nt
    padded_rows = SUBLANES * (tm // SUBLANES + SEG_PAD)
    consts = [w['n1g'], w['wqkv'], w['win'], w['gq'], w['gk'], w['oq'], w['ok'],
              w['bias_p'], w['sink_p'], w['cw8'], w['cb8'], w['bd'], w['ba8'], w['bx8'], w['lam8']]
    to_round = [w['wor32'], w['wout32'], w['wup32'], w['wdn32'], w['woa32'], w['wpg32'], w['wple32']]
    bf16_rows = 2 * SUBLANES
    n_chunks = lambda a: min(n_blocks, a.shape[0] // bf16_rows)

    def chunks(a, src_block=lambda c: c):
        assert a.shape[0] % (n_chunks(a) * bf16_rows) == 0
        block = (a.shape[0] // n_chunks(a), a.shape[1])
        at = lambda n: jnp.minimum(n, n_chunks(a) - 1)
        return (pl.BlockSpec(block, lambda n: (src_block(at(n)), 0)),
                pl.BlockSpec(block, lambda n: (at(n), 0)))

    per_head = n_chunks(w['woa32']) // N_HEADS
    assert per_head >= 1 and per_head * N_HEADS == n_chunks(w['woa32'])

    def woa_source(c):
        slot = c // per_head
        return ((slot % KV_HEADS) * GROUP + slot // KV_HEADS) * per_head + c % per_head

    round_specs = [chunks(a, woa_source if a is w['woa32'] else (lambda c: c)) for a in to_round]
    out_shape = (
        jax.ShapeDtypeStruct((b, t, d), BF16),
        jax.ShapeDtypeStruct((b, t, Q_W), BF16),
        jax.ShapeDtypeStruct((b, t, 3 * d), BF16),
        jax.ShapeDtypeStruct((b, WINDOW, KV_W), F32),
        jax.ShapeDtypeStruct((b, WINDOW, KV_W), F32),
        jax.ShapeDtypeStruct((b, CONV_W - 1, d), F32),
        jax.ShapeDtypeStruct((b, 1, d), F32),
    ) + tuple(jax.ShapeDtypeStruct(a.shape, BF16) for a in to_round)

    def cur(n):
        i = jnp.minimum(n, n_blocks - 1)
        return (i // nt, i % nt, 0)

    def prv(n):
        i = jnp.maximum(n - 1, 0)
        return (i // nt, i % nt, 0)

    per_seq = lambda n: (jnp.maximum(n - 1, 0) // nt, 0, 0)
    return pl.pallas_call(
        functools.partial(_prompt_mixer_kernel, blocks_per_seq=nt),
        out_shape=out_shape,
        grid=(n_blocks + 1,),
        in_specs=[pl.BlockSpec((None, tm, d), cur)]
                 + [_const_spec(c.shape) for c in consts]
                 + [src for src, _ in round_specs],
        out_specs=(
            pl.BlockSpec((None, tm, d), prv),
            pl.BlockSpec((None, tm, Q_W), prv),
            pl.BlockSpec((None, tm, 3 * d), cur),
            pl.BlockSpec((None, WINDOW, KV_W), per_seq),
            pl.BlockSpec((None, WINDOW, KV_W), per_seq),
            pl.BlockSpec((None, CONV_W - 1, d), per_seq),
            pl.BlockSpec((None, 1, d), per_seq),
        ) + tuple(dst for _, dst in round_specs),
        scratch_shapes=[
            pltpu.VMEM((2, KV_HEADS, 2 * WINDOW, GROUP * WINDOW), F32),
            pltpu.VMEM((2, REST0 // LANES, tm, LANES), F32),
            pltpu.VMEM((2, d // LANES, padded_rows, LANES), F32),
            pltpu.VMEM((WINDOW + tm, KV_W), BF16),
            pltpu.VMEM((KV_W, WINDOW + tm), BF16),
            pltpu.VMEM((SUBLANES, d), F32),
            pltpu.VMEM((1, d), F32),
            pltpu.VMEM((KV_HEADS, 2 * WINDOW, GROUP * WINDOW), F32),
            pltpu.VMEM((KV_HEADS, 2 * WINDOW, GROUP * WINDOW), BF16),
            pltpu.VMEM((d // MXU_DIM, tm, MXU_DIM), F32),
            pltpu.VMEM((d // MXU_DIM, tm, 2 * MXU_DIM), F32),
            pltpu.VMEM((d // MXU_DIM, tm, MXU_DIM), F32),
            pltpu.VMEM((d // MXU_DIM, tm, MXU_DIM), F32),
            pltpu.VMEM((d // MXU_DIM, MXU_DIM // LANES, padded_rows, LANES), F32),
        ],
        compiler_params=pltpu.CompilerParams(
            dimension_semantics=("arbitrary",), vmem_limit_bytes=VMEM_LIMIT),
        name="prompt_mixer",
    )(x, *consts, *to_round)


def _channel_rows(x_ref, h_ref, att_ref, pre_ref, p_ref, woa_ref, wor_ref, wout_ref,
                  n2g_ref, wup_ref, wdn_ref, pg_ref, wpg_ref, wple_ref, o_ref):
    d = x_ref.shape[1]
    pre = pre_ref[...].astype(F32)
    hg = (h_ref[...].astype(F32) * jax.nn.gelu(pre[:, :d])).astype(BF16)
    rnn = _dot(hg, wor_ref[...])
    atto = _dot(att_ref[...], woa_ref[...])
    mix = (jax.nn.sigmoid(pre[:, d:2 * d]) * atto + jax.nn.sigmoid(pre[:, 2 * d:]) * rnn).astype(BF16)
    x = x_ref[...] + _dot(mix, wout_ref[...])
    xn = _rms(x, n2g_ref[...]).astype(BF16)
    acc = x
    for c in range(wup_ref.shape[1] // FF_CHUNK):
        cols = slice(c * FF_CHUNK, (c + 1) * FF_CHUNK)
        hmid = jnp.maximum(_dot(xn, wup_ref[:, cols]), 0.0)
        acc = acc + _dot((hmid * hmid).astype(BF16), wdn_ref[cols, :])
    gate = jax.nn.sigmoid(_dot(_rms(acc, pg_ref[...]).astype(BF16), wpg_ref[...]))
    o_ref[...] = acc + gate * _dot(p_ref[...].astype(BF16), wple_ref[...])


N_ROW_INPUTS = 5


def _channel_kernel(*refs, first_steps):
    first, second = refs[:N_ROW_INPUTS], refs[N_ROW_INPUTS:2 * N_ROW_INPUTS]
    consts, (o_first, o_second) = refs[2 * N_ROW_INPUTS:-2], refs[-2:]
    i = pl.program_id(0)

    @pl.when(i < first_steps)
    def _():
        _channel_rows(*first, *consts, o_first)

    @pl.when(i >= first_steps)
    def _():
        _channel_rows(*second, *consts, o_second)


def _channel(first, second, w):
    m1, d = first[0].shape
    m2 = second[0].shape[0]
    tm = min(TM_MLP, m1, m2)
    n1, n2 = m1 // tm, m2 // tm
    consts = [w['woa'], w['wor'], w['wout'], w['n2g'], w['wup'], w['wdn'], w['pg'], w['wpg'], w['wple']]
    rows1 = lambda a: pl.BlockSpec((tm, a.shape[1]), lambda i: (jnp.minimum(i, n1 - 1), 0))
    rows2 = lambda a: pl.BlockSpec((tm, a.shape[1]), lambda i: (jnp.maximum(i - n1, 0), 0),
                                   pipeline_mode=pl.Buffered(1))
    return pl.pallas_call(
        functools.partial(_channel_kernel, first_steps=n1),
        out_shape=(jax.ShapeDtypeStruct((m1, d), F32), jax.ShapeDtypeStruct((m2, d), F32)),
        grid=(n1 + n2,),
        in_specs=[rows1(a) for a in first] + [rows2(a) for a in second]
                 + [_const_spec(c.shape) for c in consts],
        out_specs=(rows1(first[0]), rows2(second[0])),
        compiler_params=pltpu.CompilerParams(
            dimension_semantics=("arbitrary",), vmem_limit_bytes=VMEM_LIMIT),
        name="merge_mlp_ple",
    )(*first, *second, *consts)


def _decode_attn_kernel(x_ref, ck_ref, cv_ref, n1g_ref, wqkv_ref, gq_ref, gk_ref, oq_ref, ok_ref,
                        bias_c_ref, bias_n_ref, sink_ref,
                        att_ref, kwin_ref, vwin_ref, qkv_s):
    sb = ck_ref.shape[0]
    nt = att_ref.shape[0] // sb

    @pl.when(pl.program_id(0) == 0)
    def _():
        qkv_s[...] = _dot(_rms(x_ref[...], n1g_ref[...]).astype(BF16), wqkv_ref[...])

    qkv = qkv_s[pl.ds(pl.multiple_of(pl.program_id(0) * (sb * nt), sb * nt), sb * nt), :]
    qn = _head_rms(qkv[:, :Q_W], oq_ref, gq_ref[...])
    kn = _head_rms(qkv[:, Q_W:Q_W + KV_W], ok_ref, gk_ref[...])
    v = qkv[:, Q_W + KV_W:]

    n_rows = sb * nt
    ck = ck_ref[...]
    cv = cv_ref[...]
    for b in range(sb):
        rows = slice(b * nt, (b + 1) * nt)
        kwin_ref[b, 0:WINDOW - nt, :] = ck[b, nt:, :]
        vwin_ref[b, 0:WINDOW - nt, :] = cv[b, nt:, :]
        kwin_ref[b, WINDOW - nt:WINDOW, :] = kn[rows]
        vwin_ref[b, WINDOW - nt:WINDOW, :] = v[rows]

    qb = qn.astype(BF16)
    lower = _lower_half((n_rows, LANES))
    zero = jnp.zeros((n_rows, LANES), BF16)
    q_all = jnp.concatenate(
        [jnp.where(lower, qb[:, j * LANES:(j + 1) * LANES], zero) if g == 0
         else jnp.where(lower, zero, qb[:, j * LANES:(j + 1) * LANES])
         for g in range(KV_HEADS) for j in range(GROUP)], axis=0)
    n_col = q_all.shape[0]
    col_seq = (lax.broadcasted_iota(jnp.int32, (1, n_col), 1) // nt) % sb

    def own(big):
        out = big[0:WINDOW]
        for b in range(1, sb):
            out = jnp.where(col_seq == b, big[b * WINDOW:(b + 1) * WINDOW], out)
        return out

    st = own(_dot_nt(ck.reshape(sb * WINDOW, KV_W).astype(BF16), q_all)) + bias_c_ref[...]
    stx = _dot_nt(kn.astype(BF16), q_all) + bias_n_ref[...]
    sink = sink_ref[...]
    m = jnp.maximum(jnp.maximum(jnp.max(st, axis=0, keepdims=True),
                                jnp.max(stx, axis=0, keepdims=True)), sink)
    e = jnp.exp(st - m)
    ex = jnp.exp(stx - m)
    den = (jnp.sum(e, axis=0, keepdims=True) + jnp.sum(ex, axis=0, keepdims=True)
           + jnp.exp(sink - m))
    cv_t = jnp.concatenate([cv[b].T for b in range(sb)], axis=0).astype(BF16)
    out = own(_dot(cv_t, e.astype(BF16)))
    pad = WINDOW - n_rows
    v_t = jnp.concatenate([v, jnp.zeros((pad, KV_W), F32)], axis=0).T.astype(BF16)
    ex_pad = jnp.concatenate([ex, jnp.zeros((pad, n_col), F32)], axis=0).astype(BF16)
    out = (out + _dot(v_t, ex_pad)) / den
    half = n_col // KV_HEADS
    both_t = jnp.concatenate([out[:HEAD_DIM, :half], out[HEAD_DIM:, half:]], axis=0).T
    for j in range(GROUP):
        att_ref[:, j * LANES:(j + 1) * LANES] = both_t[j * n_rows:(j + 1) * n_rows].astype(att_ref.dtype)


def _decode_attn(x, ck, cv, w):
    m, d = x.shape
    nb = ck.shape[0]
    nt = m // nb
    sb = SEQ_BLOCK
    consts = [w['n1g'], w['wqkv'], w['gq'], w['gk'], w['oq'], w['ok'],
              w['bias_sc'], w['bias_sn'], w['sink_s']]
    cache_spec = pl.BlockSpec((sb, WINDOW, KV_W), lambda i: (i, 0, 0))
    return pl.pallas_call(
        _decode_attn_kernel,
        out_shape=(jax.ShapeDtypeStruct((m, Q_W), BF16),
                   jax.ShapeDtypeStruct(ck.shape, F32),
                   jax.ShapeDtypeStruct(cv.shape, F32)),
        grid=(nb // sb,),
        in_specs=[_const_spec(x.shape), cache_spec, cache_spec]
                 + [_const_spec(c.shape) for c in consts],
        out_specs=(pl.BlockSpec((sb * nt, Q_W), lambda i: (i, 0)), cache_spec, cache_spec),
        scratch_shapes=[pltpu.VMEM((m, w['wqkv'].shape[1]), F32)],
        compiler_params=pltpu.CompilerParams(
            dimension_semantics=("arbitrary",), vmem_limit_bytes=VMEM_LIMIT),
        name="decode_attn",
    )(x, ck, cv, *consts)


def _decode_mixer_kernel(x_ref, cst_ref, h0_ref, n1g_ref, win_ref, cw_ref, cb_ref, bd_ref,
                         ba_ref, bx_ref, lam_ref,
                         h_out, pre_out, cnew_ref, hnew_ref):
    nb = h0_ref.shape[0]
    nt = x_ref.shape[0] // nb
    d = x_ref.shape[1]
    xn = _rms(x_ref[...], n1g_ref[...]).astype(BF16)
    xr = _dot(xn, win_ref[:, REST0:REST0 + d])
    prev = cst_ref[...]
    slabs = [prev[k * nb:(k + 1) * nb] for k in range(CONV_W - 1)]
    slabs += [xr[k * nb:(k + 1) * nb] for k in range(nt)]
    xc = jnp.concatenate(
        [cb_ref[...] + sum(cw_ref[j:j + 1, :] * slabs[ti + j] for j in range(CONV_W))
         for ti in range(nt)], axis=0)
    cnew_ref[...] = jnp.concatenate(slabs[-(CONV_W - 1):], axis=0)

    a, bterm = _rglru_gates(xc, bd_ref, ba_ref[...], bx_ref[...], lam_ref[...])
    h = h0_ref[...]
    hs = []
    for ti in range(nt):
        h = a[ti * nb:(ti + 1) * nb] * h + bterm[ti * nb:(ti + 1) * nb]
        hs.append(h)
    hnew_ref[...] = h
    h_out[...] = jnp.concatenate(hs, axis=0).astype(BF16)
    pre_out[...] = _dot(xn, win_ref[:, REST0 + d:REST0 + 4 * d]).astype(BF16)


def _decode_mixer(x, cst, h0, w):
    m, d = x.shape
    consts = [w['n1g'], w['win'], w['cw'], w['cb'], w['bd'], w['ba'], w['bx'], w['lam']]
    ins = [x, cst, h0] + consts
    whole = lambda shape: pl.BlockSpec(shape, lambda i: (0,) * len(shape))
    return pl.pallas_call(
        _decode_mixer_kernel,
        out_shape=(jax.ShapeDtypeStruct((m, d), BF16),
                   jax.ShapeDtypeStruct((m, 3 * d), BF16),
                   jax.ShapeDtypeStruct(cst.shape, F32),
                   jax.ShapeDtypeStruct(h0.shape, F32)),
        grid=(1,),
        in_specs=[_const_spec(c.shape) for c in ins],
        out_specs=(whole((m, d)), whole((m, 3 * d)), whole(cst.shape), whole(h0.shape)),
        compiler_params=pltpu.CompilerParams(
            dimension_semantics=("arbitrary",), vmem_limit_bytes=VMEM_LIMIT),
        name="decode_mixer",
    )(*ins)


def _rel_bucket(dist):
    n = np.maximum(dist, 0)
    max_exact = REL_BUCKETS // 2
    nf = np.maximum(n, 1).astype(np.float32)
    large = max_exact + (np.log(nf / max_exact) / math.log(REL_MAX_DIST / max_exact)
                         * (REL_BUCKETS - max_exact)).astype(np.int32)
    large = np.minimum(large, REL_BUCKETS - 1)
    return np.where(n < max_exact, n, large)


def _bias_rows(rel_bias, dist, mask):
    tb = jnp.where(mask[:, :, None], rel_bias[_rel_bucket(dist)].astype(F32), NEG_INF)
    tq, tk = dist.shape
    return jnp.transpose(tb, (2, 0, 1)).reshape(N_HEADS * tq, tk)


def _prompt_bias_rows(rel_bias):
    span = 3 * WINDOW
    k = np.arange(span)
    dist = WINDOW + np.where(k < WINDOW, k, k - span)
    valid = (dist >= 0) & (dist <= WINDOW)
    return jnp.where(valid[:, None], rel_bias[_rel_bucket(dist)].astype(F32), NEG_INF).T


def _gate_blocks(rg_wa, rg_wx, n_grp):
    per = rg_wa.shape[0] // n_grp
    both = jnp.stack([rg_wa, rg_wx]).reshape(2, n_grp, per, RNN_BS, RNN_BS)
    on_diag = np.eye(per, dtype=bool)[None, None, :, None, :, None]
    blocks = jnp.where(on_diag, both[:, :, :, :, None, :], 0.0)
    blocks = jnp.transpose(blocks, (1, 2, 3, 0, 4, 5))
    return blocks.reshape(n_grp, per * RNN_BS, 2 * per * RNN_BS).astype(BF16)


def _head_avg(width):
    idx = np.arange(width) // HEAD_DIM
    return jnp.asarray((idx[:, None] == idx[None, :]).astype(np.float32) / HEAD_DIM, BF16)


def _prepare(rel_bias, norm1_g, w_in, q_norm_g, k_norm_g, sinks, w_o_attn, conv_w, conv_b,
             rg_wa, rg_ba, rg_wx, rg_bx, rg_lambda, w_o_rnn, w_out, norm2_g, w_up, w_down,
             ple_norm_g, w_ple_gate, w_ple, n_dec):
    d = w_in.shape[0]
    order = [g * GROUP + j for j in range(GROUP) for g in range(KV_HEADS)]
    perm = np.concatenate([np.arange(h * HEAD_DIM, (h + 1) * HEAD_DIM) for h in order])
    row = lambda a: a.reshape(1, -1).astype(F32)
    w = {}
    w['n1g'] = row(norm1_g)
    w['win'] = w_in.astype(BF16)
    w['wqkv'] = jnp.concatenate([w['win'][:, :Q_W][:, perm], w['win'][:, Q_W:REST0]], axis=1)
    w['gq'] = row(jnp.tile(q_norm_g, N_HEADS)) * (HEAD_DIM ** -0.5)
    w['gk'] = row(jnp.tile(k_norm_g, KV_HEADS))
    w['oq'] = _head_avg(Q_W)
    w['ok'] = _head_avg(KV_W)
    w['cw'] = conv_w.astype(F32)
    w['cb'] = row(conv_b)
    w['bd'] = _gate_blocks(rg_wa, rg_wx, d // MXU_DIM)
    w['ba'] = row(rg_ba)
    w['bx'] = row(rg_bx)
    w['lam'] = row(rg_lambda)
    rows8 = lambda a: jnp.broadcast_to(a, (SUBLANES, a.shape[-1]))
    w['cw8'] = jnp.broadcast_to(w['cw'][:, None, :], (CONV_W, SUBLANES, d))
    w['cb8'], w['ba8'], w['bx8'], w['lam8'] = (rows8(w[k]) for k in ('cb', 'ba', 'bx', 'lam'))
    w['wor32'], w['wout32'], w['wup32'], w['wdn32'] = w_o_rnn, w_out, w_up, w_down
    w['woa32'], w['wpg32'], w['wple32'] = w_o_attn, w_ple_gate, w_ple
    w['n2g'] = row(norm2_g)
    w['pg'] = row(ple_norm_g)

    w['bias_p'] = _prompt_bias_rows(rel_bias)
    sink_rows = sinks.astype(F32).reshape(KV_HEADS, GROUP, 1)
    w['sink_p'] = jnp.repeat(sink_rows, WINDOW, axis=2).reshape(KV_HEADS, 1, GROUP * WINDOW)

    sb = SEQ_BLOCK
    ti = np.arange(n_dec)[:, None]
    dist_c = ti + WINDOW - np.arange(WINDOW)[None, :]
    dist_n = ti - np.arange(n_dec)[None, :]
    n_col = N_HEADS * sb * n_dec
    col = np.arange(n_col)
    col_head_t = col // (sb * n_dec) * n_dec + col % n_dec
    spread = (np.arange(N_HEADS * n_dec)[:, None] == col_head_t[None, :]).astype(np.float32)
    expand = lambda a: jnp.dot(a.T, spread, precision=lax.Precision.HIGHEST)
    w['bias_sc'] = expand(_bias_rows(rel_bias, dist_c, dist_c <= WINDOW))
    bias_n = expand(_bias_rows(rel_bias, dist_n, dist_n >= 0))
    own = (np.arange(sb * n_dec) // n_dec)[:, None] == (col // n_dec % sb)[None, :]
    w['bias_sn'] = jnp.where(own, jnp.tile(bias_n, (sb, 1)), NEG_INF)
    w['sink_s'] = jnp.repeat(sinks.astype(F32), sb * n_dec)[None, :]
    return w


def kernel(x_prompt, x_sample, cache_k_win, cache_v_win, state_conv, state_h, p_prompt, p_sample, rel_bias, norm1_g, w_in, q_norm_g, k_norm_g, sinks, w_o_attn, conv_w, conv_b, rg_wa, rg_ba, rg_wx, rg_bx, rg_lambda, w_o_rnn, w_out, norm2_g, w_up, w_down, ple_norm_g, w_ple_gate, w_ple):
    depth = w_in.shape[0]
    assert depth == 1, "single-layer step"
    b, t, d = x_prompt.shape
    nb, nt, _ = x_sample.shape
    w = _prepare(rel_bias, norm1_g[0], w_in[0], q_norm_g[0], k_norm_g[0], sinks[0], w_o_attn[0],
                 conv_w[0], conv_b[0], rg_wa[0], rg_ba[0], rg_wx[0], rg_bx[0], rg_lambda[0],
                 w_o_rnn[0], w_out[0], norm2_g[0], w_up[0], w_down[0], ple_norm_g[0],
                 w_ple_gate[0], w_ple[0], nt)

    (hp_rows, attp, prep, kp, vp, cp, hp,
     w['wor'], w['wout'], w['wup'], w['wdn'], w['woa'], w['wpg'], w['wple']) = _prompt_mixer(x_prompt, w)
    flat = lambda a: a.reshape(b * t, a.shape[-1])

    att, ks, vs = _decode_attn(x_sample.reshape(nb * nt, d),
                               cache_k_win[0].reshape(nb, WINDOW, KV_W),
                               cache_v_win[0].reshape(nb, WINDOW, KV_W), w)
    to_tm = lambda a: jnp.swapaxes(a, 0, 1).reshape(-1, a.shape[-1])
    xs_tm = to_tm(x_sample)
    hs_rows, pres, cs, hs = _decode_mixer(xs_tm, to_tm(state_conv[0]), state_h[0], w)

    yp, ys = _channel((flat(x_prompt), flat(hp_rows), flat(attp), flat(prep), flat(p_prompt[0])),
                      (xs_tm, hs_rows, to_tm(att.reshape(nb, nt, Q_W)), pres, to_tm(p_sample[0])), w)
    yp = yp.reshape(b, t, d)
    from_tm = lambda a, n: jnp.swapaxes(a.reshape(n, nb, a.shape[-1]), 0, 1)

    kv_shape = (1, -1, WINDOW, KV_HEADS, HEAD_DIM)
    return (yp, from_tm(ys, nt),
            kp.reshape(kv_shape), vp.reshape(kv_shape), cp[None], hp.reshape(1, b, d),
            ks.reshape(kv_shape), vs.reshape(kv_shape), from_tm(cs, CONV_W - 1)[None], hs[None])
```

```python
import functools
import math

import numpy as np
import jax
import jax.numpy as jnp
from jax import lax
from jax.experimental import pallas as pl
from jax.experimental.pallas import tpu as pltpu

F32 = jnp.float32
BF16 = jnp.bfloat16

N_HEADS = 8
KV_HEADS = 2
GROUP = N_HEADS // KV_HEADS
HEAD_DIM = 64
Q_W = N_HEADS * HEAD_DIM
KV_W = KV_HEADS * HEAD_DIM
REST0 = Q_W + 2 * KV_W
WINDOW = 128
REL_BUCKETS = 32
REL_MAX_DIST = 128
RNN_BS = 64
CONV_W = 4
RG_C = 8.0
EPS = 1e-6
NEG_INF = -1e30
EXPM1_SERIES_BELOW = 2.0 ** -11
SQRT_FLOOR = 1e-30

LANES = 128
SUBLANES = 8
MXU_DIM = 256
VMEM_LIMIT = 56 * 1024 * 1024

TM_MIX = 512
TM_MLP = 512
FF_CHUNK = 1024
SEQ_BLOCK = 16
PACE_LAG = 2
SEG_PAD = 4


def _dot(a, b):
    return jnp.dot(a, b, preferred_element_type=F32)


def _dot_nt(a, b):
    return lax.dot_general(a, b, (((1,), (1,)), ((), ())), preferred_element_type=F32)


def _rms(x, g):
    ms = jnp.mean(x * x, axis=-1, keepdims=True)
    return x * lax.rsqrt(ms + EPS) * g


def _head_rms(x, ones_ref, g):
    ms = _dot((x * x).astype(BF16), ones_ref[...])
    return x * lax.rsqrt(ms + EPS) * g


def _lower_half(shape):
    return lax.broadcasted_iota(jnp.int32, shape, len(shape) - 1) < HEAD_DIM


def _neg_expm1_2x(x, exp_x):
    return jnp.where(x > -EXPM1_SERIES_BELOW, (-2.0 * x) * (1.0 + x), 1.0 - exp_x * exp_x)


def _log_decay(lam):
    return -RG_C * jax.nn.softplus(-lam)


def _gate_math(ya, yx, xc, ba, bx, log_decay):
    r = jax.nn.sigmoid(ya + ba)
    i = jax.nn.sigmoid(yx + bx)
    log_a = log_decay * r
    a = jnp.exp(log_a)
    y = _neg_expm1_2x(log_a, a)
    bterm = (y * lax.rsqrt(jnp.maximum(y, SQRT_FLOOR))) * (i * xc)
    return a, bterm


def _rglru_gates(xc, bd_ref, ba, bx, lam):
    xcb = xc.astype(BF16)
    ya, yx = [], []
    for gi in range(xc.shape[1] // MXU_DIM):
        y = _dot(xcb[:, gi * MXU_DIM:(gi + 1) * MXU_DIM], bd_ref[gi])
        ya.append(y[:, :MXU_DIM])
        yx.append(y[:, MXU_DIM:])
    return _gate_math(jnp.concatenate(ya, axis=1), jnp.concatenate(yx, axis=1), xc, ba, bx,
                      _log_decay(lam))


def _prompt_mixer_kernel(x_ref, n1g_ref, wqkv_ref, win_ref, gq_ref, gk_ref, oq_ref, ok_ref,
                         bias_rows_ref, sink_ref, cw8_ref, cb8_ref, bd_ref, ba8_ref, bx8_ref, lam8_ref,
                         wor32_ref, wout32_ref, wup32_ref, wdn32_ref, woa32_ref,
                         h_out, att, pre_out, kwin_ref, vwin_ref, cst_ref, hst_ref,
                         wor16_ref, wout16_ref, wup16_ref, wdn16_ref, woa16_ref,
                         bias_ref, proj, projx, kbuf, vbuf, xp, hc, logit_s, e_s, xc_s, y_s, hl_s, ac_s, hn_s,
                         *, blocks_per_seq):
    tm, d = x_ref.shape
    n = pl.program_id(0)
    for src, dst in ((wor32_ref, wor16_ref), (wout32_ref, wout16_ref), (wup32_ref, wup16_ref),
                     (wdn32_ref, wdn16_ref), (woa32_ref, woa16_ref)):
        dst[...] = src[...].astype(BF16)

    @pl.when(n == 0)
    def _():
        proj[1] = jnp.zeros(proj.shape[1:], F32)
        projx[1] = jnp.zeros(projx.shape[1:], F32)
        kbuf[...] = jnp.zeros(kbuf.shape, BF16)
        vbuf[...] = jnp.zeros(vbuf.shape, BF16)
        xp[...] = jnp.zeros(xp.shape, F32)
        hc[...] = jnp.zeros(hc.shape, F32)
        has_prev = lax.broadcasted_iota(jnp.int32, (2 * WINDOW, WINDOW), 0) >= WINDOW
        for h in range(N_HEADS):
            row = jnp.broadcast_to(bias_rows_ref[h:h + 1, :], (2 * WINDOW, bias_rows_ref.shape[1]))
            table = pltpu.roll(row, 0, 1, stride=1, stride_axis=0)[:, :WINDOW]
            at = (h // GROUP, slice(None), slice((h % GROUP) * WINDOW, (h % GROUP + 1) * WINDOW))
            bias_ref[(0,) + at] = table
            bias_ref[(1,) + at] = jnp.where(has_prev, table, NEG_INF)

    wslot = lax.rem(n, 2)
    rslot = 1 - wslot
    fresh = lax.rem(n + blocks_per_seq - 1, blocks_per_seq) == 0
    seg = tm // SUBLANES
    pitch = seg + SEG_PAD

    xn = _rms(x_ref[...], n1g_ref[...]).astype(BF16)
    n_qkv = wqkv_ref.shape[1]
    tokens = []

    def project(lo):
        res = _dot(xn, (wqkv_ref if lo < n_qkv else win_ref)[:, lo:lo + MXU_DIM])
        tokens.append(res[0:1, :])
        if lo < n_qkv:
            for i in range(MXU_DIM // LANES):
                proj[wslot, lo // LANES + i] = res[:, i * LANES:(i + 1) * LANES]
        elif lo < REST0 + d:
            for i in range(MXU_DIM // LANES):
                for sgm in range(SUBLANES):
                    projx[wslot, (lo - REST0) // LANES + i, sgm * pitch:sgm * pitch + seg, :] = (
                        res[sgm * seg:(sgm + 1) * seg, i * LANES:(i + 1) * LANES])
        else:
            pre_out[:, lo - REST0 - d:lo - REST0 - d + MXU_DIM] = res.astype(BF16)

    pending = list(range(0, win_ref.shape[1], MXU_DIM))

    def pace():
        if len(tokens) < PACE_LAG:
            return jnp.zeros((1, MXU_DIM), F32)
        bits = lax.bitcast_convert_type(tokens[-PACE_LAG], jnp.uint32)
        return lax.bitcast_convert_type((bits >> 16) >> 16, F32)

    def emit_stage1(count):
        for _ in range(min(count, len(pending))):
            project(pending.pop(0))

    def kept(lo, hi):
        return jnp.concatenate([proj[rslot, t] for t in range(lo // LANES, hi // LANES)], axis=1)

    emit_stage1(2)
    qn = _head_rms(kept(0, Q_W), oq_ref, gq_ref[...]).astype(BF16)
    kn = _head_rms(kept(Q_W, Q_W + KV_W), ok_ref, gk_ref[...])
    v = kept(Q_W + KV_W, n_qkv)
    kbuf[0:WINDOW, :] = jnp.where(fresh, jnp.zeros((WINDOW, KV_W), BF16), kbuf[tm:tm + WINDOW, :])
    vbuf[:, 0:WINDOW] = jnp.where(fresh, jnp.zeros((KV_W, WINDOW), BF16), vbuf[:, tm:tm + WINDOW])
    kbuf[WINDOW:WINDOW + tm, :] = kn.astype(BF16)
    vbuf[:, WINDOW:WINDOW + tm] = v.T.astype(BF16)
    kwin_ref[...] = kn[tm - WINDOW:, :]
    vwin_ref[...] = v[tm - WINDOW:, :]

    first = jnp.where(fresh, 1, 0)
    lower = _lower_half((WINDOW, LANES))
    zero = jnp.zeros((WINDOW, LANES), BF16)

    def attend(s):
        rows = slice(s * WINDOW, (s + 1) * WINDOW)
        keys = kbuf[s * WINDOW:(s + 2) * WINDOW, :]
        vals_t = vbuf[:, s * WINDOW:(s + 2) * WINDOW]
        outs = []
        for g in range(KV_HEADS):
            qs = []
            for j in range(GROUP):
                slab = qn[rows, j * LANES:(j + 1) * LANES]
                qs.append(jnp.where(lower, slab, zero) if g == 0 else jnp.where(lower, zero, slab))
            logit_s[g] = _dot_nt(keys, jnp.concatenate(qs, axis=0))
            yield
            inv = []
            for j in range(GROUP):
                blk = slice(j * WINDOW, (j + 1) * WINDOW)
                bias = bias_ref[first, g, :, blk] if s == 0 else bias_ref[0, g, :, blk]
                logit = logit_s[g, :, blk] + bias
                sink = sink_ref[g, :, blk] + pace()[:, :WINDOW]
                m = jnp.maximum(jnp.max(logit, axis=0, keepdims=True), sink)
                e = jnp.exp(logit - m)
                inv.append(1.0 / (jnp.sum(e, axis=0, keepdims=True) + jnp.exp(sink - m)))
                e_s[g, :, blk] = e.astype(BF16)
                yield
            outs.append(_dot(vals_t, e_s[g]) * jnp.concatenate(inv, axis=1))
        for j in range(GROUP):
            blk = slice(j * WINDOW, (j + 1) * WINDOW)
            both = jnp.concatenate([outs[0][:HEAD_DIM, blk], outs[1][HEAD_DIM:, blk]], axis=0)
            att[rows, j * LANES:(j + 1) * LANES] = both.T.astype(BF16)
        yield

    sub8 = lax.broadcasted_iota(jnp.int32, (SUBLANES, MXU_DIM), 0)

    def recur(gi):
        cols = slice(gi * MXU_DIM, (gi + 1) * MXU_DIM)
        tiles = range(gi * (MXU_DIM // LANES), (gi + 1) * (MXU_DIM // LANES))
        by_segment = lambda k: pl.ds(k, SUBLANES, stride=pitch)
        x_at = lambda k: jnp.concatenate([projx[rslot, t, by_segment(k), :] for t in tiles], axis=1)
        last = SUBLANES * pitch - SEG_PAD
        x_tail = lambda n_rows: jnp.concatenate(
            [projx[rslot, t, last - n_rows:last, :] for t in tiles], axis=1)
        prev = jnp.where(fresh, 0.0, xp[:, cols])
        window = [jnp.where(sub8 == 0, prev[SUBLANES - j:SUBLANES - j + 1, :],
                            pltpu.roll(x_at(seg - j), 1, axis=0)) for j in range(CONV_W - 1, 0, -1)]
        for k in range(seg):
            window.append(x_at(k))
            xc_s[gi, k * SUBLANES:(k + 1) * SUBLANES, :] = cb8_ref[:, cols] + sum(
                cw8_ref[j, :, cols] * window[j] for j in range(CONV_W))
            window.pop(0)
            if k % 16 == 15:
                yield
        xp[:, cols] = x_tail(SUBLANES)
        cst_ref[:, cols] = x_tail(CONV_W - 1)
        y_s[gi] = _dot(xc_s[gi].astype(BF16), bd_ref[gi])
        yield
        ba, bx = ba8_ref[:, cols], bx8_ref[:, cols]
        log_decay = _log_decay(lam8_ref[:, cols])
        h_loc = a_cum = None
        for k in range(seg):
            r = slice(k * SUBLANES, (k + 1) * SUBLANES)
            a, bterm = _gate_math(y_s[gi, r, :MXU_DIM], y_s[gi, r, MXU_DIM:], xc_s[gi, r, :] + pace(),
                                  ba, bx, log_decay)
            h_loc = bterm if k == 0 else a * h_loc + bterm
            a_cum = a if k == 0 else a * a_cum
            hl_s[gi, r, :] = h_loc
            ac_s[gi, r, :] = a_cum
            if k % 4 == 3:
                yield
        h_in = jnp.where(fresh, 0.0, hc[:, cols])
        carry_in = []
        for sgm in range(SUBLANES):
            carry_in.append(h_in)
            h_in = h_loc[sgm:sgm + 1, :] + a_cum[sgm:sgm + 1, :] * h_in
        hc[:, cols] = h_in
        hst_ref[:, cols] = h_in
        carry_in = jnp.concatenate(carry_in, axis=0)
        for k in range(seg):
            r = slice(k * SUBLANES, (k + 1) * SUBLANES)
            h = hl_s[gi, r, :] + ac_s[gi, r, :] * carry_in
            for i in range(MXU_DIM // LANES):
                hn_s[gi, i, by_segment(k), :] = h[:, i * LANES:(i + 1) * LANES]
            if k % 8 == 7:
                yield
        h_out[:, cols] = jnp.concatenate(
            [jnp.concatenate([hn_s[gi, i, sgm * pitch:sgm * pitch + seg, :] for sgm in range(SUBLANES)], axis=0)
             for i in range(MXU_DIM // LANES)], axis=1).astype(BF16)
        yield

    n_sub = tm // WINDOW
    n_grp = d // MXU_DIM
    pieces = []
    for u in range(max(n_sub, n_grp)):
        if u < n_sub:
            pieces.append(attend(u))
        if u < n_grp:
            pieces.append(recur(u))
    n_spots = n_sub * (KV_HEADS * (1 + GROUP) + 1) + n_grp * (2 + seg // 16 + seg // 4 + seg // 8)
    per_spot = len(pending) / n_spots
    due = 0.0
    for piece in pieces:
        for _ in piece:
            due += per_spot
            emit_stage1(int(due))
            due -= int(due)
    emit_stage1(len(pending))


def _const_spec(shape):
    nd = len(shape)
    return pl.BlockSpec(shape, lambda *_: (0,) * nd, pipeline_mode=pl.Buffered(1))


def _prompt_mixer(x, w):
    b, t, d = x.shape
    tm = TM_MIX
    nt = t // tm
    n_blocks = b * nt
    padded_rows = SUBLANES * (tm // SUBLANES + SEG_PAD)
    consts = [w['n1g'], w['wqkv'], w['win'], w['gq'], w['gk'], w['oq'], w['ok'],
              w['bias_p'], w['sink_p'], w['cw8'], w['cb8'], w['bd'], w['ba8'], w['bx8'], w['lam8']]
    to_round = [w['wor32'], w['wout32'], w['wup32'], w['wdn32'], w['woa32']]
    bf16_rows = 2 * SUBLANES
    n_chunks = lambda a: min(n_blocks, a.shape[0] // bf16_rows)

    def chunks(a, src_block=lambda c: c):
        assert a.shape[0] % (n_chunks(a) * bf16_rows) == 0
        block = (a.shape[0] // n_chunks(a), a.shape[1])
        at = lambda n: jnp.minimum(n, n_chunks(a) - 1)
        return (pl.BlockSpec(block, lambda n: (src_block(at(n)), 0)),
                pl.BlockSpec(block, lambda n: (at(n), 0)))

    per_head = n_chunks(w['woa32']) // N_HEADS
    assert per_head >= 1 and per_head * N_HEADS == n_chunks(w['woa32'])

    def woa_source(c):
        slot = c // per_head
        return ((slot % KV_HEADS) * GROUP + slot // KV_HEADS) * per_head + c % per_head

    round_specs = [chunks(a, woa_source if a is w['woa32'] else (lambda c: c)) for a in to_round]
    out_shape = (
        jax.ShapeDtypeStruct((b, t, d), BF16),
        jax.ShapeDtypeStruct((b, t, Q_W), BF16),
        jax.ShapeDtypeStruct((b, t, 3 * d), BF16),
        jax.ShapeDtypeStruct((b, WINDOW, KV_W), F32),
        jax.ShapeDtypeStruct((b, WINDOW, KV_W), F32),
        jax.ShapeDtypeStruct((b, CONV_W - 1, d), F32),
        jax.ShapeDtypeStruct((b, 1, d), F32),
    ) + tuple(jax.ShapeDtypeStruct(a.shape, BF16) for a in to_round)

    def cur(n):
        i = jnp.minimum(n, n_blocks - 1)
        return (i // nt, i % nt, 0)

    def prv(n):
        i = jnp.maximum(n - 1, 0)
        return (i // nt, i % nt, 0)

    per_seq = lambda n: (jnp.maximum(n - 1, 0) // nt, 0, 0)
    return pl.pallas_call(
        functools.partial(_prompt_mixer_kernel, blocks_per_seq=nt),
        out_shape=out_shape,
        grid=(n_blocks + 1,),
        in_specs=[pl.BlockSpec((None, tm, d), cur)]
                 + [_const_spec(c.shape) for c in consts]
                 + [src for src, _ in round_specs],
        out_specs=(
            pl.BlockSpec((None, tm, d), prv),
            pl.BlockSpec((None, tm, Q_W), prv),
            pl.BlockSpec((None, tm, 3 * d), cur),
            pl.BlockSpec((None, WINDOW, KV_W), per_seq),
            pl.BlockSpec((None, WINDOW, KV_W), per_seq),
            pl.BlockSpec((None, CONV_W - 1, d), per_seq),
            pl.BlockSpec((None, 1, d), per_seq),
        ) + tuple(dst for _, dst in round_specs),
        scratch_shapes=[
            pltpu.VMEM((2, KV_HEADS, 2 * WINDOW, GROUP * WINDOW), F32),
            pltpu.VMEM((2, REST0 // LANES, tm, LANES), F32),
            pltpu.VMEM((2, d // LANES, padded_rows, LANES), F32),
            pltpu.VMEM((WINDOW + tm, KV_W), BF16),
            pltpu.VMEM((KV_W, WINDOW + tm), BF16),
            pltpu.VMEM((SUBLANES, d), F32),
            pltpu.VMEM((1, d), F32),
            pltpu.VMEM((KV_HEADS, 2 * WINDOW, GROUP * WINDOW), F32),
            pltpu.VMEM((KV_HEADS, 2 * WINDOW, GROUP * WINDOW), BF16),
            pltpu.VMEM((d // MXU_DIM, tm, MXU_DIM), F32),
            pltpu.VMEM((d // MXU_DIM, tm, 2 * MXU_DIM), F32),
            pltpu.VMEM((d // MXU_DIM, tm, MXU_DIM), F32),
            pltpu.VMEM((d // MXU_DIM, tm, MXU_DIM), F32),
            pltpu.VMEM((d // MXU_DIM, MXU_DIM // LANES, padded_rows, LANES), F32),
        ],
        compiler_params=pltpu.CompilerParams(
            dimension_semantics=("arbitrary",), vmem_limit_bytes=VMEM_LIMIT),
        name="prompt_mixer",
    )(x, *consts, *to_round)


def _channel_rows(x_ref, h_ref, att_ref, pre_ref, p_ref, woa_ref, wor_ref, wout_ref,
                  n2g_ref, wup_ref, wdn_ref, pg_ref, wpg_ref, wple_ref, o_ref):
    d = x_ref.shape[1]
    pre = pre_ref[...].astype(F32)
    hg = (h_ref[...].astype(F32) * jax.nn.gelu(pre[:, :d])).astype(BF16)
    rnn = _dot(hg, wor_ref[...])
    atto = _dot(att_ref[...], woa_ref[...])
    mix = (jax.nn.sigmoid(pre[:, d:2 * d]) * atto + jax.nn.sigmoid(pre[:, 2 * d:]) * rnn).astype(BF16)
    x = x_ref[...] + _dot(mix, wout_ref[...])
    xn = _rms(x, n2g_ref[...]).astype(BF16)
    acc = x
    for c in range(wup_ref.shape[1] // FF_CHUNK):
        cols = slice(c * FF_CHUNK, (c + 1) * FF_CHUNK)
        hmid = jnp.maximum(_dot(xn, wup_ref[:, cols]), 0.0)
        acc = acc + _dot((hmid * hmid).astype(BF16), wdn_ref[cols, :])
    gate = jax.nn.sigmoid(_dot(_rms(acc, pg_ref[...]).astype(BF16), wpg_ref[...]))
    o_ref[...] = acc + gate * _dot(p_ref[...].astype(BF16), wple_ref[...])


N_ROW_INPUTS = 5


def _channel_kernel(*refs, first_steps):
    first, second = refs[:N_ROW_INPUTS], refs[N_ROW_INPUTS:2 * N_ROW_INPUTS]
    consts, (o_first, o_second) = refs[2 * N_ROW_INPUTS:-2], refs[-2:]
    i = pl.program_id(0)

    @pl.when(i < first_steps)
    def _():
        _channel_rows(*first, *consts, o_first)

    @pl.when(i >= first_steps)
    def _():
        _channel_rows(*second, *consts, o_second)


def _channel(first, second, w):
    m1, d = first[0].shape
    m2 = second[0].shape[0]
    tm = min(TM_MLP, m1, m2)
    n1, n2 = m1 // tm, m2 // tm
    consts = [w['woa'], w['wor'], w['wout'], w['n2g'], w['wup'], w['wdn'], w['pg'], w['wpg'], w['wple']]
    rows1 = lambda a: pl.BlockSpec((tm, a.shape[1]), lambda i: (jnp.minimum(i, n1 - 1), 0))
    rows2 = lambda a: pl.BlockSpec((tm, a.shape[1]), lambda i: (jnp.maximum(i - n1, 0), 0),
                                   pipeline_mode=pl.Buffered(1))
    return pl.pallas_call(
        functools.partial(_channel_kernel, first_steps=n1),
        out_shape=(jax.ShapeDtypeStruct((m1, d), F32), jax.ShapeDtypeStruct((m2, d), F32)),
        grid=(n1 + n2,),
        in_specs=[rows1(a) for a in first] + [rows2(a) for a in second]
                 + [_const_spec(c.shape) for c in consts],
        out_specs=(rows1(first[0]), rows2(second[0])),
        compiler_params=pltpu.CompilerParams(
            dimension_semantics=("arbitrary",), vmem_limit_bytes=VMEM_LIMIT),
        name="merge_mlp_ple",
    )(*first, *second, *consts)


def _decode_attn_kernel(x_ref, ck_ref, cv_ref, n1g_ref, wqkv_ref, gq_ref, gk_ref, oq_ref, ok_ref,
                        bias_c_ref, bias_n_ref, sink_ref,
                        att_ref, kwin_ref, vwin_ref, qkv_s):
    sb = ck_ref.shape[0]
    nt = att_ref.shape[0] // sb

    @pl.when(pl.program_id(0) == 0)
    def _():
        qkv_s[...] = _dot(_rms(x_ref[...], n1g_ref[...]).astype(BF16), wqkv_ref[...])

    qkv = qkv_s[pl.ds(pl.multiple_of(pl.program_id(0) * (sb * nt), sb * nt), sb * nt), :]
    qn = _head_rms(qkv[:, :Q_W], oq_ref, gq_ref[...])
    kn = _head_rms(qkv[:, Q_W:Q_W + KV_W], ok_ref, gk_ref[...])
    v = qkv[:, Q_W + KV_W:]

    n_rows = sb * nt
    ck = ck_ref[...]
    cv = cv_ref[...]
    for b in range(sb):
        rows = slice(b * nt, (b + 1) * nt)
        kwin_ref[b, 0:WINDOW - nt, :] = ck[b, nt:, :]
        vwin_ref[b, 0:WINDOW - nt, :] = cv[b, nt:, :]
        kwin_ref[b, WINDOW - nt:WINDOW, :] = kn[rows]
        vwin_ref[b, WINDOW - nt:WINDOW, :] = v[rows]

    qb = qn.astype(BF16)
    lower = _lower_half((n_rows, LANES))
    zero = jnp.zeros((n_rows, LANES), BF16)
    q_all = jnp.concatenate(
        [jnp.where(lower, qb[:, j * LANES:(j + 1) * LANES], zero) if g == 0
         else jnp.where(lower, zero, qb[:, j * LANES:(j + 1) * LANES])
         for g in range(KV_HEADS) for j in range(GROUP)], axis=0)
    n_col = q_all.shape[0]
    col_seq = (lax.broadcasted_iota(jnp.int32, (1, n_col), 1) // nt) % sb

    def own(big):
        out = big[0:WINDOW]
        for b in range(1, sb):
            out = jnp.where(col_seq == b, big[b * WINDOW:(b + 1) * WINDOW], out)
        return out

    st = own(_dot_nt(ck.reshape(sb * WINDOW, KV_W).astype(BF16), q_all)) + bias_c_ref[...]
    stx = _dot_nt(kn.astype(BF16), q_all) + bias_n_ref[...]
    sink = sink_ref[...]
    m = jnp.maximum(jnp.maximum(jnp.max(st, axis=0, keepdims=True),
                                jnp.max(stx, axis=0, keepdims=True)), sink)
    e = jnp.exp(st - m)
    ex = jnp.exp(stx - m)
    den = (jnp.sum(e, axis=0, keepdims=True) + jnp.sum(ex, axis=0, keepdims=True)
           + jnp.exp(sink - m))
    cv_t = jnp.concatenate([cv[b].T for b in range(sb)], axis=0).astype(BF16)
    out = own(_dot(cv_t, e.astype(BF16)))
    pad = WINDOW - n_rows
    v_t = jnp.concatenate([v, jnp.zeros((pad, KV_W), F32)], axis=0).T.astype(BF16)
    ex_pad = jnp.concatenate([ex, jnp.zeros((pad, n_col), F32)], axis=0).astype(BF16)
    out = (out + _dot(v_t, ex_pad)) / den
    half = n_col // KV_HEADS
    both_t = jnp.concatenate([out[:HEAD_DIM, :half], out[HEAD_DIM:, half:]], axis=0).T
    for j in range(GROUP):
        att_ref[:, j * LANES:(j + 1) * LANES] = both_t[j * n_rows:(j + 1) * n_rows].astype(att_ref.dtype)


def _decode_attn(x, ck, cv, w):
    m, d = x.shape
    nb = ck.shape[0]
    nt = m // nb
    sb = SEQ_BLOCK
    consts = [w['n1g'], w['wqkv'], w['gq'], w['gk'], w['oq'], w['ok'],
              w['bias_sc'], w['bias_sn'], w['sink_s']]
    cache_spec = pl.BlockSpec((sb, WINDOW, KV_W), lambda i: (i, 0, 0))
    return pl.pallas_call(
        _decode_attn_kernel,
        out_shape=(jax.ShapeDtypeStruct((m, Q_W), BF16),
                   jax.ShapeDtypeStruct(ck.shape, F32),
                   jax.ShapeDtypeStruct(cv.shape, F32)),
        grid=(nb // sb,),
        in_specs=[_const_spec(x.shape), cache_spec, cache_spec]
                 + [_const_spec(c.shape) for c in consts],
        out_specs=(pl.BlockSpec((sb * nt, Q_W), lambda i: (i, 0)), cache_spec, cache_spec),
        scratch_shapes=[pltpu.VMEM((m, w['wqkv'].shape[1]), F32)],
        compiler_params=pltpu.CompilerParams(
            dimension_semantics=("arbitrary",), vmem_limit_bytes=VMEM_LIMIT),
        name="decode_attn",
    )(x, ck, cv, *consts)


def _decode_mixer_kernel(x_ref, cst_ref, h0_ref, n1g_ref, win_ref, cw_ref, cb_ref, bd_ref,
                         ba_ref, bx_ref, lam_ref,
                         h_out, pre_out, cnew_ref, hnew_ref):
    nb = h0_ref.shape[0]
    nt = x_ref.shape[0] // nb
    d = x_ref.shape[1]
    xn = _rms(x_ref[...], n1g_ref[...]).astype(BF16)
    xr = _dot(xn, win_ref[:, REST0:REST0 + d])
    prev = cst_ref[...]
    slabs = [prev[k * nb:(k + 1) * nb] for k in range(CONV_W - 1)]
    slabs += [xr[k * nb:(k + 1) * nb] for k in range(nt)]
    xc = jnp.concatenate(
        [cb_ref[...] + sum(cw_ref[j:j + 1, :] * slabs[ti + j] for j in range(CONV_W))
         for ti in range(nt)], axis=0)
    cnew_ref[...] = jnp.concatenate(slabs[-(CONV_W - 1):], axis=0)

    a, bterm = _rglru_gates(xc, bd_ref, ba_ref[...], bx_ref[...], lam_ref[...])
    h = h0_ref[...]
    hs = []
    for ti in range(nt):
        h = a[ti * nb:(ti + 1) * nb] * h + bterm[ti * nb:(ti + 1) * nb]
        hs.append(h)
    hnew_ref[...] = h
    h_out[...] = jnp.concatenate(hs, axis=0).astype(BF16)
    pre_out[...] = _dot(xn, win_ref[:, REST0 + d:REST0 + 4 * d]).astype(BF16)


def _decode_mixer(x, cst, h0, w):
    m, d = x.shape
    consts = [w['n1g'], w['win'], w['cw'], w['cb'], w['bd'], w['ba'], w['bx'], w['lam']]
    ins = [x, cst, h0] + consts
    whole = lambda shape: pl.BlockSpec(shape, lambda i: (0,) * len(shape))
    return pl.pallas_call(
        _decode_mixer_kernel,
        out_shape=(jax.ShapeDtypeStruct((m, d), BF16),
                   jax.ShapeDtypeStruct((m, 3 * d), BF16),
                   jax.ShapeDtypeStruct(cst.shape, F32),
                   jax.ShapeDtypeStruct(h0.shape, F32)),
        grid=(1,),
        in_specs=[_const_spec(c.shape) for c in ins],
        out_specs=(whole((m, d)), whole((m, 3 * d)), whole(cst.shape), whole(h0.shape)),
        compiler_params=pltpu.CompilerParams(
            dimension_semantics=("arbitrary",), vmem_limit_bytes=VMEM_LIMIT),
        name="decode_mixer",
    )(*ins)


def _rel_bucket(dist):
    n = np.maximum(dist, 0)
    max_exact = REL_BUCKETS // 2
    nf = np.maximum(n, 1).astype(np.float32)
    large = max_exact + (np.log(nf / max_exact) / math.log(REL_MAX_DIST / max_exact)
                         * (REL_BUCKETS - max_exact)).astype(np.int32)
    large = np.minimum(large, REL_BUCKETS - 1)
    return np.where(n < max_exact, n, large)


def _bias_rows(rel_bias, dist, mask):
    tb = jnp.where(mask[:, :, None], rel_bias[_rel_bucket(dist)].astype(F32), NEG_INF)
    tq, tk = dist.shape
    return jnp.transpose(tb, (2, 0, 1)).reshape(N_HEADS * tq, tk)


def _prompt_bias_rows(rel_bias):
    span = 3 * WINDOW
    k = np.arange(span)
    dist = WINDOW + np.where(k < WINDOW, k, k - span)
    valid = (dist >= 0) & (dist <= WINDOW)
    return jnp.where(valid[:, None], rel_bias[_rel_bucket(dist)].astype(F32), NEG_INF).T


def _gate_blocks(rg_wa, rg_wx, n_grp):
    per = rg_wa.shape[0] // n_grp
    both = jnp.stack([rg_wa, rg_wx]).reshape(2, n_grp, per, RNN_BS, RNN_BS)
    on_diag = np.eye(per, dtype=bool)[None, None, :, None, :, None]
    blocks = jnp.where(on_diag, both[:, :, :, :, None, :], 0.0)
    blocks = jnp.transpose(blocks, (1, 2, 3, 0, 4, 5))
    return blocks.reshape(n_grp, per * RNN_BS, 2 * per * RNN_BS).astype(BF16)


def _head_avg(width):
    idx = np.arange(width) // HEAD_DIM
    return jnp.asarray((idx[:, None] == idx[None, :]).astype(np.float32) / HEAD_DIM, BF16)


def _prepare(rel_bias, norm1_g, w_in, q_norm_g, k_norm_g, sinks, w_o_attn, conv_w, conv_b,
             rg_wa, rg_ba, rg_wx, rg_bx, rg_lambda, w_o_rnn, w_out, norm2_g, w_up, w_down,
             ple_norm_g, w_ple_gate, w_ple, n_dec):
    d = w_in.shape[0]
    order = [g * GROUP + j for j in range(GROUP) for g in range(KV_HEADS)]
    perm = np.concatenate([np.arange(h * HEAD_DIM, (h + 1) * HEAD_DIM) for h in order])
    row = lambda a: a.reshape(1, -1).astype(F32)
    w = {}
    w['n1g'] = row(norm1_g)
    w['win'] = w_in.astype(BF16)
    w['wqkv'] = jnp.concatenate([w['win'][:, :Q_W][:, perm], w['win'][:, Q_W:REST0]], axis=1)
    w['gq'] = row(jnp.tile(q_norm_g, N_HEADS)) * (HEAD_DIM ** -0.5)
    w['gk'] = row(jnp.tile(k_norm_g, KV_HEADS))
    w['oq'] = _head_avg(Q_W)
    w['ok'] = _head_avg(KV_W)
    w['cw'] = conv_w.astype(F32)
    w['cb'] = row(conv_b)
    w['bd'] = _gate_blocks(rg_wa, rg_wx, d // MXU_DIM)
    w['ba'] = row(rg_ba)
    w['bx'] = row(rg_bx)
    w['lam'] = row(rg_lambda)
    rows8 = lambda a: jnp.broadcast_to(a, (SUBLANES, a.shape[-1]))
    w['cw8'] = jnp.broadcast_to(w['cw'][:, None, :], (CONV_W, SUBLANES, d))
    w['cb8'], w['ba8'], w['bx8'], w['lam8'] = (rows8(w[k]) for k in ('cb', 'ba', 'bx', 'lam'))
    w['wor32'], w['wout32'], w['wup32'], w['wdn32'] = w_o_rnn, w_out, w_up, w_down
    w['woa32'] = w_o_attn
    w['n2g'] = row(norm2_g)
    w['pg'] = row(ple_norm_g)
    w['wpg'] = w_ple_gate.astype(BF16)
    w['wple'] = w_ple.astype(BF16)

    w['bias_p'] = _prompt_bias_rows(rel_bias)
    sink_rows = sinks.astype(F32).reshape(KV_HEADS, GROUP, 1)
    w['sink_p'] = jnp.repeat(sink_rows, WINDOW, axis=2).reshape(KV_HEADS, 1, GROUP * WINDOW)

    sb = SEQ_BLOCK
    ti = np.arange(n_dec)[:, None]
    dist_c = ti + WINDOW - np.arange(WINDOW)[None, :]
    dist_n = ti - np.arange(n_dec)[None, :]
    n_col = N_HEADS * sb * n_dec
    col = np.arange(n_col)
    col_head_t = col // (sb * n_dec) * n_dec + col % n_dec
    spread = (np.arange(N_HEADS * n_dec)[:, None] == col_head_t[None, :]).astype(np.float32)
    expand = lambda a: jnp.dot(a.T, spread, precision=lax.Precision.HIGHEST)
    w['bias_sc'] = expand(_bias_rows(rel_bias, dist_c, dist_c <= WINDOW))
    bias_n = expand(_bias_rows(rel_bias, dist_n, dist_n >= 0))
    own = (np.arange(sb * n_dec) // n_dec)[:, None] == (col // n_dec % sb)[None, :]
    w['bias_sn'] = jnp.where(own, jnp.tile(bias_n, (sb, 1)), NEG_INF)
    w['sink_s'] = jnp.repeat(sinks.astype(F32), sb * n_dec)[None, :]
    return w


def kernel(x_prompt, x_sample, cache_k_win, cache_v_win, state_conv, state_h, p_prompt, p_sample, rel_bias, norm1_g, w_in, q_norm_g, k_norm_g, sinks, w_o_attn, conv_w, conv_b, rg_wa, rg_ba, rg_wx, rg_bx, rg_lambda, w_o_rnn, w_out, norm2_g, w_up, w_down, ple_norm_g, w_ple_gate, w_ple):
    depth = w_in.shape[0]
    assert depth == 1, "single-layer step"
    b, t, d = x_prompt.shape
    nb, nt, _ = x_sample.shape
    w = _prepare(rel_bias, norm1_g[0], w_in[0], q_norm_g[0], k_norm_g[0], sinks[0], w_o_attn[0],
                 conv_w[0], conv_b[0], rg_wa[0], rg_ba[0], rg_wx[0], rg_bx[0], rg_lambda[0],
                 w_o_rnn[0], w_out[0], norm2_g[0], w_up[0], w_down[0], ple_norm_g[0],
                 w_ple_gate[0], w_ple[0], nt)

    (hp_rows, attp, prep, kp, vp, cp, hp,
     w['wor'], w['wout'], w['wup'], w['wdn'], w['woa']) = _prompt_mixer(x_prompt, w)
    flat = lambda a: a.reshape(b * t, a.shape[-1])

    att, ks, vs = _decode_attn(x_sample.reshape(nb * nt, d),
                               cache_k_win[0].reshape(nb, WINDOW, KV_W),
                               cache_v_win[0].reshape(nb, WINDOW, KV_W), w)
    to_tm = lambda a: jnp.swapaxes(a, 0, 1).reshape(-1, a.shape[-1])
    xs_tm = to_tm(x_sample)
    hs_rows, pres, cs, hs = _decode_mixer(xs_tm, to_tm(state_conv[0]), state_h[0], w)

    yp, ys = _channel((flat(x_prompt), flat(hp_rows), flat(attp), flat(prep), flat(p_prompt[0])),
                      (xs_tm, hs_rows, to_tm(att.reshape(nb, nt, Q_W)), pres, to_tm(p_sample[0])), w)
    yp = yp.reshape(b, t, d)
    from_tm = lambda a, n: jnp.swapaxes(a.reshape(n, nb, a.shape[-1]), 0, 1)

    kv_shape = (1, -1, WINDOW, KV_HEADS, HEAD_DIM)
    return (yp, from_tm(ys, nt),
            kp.reshape(kv_shape), vp.reshape(kv_shape), cp[None], hp.reshape(1, b, d),
            ks.reshape(kv_shape), vs.reshape(kv_shape), from_tm(cs, CONV_W - 1)[None], hs[None])
```

```python
import functools
import math

import numpy as np
import jax
import jax.numpy as jnp
from jax import lax
from jax.experimental import pallas as pl
from jax.experimental.pallas import tpu as pltpu

F32 = jnp.float32
BF16 = jnp.bfloat16

N_HEADS = 8
KV_HEADS = 2
GROUP = N_HEADS // KV_HEADS
HEAD_DIM = 64
Q_W = N_HEADS * HEAD_DIM
KV_W = KV_HEADS * HEAD_DIM
REST0 = Q_W + 2 * KV_W
WINDOW = 128
REL_BUCKETS = 32
REL_MAX_DIST = 128
RNN_BS = 64
CONV_W = 4
RG_C = 8.0
EPS = 1e-6
NEG_INF = -1e30
EXPM1_SERIES_BELOW = 2.0 ** -11
SQRT_FLOOR = 1e-30

LANES = 128
SUBLANES = 8
MXU_DIM = 256
VMEM_LIMIT = 56 * 1024 * 1024

TM_MIX = 512
TM_MLP = 512
FF_CHUNK = 1024
SEQ_BLOCK = 16
PACE_LAG = 2
SEG_PAD = 4


def _dot(a, b):
    return jnp.dot(a, b, preferred_element_type=F32)


def _dot_nt(a, b):
    return lax.dot_general(a, b, (((1,), (1,)), ((), ())), preferred_element_type=F32)


def _rms(x, g):
    ms = jnp.mean(x * x, axis=-1, keepdims=True)
    return x * lax.rsqrt(ms + EPS) * g


def _head_rms(x, ones_ref, g):
    ms = _dot((x * x).astype(BF16), ones_ref[...])
    return x * lax.rsqrt(ms + EPS) * g


def _lower_half(shape):
    return lax.broadcasted_iota(jnp.int32, shape, len(shape) - 1) < HEAD_DIM


def _neg_expm1_2x(x, exp_x):
    return jnp.where(x > -EXPM1_SERIES_BELOW, (-2.0 * x) * (1.0 + x), 1.0 - exp_x * exp_x)


def _log_decay(lam):
    return -RG_C * jax.nn.softplus(-lam)


def _gate_math(ya, yx, xc, ba, bx, log_decay):
    r = jax.nn.sigmoid(ya + ba)
    i = jax.nn.sigmoid(yx + bx)
    log_a = log_decay * r
    a = jnp.exp(log_a)
    y = _neg_expm1_2x(log_a, a)
    bterm = (y * lax.rsqrt(jnp.maximum(y, SQRT_FLOOR))) * (i * xc)
    return a, bterm


def _rglru_gates(xc, bd_ref, ba, bx, lam):
    xcb = xc.astype(BF16)
    ya, yx = [], []
    for gi in range(xc.shape[1] // MXU_DIM):
        y = _dot(xcb[:, gi * MXU_DIM:(gi + 1) * MXU_DIM], bd_ref[gi])
        ya.append(y[:, :MXU_DIM])
        yx.append(y[:, MXU_DIM:])
    return _gate_math(jnp.concatenate(ya, axis=1), jnp.concatenate(yx, axis=1), xc, ba, bx,
                      _log_decay(lam))


def _prompt_mixer_kernel(x_ref, n1g_ref, wqkv_ref, win_ref, gq_ref, gk_ref, oq_ref, ok_ref,
                         bias_rows_ref, sink_ref, cw8_ref, cb8_ref, bd_ref, ba8_ref, bx8_ref, lam8_ref,
                         wor32_ref, wout32_ref, wup32_ref, wdn32_ref, woa32_ref, wpg32_ref, wple32_ref,
                         h_out, att, pre_out, kwin_ref, vwin_ref, cst_ref, hst_ref,
                         wor16_ref, wout16_ref, wup16_ref, wdn16_ref, woa16_ref, wpg16_ref, wple16_ref,
                         bias_ref, proj, projx, kbuf, vbuf, xp, hc, logit_s, e_s, xc_s, y_s, hl_s, ac_s, hn_s,
                         *, blocks_per_seq):
    tm, d = x_ref.shape
    n = pl.program_id(0)
    for src, dst in ((wor32_ref, wor16_ref), (wout32_ref, wout16_ref), (wup32_ref, wup16_ref),
                     (wdn32_ref, wdn16_ref), (woa32_ref, woa16_ref), (wpg32_ref, wpg16_ref),
                     (wple32_ref, wple16_ref)):
        dst[...] = src[...].astype(BF16)

    @pl.when(n == 0)
    def _():
        proj[1] = jnp.zeros(proj.shape[1:], F32)
        projx[1] = jnp.zeros(projx.shape[1:], F32)
        kbuf[...] = jnp.zeros(kbuf.shape, BF16)
        vbuf[...] = jnp.zeros(vbuf.shape, BF16)
        xp[...] = jnp.zeros(xp.shape, F32)
        hc[...] = jnp.zeros(hc.shape, F32)
        has_prev = lax.broadcasted_iota(jnp.int32, (2 * WINDOW, WINDOW), 0) >= WINDOW
        for h in range(N_HEADS):
            row = jnp.broadcast_to(bias_rows_ref[h:h + 1, :], (2 * WINDOW, bias_rows_ref.shape[1]))
            table = pltpu.roll(row, 0, 1, stride=1, stride_axis=0)[:, :WINDOW]
            at = (h // GROUP, slice(None), slice((h % GROUP) * WINDOW, (h % GROUP + 1) * WINDOW))
            bias_ref[(0,) + at] = table
            bias_ref[(1,) + at] = jnp.where(has_prev, table, NEG_INF)

    wslot = lax.rem(n, 2)
    rslot = 1 - wslot
    fresh = lax.rem(n + blocks_per_seq - 1, blocks_per_seq) == 0
    seg = tm // SUBLANES
    pitch = seg + SEG_PAD

    xn = _rms(x_ref[...], n1g_ref[...]).astype(BF16)
    n_qkv = wqkv_ref.shape[1]
    tokens = []

    def project(lo):
        res = _dot(xn, (wqkv_ref if lo < n_qkv else win_ref)[:, lo:lo + MXU_DIM])
        tokens.append(res[0:1, :])
        if lo < n_qkv:
            for i in range(MXU_DIM // LANES):
                proj[wslot, lo // LANES + i] = res[:, i * LANES:(i + 1) * LANES]
        elif lo < REST0 + d:
            for i in range(MXU_DIM // LANES):
                for sgm in range(SUBLANES):
                    projx[wslot, (lo - REST0) // LANES + i, sgm * pitch:sgm * pitch + seg, :] = (
                        res[sgm * seg:(sgm + 1) * seg, i * LANES:(i + 1) * LANES])
        else:
            pre_out[:, lo - REST0 - d:lo - REST0 - d + MXU_DIM] = res.astype(BF16)

    pending = list(range(0, win_ref.shape[1], MXU_DIM))

    def pace():
        if len(tokens) < PACE_LAG:
            return jnp.zeros((1, MXU_DIM), F32)
        bits = lax.bitcast_convert_type(tokens[-PACE_LAG], jnp.uint32)
        return lax.bitcast_convert_type((bits >> 16) >> 16, F32)

    def emit_stage1(count):
        for _ in range(min(count, len(pending))):
            project(pending.pop(0))

    def kept(lo, hi):
        return jnp.concatenate([proj[rslot, t] for t in range(lo // LANES, hi // LANES)], axis=1)

    emit_stage1(2)
    qn = _head_rms(kept(0, Q_W), oq_ref, gq_ref[...]).astype(BF16)
    kn = _head_rms(kept(Q_W, Q_W + KV_W), ok_ref, gk_ref[...])
    v = kept(Q_W + KV_W, n_qkv)
    kbuf[0:WINDOW, :] = jnp.where(fresh, jnp.zeros((WINDOW, KV_W), BF16), kbuf[tm:tm + WINDOW, :])
    vbuf[:, 0:WINDOW] = jnp.where(fresh, jnp.zeros((KV_W, WINDOW), BF16), vbuf[:, tm:tm + WINDOW])
    kbuf[WINDOW:WINDOW + tm, :] = kn.astype(BF16)
    vbuf[:, WINDOW:WINDOW + tm] = v.T.astype(BF16)
    kwin_ref[...] = kn[tm - WINDOW:, :]
    vwin_ref[...] = v[tm - WINDOW:, :]

    first = jnp.where(fresh, 1, 0)
    lower = _lower_half((WINDOW, LANES))
    zero = jnp.zeros((WINDOW, LANES), BF16)

    def attend(s):
        rows = slice(s * WINDOW, (s + 1) * WINDOW)
        keys = kbuf[s * WINDOW:(s + 2) * WINDOW, :]
        vals_t = vbuf[:, s * WINDOW:(s + 2) * WINDOW]
        outs = []
        for g in range(KV_HEADS):
            qs = []
            for j in range(GROUP):
                slab = qn[rows, j * LANES:(j + 1) * LANES]
                qs.append(jnp.where(lower, slab, zero) if g == 0 else jnp.where(lower, zero, slab))
            logit_s[g] = _dot_nt(keys, jnp.concatenate(qs, axis=0))
            yield
            inv = []
            for j in range(GROUP):
                blk = slice(j * WINDOW, (j + 1) * WINDOW)
                bias = bias_ref[first, g, :, blk] if s == 0 else bias_ref[0, g, :, blk]
                logit = logit_s[g, :, blk] + bias
                sink = sink_ref[g, :, blk] + pace()[:, :WINDOW]
                m = jnp.maximum(jnp.max(logit, axis=0, keepdims=True), sink)
                e = jnp.exp(logit - m)
                inv.append(1.0 / (jnp.sum(e, axis=0, keepdims=True) + jnp.exp(sink - m)))
                e_s[g, :, blk] = e.astype(BF16)
                yield
            outs.append(_dot(vals_t, e_s[g]) * jnp.concatenate(inv, axis=1))
        for j in range(GROUP):
            blk = slice(j * WINDOW, (j + 1) * WINDOW)
            both = jnp.concatenate([outs[0][:HEAD_DIM, blk], outs[1][HEAD_DIM:, blk]], axis=0)
            att[rows, j * LANES:(j + 1) * LANES] = both.T.astype(BF16)
        yield

    sub8 = lax.broadcasted_iota(jnp.int32, (SUBLANES, MXU_DIM), 0)

    def recur(gi):
        cols = slice(gi * MXU_DIM, (gi + 1) * MXU_DIM)
        tiles = range(gi * (MXU_DIM // LANES), (gi + 1) * (MXU_DIM // LANES))
        by_segment = lambda k: pl.ds(k, SUBLANES, stride=pitch)
        x_at = lambda k: jnp.concatenate([projx[rslot, t, by_segment(k), :] for t in tiles], axis=1)
        last = SUBLANES * pitch - SEG_PAD
        x_tail = lambda n_rows: jnp.concatenate(
            [projx[rslot, t, last - n_rows:last, :] for t in tiles], axis=1)
        prev = jnp.where(fresh, 0.0, xp[:, cols])
        window = [jnp.where(sub8 == 0, prev[SUBLANES - j:SUBLANES - j + 1, :],
                            pltpu.roll(x_at(seg - j), 1, axis=0)) for j in range(CONV_W - 1, 0, -1)]
        for k in range(seg):
            window.append(x_at(k))
            xc_s[gi, k * SUBLANES:(k + 1) * SUBLANES, :] = cb8_ref[:, cols] + sum(
                cw8_ref[j, :, cols] * window[j] for j in range(CONV_W))
            window.pop(0)
            if k % 16 == 15:
                yield
        xp[:, cols] = x_tail(SUBLANES)
        cst_ref[:, cols] = x_tail(CONV_W - 1)
        y_s[gi] = _dot(xc_s[gi].astype(BF16), bd_ref[gi])
        yield
        ba, bx = ba8_ref[:, cols], bx8_ref[:, cols]
        log_decay = _log_decay(lam8_ref[:, cols])
        h_loc = a_cum = None
        for k in range(seg):
            r = slice(k * SUBLANES, (k + 1) * SUBLANES)
            a, bterm = _gate_math(y_s[gi, r, :MXU_DIM], y_s[gi, r, MXU_DIM:], xc_s[gi, r, :] + pace(),
                                  ba, bx, log_decay)
            h_loc = bterm if k == 0 else a * h_loc + bterm
            a_cum = a if k == 0 else a * a_cum
            hl_s[gi, r, :] = h_loc
            ac_s[gi, r, :] = a_cum
            if k % 4 == 3:
                yield
        h_in = jnp.where(fresh, 0.0, hc[:, cols])
        carry_in = []
        for sgm in range(SUBLANES):
            carry_in.append(h_in)
            h_in = h_loc[sgm:sgm + 1, :] + a_cum[sgm:sgm + 1, :] * h_in
        hc[:, cols] = h_in
        hst_ref[:, cols] = h_in
        carry_in = jnp.concatenate(carry_in, axis=0)
        for k in range(seg):
            r = slice(k * SUBLANES, (k + 1) * SUBLANES)
            h = hl_s[gi, r, :] + ac_s[gi, r, :] * carry_in
            for i in range(MXU_DIM // LANES):
                hn_s[gi, i, by_segment(k), :] = h[:, i * LANES:(i + 1) * LANES]
            if k % 8 == 7:
                yield
        h_out[:, cols] = jnp.concatenate(
            [jnp.concatenate([hn_s[gi, i, sgm * pitch:sgm * pitch + seg, :] for sgm in range(SUBLANES)], axis=0)
             for i in range(MXU_DIM // LANES)], axis=1).astype(BF16)
        yield

    n_sub = tm // WINDOW
    n_grp = d // MXU_DIM
    pieces = []
    for u in range(max(n_sub, n_grp)):
        if u < n_sub:
            pieces.append(attend(u))
        if u < n_grp:
            pieces.append(recur(u))
    n_spots = n_sub * (KV_HEADS * (1 + GROUP) + 1) + n_grp * (2 + seg // 16 + seg // 4 + seg // 8)
    per_spot = len(pending) / n_spots
    due = 0.0
    for piece in pieces:
        for _ in piece:
            due += per_spot
            emit_stage1(int(due))
            due -= int(due)
    emit_stage1(len(pending))


def _const_spec(shape):
    nd = len(shape)
    return pl.BlockSpec(shape, lambda *_: (0,) * nd, pipeline_mode=pl.Buffered(1))


def _prompt_mixer(x, w):
    b, t, d = x.shape
    tm = TM_MIX
    nt = t // tm
    n_blocks = b * nt
    padded_rows = SUBLANES * (tm // SUBLANES + SEG_PAD)
    consts = [w['n1g'], w['wqkv'], w['win'], w['gq'], w['gk'], w['oq'], w['ok'],
              w['bias_p'], w['sink_p'], w['cw8'], w['cb8'], w['bd'], w['ba8'], w['bx8'], w['lam8']]
    to_round = [w['wor32'], w['wout32'], w['wup32'], w['wdn32'], w['woa32'], w['wpg32'], w['wple32']]
    bf16_rows = 2 * SUBLANES
    n_chunks = lambda a: min(n_blocks, a.shape[0] // bf16_rows)

    def chunks(a, src_block=lambda c: c):
        assert a.shape[0] % (n_chunks(a) * bf16_rows) == 0
        block = (a.shape[0] // n_chunks(a), a.shape[1])
        at = lambda n: jnp.minimum(n, n_chunks(a) - 1)
        return (pl.BlockSpec(block, lambda n: (src_block(at(n)), 0)),
                pl.BlockSpec(block, lambda n: (at(n), 0)))

    per_head = n_chunks(w['woa32']) // N_HEADS
    assert per_head >= 1 and per_head * N_HEADS == n_chunks(w['woa32'])

    def woa_source(c):
        slot = c // per_head
        return ((slot % KV_HEADS) * GROUP + slot // KV_HEADS) * per_head + c % per_head

    round_specs = [chunks(a, woa_source if a is w['woa32'] else (lambda c: c)) for a in to_round]
    out_shape = (
        jax.ShapeDtypeStruct((b, t, d), BF16),
        jax.ShapeDtypeStruct((b, t, Q_W), BF16),
        jax.ShapeDtypeStruct((b, t, 3 * d), BF16),
        jax.ShapeDtypeStruct((b, WINDOW, KV_W), F32),
        jax.ShapeDtypeStruct((b, WINDOW, KV_W), F32),
        jax.ShapeDtypeStruct((b, CONV_W - 1, d), F32),
        jax.ShapeDtypeStruct((b, 1, d), F32),
    ) + tuple(jax.ShapeDtypeStruct(a.shape, BF16) for a in to_round)

    def cur(n):
        i = jnp.minimum(n, n_blocks - 1)
        return (i // nt, i % nt, 0)

    def prv(n):
        i = jnp.maximum(n - 1, 0)
        return (i // nt, i % nt, 0)

    per_seq = lambda n: (jnp.maximum(n - 1, 0) // nt, 0, 0)
    return pl.pallas_call(
        functools.partial(_prompt_mixer_kernel, blocks_per_seq=nt),
        out_shape=out_shape,
        grid=(n_blocks + 1,),
        in_specs=[pl.BlockSpec((None, tm, d), cur)]
                 + [_const_spec(c.shape) for c in consts]
                 + [src for src, _ in round_specs],
        out_specs=(
            pl.BlockSpec((None, tm, d), prv),
            pl.BlockSpec((None, tm, Q_W), prv),
            pl.BlockSpec((None, tm, 3 * d), cur),
            pl.BlockSpec((None, WINDOW, KV_W), per_seq),
            pl.BlockSpec((None, WINDOW, KV_W), per_seq),
            pl.BlockSpec((None, CONV_W - 1, d), per_seq),
            pl.BlockSpec((None, 1, d), per_seq),
        ) + tuple(dst for _, dst in round_specs),
        scratch_shapes=[
            pltpu.VMEM((2, KV_HEADS, 2 * WINDOW, GROUP * WINDOW), F32),
            pltpu.VMEM((2, REST0 // LANES, tm, LANES), F32),
            pltpu.VMEM((2, d // LANES, padded_rows, LANES), F32),
            pltpu.VMEM((WINDOW + tm, KV_W), BF16),
            pltpu.VMEM((KV_W, WINDOW + tm), BF16),
            pltpu.VMEM((SUBLANES, d), F32),
            pltpu.VMEM((1, d), F32),
            pltpu.VMEM((KV_HEADS, 2 * WINDOW, GROUP * WINDOW), F32),
            pltpu.VMEM((KV_HEADS, 2 * WINDOW, GROUP * WINDOW), BF16),
            pltpu.VMEM((d // MXU_DIM, tm, MXU_DIM), F32),
            pltpu.VMEM((d // MXU_DIM, tm, 2 * MXU_DIM), F32),
            pltpu.VMEM((d // MXU_DIM, tm, MXU_DIM), F32),
            pltpu.VMEM((d // MXU_DIM, tm, MXU_DIM), F32),
            pltpu.VMEM((d // MXU_DIM, MXU_DIM // LANES, padded_rows, LANES), F32),
        ],
        compiler_params=pltpu.CompilerParams(
            dimension_semantics=("arbitrary",), vmem_limit_bytes=VMEM_LIMIT),
        name="prompt_mixer",
    )(x, *consts, *to_round)


def _channel_rows(x_ref, h_ref, att_ref, pre_ref, p_ref, woa_ref, wor_ref, wout_ref,
                  n2g_ref, wup_ref, wdn_ref, pg_ref, wpg_ref, wple_ref, o_ref, before_mlp=None):
    d = x_ref.shape[1]
    pre = pre_ref[...].astype(F32)
    hg = (h_ref[...].astype(F32) * jax.nn.gelu(pre[:, :d])).astype(BF16)
    rnn = _dot(hg, wor_ref[...])
    atto = _dot(att_ref[...], woa_ref[...])
    mix = (jax.nn.sigmoid(pre[:, d:2 * d]) * atto + jax.nn.sigmoid(pre[:, 2 * d:]) * rnn).astype(BF16)
    x = x_ref[...] + _dot(mix, wout_ref[...])
    xn = _rms(x, n2g_ref[...]).astype(BF16)
    acc = x
    if before_mlp is not None:
        before_mlp()
    for c in range(wup_ref.shape[1] // FF_CHUNK):
        cols = slice(c * FF_CHUNK, (c + 1) * FF_CHUNK)
        hmid = jnp.maximum(_dot(xn, wup_ref[:, cols]), 0.0)
        acc = acc + _dot((hmid * hmid).astype(BF16), wdn_ref[cols, :])
    gate = jax.nn.sigmoid(_dot(_rms(acc, pg_ref[...]).astype(BF16), wpg_ref[...]))
    o_ref[...] = acc + gate * _dot(p_ref[...].astype(BF16), wple_ref[...])


N_ROW_INPUTS = 5
WUP_AT = 4


def _channel_kernel(*refs, first_steps):
    first, second = refs[:N_ROW_INPUTS], refs[N_ROW_INPUTS:2 * N_ROW_INPUTS]
    consts = list(refs[2 * N_ROW_INPUTS:-5])
    o_first, o_second, wup_v, wdn_v, sem = refs[-5:]
    i = pl.program_id(0)
    fetches = [pltpu.make_async_copy(consts[WUP_AT], wup_v, sem.at[0]),
               pltpu.make_async_copy(consts[WUP_AT + 1], wdn_v, sem.at[1])]
    consts[WUP_AT:WUP_AT + 2] = [wup_v, wdn_v]

    @pl.when(i == 0)
    def _():
        for f in fetches:
            f.start()

    def wait_fetches():
        @pl.when(i == 0)
        def _():
            for f in fetches:
                f.wait()

    @pl.when(i < first_steps)
    def _():
        _channel_rows(*first, *consts, o_first, before_mlp=wait_fetches)

    @pl.when(i >= first_steps)
    def _():
        _channel_rows(*second, *consts, o_second)


def _channel(first, second, w):
    m1, d = first[0].shape
    m2 = second[0].shape[0]
    tm = min(TM_MLP, m1, m2)
    n1, n2 = m1 // tm, m2 // tm
    consts = [w['woa'], w['wor'], w['wout'], w['n2g'], w['wup'], w['wdn'], w['pg'], w['wpg'], w['wple']]
    rows1 = lambda a: pl.BlockSpec((tm, a.shape[1]), lambda i: (jnp.minimum(i, n1 - 1), 0))
    rows2 = lambda a: pl.BlockSpec((tm, a.shape[1]), lambda i: (jnp.maximum(i - n1, 0), 0),
                                   pipeline_mode=pl.Buffered(1))
    return pl.pallas_call(
        functools.partial(_channel_kernel, first_steps=n1),
        out_shape=(jax.ShapeDtypeStruct((m1, d), F32), jax.ShapeDtypeStruct((m2, d), F32)),
        grid=(n1 + n2,),
        in_specs=[rows1(a) for a in first] + [rows2(a) for a in second]
                 + [pl.BlockSpec(memory_space=pl.ANY) if k in (WUP_AT, WUP_AT + 1) else _const_spec(c.shape)
                    for k, c in enumerate(consts)],
        out_specs=(rows1(first[0]), rows2(second[0])),
        scratch_shapes=[pltpu.VMEM(w['wup'].shape, BF16), pltpu.VMEM(w['wdn'].shape, BF16),
                        pltpu.SemaphoreType.DMA((2,))],
        compiler_params=pltpu.CompilerParams(
            dimension_semantics=("arbitrary",), vmem_limit_bytes=VMEM_LIMIT),
        name="merge_mlp_ple",
    )(*first, *second, *consts)


def _decode_attn_kernel(x_ref, ck_ref, cv_ref, n1g_ref, wqkv_ref, gq_ref, gk_ref, oq_ref, ok_ref,
                        bias_c_ref, bias_n_ref, sink_ref,
                        att_ref, kwin_ref, vwin_ref, qkv_s):
    sb = ck_ref.shape[0]
    nt = att_ref.shape[0] // sb

    @pl.when(pl.program_id(0) == 0)
    def _():
        qkv_s[...] = _dot(_rms(x_ref[...], n1g_ref[...]).astype(BF16), wqkv_ref[...])

    qkv = qkv_s[pl.ds(pl.multiple_of(pl.program_id(0) * (sb * nt), sb * nt), sb * nt), :]
    qn = _head_rms(qkv[:, :Q_W], oq_ref, gq_ref[...])
    kn = _head_rms(qkv[:, Q_W:Q_W + KV_W], ok_ref, gk_ref[...])
    v = qkv[:, Q_W + KV_W:]

    n_rows = sb * nt
    ck = ck_ref[...]
    cv = cv_ref[...]
    for b in range(sb):
        rows = slice(b * nt, (b + 1) * nt)
        kwin_ref[b, 0:WINDOW - nt, :] = ck[b, nt:, :]
        vwin_ref[b, 0:WINDOW - nt, :] = cv[b, nt:, :]
        kwin_ref[b, WINDOW - nt:WINDOW, :] = kn[rows]
        vwin_ref[b, WINDOW - nt:WINDOW, :] = v[rows]

    qb = qn.astype(BF16)
    lower = _lower_half((n_rows, LANES))
    zero = jnp.zeros((n_rows, LANES), BF16)
    q_all = jnp.concatenate(
        [jnp.where(lower, qb[:, j * LANES:(j + 1) * LANES], zero) if g == 0
         else jnp.where(lower, zero, qb[:, j * LANES:(j + 1) * LANES])
         for g in range(KV_HEADS) for j in range(GROUP)], axis=0)
    n_col = q_all.shape[0]
    col_seq = (lax.broadcasted_iota(jnp.int32, (1, n_col), 1) // nt) % sb

    def own(big):
        out = big[0:WINDOW]
        for b in range(1, sb):
            out = jnp.where(col_seq == b, big[b * WINDOW:(b + 1) * WINDOW], out)
        return out

    st = own(_dot_nt(ck.reshape(sb * WINDOW, KV_W).astype(BF16), q_all)) + bias_c_ref[...]
    stx = _dot_nt(kn.astype(BF16), q_all) + bias_n_ref[...]
    sink = sink_ref[...]
    m = jnp.maximum(jnp.maximum(jnp.max(st, axis=0, keepdims=True),
                                jnp.max(stx, axis=0, keepdims=True)), sink)
    e = jnp.exp(st - m)
    ex = jnp.exp(stx - m)
    den = (jnp.sum(e, axis=0, keepdims=True) + jnp.sum(ex, axis=0, keepdims=True)
           + jnp.exp(sink - m))
    cv_t = jnp.concatenate([cv[b].T for b in range(sb)], axis=0).astype(BF16)
    out = own(_dot(cv_t, e.astype(BF16)))
    pad = WINDOW - n_rows
    v_t = jnp.concatenate([v, jnp.zeros((pad, KV_W), F32)], axis=0).T.astype(BF16)
    ex_pad = jnp.concatenate([ex, jnp.zeros((pad, n_col), F32)], axis=0).astype(BF16)
    out = (out + _dot(v_t, ex_pad)) / den
    half = n_col // KV_HEADS
    both_t = jnp.concatenate([out[:HEAD_DIM, :half], out[HEAD_DIM:, half:]], axis=0).T
    for j in range(GROUP):
        att_ref[:, j * LANES:(j + 1) * LANES] = both_t[j * n_rows:(j + 1) * n_rows].astype(att_ref.dtype)


def _decode_attn(x, ck, cv, w):
    m, d = x.shape
    nb = ck.shape[0]
    nt = m // nb
    sb = SEQ_BLOCK
    consts = [w['n1g'], w['wqkv'], w['gq'], w['gk'], w['oq'], w['ok'],
              w['bias_sc'], w['bias_sn'], w['sink_s']]
    cache_spec = pl.BlockSpec((sb, WINDOW, KV_W), lambda i: (i, 0, 0))
    return pl.pallas_call(
        _decode_attn_kernel,
        out_shape=(jax.ShapeDtypeStruct((m, Q_W), BF16),
                   jax.ShapeDtypeStruct(ck.shape, F32),
                   jax.ShapeDtypeStruct(cv.shape, F32)),
        grid=(nb // sb,),
        in_specs=[_const_spec(x.shape), cache_spec, cache_spec]
                 + [_const_spec(c.shape) for c in consts],
        out_specs=(pl.BlockSpec((sb * nt, Q_W), lambda i: (i, 0)), cache_spec, cache_spec),
        scratch_shapes=[pltpu.VMEM((m, w['wqkv'].shape[1]), F32)],
        compiler_params=pltpu.CompilerParams(
            dimension_semantics=("arbitrary",), vmem_limit_bytes=VMEM_LIMIT),
        name="decode_attn",
    )(x, ck, cv, *consts)


def _decode_mixer_kernel(x_ref, cst_ref, h0_ref, n1g_ref, win_ref, cw_ref, cb_ref, bd_ref,
                         ba_ref, bx_ref, lam_ref,
                         h_out, pre_out, cnew_ref, hnew_ref):
    nb = h0_ref.shape[0]
    nt = x_ref.shape[0] // nb
    d = x_ref.shape[1]
    xn = _rms(x_ref[...], n1g_ref[...]).astype(BF16)
    xr = _dot(xn, win_ref[:, REST0:REST0 + d])
    prev = cst_ref[...]
    slabs = [prev[k * nb:(k + 1) * nb] for k in range(CONV_W - 1)]
    slabs += [xr[k * nb:(k + 1) * nb] for k in range(nt)]
    xc = jnp.concatenate(
        [cb_ref[...] + sum(cw_ref[j:j + 1, :] * slabs[ti + j] for j in range(CONV_W))
         for ti in range(nt)], axis=0)
    cnew_ref[...] = jnp.concatenate(slabs[-(CONV_W - 1):], axis=0)

    a, bterm = _rglru_gates(xc, bd_ref, ba_ref[...], bx_ref[...], lam_ref[...])
    h = h0_ref[...]
    hs = []
    for ti in range(nt):
        h = a[ti * nb:(ti + 1) * nb] * h + bterm[ti * nb:(ti + 1) * nb]
        hs.append(h)
    hnew_ref[...] = h
    h_out[...] = jnp.concatenate(hs, axis=0).astype(BF16)
    pre_out[...] = _dot(xn, win_ref[:, REST0 + d:REST0 + 4 * d]).astype(BF16)


def _decode_mixer(x, cst, h0, w):
    m, d = x.shape
    consts = [w['n1g'], w['win'], w['cw'], w['cb'], w['bd'], w['ba'], w['bx'], w['lam']]
    ins = [x, cst, h0] + consts
    whole = lambda shape: pl.BlockSpec(shape, lambda i: (0,) * len(shape))
    return pl.pallas_call(
        _decode_mixer_kernel,
        out_shape=(jax.ShapeDtypeStruct((m, d), BF16),
                   jax.ShapeDtypeStruct((m, 3 * d), BF16),
                   jax.ShapeDtypeStruct(cst.shape, F32),
                   jax.ShapeDtypeStruct(h0.shape, F32)),
        grid=(1,),
        in_specs=[_const_spec(c.shape) for c in ins],
        out_specs=(whole((m, d)), whole((m, 3 * d)), whole(cst.shape), whole(h0.shape)),
        compiler_params=pltpu.CompilerParams(
            dimension_semantics=("arbitrary",), vmem_limit_bytes=VMEM_LIMIT),
        name="decode_mixer",
    )(*ins)


def _rel_bucket(dist):
    n = np.maximum(dist, 0)
    max_exact = REL_BUCKETS // 2
    nf = np.maximum(n, 1).astype(np.float32)
    large = max_exact + (np.log(nf / max_exact) / math.log(REL_MAX_DIST / max_exact)
                         * (REL_BUCKETS - max_exact)).astype(np.int32)
    large = np.minimum(large, REL_BUCKETS - 1)
    return np.where(n < max_exact, n, large)


def _bias_rows(rel_bias, dist, mask):
    tb = jnp.where(mask[:, :, None], rel_bias[_rel_bucket(dist)].astype(F32), NEG_INF)
    tq, tk = dist.shape
    return jnp.transpose(tb, (2, 0, 1)).reshape(N_HEADS * tq, tk)


def _prompt_bias_rows(rel_bias):
    span = 3 * WINDOW
    k = np.arange(span)
    dist = WINDOW + np.where(k < WINDOW, k, k - span)
    valid = (dist >= 0) & (dist <= WINDOW)
    return jnp.where(valid[:, None], rel_bias[_rel_bucket(dist)].astype(F32), NEG_INF).T


def _gate_blocks(rg_wa, rg_wx, n_grp):
    per = rg_wa.shape[0] // n_grp
    both = jnp.stack([rg_wa, rg_wx]).reshape(2, n_grp, per, RNN_BS, RNN_BS)
    on_diag = np.eye(per, dtype=bool)[None, None, :, None, :, None]
    blocks = jnp.where(on_diag, both[:, :, :, :, None, :], 0.0)
    blocks = jnp.transpose(blocks, (1, 2, 3, 0, 4, 5))
    return blocks.reshape(n_grp, per * RNN_BS, 2 * per * RNN_BS).astype(BF16)


def _head_avg(width):
    idx = np.arange(width) // HEAD_DIM
    return jnp.asarray((idx[:, None] == idx[None, :]).astype(np.float32) / HEAD_DIM, BF16)


def _prepare(rel_bias, norm1_g, w_in, q_norm_g, k_norm_g, sinks, w_o_attn, conv_w, conv_b,
             rg_wa, rg_ba, rg_wx, rg_bx, rg_lambda, w_o_rnn, w_out, norm2_g, w_up, w_down,
             ple_norm_g, w_ple_gate, w_ple, n_dec):
    d = w_in.shape[0]
    order = [g * GROUP + j for j in range(GROUP) for g in range(KV_HEADS)]
    perm = np.concatenate([np.arange(h * HEAD_DIM, (h + 1) * HEAD_DIM) for h in order])
    row = lambda a: a.reshape(1, -1).astype(F32)
    w = {}
    w['n1g'] = row(norm1_g)
    w['win'] = w_in.astype(BF16)
    w['wqkv'] = jnp.concatenate([w['win'][:, :Q_W][:, perm], w['win'][:, Q_W:REST0]], axis=1)
    w['gq'] = row(jnp.tile(q_norm_g, N_HEADS)) * (HEAD_DIM ** -0.5)
    w['gk'] = row(jnp.tile(k_norm_g, KV_HEADS))
    w['oq'] = _head_avg(Q_W)
    w['ok'] = _head_avg(KV_W)
    w['cw'] = conv_w.astype(F32)
    w['cb'] = row(conv_b)
    w['bd'] = _gate_blocks(rg_wa, rg_wx, d // MXU_DIM)
    w['ba'] = row(rg_ba)
    w['bx'] = row(rg_bx)
    w['lam'] = row(rg_lambda)
    rows8 = lambda a: jnp.broadcast_to(a, (SUBLANES, a.shape[-1]))
    w['cw8'] = jnp.broadcast_to(w['cw'][:, None, :], (CONV_W, SUBLANES, d))
    w['cb8'], w['ba8'], w['bx8'], w['lam8'] = (rows8(w[k]) for k in ('cb', 'ba', 'bx', 'lam'))
    w['wor32'], w['wout32'], w['wup32'], w['wdn32'] = w_o_rnn, w_out, w_up, w_down
    w['woa32'], w['wpg32'], w['wple32'] = w_o_attn, w_ple_gate, w_ple
    w['n2g'] = row(norm2_g)
    w['pg'] = row(ple_norm_g)

    w['bias_p'] = _prompt_bias_rows(rel_bias)
    sink_rows = sinks.astype(F32).reshape(KV_HEADS, GROUP, 1)
    w['sink_p'] = jnp.repeat(sink_rows, WINDOW, axis=2).reshape(KV_HEADS, 1, GROUP * WINDOW)

    sb = SEQ_BLOCK
    ti = np.arange(n_dec)[:, None]
    dist_c = ti + WINDOW - np.arange(WINDOW)[None, :]
    dist_n = ti - np.arange(n_dec)[None, :]
    n_col = N_HEADS * sb * n_dec
    col = np.arange(n_col)
    col_head_t = col // (sb * n_dec) * n_dec + col % n_dec
    spread = (np.arange(N_HEADS * n_dec)[:, None] == col_head_t[None, :]).astype(np.float32)
    expand = lambda a: jnp.dot(a.T, spread, precision=lax.Precision.HIGHEST)
    w['bias_sc'] = expand(_bias_rows(rel_bias, dist_c, dist_c <= WINDOW))
    bias_n = expand(_bias_rows(rel_bias, dist_n, dist_n >= 0))
    own = (np.arange(sb * n_dec) // n_dec)[:, None] == (col // n_dec % sb)[None, :]
    w['bias_sn'] = jnp.where(own, jnp.tile(bias_n, (sb, 1)), NEG_INF)
    w['sink_s'] = jnp.repeat(sinks.astype(F32), sb * n_dec)[None, :]
    return w


def kernel(x_prompt, x_sample, cache_k_win, cache_v_win, state_conv, state_h, p_prompt, p_sample, rel_bias, norm1_g, w_in, q_norm_g, k_norm_g, sinks, w_o_attn, conv_w, conv_b, rg_wa, rg_ba, rg_wx, rg_bx, rg_lambda, w_o_rnn, w_out, norm2_g, w_up, w_down, ple_norm_g, w_ple_gate, w_ple):
    depth = w_in.shape[0]
    assert depth == 1, "single-layer step"
    b, t, d = x_prompt.shape
    nb, nt, _ = x_sample.shape
    w = _prepare(rel_bias, norm1_g[0], w_in[0], q_norm_g[0], k_norm_g[0], sinks[0], w_o_attn[0],
                 conv_w[0], conv_b[0], rg_wa[0], rg_ba[0], rg_wx[0], rg_bx[0], rg_lambda[0],
                 w_o_rnn[0], w_out[0], norm2_g[0], w_up[0], w_down[0], ple_norm_g[0],
                 w_ple_gate[0], w_ple[0], nt)

    (hp_rows, attp, prep, kp, vp, cp, hp,
     w['wor'], w['wout'], w['wup'], w['wdn'], w['woa'], w['wpg'], w['wple']) = _prompt_mixer(x_prompt, w)
    flat = lambda a: a.reshape(b * t, a.shape[-1])

    att, ks, vs = _decode_attn(x_sample.reshape(nb * nt, d),
                               cache_k_win[0].reshape(nb, WINDOW, KV_W),
                               cache_v_win[0].reshape(nb, WINDOW, KV_W), w)
    to_tm = lambda a: jnp.swapaxes(a, 0, 1).reshape(-1, a.shape[-1])
    xs_tm = to_tm(x_sample)
    hs_rows, pres, cs, hs = _decode_mixer(xs_tm, to_tm(state_conv[0]), state_h[0], w)

    yp, ys = _channel((flat(x_prompt), flat(hp_rows), flat(attp), flat(prep), flat(p_prompt[0])),
                      (xs_tm, hs_rows, to_tm(att.reshape(nb, nt, Q_W)), pres, to_tm(p_sample[0])), w)
    yp = yp.reshape(b, t, d)
    from_tm = lambda a, n: jnp.swapaxes(a.reshape(n, nb, a.shape[-1]), 0, 1)

    kv_shape = (1, -1, WINDOW, KV_HEADS, HEAD_DIM)
    return (yp, from_tm(ys, nt),
            kp.reshape(kv_shape), vp.reshape(kv_shape), cp[None], hp.reshape(1, b, d),
            ks.reshape(kv_shape), vs.reshape(kv_shape), from_tm(cs, CONV_W - 1)[None], hs[None])
```
